```python
import functools
import jax, jax.numpy as jnp
from jax import lax
import numpy as np

D_MODEL = 2048
BATCH = 4
SEQ = 2048
DEPTH = 2
DEC_BATCH = 8
DEC_SEQ = 4
PAST_LEN = 16384
PAGE_SIZE = 128

HEAD_DIM = 128
N_A_LAYERS = DEPTH // 2
N_B_LAYERS = DEPTH - N_A_LAYERS
MIX_WIDTH = D_MODEL
MEM_HEADS = 4
MEM_WIDTH = MEM_HEADS * HEAD_DIM
MAIN_WIDTH = MIX_WIDTH - MEM_WIDTH
MAIN_HEADS = MAIN_WIDTH // HEAD_DIM
CHUNK = 128
Q_BLOCK = 128
N_MEM = 256
N_GROUPS = 4
EXPERTS_PER_GROUP = 4
N_EXPERTS = N_GROUPS * EXPERTS_PER_GROUP
TOP_K_INNER = 2
D_EXPERT = 512
EPS = 1e-6
FORGET_BIAS_INIT = 3.0
ATTN_SCALE = HEAD_DIM ** -0.5

kernel_name = "yoco_gmlp_fox_hmoe_step"


def rms_norm(x, gain):
    xf = x.astype(jnp.float32)
    y = xf * lax.rsqrt(jnp.mean(xf * xf, axis=-1, keepdims=True) + EPS)
    return (y * gain.astype(jnp.float32)).astype(x.dtype)


def split_heads(x, n):
    return x.reshape(x.shape[:-1] + (n, HEAD_DIM))


def chunk_spatial_gate(z_u, z_v, v_gain, w_s, b_s):
    b, t, _ = z_u.shape
    L = min(t, CHUNK)
    nc = t // L
    u = jax.nn.gelu(z_u)
    v = rms_norm(jax.nn.gelu(z_v), v_gain)
    vh = v.reshape(b, nc, L, MAIN_HEADS, HEAD_DIM)
    w = w_s[:, :L, :L] * jnp.tril(jnp.ones((L, L), w_s.dtype))
    mixed = jnp.einsum('gts,bnsgc->bntgc', w, vh) + b_s[:, :L].T[None, None, :, :, None]
    return u * mixed.reshape(b, t, MAIN_WIDTH), v


def mem_keys_values(mem, gain, w_kv, k_gain):
    m = rms_norm(mem, gain)
    kv = m @ w_kv
    k = rms_norm(split_heads(kv[..., :MEM_WIDTH], MEM_HEADS), k_gain)
    v = split_heads(kv[..., MEM_WIDTH:], MEM_HEADS)
    return k, v


def mem_attend(q, k, v):
    b, t = q.shape[:2]
    s = jnp.einsum('bthd,bmhd->bhtm', q, k).astype(jnp.float32) * ATTN_SCALE
    p = jax.nn.softmax(s, axis=-1).astype(v.dtype)
    return jnp.einsum('bhtm,bmhd->bthd', p, v).reshape(b, t, MEM_WIDTH)


def shared_kv(h, gain, w_kv, b_forget, k_gain):
    x = rms_norm(h, gain)
    z = x @ w_kv
    k = rms_norm(split_heads(z[..., :MAIN_WIDTH], MAIN_HEADS), k_gain)
    v = split_heads(z[..., MAIN_WIDTH:2 * MAIN_WIDTH], MAIN_HEADS)
    logf = jax.nn.log_sigmoid(z[..., 2 * MAIN_WIDTH:].astype(jnp.float32)
                              + b_forget.astype(jnp.float32))
    return k, v, logf


def forgetting_attention_prompt(q, k, v, logf):
    b, s, h, d = q.shape
    c = jnp.cumsum(logf.astype(jnp.float32), axis=1)
    c_k = c.transpose(0, 2, 1)
    nb = s // Q_BLOCK
    qb = q.reshape(b, nb, Q_BLOCK, h, d).transpose(1, 0, 2, 3, 4)
    cb = c.reshape(b, nb, Q_BLOCK, h).transpose(1, 0, 2, 3)
    starts = jnp.arange(nb, dtype=jnp.int32) * Q_BLOCK
    key_pos = jnp.arange(s, dtype=jnp.int32)

    def block(args):
        q_i, c_i, t0 = args
        sc = jnp.einsum('bqhd,bkhd->bhqk', q_i, k).astype(jnp.float32) * ATTN_SCALE
        sc = sc + c_i.transpose(0, 2, 1)[..., None] - c_k[:, :, None, :]
        q_pos = t0 + jnp.arange(Q_BLOCK, dtype=jnp.int32)
        sc = jnp.where(key_pos[None, :] <= q_pos[:, None], sc, -jnp.inf)
        p = jax.nn.softmax(sc, axis=-1).astype(v.dtype)
        return jnp.einsum('bhqk,bkhd->bqhd', p, v)

    out = lax.map(block, (qb, cb, starts))
    return out.transpose(1, 0, 2, 3, 4).reshape(b, s, h * d)


def forgetting_attention_sample(q, k_new, v_new, logf_new, k_past, v_past, logf_past):
    b, t, h, d = q.shape
    c_new = jnp.cumsum(logf_new.astype(jnp.float32), axis=1)
    lp = logf_past.astype(jnp.float32)
    after = lax.cumsum(lp, axis=1, reverse=True) - lp
    cq = c_new.transpose(0, 2, 1)
    s_past = (jnp.einsum('bqhd,bkhd->bhqk', q, k_past).astype(jnp.float32) * ATTN_SCALE
              + cq[..., None] + after.transpose(0, 2, 1)[:, :, None, :])
    s_new = (jnp.einsum('bqhd,bkhd->bhqk', q, k_new).astype(jnp.float32) * ATTN_SCALE
             + cq[..., None] - cq[:, :, None, :])
    causal = jnp.tril(jnp.ones((t, t), dtype=bool))
    s_new = jnp.where(causal, s_new, -jnp.inf)
    p = jax.nn.softmax(jnp.concatenate([s_past, s_new], axis=-1), axis=-1).astype(v_new.dtype)
    n_past = k_past.shape[1]
    out = (jnp.einsum('bhqk,bkhd->bqhd', p[..., :n_past], v_past)
           + jnp.einsum('bhqk,bkhd->bqhd', p[..., n_past:], v_new))
    return out.reshape(b, t, h * d)


def hier_moe(x, w_rg, b_rg, w_re, b_re, w_gate, w_up, w_down):
    shp = x.shape
    t = x.reshape(-1, shp[-1])
    p_grp = jax.nn.softmax((t @ w_rg + b_rg).astype(jnp.float32), axis=-1)
    p_top, g_idx = lax.top_k(p_grp, 1)
    e_logits = (t @ w_re + b_re).astype(jnp.float32).reshape(-1, N_GROUPS, EXPERTS_PER_GROUP)
    e_in = jnp.take_along_axis(e_logits, g_idx[:, :, None], axis=1)[:, 0]
    e_top, e_idx = lax.top_k(e_in, TOP_K_INNER)
    gates = p_top * jax.nn.softmax(e_top, axis=-1)
    expert = g_idx * EXPERTS_PER_GROUP + e_idx
    dense_gate = jnp.sum(jax.nn.one_hot(expert, N_EXPERTS, dtype=jnp.float32) * gates[..., None], axis=1)
    hdn = jax.nn.silu(jnp.einsum('td,edf->tef', t, w_gate)) * jnp.einsum('td,edf->tef', t, w_up)
    hdn = hdn * dense_gate.astype(hdn.dtype)[..., None]
    return jnp.einsum('tef,efd->td', hdn, w_down).reshape(shp)


def gather_pages(pool, page_table):
    g = pool[page_table]
    return g.reshape((page_table.shape[0], page_table.shape[1] * PAGE_SIZE) + pool.shape[2:])


def _trunk(x, mem_kvs, fox, w):
    h = x
    v_rows = []
    kv = None
    for l in range(DEPTH):
        a = rms_norm(h, w['norm1_gain'][l])
        if l < N_A_LAYERS:
            z = a @ w['w_in_a'][l]
            y_main, v = chunk_spatial_gate(z[..., :MAIN_WIDTH], z[..., MAIN_WIDTH:2 * MAIN_WIDTH],
                                           w['v_gain_a'][l], w['w_s_a'][l], w['b_s_a'][l])
            v_rows.append(v)
            zq = z[..., 2 * MAIN_WIDTH:]
        else:
            j = l - N_A_LAYERS
            z = a @ w['w_in_b'][j]
            q = rms_norm(split_heads(z[..., :MAIN_WIDTH], MAIN_HEADS), w['q_gain_b'][j])
            y_main = fox(q, kv[0], kv[1], kv[2])
            zq = z[..., MAIN_WIDTH:]
        qm = rms_norm(split_heads(zq, MEM_HEADS), w['mem_q_gain'][l])
        y_mem = mem_attend(qm, mem_kvs[l][0], mem_kvs[l][1])
        h = h + jnp.concatenate([y_main, y_mem], axis=-1) @ w['w_out'][l]
        h = h + hier_moe(rms_norm(h, w['norm2_gain'][l]), w['w_router_group'][l], w['b_router_group'][l],
                         w['w_router_expert'][l], w['b_router_expert'][l],
                         w['w_gate'][l], w['w_up'][l], w['w_down'][l])
        if l == N_A_LAYERS - 1:
            kv = shared_kv(h, w['kv_norm_gain'], w['w_kv_shared'], w['b_forget'], w['k_gain_shared'])
    return h, kv, v_rows


def setup_inputs(seed: int = 0) -> dict:
    key = jax.random.key(seed)
    ks = iter(jax.random.split(key, 48))

    def nrm(shape, scale=1.0):
        return jax.random.normal(next(ks), shape, jnp.float32) * scale

    def gain(shape):
        return 1.0 + nrm(shape, 0.05)

    n_pages = PAST_LEN // PAGE_SIZE
    n_used = DEC_BATCH * n_pages
    n_phys = n_used + n_used // 4
    x_prompt = nrm((BATCH, SEQ, D_MODEL))
    x_sample = nrm((DEC_BATCH, DEC_SEQ, D_MODEL))
    cache_k = nrm((n_phys, PAGE_SIZE, MAIN_HEADS, HEAD_DIM))
    cache_v = nrm((n_phys, PAGE_SIZE, MAIN_HEADS, HEAD_DIM))
    cache_logf = jax.nn.log_sigmoid(FORGET_BIAS_INIT + nrm((n_phys, PAGE_SIZE, MAIN_HEADS)))
    cache_mem_k = nrm((DEPTH, DEC_BATCH, N_MEM, MEM_HEADS, HEAD_DIM))
    cache_mem_v = nrm((DEPTH, DEC_BATCH, N_MEM, MEM_HEADS, HEAD_DIM))
    page_table = jax.random.permutation(next(ks), n_phys)[:n_used].astype(jnp.int32).reshape(DEC_BATCH, n_pages)
    mem_prompt = nrm((BATCH, N_MEM, D_MODEL))
    sd = D_MODEL ** -0.5
    return {
        "x_prompt": x_prompt,
        "x_sample": x_sample,
        "cache_k": cache_k,
        "cache_v": cache_v,
        "cache_logf": cache_logf,
        "cache_mem_k": cache_mem_k,
        "cache_mem_v": cache_mem_v,
        "page_table": page_table,
        "mem_prompt": mem_prompt,
        "norm1_gain": gain((DEPTH, D_MODEL)),
        "norm2_gain": gain((DEPTH, D_MODEL)),
        "w_in_a": nrm((N_A_LAYERS, D_MODEL, 2 * MAIN_WIDTH + MEM_WIDTH), sd),
        "v_gain_a": gain((N_A_LAYERS, MAIN_WIDTH)),
        "w_s_a": nrm((N_A_LAYERS, MAIN_HEADS, CHUNK, CHUNK), CHUNK ** -0.5),
        "b_s_a": gain((N_A_LAYERS, MAIN_HEADS, CHUNK)),
        "w_in_b": nrm((N_B_LAYERS, D_MODEL, MAIN_WIDTH + MEM_WIDTH), sd),
        "q_gain_b": gain((N_B_LAYERS, HEAD_DIM)),
        "kv_norm_gain": gain((D_MODEL,)),
        "w_kv_shared": nrm((D_MODEL, 2 * MAIN_WIDTH + MAIN_HEADS), sd),
        "b_forget": FORGET_BIAS_INIT + nrm((MAIN_HEADS,), 0.1),
        "k_gain_shared": gain((HEAD_DIM,)),
        "mem_norm_gain": gain((DEPTH, D_MODEL)),
        "w_mem_kv": nrm((DEPTH, D_MODEL, 2 * MEM_WIDTH), sd),
        "mem_q_gain": gain((DEPTH, HEAD_DIM)),
        "mem_k_gain": gain((DEPTH, HEAD_DIM)),
        "w_out": nrm((DEPTH, MIX_WIDTH, D_MODEL), MIX_WIDTH ** -0.5),
        "w_router_group": nrm((DEPTH, D_MODEL, N_GROUPS), sd),
        "b_router_group": nrm((DEPTH, N_GROUPS), 0.01),
        "w_router_expert": nrm((DEPTH, D_MODEL, N_EXPERTS), sd),
        "b_router_expert": nrm((DEPTH, N_EXPERTS), 0.01),
        "w_gate": nrm((DEPTH, N_EXPERTS, D_MODEL, D_EXPERT), sd),
        "w_up": nrm((DEPTH, N_EXPERTS, D_MODEL, D_EXPERT), sd),
        "w_down": nrm((DEPTH, N_EXPERTS, D_EXPERT, D_MODEL), D_EXPERT ** -0.5),
    }


def reference(x_prompt, x_sample, cache_k, cache_v, cache_logf, cache_mem_k, cache_mem_v, page_table, mem_prompt,
              norm1_gain, norm2_gain, w_in_a, v_gain_a, w_s_a, b_s_a, w_in_b, q_gain_b,
              kv_norm_gain, w_kv_shared, b_forget, k_gain_shared,
              mem_norm_gain, w_mem_kv, mem_q_gain, mem_k_gain, w_out,
              w_router_group, b_router_group, w_router_expert, b_router_expert, w_gate, w_up, w_down):
    w = dict(norm1_gain=norm1_gain, norm2_gain=norm2_gain, w_in_a=w_in_a, v_gain_a=v_gain_a,
             w_s_a=w_s_a, b_s_a=b_s_a, w_in_b=w_in_b, q_gain_b=q_gain_b,
             kv_norm_gain=kv_norm_gain, w_kv_shared=w_kv_shared, b_forget=b_forget,
             k_gain_shared=k_gain_shared, mem_q_gain=mem_q_gain, w_out=w_out,
             w_router_group=w_router_group, b_router_group=b_router_group,
             w_router_expert=w_router_expert, b_router_expert=b_router_expert,
             w_gate=w_gate, w_up=w_up, w_down=w_down)

    mem_kv_p = [mem_keys_values(mem_prompt, mem_norm_gain[l], w_mem_kv[l], mem_k_gain[l]) for l in range(DEPTH)]
    y_prompt, kv_p, vrows_p = _trunk(x_prompt, mem_kv_p, forgetting_attention_prompt, w)

    k_past = gather_pages(cache_k, page_table)
    v_past = gather_pages(cache_v, page_table)
    logf_past = gather_pages(cache_logf, page_table)
    fox_s = functools.partial(forgetting_attention_sample, k_past=k_past, v_past=v_past, logf_past=logf_past)
    mem_kv_s = [(cache_mem_k[l], cache_mem_v[l]) for l in range(DEPTH)]
    y_sample, kv_s, vrows_s = _trunk(x_sample, mem_kv_s, fox_s, w)

    mem_k_prompt = jnp.stack([kv[0] for kv in mem_kv_p])
    mem_v_prompt = jnp.stack([kv[1] for kv in mem_kv_p])
    chunk_v_prompt = jnp.stack([v[:, -CHUNK:] for v in vrows_p])
    chunk_v_sample = jnp.stack(vrows_s)
    logf_prompt = kv_p[2].astype(x_prompt.dtype)
    logf_sample = kv_s[2].astype(x_sample.dtype)
    return (y_prompt, y_sample, kv_p[0], kv_p[1], logf_prompt, kv_s[0], kv_s[1], logf_sample,
            mem_k_prompt, mem_v_prompt, chunk_v_prompt, chunk_v_sample)
```

```python
import functools

import jax
import jax.numpy as jnp
from jax import lax
from jax.experimental import pallas as pl
from jax.experimental.pallas import tpu as pltpu

D_MODEL = 2048
HEAD_DIM = 128
MAIN_HEADS = 12
MEM_HEADS = 4
MAIN_WIDTH = MAIN_HEADS * HEAD_DIM
MEM_WIDTH = MEM_HEADS * HEAD_DIM
N_MEM = 256
N_GROUPS = 4
EXPERTS_PER_GROUP = 4
N_EXPERTS = N_GROUPS * EXPERTS_PER_GROUP
D_EXPERT = 512
CHUNK = 128
PAGE_SIZE = 128
EPS = 1e-6
ATTN_SCALE = HEAD_DIM ** -0.5

LANES = 128
SUBLANES = 8
VMEM_LIMIT_BYTES = 56 * 1024 * 1024
SAMPLE_ROWS = SUBLANES

F32 = jnp.float32
_NT = (((1,), (1,)), ((), ()))


def _params(n_axes):
    return pltpu.CompilerParams(dimension_semantics=("arbitrary",) * n_axes,
                                vmem_limit_bytes=VMEM_LIMIT_BYTES)


HIGHEST = lax.Precision.HIGHEST


def _dot(a, b, prec=None):
    return jnp.dot(a, b, preferred_element_type=F32, precision=prec)


def _dot_nt(a, b, prec=None):
    return lax.dot_general(a, b, _NT, preferred_element_type=F32, precision=prec)


def _split_hi_lo(x):
    hi = lax.bitcast_convert_type(lax.bitcast_convert_type(x, jnp.uint32) & jnp.uint32(0xFFFF0000), F32)
    return hi, x - hi


def _split3(x):
    hi = x.astype(jnp.bfloat16).astype(F32)
    r = x - hi
    mid = r.astype(jnp.bfloat16).astype(F32)
    lo = (r - mid).astype(jnp.bfloat16).astype(F32)
    return hi, mid, lo


def _dot3_right(a_exact, x):
    hi, mid, lo = _split3(x)
    return _dot(a_exact, hi) + _dot(a_exact, mid) + _dot(a_exact, lo)


def _dot3_left(x, b_exact):
    hi, mid, lo = _split3(x)
    return _dot(hi, b_exact) + _dot(mid, b_exact) + _dot(lo, b_exact)


def _head_norm(z, gain_row):
    return z * lax.rsqrt(jnp.mean(z * z, axis=-1, keepdims=True) + EPS) * gain_row


def _norm_matmul_kernel(*refs, segs, tn, with_forget, prec):
    it = iter(refs)
    x_ref, g_ref, w_ref, hg_ref = next(it), next(it), next(it), next(it)
    if with_forget:
        wf_ref, wft_ref, bfr_ref, bfc_ref = next(it), next(it), next(it), next(it)
    out_refs = [next(it) for _ in segs]
    if with_forget:
        lf_ref, lft_ref = next(it), next(it)
    xn_ref = next(it)
    j = pl.program_id(1)

    @pl.when(j == 0)
    def _prologue():
        x = x_ref[...]
        xn = x * lax.rsqrt(jnp.mean(x * x, axis=-1, keepdims=True) + EPS) * g_ref[...]
        xn_ref[...] = xn
        if with_forget:
            lf_ref[...] = jax.nn.log_sigmoid(_dot(xn, wf_ref[...], prec) + bfr_ref[...])
            lft_ref[...] = jax.nn.log_sigmoid(_dot_nt(wft_ref[...], xn, prec) + bfc_ref[...])

    z = _dot(xn_ref[...], w_ref[...], prec)
    lo = 0
    for (ncols, kind, grow), o_ref in zip(segs, out_refs):
        nt = ncols // tn

        @pl.when((j >= lo) & (j < lo + nt))
        def _store(o_ref=o_ref, kind=kind, grow=grow):
            if kind == "plain":
                o_ref[...] = z
            elif kind == "gelu":
                o_ref[...] = jax.nn.gelu(z)
            else:
                gain = hg_ref[grow:grow + 1, :]
                for c in range(tn // HEAD_DIM):
                    sl = slice(c * HEAD_DIM, (c + 1) * HEAD_DIM)
                    o_ref[:, sl] = _head_norm(z[:, sl], gain)

        lo += nt


def _norm_matmul(x, gain, w, layer, segs, head_gains, forget=None, prec=None):
    t = x.shape[0]
    tm = min(t, 1024)
    tn = 512
    n_tiles = sum(s[0] for s in segs) // tn
    with_forget = forget is not None
    in_specs = [
        pl.BlockSpec((tm, D_MODEL), lambda i, j: (i, 0)),
        pl.BlockSpec((1, D_MODEL), lambda i, j: (0, 0)),
        pl.BlockSpec((None, D_MODEL, tn), lambda i, j: (layer, 0, j)) if w.ndim == 3
        else pl.BlockSpec((D_MODEL, tn), lambda i, j: (0, j)),
        pl.BlockSpec(head_gains.shape, lambda i, j: (0, 0)),
    ]
    args = [x, gain, w, head_gains]
    if with_forget:
        wf, wft, bfr, bfc = forget
        in_specs += [pl.BlockSpec(wf.shape, lambda i, j: (0, 0)), pl.BlockSpec(wft.shape, lambda i, j: (0, 0)),
                     pl.BlockSpec(bfr.shape, lambda i, j: (0, 0)), pl.BlockSpec(bfc.shape, lambda i, j: (0, 0))]
        args += [wf, wft, bfr, bfc]
    out_shape, out_specs = [], []
    lo = 0
    for ncols, _, _ in segs:
        nt = ncols // tn
        out_shape.append(jax.ShapeDtypeStruct((t, ncols), F32))
        out_specs.append(pl.BlockSpec((tm, tn), lambda i, j, lo=lo, nt=nt: (i, jnp.clip(j - lo, 0, nt - 1))))
        lo += nt
    if with_forget:
        out_shape += [jax.ShapeDtypeStruct((t, LANES), F32), jax.ShapeDtypeStruct((2 * SUBLANES, t), F32)]
        out_specs += [pl.BlockSpec((tm, LANES), lambda i, j: (i, 0)),
                      pl.BlockSpec((2 * SUBLANES, tm), lambda i, j: (0, i))]
    return pl.pallas_call(
        functools.partial(_norm_matmul_kernel, segs=tuple(segs), tn=tn, with_forget=with_forget, prec=prec),
        grid=(t // tm, n_tiles),
        in_specs=in_specs,
        out_specs=out_specs,
        out_shape=out_shape,
        scratch_shapes=[pltpu.VMEM((tm, D_MODEL), F32)],
        compiler_params=_params(2),
    )(*args)


def _gmlp_kernel(u_ref, v_ref, vg_ref, w_ref, b_ref, y_ref, vn_ref, *, rows, grp, valid, prec):
    v = v_ref[...]
    vn = v * lax.rsqrt(jnp.mean(v * v, axis=-1, keepdims=True) + EPS) * vg_ref[...]
    vn_ref[...] = vn
    r = lax.broadcasted_iota(jnp.int32, (rows, rows), 0)
    c = lax.broadcasted_iota(jnp.int32, (rows, rows), 1)
    shift = grp.bit_length() - 1
    allowed = ((c & (grp - 1)) <= (r & (grp - 1))) & ((c & (grp - 1)) < valid)
    if grp < rows:
        allowed = allowed & ((r >> shift) == (c >> shift))
    for g in range(MAIN_HEADS):
        sl = slice(g * HEAD_DIM, (g + 1) * HEAD_DIM)
        w = jnp.where(allowed, w_ref[g], 0.0)
        mixed = _dot(w, vn[:, sl], prec) + b_ref[:, g:g + 1]
        y_ref[:, sl] = u_ref[:, sl] * mixed


def _gmlp(u, v, v_gain, w_mix, b_mix, rows, grp, valid, prec):
    t = u.shape[0]
    return pl.pallas_call(
        functools.partial(_gmlp_kernel, rows=rows, grp=grp, valid=valid, prec=prec),
        grid=(t // rows,),
        in_specs=[
            pl.BlockSpec((rows, MAIN_WIDTH), lambda i: (i, 0)),
            pl.BlockSpec((rows, MAIN_WIDTH), lambda i: (i, 0)),
            pl.BlockSpec((1, MAIN_WIDTH), lambda i: (0, 0)),
            pl.BlockSpec((MAIN_HEADS, rows, rows), lambda i: (0, 0, 0)),
            pl.BlockSpec((rows, MAIN_HEADS), lambda i: (0, 0)),
        ],
        out_specs=[pl.BlockSpec((rows, MAIN_WIDTH), lambda i: (i, 0)),
                   pl.BlockSpec((rows, MAIN_WIDTH), lambda i: (i, 0))],
        out_shape=[jax.ShapeDtypeStruct((t, MAIN_WIDTH), F32), jax.ShapeDtypeStruct((t, MAIN_WIDTH), F32)],
        compiler_params=_params(1),
    )(u, v, v_gain, w_mix, b_mix)


def _mem_attn_kernel(q_ref, k_ref, v_ref, o_ref, *, prec):
    for h in range(MEM_HEADS):
        sl = slice(h * HEAD_DIM, (h + 1) * HEAD_DIM)
        s = _dot_nt(q_ref[:, sl], k_ref[:, sl], prec) * ATTN_SCALE
        e = jnp.exp(s - jnp.max(s, axis=-1, keepdims=True))
        p = e / jnp.sum(e, axis=-1, keepdims=True)
        o_ref[:, sl] = _dot(p, v_ref[:, sl], prec)


def _mem_attn(q, k, v, layer, nb, rows_per_batch, prec):
    tq = min(rows_per_batch, 1024)
    nq = rows_per_batch // tq
    kv_spec = pl.BlockSpec((None, None, N_MEM, MEM_WIDTH), lambda b, i: (layer, b, 0, 0))
    return pl.pallas_call(
        functools.partial(_mem_attn_kernel, prec=prec),
        grid=(nb, nq),
        in_specs=[pl.BlockSpec((tq, MEM_WIDTH), lambda b, i: (b * nq + i, 0)), kv_spec, kv_spec],
        out_specs=pl.BlockSpec((tq, MEM_WIDTH), lambda b, i: (b * nq + i, 0)),
        out_shape=jax.ShapeDtypeStruct(q.shape, F32),
        compiler_params=_params(2),
    )(q, k, v)


def _out_proj_kernel(ym_ref, ymem_ref, w1_ref, w2_ref, h_ref, o_ref, *, prec):
    o_ref[...] = h_ref[...] + _dot(ym_ref[...], w1_ref[...], prec) + _dot(ymem_ref[...], w2_ref[...], prec)


def _out_proj(y_main, y_mem, w_out, layer, h, prec):
    t = h.shape[0]
    tm = min(t, 1024)
    tn = 512
    return pl.pallas_call(
        functools.partial(_out_proj_kernel, prec=prec),
        grid=(t // tm, D_MODEL // tn),
        in_specs=[
            pl.BlockSpec((tm, MAIN_WIDTH), lambda i, j: (i, 0)),
            pl.BlockSpec((tm, MEM_WIDTH), lambda i, j: (i, 0)),
            pl.BlockSpec((None, MAIN_WIDTH, tn), lambda i, j: (layer, 0, j)),
            pl.BlockSpec((None, MEM_WIDTH, tn), lambda i, j: (layer, MAIN_WIDTH // MEM_WIDTH, j)),
            pl.BlockSpec((tm, tn), lambda i, j: (i, j)),
        ],
        out_specs=pl.BlockSpec((tm, tn), lambda i, j: (i, j)),
        out_shape=jax.ShapeDtypeStruct((t, D_MODEL), F32),
        compiler_params=_params(2),
    )(y_main, y_mem, w_out, w_out, h)


def _router_kernel(h_ref, g_ref, wr_ref, br_ref, x2_ref, idx_ref, gate_ref, *, prec):
    x = h_ref[...]
    xn = x * lax.rsqrt(jnp.mean(x * x, axis=-1, keepdims=True) + EPS) * g_ref[...]
    x2_ref[...] = xn
    lg = _dot(xn, wr_ref[...], prec) + br_ref[...]
    lane = lax.broadcasted_iota(jnp.int32, lg.shape, 1)
    neg = -jnp.inf
    is_grp = lane < N_GROUPS
    gl = jnp.where(is_grp, lg, neg)
    ge = jnp.where(is_grp, jnp.exp(gl - jnp.max(gl, axis=-1, keepdims=True)), 0.0)
    pg = ge / jnp.sum(ge, axis=-1, keepdims=True)
    p_top = jnp.max(pg, axis=-1, keepdims=True)
    g_idx = jnp.min(jnp.where(is_grp & (pg == p_top), lane, LANES), axis=-1, keepdims=True)
    first = N_GROUPS + EXPERTS_PER_GROUP * g_idx
    in_grp = (lane >= first) & (lane < first + EXPERTS_PER_GROUP)
    e1 = jnp.max(jnp.where(in_grp, lg, neg), axis=-1, keepdims=True)
    i1 = jnp.min(jnp.where(in_grp & (lg == e1), lane, LANES), axis=-1, keepdims=True)
    rest = in_grp & (lane != i1)
    e2 = jnp.max(jnp.where(rest, lg, neg), axis=-1, keepdims=True)
    i2 = jnp.min(jnp.where(rest & (lg == e2), lane, LANES), axis=-1, keepdims=True)
    t2 = jnp.exp(e2 - e1)
    den = 1.0 + t2
    idx_ref[...] = jnp.where(lane == 0, i1 - N_GROUPS, jnp.where(lane == 1, i2 - N_GROUPS, 0))
    gate_ref[...] = jnp.where(lane == 0, p_top * (1.0 / den), jnp.where(lane == 1, p_top * (t2 / den), 0.0))


def _router(h, gain, wr, br, prec):
    t = h.shape[0]
    tm = min(t, 512)
    return pl.pallas_call(
        functools.partial(_router_kernel, prec=prec),
        grid=(t // tm,),
        in_specs=[pl.BlockSpec((tm, D_MODEL), lambda i: (i, 0)), pl.BlockSpec((1, D_MODEL), lambda i: (0, 0)),
                  pl.BlockSpec((D_MODEL, LANES), lambda i: (0, 0)), pl.BlockSpec((1, LANES), lambda i: (0, 0))],
        out_specs=[pl.BlockSpec((tm, D_MODEL), lambda i: (i, 0)), pl.BlockSpec((tm, LANES), lambda i: (i, 0)),
                   pl.BlockSpec((tm, LANES), lambda i: (i, 0))],
        out_shape=[jax.ShapeDtypeStruct((t, D_MODEL), F32), jax.ShapeDtypeStruct((t, LANES), jnp.int32),
                   jax.ShapeDtypeStruct((t, LANES), F32)],
        compiler_params=_params(1),
    )(h, gain, wr, br)


def _plan_kernel(e_ref, pos_ref, te_ref, nv_ref, ft_ref, rank_ref, *, t2, blk, tm):
    shift = tm.bit_length() - 1
    nblk = t2 // blk
    r = lax.broadcasted_iota(jnp.int32, (blk, blk), 0)
    c = lax.broadcasted_iota(jnp.int32, (blk, blk), 1)
    upper = (r <= c).astype(F32)
    sub = lax.broadcasted_iota(jnp.int32, (N_EXPERTS, blk), 0)
    carry = jnp.zeros((N_EXPERTS, 1), F32)
    for b in range(nblk):
        sl = slice(b * blk, (b + 1) * blk)
        oh = (sub == e_ref[:, sl]).astype(F32)
        cs = _dot(oh, upper) + carry
        rank_ref[:, sl] = jnp.sum(oh * (cs - 1.0), axis=0, keepdims=True)
        carry = cs[:, blk - 1:blk]
    counts = carry.astype(jnp.int32)
    padc = ((counts + (tm - 1)) >> shift) << shift
    sub1 = lax.broadcasted_iota(jnp.int32, (N_EXPERTS, 1), 0)
    off = jnp.zeros((N_EXPERTS, 1), jnp.int32)
    run = jnp.zeros((1, 1), jnp.int32)
    for e in range(N_EXPERTS):
        off = jnp.where(sub1 == e, run, off)
        run = run + padc[e:e + 1, :]
    ends = off + padc
    for b in range(nblk):
        sl = slice(b * blk, (b + 1) * blk)
        offv = jnp.sum(jnp.where(sub == e_ref[:, sl], off, 0), axis=0, keepdims=True)
        pos_ref[:, sl] = rank_ref[:, sl].astype(jnp.int32) + offv
    lane = lax.broadcasted_iota(jnp.int32, (N_EXPERTS, LANES), 1)
    lane1 = lax.broadcasted_iota(jnp.int32, (1, LANES), 1)
    nv = run >> shift
    te = jnp.sum((ends <= lane * tm).astype(jnp.int32), axis=0, keepdims=True)
    te_last = jnp.sum((ends <= (nv - 1) * tm).astype(jnp.int32), axis=0, keepdims=True)
    te_ref[...] = jnp.minimum(jnp.where(lane1 >= nv, te_last, te), N_EXPERTS - 1)
    nv_ref[...] = jnp.broadcast_to(nv, (1, LANES))
    ft_ref[...] = jnp.broadcast_to(jnp.where(padc > 0, (ends >> shift) - 1, -1), (N_EXPERTS, LANES))


def _plan(e_flat, tm):
    t2 = e_flat.shape[1]
    blk = min(t2, 512)
    return pl.pallas_call(
        functools.partial(_plan_kernel, t2=t2, blk=blk, tm=tm),
        out_shape=[jax.ShapeDtypeStruct((1, t2), jnp.int32), jax.ShapeDtypeStruct((1, LANES), jnp.int32),
                   jax.ShapeDtypeStruct((1, LANES), jnp.int32), jax.ShapeDtypeStruct((N_EXPERTS, LANES), jnp.int32)],
        scratch_shapes=[pltpu.VMEM((1, t2), F32)],
        compiler_params=pltpu.CompilerParams(vmem_limit_bytes=VMEM_LIMIT_BYTES),
    )(e_flat)


DISPATCH_CHUNK = 256


def _dispatch_kernel(pos_ref, ft_ref, nv_ref, x_ref, xs_ref, fill_sem, sems, *, t, t2, tm, chunk, n_tiles):
    def fill_tile(tile):
        cp = pltpu.make_async_copy(x_ref.at[pl.ds(0, tm)], xs_ref.at[pl.ds(tile * tm, tm)], fill_sem)
        cp.start()
        cp.wait()

    def fill(e, carry):
        ft = ft_ref[e]

        @pl.when(ft >= 0)
        def _():
            fill_tile(ft)

        return carry

    lax.fori_loop(0, N_EXPERTS, fill, 0)

    def fill_tail(tile, carry):
        fill_tile(tile)
        return carry

    lax.fori_loop(nv_ref[0], n_tiles, fill_tail, 0)
    n_chunks = t2 // chunk

    def row_copy(i, slot):
        tok = jnp.where(i >= t, i - t, i)
        return pltpu.make_async_copy(x_ref.at[pl.ds(tok, 1)], xs_ref.at[pl.ds(pos_ref[i], 1)], sems.at[slot])

    def start_chunk(cidx):
        def body(r, carry):
            row_copy(cidx * chunk + r, cidx % 2).start()
            return carry
        lax.fori_loop(0, chunk, body, 0)

    def wait_chunk(cidx):
        def body(r, carry):
            row_copy(cidx * chunk + r, cidx % 2).wait()
            return carry
        lax.fori_loop(0, chunk, body, 0)

    start_chunk(0)

    def step(cidx, carry):
        @pl.when(cidx + 1 < n_chunks)
        def _():
            start_chunk(cidx + 1)
        wait_chunk(cidx)
        return carry

    lax.fori_loop(0, n_chunks, step, 0)


def _dispatch(pos, ft, nv, x2, n_rows, tm):
    t = x2.shape[0]
    t2 = pos.shape[0]
    chunk = min(DISPATCH_CHUNK, t2)
    return pl.pallas_call(
        functools.partial(_dispatch_kernel, t=t, t2=t2, tm=tm, chunk=chunk, n_tiles=n_rows // tm),
        grid_spec=pltpu.PrefetchScalarGridSpec(
            num_scalar_prefetch=3,
            grid=(1,),
            in_specs=[pl.BlockSpec(memory_space=pl.ANY)],
            out_specs=pl.BlockSpec(memory_space=pl.ANY),
            scratch_shapes=[pltpu.SemaphoreType.DMA(()), pltpu.SemaphoreType.DMA((2,))],
        ),
        out_shape=jax.ShapeDtypeStruct((n_rows, D_MODEL), F32),
        compiler_params=_params(1),
    )(pos, ft, nv, x2)


def _moe_ffn_kernel(te_ref, nv_ref, xs_ref, wg_ref, wu_ref, wd_ref, y_ref, *, prec):
    i = pl.program_id(0)

    @pl.when(i < nv_ref[0])
    def _():
        x = xs_ref[...]
        hdn = jax.nn.silu(_dot(x, wg_ref[...], prec)) * _dot(x, wu_ref[...], prec)
        y_ref[...] = _dot(hdn, wd_ref[...], prec)

    @pl.when(i >= nv_ref[0])
    def _():
        y_ref[...] = jnp.zeros_like(y_ref)


def _moe_ffn(te, nv, xs, w_gate, w_up, w_down, layer, tm, prec):
    n_rows = xs.shape[0]
    n_tiles = n_rows // tm
    return pl.pallas_call(
        functools.partial(_moe_ffn_kernel, prec=prec),
        grid_spec=pltpu.PrefetchScalarGridSpec(
            num_scalar_prefetch=2,
            grid=(n_tiles,),
            in_specs=[
                pl.BlockSpec((tm, D_MODEL), lambda i, te, nv: (jnp.minimum(i, nv[0] - 1), 0)),
                pl.BlockSpec((None, None, D_MODEL, D_EXPERT), lambda i, te, nv: (layer, te[i], 0, 0)),
                pl.BlockSpec((None, None, D_MODEL, D_EXPERT), lambda i, te, nv: (layer, te[i], 0, 0)),
                pl.BlockSpec((None, None, D_EXPERT, D_MODEL), lambda i, te, nv: (layer, te[i], 0, 0)),
            ],
            out_specs=pl.BlockSpec((tm, D_MODEL), lambda i, te, nv: (i, 0)),
        ),
        out_shape=jax.ShapeDtypeStruct((n_rows, D_MODEL), F32),
        compiler_params=_params(1),
    )(te, nv, xs, w_gate, w_up, w_down)


def _combine_kernel(pos_ref, h_ref, g_ref, y_ref, o_ref, y0_ref, y1_ref, sems, *, t, tm):
    base = pl.program_id(0) * tm

    def copies(r):
        return (pltpu.make_async_copy(y_ref.at[pl.ds(pos_ref[base + r], 1)], y0_ref.at[pl.ds(r, 1)], sems.at[0]),
                pltpu.make_async_copy(y_ref.at[pl.ds(pos_ref[t + base + r], 1)], y1_ref.at[pl.ds(r, 1)], sems.at[1]))

    def start(r, carry):
        for cp in copies(r):
            cp.start()
        return carry

    def wait(r, carry):
        for cp in copies(r):
            cp.wait()
        return carry

    lax.fori_loop(0, tm, start, 0)
    lax.fori_loop(0, tm, wait, 0)
    g = g_ref[...]
    o_ref[...] = h_ref[...] + g[:, 0:1] * y0_ref[...] + g[:, 1:2] * y1_ref[...]


def _combine(pos, h, gates, y):
    t = h.shape[0]
    tm = min(t, 256)
    return pl.pallas_call(
        functools.partial(_combine_kernel, t=t, tm=tm),
        grid_spec=pltpu.PrefetchScalarGridSpec(
            num_scalar_prefetch=1,
            grid=(t // tm,),
            in_specs=[pl.BlockSpec((tm, D_MODEL), lambda i, pos: (i, 0)),
                      pl.BlockSpec((tm, LANES), lambda i, pos: (i, 0)),
                      pl.BlockSpec(memory_space=pl.ANY)],
            out_specs=pl.BlockSpec((tm, D_MODEL), lambda i, pos: (i, 0)),
            scratch_shapes=[pltpu.VMEM((tm, D_MODEL), F32), pltpu.VMEM((tm, D_MODEL), F32),
                            pltpu.SemaphoreType.DMA((2,))],
        ),
        out_shape=jax.ShapeDtypeStruct((t, D_MODEL), F32),
        compiler_params=_params(1),
    )(pos, h, gates, y)


def _hier_moe(h, gain, wr, br, w_gate, w_up, w_down, layer, prec):
    t = h.shape[0]
    tm = 256 if t >= 1024 else 16
    n_tiles = (2 * t) // tm + N_EXPERTS
    assert n_tiles <= LANES
    x2, idx, gates = _router(h, gain, wr, br, prec)
    e_flat = idx[:, :2].T.reshape(1, 2 * t)
    pos, te, nv, ft = _plan(e_flat, tm)
    pos = pos.reshape(2 * t)
    te = te[0, :n_tiles]
    nv = nv[0, :1]
    xs = _dispatch(pos, ft[:, 0], nv, x2, n_tiles * tm, tm)
    y = _moe_ffn(te, nv, xs, w_gate, w_up, w_down, layer, tm, prec)
    return _combine(pos, h, gates, y)


def _fox_cumsum_kernel(lft_ref, lf_ref, ct_ref, c_ref, *, s, blk):
    r = lax.broadcasted_iota(jnp.int32, (blk, blk), 0)
    c = lax.broadcasted_iota(jnp.int32, (blk, blk), 1)
    upper = (r <= c).astype(F32)
    lower = (r >= c).astype(F32)
    carry_t = jnp.zeros((2 * SUBLANES, 1), F32)
    carry = jnp.zeros((1, LANES), F32)
    for b in range(s // blk):
        sl = slice(b * blk, (b + 1) * blk)
        ct = _dot3_left(lft_ref[:, sl], upper) + carry_t
        ct_ref[:, sl] = ct
        carry_t = ct[:, blk - 1:blk]
        cc = _dot3_right(lower, lf_ref[sl, :]) + carry
        c_ref[sl, :] = cc
        carry = cc[blk - 1:blk, :]


def _fox_cumsum(lft, lf, nb, s):
    t = lf.shape[0]
    blk = min(s, 256)
    return pl.pallas_call(
        functools.partial(_fox_cumsum_kernel, s=s, blk=blk),
        grid=(nb,),
        in_specs=[pl.BlockSpec((2 * SUBLANES, s), lambda b: (0, b)), pl.BlockSpec((s, LANES), lambda b: (b, 0))],
        out_specs=[pl.BlockSpec((2 * SUBLANES, s), lambda b: (0, b)), pl.BlockSpec((s, LANES), lambda b: (b, 0))],
        out_shape=[jax.ShapeDtypeStruct((2 * SUBLANES, t), F32), jax.ShapeDtypeStruct((t, LANES), F32)],
        compiler_params=_params(1),
    )(lft, lf)


def _fox_prompt_kernel(q_ref, k_ref, v_ref, ct_ref, c_ref, o_ref, *, s, tq):
    h = pl.program_id(1)
    lane = lax.broadcasted_iota(jnp.int32, (s, LANES), 1)
    c_col = jnp.sum(jnp.where(lane == h, c_ref[...], 0.0), axis=1, keepdims=True)
    c_row = ct_ref[pl.ds(h, 1), :]
    ri = lax.broadcasted_iota(jnp.int32, (tq, tq), 0)
    ci = lax.broadcasted_iota(jnp.int32, (tq, tq), 1)
    causal = ci <= ri
    for qi in range(s // tq):
        qs = slice(qi * tq, (qi + 1) * tq)
        q = q_ref[qs, :]
        m = jnp.full((tq, 1), -jnp.inf, F32)
        l = jnp.zeros((tq, 1), F32)
        acc = jnp.zeros((tq, HEAD_DIM), F32)
        for kj in range(qi + 1):
            ks = slice(kj * tq, (kj + 1) * tq)
            sc = lax.dot_general(q, k_ref[ks, :], _NT, preferred_element_type=F32) * ATTN_SCALE
            sc = sc + c_col[qs, :] - c_row[:, ks]
            if kj == qi:
                sc = jnp.where(causal, sc, -jnp.inf)
            m_new = jnp.maximum(m, jnp.max(sc, axis=-1, keepdims=True))
            alpha = jnp.exp(m - m_new)
            p = jnp.exp(sc - m_new)
            l = alpha * l + jnp.sum(p, axis=-1, keepdims=True)
            acc = alpha * acc + _dot(p, v_ref[ks, :])
            m = m_new
        o_ref[qs, :] = acc / l


def _fox_prompt(q, k, v, ct, c, nb, s):
    t = q.shape[0]
    tq = min(s, 512)
    head_spec = pl.BlockSpec((s, HEAD_DIM), lambda b, h: (b, h))
    return pl.pallas_call(
        functools.partial(_fox_prompt_kernel, s=s, tq=tq),
        grid=(nb, MAIN_HEADS),
        in_specs=[head_spec, head_spec, head_spec,
                  pl.BlockSpec((2 * SUBLANES, s), lambda b, h: (0, b)),
                  pl.BlockSpec((s, LANES), lambda b, h: (b, 0))],
        out_specs=head_spec,
        out_shape=jax.ShapeDtypeStruct((t, MAIN_WIDTH), F32),
        compiler_params=_params(2),
    )(q, k, v, ct, c)


def _fox_past_bias_kernel(pt_ref, lfc_ref, o_ref, buf_ref, msel_ref, msum_ref, sem, *, n_pages):
    b = pl.program_id(0)
    width = PAGE_SIZE * MAIN_HEADS

    def page_copy(p):
        return pltpu.make_async_copy(lfc_ref.at[pl.ds(pt_ref[b * n_pages + p], 1)], buf_ref.at[pl.ds(p, 1)], sem)

    def start(p, carry):
        page_copy(p).start()
        return carry

    def wait(p, carry):
        page_copy(p).wait()
        return carry

    lax.fori_loop(0, n_pages, start, 0)

    @pl.when(b == 0)
    def _():
        r = lax.broadcasted_iota(jnp.int32, (width, width), 0)
        c = lax.broadcasted_iota(jnp.int32, (width, width), 1)
        rpos = ((r.astype(F32) + 0.5) * (1.0 / MAIN_HEADS)).astype(jnp.int32)
        rhead = r - MAIN_HEADS * rpos
        msel_ref[...] = ((rhead == (c >> 7)) & (rpos > (c & (PAGE_SIZE - 1)))).astype(F32)
        r2 = lax.broadcasted_iota(jnp.int32, (width, LANES), 0)
        c2 = lax.broadcasted_iota(jnp.int32, (width, LANES), 1)
        rpos2 = ((r2.astype(F32) + 0.5) * (1.0 / MAIN_HEADS)).astype(jnp.int32)
        msum_ref[...] = ((r2 - MAIN_HEADS * rpos2) == c2).astype(F32)

    lax.fori_loop(0, n_pages, wait, 0)
    lp = buf_ref[...]
    hi, mid, lo = _split3(lp)
    msel = msel_ref[...]
    within = _dot(hi, msel) + _dot(mid, msel) + _dot(lo, msel)
    msum = msum_ref[...]
    tot = _dot(hi, msum) + _dot(mid, msum) + _dot(lo, msum)
    pr = lax.broadcasted_iota(jnp.int32, (n_pages, n_pages), 0)
    pc = lax.broadcasted_iota(jnp.int32, (n_pages, n_pages), 1)
    later = _dot3_right((pc > pr).astype(F32), tot)
    for h in range(MAIN_HEADS):
        sl = slice(h * HEAD_DIM, (h + 1) * HEAD_DIM)
        o_ref[:, sl] = within[:, sl] + later[:, h:h + 1]


def _fox_past_bias(page_table_flat, cache_logf2, nb, n_pages):
    width = PAGE_SIZE * MAIN_HEADS
    return pl.pallas_call(
        functools.partial(_fox_past_bias_kernel, n_pages=n_pages),
        grid_spec=pltpu.PrefetchScalarGridSpec(
            num_scalar_prefetch=1,
            grid=(nb,),
            in_specs=[pl.BlockSpec(memory_space=pl.ANY)],
            out_specs=pl.BlockSpec((None, n_pages, width), lambda b, pt: (b, 0, 0)),
            scratch_shapes=[pltpu.VMEM((n_pages, width), F32), pltpu.VMEM((width, width), F32),
                            pltpu.VMEM((width, LANES), F32), pltpu.SemaphoreType.DMA(())],
        ),
        out_shape=jax.ShapeDtypeStruct((nb, n_pages, width), F32),
        compiler_params=_params(1),
    )(page_table_flat, cache_logf2)


def _fox_sample_kernel(*refs, ppb, n_steps, valid):
    pt_ref, q_ref, kn_ref, vn_ref, lf_ref, bias_ref = refs[:6]
    k_refs = refs[6:6 + ppb]
    v_refs = refs[6 + ppb:6 + 2 * ppb]
    o_ref, m_ref, l_ref, acc_ref = refs[6 + 2 * ppb:]
    g = pl.program_id(1)
    rows = SAMPLE_ROWS

    @pl.when(g == 0)
    def _():
        m_ref[...] = jnp.full(m_ref.shape, -jnp.inf, F32)
        l_ref[...] = jnp.zeros(l_ref.shape, F32)
        acc_ref[...] = jnp.zeros(acc_ref.shape, F32)

    r8 = lax.broadcasted_iota(jnp.int32, (rows, rows), 0)
    c8 = lax.broadcasted_iota(jnp.int32, (rows, rows), 1)
    new_mask = (c8 <= r8) & (c8 < valid)
    c_new = _dot3_right(new_mask.astype(F32), lf_ref[...])

    def qk3(q_hi, q_lo, k):
        k_hi, k_lo = _split_hi_lo(k)
        a = _dot_nt(jnp.concatenate([q_hi, q_lo], axis=0), k_hi)
        return a[:rows] + a[rows:] + _dot_nt(q_hi, k_lo)

    def pv3(p, v):
        p_hi, p_lo = _split_hi_lo(p)
        v_hi, v_lo = _split_hi_lo(v)
        a = _dot(jnp.concatenate([p_hi, p_lo], axis=0), v_hi)
        return a[:rows] + a[rows:] + _dot(p_hi, v_lo)

    for h in range(MAIN_HEADS):
        sl = slice(h * HEAD_DIM, (h + 1) * HEAD_DIM)
        q_hi, q_lo = _split_hi_lo(q_ref[:, sl])
        cq = c_new[:, h:h + 1]
        scores = []
        for u in range(ppb):
            kh = k_refs[u][pl.ds(h, PAGE_SIZE, stride=MAIN_HEADS), :]
            scores.append(qk3(q_hi, q_lo, kh) * ATTN_SCALE + cq + bias_ref[u:u + 1, sl])
        m_old = m_ref[:, h:h + 1]
        m_new = m_old
        for sc in scores:
            m_new = jnp.maximum(m_new, jnp.max(sc, axis=-1, keepdims=True))
        alpha = jnp.exp(m_old - m_new)
        l_new = alpha * l_ref[:, h:h + 1]
        acc = alpha * acc_ref[:, sl]
        for u, sc in enumerate(scores):
            p = jnp.exp(sc - m_new)
            l_new = l_new + jnp.sum(p, axis=-1, keepdims=True)
            acc = acc + pv3(p, v_refs[u][pl.ds(h, PAGE_SIZE, stride=MAIN_HEADS), :])
        m_ref[:, h:h + 1] = m_new
        l_ref[:, h:h + 1] = l_new
        acc_ref[:, sl] = acc

    @pl.when(g == n_steps - 1)
    def _():
        for h in range(MAIN_HEADS):
            sl = slice(h * HEAD_DIM, (h + 1) * HEAD_DIM)
            qh = q_ref[:, sl]
            cq = c_new[:, h:h + 1]
            cq_row = jnp.sum(jnp.where(r8 == c8, jnp.broadcast_to(cq, (rows, rows)), 0.0), axis=0, keepdims=True)
            sn = _dot_nt(qh, kn_ref[:, sl], HIGHEST) * ATTN_SCALE + cq - cq_row
            sn = jnp.where(new_mask, sn, -jnp.inf)
            m_old = m_ref[:, h:h + 1]
            m_new = jnp.maximum(m_old, jnp.max(sn, axis=-1, keepdims=True))
            alpha = jnp.exp(m_old - m_new)
            p = jnp.exp(sn - m_new)
            l_new = alpha * l_ref[:, h:h + 1] + jnp.sum(p, axis=-1, keepdims=True)
            acc = alpha * acc_ref[:, sl] + _dot(p, vn_ref[:, sl], HIGHEST)
            o_ref[:, sl] = acc / l_new


def _fox_sample(page_table_flat, q, k_new, v_new, lf, bias, cache_k2, cache_v2, nb, n_pages, valid):
    ppb = 8 if n_pages % 8 == 0 else n_pages
    n_steps = n_pages // ppb
    width = PAGE_SIZE * MAIN_HEADS
    bias4 = bias.reshape(nb, n_steps, ppb, width)
    row_spec = pl.BlockSpec((SAMPLE_ROWS, MAIN_WIDTH), lambda b, g, pt: (b, 0))

    def page_spec(u):
        return pl.BlockSpec((None, width, HEAD_DIM), lambda b, g, pt, u=u: (pt[b * n_pages + g * ppb + u], 0, 0))

    in_specs = [row_spec, row_spec, row_spec,
                pl.BlockSpec((SAMPLE_ROWS, LANES), lambda b, g, pt: (b, 0)),
                pl.BlockSpec((None, None, ppb, width), lambda b, g, pt: (b, g, 0, 0))]
    in_specs += [page_spec(u) for u in range(ppb)] + [page_spec(u) for u in range(ppb)]
    return pl.pallas_call(
        functools.partial(_fox_sample_kernel, ppb=ppb, n_steps=n_steps, valid=valid),
        grid_spec=pltpu.PrefetchScalarGridSpec(
            num_scalar_prefetch=1,
            grid=(nb, n_steps),
            in_specs=in_specs,
            out_specs=row_spec,
            scratch_shapes=[pltpu.VMEM((SAMPLE_ROWS, LANES), F32), pltpu.VMEM((SAMPLE_ROWS, LANES), F32),
                            pltpu.VMEM((SAMPLE_ROWS, MAIN_WIDTH), F32)],
        ),
        out_shape=jax.ShapeDtypeStruct(q.shape, F32),
        compiler_params=_params(2),
    )(page_table_flat, q, k_new, v_new, lf, bias4, *([cache_k2] * ppb), *([cache_v2] * ppb))


def _trunk(x, nb, rows, valid, mem_k, mem_v, fox, w, prec):
    u, v, qm = _norm_matmul(x, w["norm1_gain"][0:1], w["w_in_a"], 0,
                            [(MAIN_WIDTH, "gelu", 0), (MAIN_WIDTH, "gelu", 0), (MEM_WIDTH, "headnorm", 0)],
                            w["mem_q_gain"][0:1], prec=prec)
    y_main, v_rows = _gmlp(u, v, w["v_gain_a"][0:1], w["w_mix"], w["b_mix"], w["gmlp_rows"], w["gmlp_grp"], valid,
                           prec)
    y_mem = _mem_attn(qm, mem_k, mem_v, 0, nb, rows, prec)
    h = _out_proj(y_main, y_mem, w["w_out"], 0, x, prec)
    h = _hier_moe(h, w["norm2_gain"][0:1], w["w_router"][0], w["b_router"][0], w["w_gate"], w["w_up"], w["w_down"],
                  0, prec)
    k, vv, lf, lft = _norm_matmul(h, w["kv_norm_gain"], w["w_kv_shared"], 0,
                                  [(MAIN_WIDTH, "headnorm", 0), (MAIN_WIDTH, "plain", 0)],
                                  w["k_gain_shared"], forget=w["forget"], prec=prec)
    q, qm = _norm_matmul(h, w["norm1_gain"][1:2], w["w_in_b"], 0,
                         [(MAIN_WIDTH, "headnorm", 0), (MEM_WIDTH, "headnorm", 1)],
                         jnp.concatenate([w["q_gain_b"][0:1], w["mem_q_gain"][1:2]], axis=0), prec=prec)
    y_main = fox(q, k, vv, lf, lft)
    y_mem = _mem_attn(qm, mem_k, mem_v, 1, nb, rows, prec)
    h = _out_proj(y_main, y_mem, w["w_out"], 1, h, prec)
    h = _hier_moe(h, w["norm2_gain"][1:2], w["w_router"][1], w["b_router"][1], w["w_gate"], w["w_up"], w["w_down"],
                  1, prec)
    return h, k, vv, lf, v_rows


def kernel(x_prompt, x_sample, cache_k, cache_v, cache_logf, cache_mem_k, cache_mem_v, page_table, mem_prompt, norm1_gain, norm2_gain, w_in_a, v_gain_a, w_s_a, b_s_a, w_in_b, q_gain_b, kv_norm_gain, w_kv_shared, b_forget, k_gain_shared, mem_norm_gain, w_mem_kv, mem_q_gain, mem_k_gain, w_out, w_router_group, b_router_group, w_router_expert, b_router_expert, w_gate, w_up, w_down):
    batch, seq, _ = x_prompt.shape
    dec_batch, dec_seq, _ = x_sample.shape
    n_phys = cache_k.shape[0]
    n_pages = page_table.shape[1]
    depth = norm1_gain.shape[0]
    assert depth == 2 and dec_seq <= SAMPLE_ROWS and seq % CHUNK == 0

    n_route = N_GROUPS + N_EXPERTS
    w_router = jnp.pad(jnp.concatenate([w_router_group, w_router_expert], axis=-1),
                       ((0, 0), (0, 0), (0, LANES - n_route)))
    b_router = jnp.pad(jnp.concatenate([b_router_group, b_router_expert], axis=-1),
                       ((0, 0), (0, LANES - n_route)))[:, None, :]
    w_f = w_kv_shared[:, 2 * MAIN_WIDTH:]
    forget = (jnp.pad(w_f, ((0, 0), (0, LANES - MAIN_HEADS))),
              jnp.pad(w_f.T, ((0, 2 * SUBLANES - MAIN_HEADS), (0, 0))),
              jnp.pad(b_forget, (0, LANES - MAIN_HEADS))[None, :],
              jnp.pad(b_forget, (0, 2 * SUBLANES - MAIN_HEADS))[:, None])
    common = dict(norm1_gain=norm1_gain, norm2_gain=norm2_gain, w_in_a=w_in_a, v_gain_a=v_gain_a, w_in_b=w_in_b,
                  q_gain_b=q_gain_b, kv_norm_gain=kv_norm_gain[None, :], w_kv_shared=w_kv_shared,
                  k_gain_shared=k_gain_shared[None, :], mem_q_gain=mem_q_gain, w_out=w_out,
                  w_router=w_router, b_router=b_router, w_gate=w_gate, w_up=w_up, w_down=w_down, forget=forget)

    t_p = batch * seq
    mem_tok = mem_prompt.reshape(batch * N_MEM, D_MODEL)
    mem_k_layers, mem_v_layers = [], []
    for l in range(depth):
        mk, mv = _norm_matmul(mem_tok, mem_norm_gain[l:l + 1], w_mem_kv, l,
                              [(MEM_WIDTH, "headnorm", 0), (MEM_WIDTH, "plain", 0)], mem_k_gain[l:l + 1])
        mem_k_layers.append(mk.reshape(batch, N_MEM, MEM_WIDTH))
        mem_v_layers.append(mv.reshape(batch, N_MEM, MEM_WIDTH))
    mem_k_p = jnp.stack(mem_k_layers)
    mem_v_p = jnp.stack(mem_v_layers)

    def fox_prompt(q, k, v, lf, lft):
        ct, c = _fox_cumsum(lft, lf, batch, seq)
        return _fox_prompt(q, k, v, ct, c, batch, seq)

    w_p = dict(common, w_mix=w_s_a[0], b_mix=b_s_a[0].T, gmlp_rows=CHUNK, gmlp_grp=CHUNK)
    y_p, k_p, v_p, lf_p, vrows_p = _trunk(x_prompt.reshape(t_p, D_MODEL), batch, seq, seq,
                                          mem_k_p, mem_v_p, fox_prompt, w_p, None)

    rows = SAMPLE_ROWS
    t_s = dec_batch * rows
    x_s = jnp.pad(x_sample, ((0, 0), (0, rows - dec_seq), (0, 0))).reshape(t_s, D_MODEL)
    pt_flat = page_table.reshape(dec_batch * n_pages)
    width = PAGE_SIZE * MAIN_HEADS
    bias = _fox_past_bias(pt_flat, cache_logf.reshape(n_phys, width), dec_batch, n_pages)
    cache_k2 = cache_k.reshape(n_phys, width, HEAD_DIM)
    cache_v2 = cache_v.reshape(n_phys, width, HEAD_DIM)

    def fox_sample(q, k, v, lf, lft):
        return _fox_sample(pt_flat, q, k, v, lf, bias, cache_k2, cache_v2, dec_batch, n_pages, dec_seq)

    w_s = dict(common, w_mix=jnp.tile(w_s_a[0][:, :rows, :rows], (1, dec_batch, dec_batch)),
               b_mix=jnp.tile(b_s_a[0][:, :rows].T, (dec_batch, 1)), gmlp_rows=t_s, gmlp_grp=rows)
    mem_k_s = cache_mem_k.reshape(depth, dec_batch, N_MEM, MEM_WIDTH)
    mem_v_s = cache_mem_v.reshape(depth, dec_batch, N_MEM, MEM_WIDTH)
    y_s, k_s, v_s, lf_s, vrows_s = _trunk(x_s, dec_batch, rows, dec_seq, mem_k_s, mem_v_s, fox_sample, w_s, HIGHEST)

    def unpad(a, *tail):
        return a.reshape((dec_batch, rows) + tail)[:, :dec_seq]

    return (y_p.reshape(batch, seq, D_MODEL),
            unpad(y_s, D_MODEL),
            k_p.reshape(batch, seq, MAIN_HEADS, HEAD_DIM),
            v_p.reshape(batch, seq, MAIN_HEADS, HEAD_DIM),
            lf_p[:, :MAIN_HEADS].reshape(batch, seq, MAIN_HEADS),
            unpad(k_s, MAIN_HEADS, HEAD_DIM),
            unpad(v_s, MAIN_HEADS, HEAD_DIM),
            unpad(lf_s[:, :MAIN_HEADS], MAIN_HEADS),
            mem_k_p.reshape(depth, batch, N_MEM, MEM_HEADS, HEAD_DIM),
            mem_v_p.reshape(depth, batch, N_MEM, MEM_HEADS, HEAD_DIM),
            vrows_p.reshape(batch, seq, MAIN_WIDTH)[:, -CHUNK:][None],
            unpad(vrows_s, MAIN_WIDTH)[None])
```

```python
import functools

import jax
import jax.numpy as jnp
from jax import lax
from jax.experimental import pallas as pl
from jax.experimental.pallas import tpu as pltpu

D_MODEL = 2048
HEAD_DIM = 128
MAIN_HEADS = 12
MEM_HEADS = 4
MAIN_WIDTH = MAIN_HEADS * HEAD_DIM
MEM_WIDTH = MEM_HEADS * HEAD_DIM
N_MEM = 256
N_GROUPS = 4
EXPERTS_PER_GROUP = 4
N_EXPERTS = N_GROUPS * EXPERTS_PER_GROUP
D_EXPERT = 512
CHUNK = 128
PAGE_SIZE = 128
EPS = 1e-6
ATTN_SCALE = HEAD_DIM ** -0.5

LANES = 128
SUBLANES = 8
VMEM_LIMIT_BYTES = 56 * 1024 * 1024
SAMPLE_ROWS = SUBLANES

F32 = jnp.float32
_NT = (((1,), (1,)), ((), ()))


def _params(n_axes):
    return pltpu.CompilerParams(dimension_semantics=("arbitrary",) * n_axes,
                                vmem_limit_bytes=VMEM_LIMIT_BYTES)


HIGHEST = lax.Precision.HIGHEST


def _dot(a, b, prec=None):
    return jnp.dot(a, b, preferred_element_type=F32, precision=prec)


def _dot_nt(a, b, prec=None):
    return lax.dot_general(a, b, _NT, preferred_element_type=F32, precision=prec)


def _split_hi_lo(x):
    hi = x.astype(jnp.bfloat16).astype(F32)
    return hi, x - hi


def _split3(x):
    hi = x.astype(jnp.bfloat16).astype(F32)
    r = x - hi
    mid = r.astype(jnp.bfloat16).astype(F32)
    lo = (r - mid).astype(jnp.bfloat16).astype(F32)
    return hi, mid, lo


def _dot3_right(a_exact, x):
    hi, mid, lo = _split3(x)
    return _dot(a_exact, hi) + _dot(a_exact, mid) + _dot(a_exact, lo)


def _dot3_left(x, b_exact):
    hi, mid, lo = _split3(x)
    return _dot(hi, b_exact) + _dot(mid, b_exact) + _dot(lo, b_exact)


def _head_norm(z, gain_row):
    return z * lax.rsqrt(jnp.mean(z * z, axis=-1, keepdims=True) + EPS) * gain_row


def _norm_matmul_kernel(*refs, segs, tn, with_forget, prec):
    it = iter(refs)
    x_ref, g_ref, w_ref, hg_ref = next(it), next(it), next(it), next(it)
    if with_forget:
        wf_ref, wft_ref, bfr_ref, bfc_ref = next(it), next(it), next(it), next(it)
    out_refs = [next(it) for _ in segs]
    if with_forget:
        lf_ref, lft_ref = next(it), next(it)
    xn_ref = next(it)
    j = pl.program_id(1)

    @pl.when(j == 0)
    def _prologue():
        x = x_ref[...]
        xn = x * lax.rsqrt(jnp.mean(x * x, axis=-1, keepdims=True) + EPS) * g_ref[...]
        xn_ref[...] = xn
        if with_forget:
            lf_ref[...] = jax.nn.log_sigmoid(_dot(xn, wf_ref[...], prec) + bfr_ref[...])
            lft_ref[...] = jax.nn.log_sigmoid(_dot_nt(wft_ref[...], xn, prec) + bfc_ref[...])

    z = _dot(xn_ref[...], w_ref[...], prec)
    lo = 0
    for (ncols, kind, grow), o_ref in zip(segs, out_refs):
        nt = ncols // tn

        @pl.when((j >= lo) & (j < lo + nt))
        def _store(o_ref=o_ref, kind=kind, grow=grow):
            if kind == "plain":
                o_ref[...] = z
            elif kind == "gelu":
                o_ref[...] = jax.nn.gelu(z)
            else:
                gain = hg_ref[grow:grow + 1, :]
                for c in range(tn // HEAD_DIM):
                    sl = slice(c * HEAD_DIM, (c + 1) * HEAD_DIM)
                    o_ref[:, sl] = _head_norm(z[:, sl], gain)

        lo += nt


def _norm_matmul(x, gain, w, layer, segs, head_gains, forget=None, prec=None):
    t = x.shape[0]
    tm = min(t, 1024)
    tn = 512
    n_tiles = sum(s[0] for s in segs) // tn
    with_forget = forget is not None
    in_specs = [
        pl.BlockSpec((tm, D_MODEL), lambda i, j: (i, 0)),
        pl.BlockSpec((1, D_MODEL), lambda i, j: (0, 0)),
        pl.BlockSpec((None, D_MODEL, tn), lambda i, j: (layer, 0, j)) if w.ndim == 3
        else pl.BlockSpec((D_MODEL, tn), lambda i, j: (0, j)),
        pl.BlockSpec(head_gains.shape, lambda i, j: (0, 0)),
    ]
    args = [x, gain, w, head_gains]
    if with_forget:
        wf, wft, bfr, bfc = forget
        in_specs += [pl.BlockSpec(wf.shape, lambda i, j: (0, 0)), pl.BlockSpec(wft.shape, lambda i, j: (0, 0)),
                     pl.BlockSpec(bfr.shape, lambda i, j: (0, 0)), pl.BlockSpec(bfc.shape, lambda i, j: (0, 0))]
        args += [wf, wft, bfr, bfc]
    out_shape, out_specs = [], []
    lo = 0
    for ncols, _, _ in segs:
        nt = ncols // tn
        out_shape.append(jax.ShapeDtypeStruct((t, ncols), F32))
        out_specs.append(pl.BlockSpec((tm, tn), lambda i, j, lo=lo, nt=nt: (i, jnp.clip(j - lo, 0, nt - 1))))
        lo += nt
    if with_forget:
        out_shape += [jax.ShapeDtypeStruct((t, LANES), F32), jax.ShapeDtypeStruct((2 * SUBLANES, t), F32)]
        out_specs += [pl.BlockSpec((tm, LANES), lambda i, j: (i, 0)),
                      pl.BlockSpec((2 * SUBLANES, tm), lambda i, j: (0, i))]
    return pl.pallas_call(
        functools.partial(_norm_matmul_kernel, segs=tuple(segs), tn=tn, with_forget=with_forget, prec=prec),
        grid=(t // tm, n_tiles),
        in_specs=in_specs,
        out_specs=out_specs,
        out_shape=out_shape,
        scratch_shapes=[pltpu.VMEM((tm, D_MODEL), F32)],
        compiler_params=_params(2),
    )(*args)


def _gmlp_kernel(u_ref, v_ref, vg_ref, w_ref, b_ref, y_ref, vn_ref, *, rows, grp, valid, prec):
    v = v_ref[...]
    vn = v * lax.rsqrt(jnp.mean(v * v, axis=-1, keepdims=True) + EPS) * vg_ref[...]
    vn_ref[...] = vn
    r = lax.broadcasted_iota(jnp.int32, (rows, rows), 0)
    c = lax.broadcasted_iota(jnp.int32, (rows, rows), 1)
    shift = grp.bit_length() - 1
    allowed = ((c & (grp - 1)) <= (r & (grp - 1))) & ((c & (grp - 1)) < valid)
    if grp < rows:
        allowed = allowed & ((r >> shift) == (c >> shift))
    for g in range(MAIN_HEADS):
        sl = slice(g * HEAD_DIM, (g + 1) * HEAD_DIM)
        w = jnp.where(allowed, w_ref[g], 0.0)
        mixed = _dot(w, vn[:, sl], prec) + b_ref[:, g:g + 1]
        y_ref[:, sl] = u_ref[:, sl] * mixed


def _gmlp(u, v, v_gain, w_mix, b_mix, rows, grp, valid, prec):
    t = u.shape[0]
    return pl.pallas_call(
        functools.partial(_gmlp_kernel, rows=rows, grp=grp, valid=valid, prec=prec),
        grid=(t // rows,),
        in_specs=[
            pl.BlockSpec((rows, MAIN_WIDTH), lambda i: (i, 0)),
            pl.BlockSpec((rows, MAIN_WIDTH), lambda i: (i, 0)),
            pl.BlockSpec((1, MAIN_WIDTH), lambda i: (0, 0)),
            pl.BlockSpec((MAIN_HEADS, rows, rows), lambda i: (0, 0, 0)),
            pl.BlockSpec((rows, MAIN_HEADS), lambda i: (0, 0)),
        ],
        out_specs=[pl.BlockSpec((rows, MAIN_WIDTH), lambda i: (i, 0)),
                   pl.BlockSpec((rows, MAIN_WIDTH), lambda i: (i, 0))],
        out_shape=[jax.ShapeDtypeStruct((t, MAIN_WIDTH), F32), jax.ShapeDtypeStruct((t, MAIN_WIDTH), F32)],
        compiler_params=_params(1),
    )(u, v, v_gain, w_mix, b_mix)


def _mem_attn_kernel(q_ref, k_ref, v_ref, o_ref, *, prec):
    for h in range(MEM_HEADS):
        sl = slice(h * HEAD_DIM, (h + 1) * HEAD_DIM)
        s = _dot_nt(q_ref[:, sl], k_ref[:, sl], prec) * ATTN_SCALE
        e = jnp.exp(s - jnp.max(s, axis=-1, keepdims=True))
        p = e / jnp.sum(e, axis=-1, keepdims=True)
        o_ref[:, sl] = _dot(p, v_ref[:, sl], prec)


def _mem_attn(q, k, v, layer, nb, rows_per_batch, prec):
    tq = min(rows_per_batch, 1024)
    nq = rows_per_batch // tq
    kv_spec = pl.BlockSpec((None, None, N_MEM, MEM_WIDTH), lambda b, i: (layer, b, 0, 0))
    return pl.pallas_call(
        functools.partial(_mem_attn_kernel, prec=prec),
        grid=(nb, nq),
        in_specs=[pl.BlockSpec((tq, MEM_WIDTH), lambda b, i: (b * nq + i, 0)), kv_spec, kv_spec],
        out_specs=pl.BlockSpec((tq, MEM_WIDTH), lambda b, i: (b * nq + i, 0)),
        out_shape=jax.ShapeDtypeStruct(q.shape, F32),
        compiler_params=_params(2),
    )(q, k, v)


def _out_proj_kernel(ym_ref, ymem_ref, w1_ref, w2_ref, h_ref, o_ref, *, prec):
    o_ref[...] = h_ref[...] + _dot(ym_ref[...], w1_ref[...], prec) + _dot(ymem_ref[...], w2_ref[...], prec)


def _out_proj(y_main, y_mem, w_out, layer, h, prec):
    t = h.shape[0]
    tm = min(t, 1024)
    tn = 512
    return pl.pallas_call(
        functools.partial(_out_proj_kernel, prec=prec),
        grid=(t // tm, D_MODEL // tn),
        in_specs=[
            pl.BlockSpec((tm, MAIN_WIDTH), lambda i, j: (i, 0)),
            pl.BlockSpec((tm, MEM_WIDTH), lambda i, j: (i, 0)),
            pl.BlockSpec((None, MAIN_WIDTH, tn), lambda i, j: (layer, 0, j)),
            pl.BlockSpec((None, MEM_WIDTH, tn), lambda i, j: (layer, MAIN_WIDTH // MEM_WIDTH, j)),
            pl.BlockSpec((tm, tn), lambda i, j: (i, j)),
        ],
        out_specs=pl.BlockSpec((tm, tn), lambda i, j: (i, j)),
        out_shape=jax.ShapeDtypeStruct((t, D_MODEL), F32),
        compiler_params=_params(2),
    )(y_main, y_mem, w_out, w_out, h)


def _router_kernel(h_ref, g_ref, wr_ref, br_ref, x2_ref, idx_ref, gate_ref, *, prec):
    x = h_ref[...]
    xn = x * lax.rsqrt(jnp.mean(x * x, axis=-1, keepdims=True) + EPS) * g_ref[...]
    x2_ref[...] = xn
    lg = _dot(xn, wr_ref[...], prec) + br_ref[...]
    lane = lax.broadcasted_iota(jnp.int32, lg.shape, 1)
    neg = -jnp.inf
    is_grp = lane < N_GROUPS
    gl = jnp.where(is_grp, lg, neg)
    ge = jnp.where(is_grp, jnp.exp(gl - jnp.max(gl, axis=-1, keepdims=True)), 0.0)
    pg = ge / jnp.sum(ge, axis=-1, keepdims=True)
    p_top = jnp.max(pg, axis=-1, keepdims=True)
    g_idx = jnp.min(jnp.where(is_grp & (pg == p_top), lane, LANES), axis=-1, keepdims=True)
    first = N_GROUPS + EXPERTS_PER_GROUP * g_idx
    in_grp = (lane >= first) & (lane < first + EXPERTS_PER_GROUP)
    e1 = jnp.max(jnp.where(in_grp, lg, neg), axis=-1, keepdims=True)
    i1 = jnp.min(jnp.where(in_grp & (lg == e1), lane, LANES), axis=-1, keepdims=True)
    rest = in_grp & (lane != i1)
    e2 = jnp.max(jnp.where(rest, lg, neg), axis=-1, keepdims=True)
    i2 = jnp.min(jnp.where(rest & (lg == e2), lane, LANES), axis=-1, keepdims=True)
    t2 = jnp.exp(e2 - e1)
    den = 1.0 + t2
    idx_ref[...] = jnp.where(lane == 0, i1 - N_GROUPS, jnp.where(lane == 1, i2 - N_GROUPS, 0))
    gate_ref[...] = jnp.where(lane == 0, p_top * (1.0 / den), jnp.where(lane == 1, p_top * (t2 / den), 0.0))


def _router(h, gain, wr, br, prec):
    t = h.shape[0]
    tm = min(t, 512)
    return pl.pallas_call(
        functools.partial(_router_kernel, prec=prec),
        grid=(t // tm,),
        in_specs=[pl.BlockSpec((tm, D_MODEL), lambda i: (i, 0)), pl.BlockSpec((1, D_MODEL), lambda i: (0, 0)),
                  pl.BlockSpec((D_MODEL, LANES), lambda i: (0, 0)), pl.BlockSpec((1, LANES), lambda i: (0, 0))],
        out_specs=[pl.BlockSpec((tm, D_MODEL), lambda i: (i, 0)), pl.BlockSpec((tm, LANES), lambda i: (i, 0)),
                   pl.BlockSpec((tm, LANES), lambda i: (i, 0))],
        out_shape=[jax.ShapeDtypeStruct((t, D_MODEL), F32), jax.ShapeDtypeStruct((t, LANES), jnp.int32),
                   jax.ShapeDtypeStruct((t, LANES), F32)],
        compiler_params=_params(1),
    )(h, gain, wr, br)


def _plan_kernel(e_ref, pos_ref, te_ref, nv_ref, rank_ref, *, t2, blk, tm):
    shift = tm.bit_length() - 1
    nblk = t2 // blk
    r = lax.broadcasted_iota(jnp.int32, (blk, blk), 0)
    c = lax.broadcasted_iota(jnp.int32, (blk, blk), 1)
    upper = (r <= c).astype(F32)
    sub = lax.broadcasted_iota(jnp.int32, (N_EXPERTS, blk), 0)
    carry = jnp.zeros((N_EXPERTS, 1), F32)
    for b in range(nblk):
        sl = slice(b * blk, (b + 1) * blk)
        oh = (sub == e_ref[:, sl]).astype(F32)
        cs = _dot(oh, upper) + carry
        rank_ref[:, sl] = jnp.sum(oh * (cs - 1.0), axis=0, keepdims=True)
        carry = cs[:, blk - 1:blk]
    counts = carry.astype(jnp.int32)
    padc = ((counts + (tm - 1)) >> shift) << shift
    sub1 = lax.broadcasted_iota(jnp.int32, (N_EXPERTS, 1), 0)
    off = jnp.zeros((N_EXPERTS, 1), jnp.int32)
    run = jnp.zeros((1, 1), jnp.int32)
    for e in range(N_EXPERTS):
        off = jnp.where(sub1 == e, run, off)
        run = run + padc[e:e + 1, :]
    ends = off + padc
    for b in range(nblk):
        sl = slice(b * blk, (b + 1) * blk)
        offv = jnp.sum(jnp.where(sub == e_ref[:, sl], off, 0), axis=0, keepdims=True)
        pos_ref[:, sl] = rank_ref[:, sl].astype(jnp.int32) + offv
    lane = lax.broadcasted_iota(jnp.int32, (N_EXPERTS, LANES), 1)
    lane1 = lax.broadcasted_iota(jnp.int32, (1, LANES), 1)
    nv = run >> shift
    te = jnp.sum((ends <= lane * tm).astype(jnp.int32), axis=0, keepdims=True)
    te_last = jnp.sum((ends <= (nv - 1) * tm).astype(jnp.int32), axis=0, keepdims=True)
    te_ref[...] = jnp.minimum(jnp.where(lane1 >= nv, te_last, te), N_EXPERTS - 1)
    nv_ref[...] = jnp.broadcast_to(nv, (1, LANES))


def _plan(e_flat, tm):
    t2 = e_flat.shape[1]
    blk = min(t2, 512)
    return pl.pallas_call(
        functools.partial(_plan_kernel, t2=t2, blk=blk, tm=tm),
        out_shape=[jax.ShapeDtypeStruct((1, t2), jnp.int32), jax.ShapeDtypeStruct((1, LANES), jnp.int32),
                   jax.ShapeDtypeStruct((1, LANES), jnp.int32)],
        scratch_shapes=[pltpu.VMEM((1, t2), F32)],
        compiler_params=pltpu.CompilerParams(vmem_limit_bytes=VMEM_LIMIT_BYTES),
    )(e_flat)


def _invert_kernel(pos_ref, src_ref, *, t, t2, n_rows):
    def zero(p, carry):
        src_ref[p] = 0
        return carry

    lax.fori_loop(0, n_rows, zero, 0, unroll=8)

    def put(i, carry):
        src_ref[pos_ref[i]] = jnp.where(i >= t, i - t, i)
        return carry

    lax.fori_loop(0, t2, put, 0, unroll=8)


def _invert(pos, t, n_rows):
    return pl.pallas_call(
        functools.partial(_invert_kernel, t=t, t2=pos.shape[0], n_rows=n_rows),
        grid_spec=pltpu.PrefetchScalarGridSpec(
            num_scalar_prefetch=1,
            grid=(1,),
            in_specs=[],
            out_specs=pl.BlockSpec(memory_space=pltpu.SMEM),
        ),
        out_shape=jax.ShapeDtypeStruct((n_rows,), jnp.int32),
        compiler_params=_params(1),
    )(pos)


def _moe_ffn_kernel(te_ref, nv_ref, src_ref, x_ref, wg_ref, wu_ref, wd_ref, y_ref, xbuf_ref, sems, *, tm, prec):
    i = pl.program_id(0)
    nv = nv_ref[0]
    slot = i % 2

    def start_tile(tile, dst_slot):
        base = tile * tm
        for r in range(tm):
            pltpu.make_async_copy(x_ref.at[pl.ds(src_ref[base + r], 1)], xbuf_ref.at[dst_slot, pl.ds(r, 1)],
                                  sems.at[dst_slot]).start()

    def wait_tile():
        pltpu.make_async_copy(xbuf_ref.at[slot], xbuf_ref.at[slot], sems.at[slot]).wait()

    def ffn():
        x = xbuf_ref[slot]
        hdn = jax.nn.silu(_dot(x, wg_ref[...], prec)) * _dot(x, wu_ref[...], prec)
        y_ref[...] = _dot(hdn, wd_ref[...], prec)

    @pl.when(i == 0)
    def _():
        start_tile(0, 0)

    @pl.when(i + 1 < nv)
    def _():
        wait_tile()
        start_tile(i + 1, 1 - slot)
        ffn()

    @pl.when(i + 1 == nv)
    def _():
        wait_tile()
        ffn()

    @pl.when(i >= nv)
    def _():
        y_ref[...] = jnp.zeros_like(y_ref)


def _moe_ffn(te, nv, src, x2, w_gate, w_up, w_down, layer, tm, prec):
    n_rows = src.shape[0]
    n_tiles = n_rows // tm
    w_in_spec = pl.BlockSpec((None, None, D_MODEL, D_EXPERT), lambda i, te, nv, src: (layer, te[i], 0, 0))
    return pl.pallas_call(
        functools.partial(_moe_ffn_kernel, tm=tm, prec=prec),
        grid_spec=pltpu.PrefetchScalarGridSpec(
            num_scalar_prefetch=3,
            grid=(n_tiles,),
            in_specs=[
                pl.BlockSpec(memory_space=pl.ANY),
                w_in_spec,
                w_in_spec,
                pl.BlockSpec((None, None, D_EXPERT, D_MODEL), lambda i, te, nv, src: (layer, te[i], 0, 0)),
            ],
            out_specs=pl.BlockSpec((tm, D_MODEL), lambda i, te, nv, src: (i, 0)),
            scratch_shapes=[pltpu.VMEM((2, tm, D_MODEL), F32), pltpu.SemaphoreType.DMA((2,))],
        ),
        out_shape=jax.ShapeDtypeStruct((n_rows, D_MODEL), F32),
        compiler_params=_params(1),
    )(te, nv, src, x2, w_gate, w_up, w_down)


def _combine_kernel(pos_ref, h_ref, g_ref, y_ref, o_ref, y0_ref, y1_ref, sems, *, t, tm):
    base = pl.program_id(0) * tm

    def copies(r):
        return (pltpu.make_async_copy(y_ref.at[pl.ds(pos_ref[base + r], 1)], y0_ref.at[pl.ds(r, 1)], sems.at[0]),
                pltpu.make_async_copy(y_ref.at[pl.ds(pos_ref[t + base + r], 1)], y1_ref.at[pl.ds(r, 1)], sems.at[1]))

    def start(r, carry):
        for cp in copies(r):
            cp.start()
        return carry

    def wait(r, carry):
        for cp in copies(r):
            cp.wait()
        return carry

    lax.fori_loop(0, tm, start, 0)
    lax.fori_loop(0, tm, wait, 0)
    g = g_ref[...]
    o_ref[...] = h_ref[...] + g[:, 0:1] * y0_ref[...] + g[:, 1:2] * y1_ref[...]


def _combine(pos, h, gates, y):
    t = h.shape[0]
    tm = min(t, 256)
    return pl.pallas_call(
        functools.partial(_combine_kernel, t=t, tm=tm),
        grid_spec=pltpu.PrefetchScalarGridSpec(
            num_scalar_prefetch=1,
            grid=(t // tm,),
            in_specs=[pl.BlockSpec((tm, D_MODEL), lambda i, pos: (i, 0)),
                      pl.BlockSpec((tm, LANES), lambda i, pos: (i, 0)),
                      pl.BlockSpec(memory_space=pl.ANY)],
            out_specs=pl.BlockSpec((tm, D_MODEL), lambda i, pos: (i, 0)),
            scratch_shapes=[pltpu.VMEM((tm, D_MODEL), F32), pltpu.VMEM((tm, D_MODEL), F32),
                            pltpu.SemaphoreType.DMA((2,))],
        ),
        out_shape=jax.ShapeDtypeStruct((t, D_MODEL), F32),
        compiler_params=_params(1),
    )(pos, h, gates, y)


def _hier_moe(h, gain, wr, br, w_gate, w_up, w_down, layer, prec):
    t = h.shape[0]
    tm = 256 if t >= 1024 else 16
    n_tiles = (2 * t) // tm + N_EXPERTS
    assert n_tiles <= LANES
    x2, idx, gates = _router(h, gain, wr, br, prec)
    e_flat = idx[:, :2].T.reshape(1, 2 * t)
    pos, te, nv = _plan(e_flat, tm)
    pos = pos.reshape(2 * t)
    te = te[0, :n_tiles]
    nv = nv[0, :1]
    src = _invert(pos, t, n_tiles * tm)
    y = _moe_ffn(te, nv, src, x2, w_gate, w_up, w_down, layer, tm, prec)
    return _combine(pos, h, gates, y)


def _fox_cumsum_kernel(lft_ref, lf_ref, ct_ref, c_ref, *, s, blk):
    r = lax.broadcasted_iota(jnp.int32, (blk, blk), 0)
    c = lax.broadcasted_iota(jnp.int32, (blk, blk), 1)
    upper = (r <= c).astype(F32)
    lower = (r >= c).astype(F32)
    carry_t = jnp.zeros((2 * SUBLANES, 1), F32)
    carry = jnp.zeros((1, LANES), F32)
    for b in range(s // blk):
        sl = slice(b * blk, (b + 1) * blk)
        ct = _dot3_left(lft_ref[:, sl], upper) + carry_t
        ct_ref[:, sl] = ct
        carry_t = ct[:, blk - 1:blk]
        cc = _dot3_right(lower, lf_ref[sl, :]) + carry
        c_ref[sl, :] = cc
        carry = cc[blk - 1:blk, :]


def _fox_cumsum(lft, lf, nb, s):
    t = lf.shape[0]
    blk = min(s, 256)
    return pl.pallas_call(
        functools.partial(_fox_cumsum_kernel, s=s, blk=blk),
        grid=(nb,),
        in_specs=[pl.BlockSpec((2 * SUBLANES, s), lambda b: (0, b)), pl.BlockSpec((s, LANES), lambda b: (b, 0))],
        out_specs=[pl.BlockSpec((2 * SUBLANES, s), lambda b: (0, b)), pl.BlockSpec((s, LANES), lambda b: (b, 0))],
        out_shape=[jax.ShapeDtypeStruct((2 * SUBLANES, t), F32), jax.ShapeDtypeStruct((t, LANES), F32)],
        compiler_params=_params(1),
    )(lft, lf)


def _fox_prompt_kernel(q_ref, k_ref, v_ref, ct_ref, c_ref, o_ref, *, s, tq):
    h = pl.program_id(1)
    lane = lax.broadcasted_iota(jnp.int32, (s, LANES), 1)
    c_col = jnp.sum(jnp.where(lane == h, c_ref[...], 0.0), axis=1, keepdims=True)
    c_row = ct_ref[pl.ds(h, 1), :]
    ri = lax.broadcasted_iota(jnp.int32, (tq, tq), 0)
    ci = lax.broadcasted_iota(jnp.int32, (tq, tq), 1)
    causal = ci <= ri
    for qi in range(s // tq):
        qs = slice(qi * tq, (qi + 1) * tq)
        q = q_ref[qs, :]
        m = jnp.full((tq, 1), -jnp.inf, F32)
        l = jnp.zeros((tq, 1), F32)
        acc = jnp.zeros((tq, HEAD_DIM), F32)
        for kj in range(qi + 1):
            ks = slice(kj * tq, (kj + 1) * tq)
            sc = lax.dot_general(q, k_ref[ks, :], _NT, preferred_element_type=F32) * ATTN_SCALE
            sc = sc + c_col[qs, :] - c_row[:, ks]
            if kj == qi:
                sc = jnp.where(causal, sc, -jnp.inf)
            m_new = jnp.maximum(m, jnp.max(sc, axis=-1, keepdims=True))
            alpha = jnp.exp(m - m_new)
            p = jnp.exp(sc - m_new)
            l = alpha * l + jnp.sum(p, axis=-1, keepdims=True)
            acc = alpha * acc + _dot(p, v_ref[ks, :])
            m = m_new
        o_ref[qs, :] = acc / l


def _fox_prompt(q, k, v, ct, c, nb, s):
    t = q.shape[0]
    tq = min(s, 512)
    head_spec = pl.BlockSpec((s, HEAD_DIM), lambda b, h: (b, h))
    return pl.pallas_call(
        functools.partial(_fox_prompt_kernel, s=s, tq=tq),
        grid=(nb, MAIN_HEADS),
        in_specs=[head_spec, head_spec, head_spec,
                  pl.BlockSpec((2 * SUBLANES, s), lambda b, h: (0, b)),
                  pl.BlockSpec((s, LANES), lambda b, h: (b, 0))],
        out_specs=head_spec,
        out_shape=jax.ShapeDtypeStruct((t, MAIN_WIDTH), F32),
        compiler_params=_params(2),
    )(q, k, v, ct, c)


def _fox_past_bias_kernel(pt_ref, lfc_ref, o_ref, buf_ref, sem, *, n_pages):
    b = pl.program_id(0)

    def page_copy(p):
        page = pt_ref[b * n_pages + p]
        return pltpu.make_async_copy(lfc_ref.at[:, pl.ds(page, 1), :], buf_ref.at[:, pl.ds(p, 1), :], sem)

    def start(p, carry):
        page_copy(p).start()
        return carry

    def wait(p, carry):
        page_copy(p).wait()
        return carry

    lax.fori_loop(0, n_pages, start, 0)
    r = lax.broadcasted_iota(jnp.int32, (PAGE_SIZE, PAGE_SIZE), 0)
    c = lax.broadcasted_iota(jnp.int32, (PAGE_SIZE, PAGE_SIZE), 1)
    after_in_page = (r > c).astype(F32)
    pr = lax.broadcasted_iota(jnp.int32, (n_pages, n_pages), 0)
    pc = lax.broadcasted_iota(jnp.int32, (n_pages, n_pages), 1)
    later_pages = (pc > pr).astype(F32)
    lax.fori_loop(0, n_pages, wait, 0)
    for h in range(MAIN_HEADS):
        lp = buf_ref[h]
        tot = jnp.broadcast_to(jnp.sum(lp, axis=1, keepdims=True), lp.shape)
        o_ref[h] = _dot3_left(lp, after_in_page) + _dot3_right(later_pages, tot)


def _fox_past_bias(page_table_flat, cache_logf_t, nb, n_pages):
    return pl.pallas_call(
        functools.partial(_fox_past_bias_kernel, n_pages=n_pages),
        grid_spec=pltpu.PrefetchScalarGridSpec(
            num_scalar_prefetch=1,
            grid=(nb,),
            in_specs=[pl.BlockSpec(memory_space=pl.ANY)],
            out_specs=pl.BlockSpec((None, MAIN_HEADS, n_pages, PAGE_SIZE), lambda b, pt: (b, 0, 0, 0)),
            scratch_shapes=[pltpu.VMEM((MAIN_HEADS, n_pages, PAGE_SIZE), F32), pltpu.SemaphoreType.DMA(())],
        ),
        out_shape=jax.ShapeDtypeStruct((nb, MAIN_HEADS, n_pages, PAGE_SIZE), F32),
        compiler_params=_params(1),
    )(page_table_flat, cache_logf_t)


def _fox_sample_kernel(*refs, ppb, n_steps, valid):
    pt_ref, q_ref, kn_ref, vn_ref, lf_ref, bias_ref = refs[:6]
    k_refs = refs[6:6 + ppb]
    v_refs = refs[6 + ppb:6 + 2 * ppb]
    o_ref, m_ref, l_ref, acc_ref = refs[6 + 2 * ppb:]
    g = pl.program_id(1)
    rows = SAMPLE_ROWS

    @pl.when(g == 0)
    def _():
        m_ref[...] = jnp.full(m_ref.shape, -jnp.inf, F32)
        l_ref[...] = jnp.zeros(l_ref.shape, F32)
        acc_ref[...] = jnp.zeros(acc_ref.shape, F32)

    r8 = lax.broadcasted_iota(jnp.int32, (rows, rows), 0)
    c8 = lax.broadcasted_iota(jnp.int32, (rows, rows), 1)
    new_mask = (c8 <= r8) & (c8 < valid)
    c_new = _dot3_right(new_mask.astype(F32), lf_ref[...])

    def qk3(q_hi, q_lo, k):
        k_hi, k_lo = _split_hi_lo(k)
        a = _dot_nt(jnp.concatenate([q_hi, q_lo], axis=0), k_hi)
        return a[:rows] + a[rows:] + _dot_nt(q_hi, k_lo)

    def pv3(p, v):
        p_hi, p_lo = _split_hi_lo(p)
        v_hi, v_lo = _split_hi_lo(v)
        a = _dot(jnp.concatenate([p_hi, p_lo], axis=0), v_hi)
        return a[:rows] + a[rows:] + _dot(p_hi, v_lo)

    for h in range(MAIN_HEADS):
        sl = slice(h * HEAD_DIM, (h + 1) * HEAD_DIM)
        q_hi, q_lo = _split_hi_lo(q_ref[:, sl])
        cq = c_new[:, h:h + 1]
        scores = []
        for u in range(ppb):
            scores.append(qk3(q_hi, q_lo, k_refs[u][h]) * ATTN_SCALE + cq + bias_ref[h, u:u + 1, :])
        m_old = m_ref[:, h:h + 1]
        m_new = m_old
        for sc in scores:
            m_new = jnp.maximum(m_new, jnp.max(sc, axis=-1, keepdims=True))
        alpha = jnp.exp(m_old - m_new)
        l_new = alpha * l_ref[:, h:h + 1]
        acc = alpha * acc_ref[:, sl]
        for u, sc in enumerate(scores):
            p = jnp.exp(sc - m_new)
            l_new = l_new + jnp.sum(p, axis=-1, keepdims=True)
            acc = acc + pv3(p, v_refs[u][h])
        m_ref[:, h:h + 1] = m_new
        l_ref[:, h:h + 1] = l_new
        acc_ref[:, sl] = acc

    @pl.when(g == n_steps - 1)
    def _():
        for h in range(MAIN_HEADS):
            sl = slice(h * HEAD_DIM, (h + 1) * HEAD_DIM)
            qh = q_ref[:, sl]
            cq = c_new[:, h:h + 1]
            cq_row = jnp.sum(jnp.where(r8 == c8, jnp.broadcast_to(cq, (rows, rows)), 0.0), axis=0, keepdims=True)
            sn = _dot_nt(qh, kn_ref[:, sl], HIGHEST) * ATTN_SCALE + cq - cq_row
            sn = jnp.where(new_mask, sn, -jnp.inf)
            m_old = m_ref[:, h:h + 1]
            m_new = jnp.maximum(m_old, jnp.max(sn, axis=-1, keepdims=True))
            alpha = jnp.exp(m_old - m_new)
            p = jnp.exp(sn - m_new)
            l_new = alpha * l_ref[:, h:h + 1] + jnp.sum(p, axis=-1, keepdims=True)
            acc = alpha * acc_ref[:, sl] + _dot(p, vn_ref[:, sl], HIGHEST)
            o_ref[:, sl] = acc / l_new


def _fox_sample(page_table_flat, q, k_new, v_new, lf, bias, cache_k_t, cache_v_t, nb, n_pages, valid):
    ppb = 8 if n_pages % 8 == 0 else n_pages
    n_steps = n_pages // ppb
    row_spec = pl.BlockSpec((SAMPLE_ROWS, MAIN_WIDTH), lambda b, g, pt: (b, 0))

    def page_spec(u):
        return pl.BlockSpec((None, MAIN_HEADS, PAGE_SIZE, HEAD_DIM),
                            lambda b, g, pt, u=u: (pt[b * n_pages + g * ppb + u], 0, 0, 0))

    in_specs = [row_spec, row_spec, row_spec,
                pl.BlockSpec((SAMPLE_ROWS, LANES), lambda b, g, pt: (b, 0)),
                pl.BlockSpec((None, MAIN_HEADS, ppb, PAGE_SIZE), lambda b, g, pt: (b, 0, g, 0))]
    in_specs += [page_spec(u) for u in range(ppb)] + [page_spec(u) for u in range(ppb)]
    return pl.pallas_call(
        functools.partial(_fox_sample_kernel, ppb=ppb, n_steps=n_steps, valid=valid),
        grid_spec=pltpu.PrefetchScalarGridSpec(
            num_scalar_prefetch=1,
            grid=(nb, n_steps),
            in_specs=in_specs,
            out_specs=row_spec,
            scratch_shapes=[pltpu.VMEM((SAMPLE_ROWS, LANES), F32), pltpu.VMEM((SAMPLE_ROWS, LANES), F32),
                            pltpu.VMEM((SAMPLE_ROWS, MAIN_WIDTH), F32)],
        ),
        out_shape=jax.ShapeDtypeStruct(q.shape, F32),
        compiler_params=_params(2),
    )(page_table_flat, q, k_new, v_new, lf, bias, *([cache_k_t] * ppb), *([cache_v_t] * ppb))


def _trunk(x, nb, rows, valid, mem_k, mem_v, fox, w, prec):
    u, v, qm = _norm_matmul(x, w["norm1_gain"][0:1], w["w_in_a"], 0,
                            [(MAIN_WIDTH, "gelu", 0), (MAIN_WIDTH, "gelu", 0), (MEM_WIDTH, "headnorm", 0)],
                            w["mem_q_gain"][0:1], prec=prec)
    y_main, v_rows = _gmlp(u, v, w["v_gain_a"][0:1], w["w_mix"], w["b_mix"], w["gmlp_rows"], w["gmlp_grp"], valid,
                           prec)
    y_mem = _mem_attn(qm, mem_k, mem_v, 0, nb, rows, prec)
    h = _out_proj(y_main, y_mem, w["w_out"], 0, x, prec)
    h = _hier_moe(h, w["norm2_gain"][0:1], w["w_router"][0], w["b_router"][0], w["w_gate"], w["w_up"], w["w_down"],
                  0, prec)
    k, vv, lf, lft = _norm_matmul(h, w["kv_norm_gain"], w["w_kv_shared"], 0,
                                  [(MAIN_WIDTH, "headnorm", 0), (MAIN_WIDTH, "plain", 0)],
                                  w["k_gain_shared"], forget=w["forget"], prec=prec)
    q, qm = _norm_matmul(h, w["norm1_gain"][1:2], w["w_in_b"], 0,
                         [(MAIN_WIDTH, "headnorm", 0), (MEM_WIDTH, "headnorm", 1)],
                         jnp.concatenate([w["q_gain_b"][0:1], w["mem_q_gain"][1:2]], axis=0), prec=prec)
    y_main = fox(q, k, vv, lf, lft)
    y_mem = _mem_attn(qm, mem_k, mem_v, 1, nb, rows, prec)
    h = _out_proj(y_main, y_mem, w["w_out"], 1, h, prec)
    h = _hier_moe(h, w["norm2_gain"][1:2], w["w_router"][1], w["b_router"][1], w["w_gate"], w["w_up"], w["w_down"],
                  1, prec)
    return h, k, vv, lf, v_rows


def kernel(x_prompt, x_sample, cache_k, cache_v, cache_logf, cache_mem_k, cache_mem_v, page_table, mem_prompt, norm1_gain, norm2_gain, w_in_a, v_gain_a, w_s_a, b_s_a, w_in_b, q_gain_b, kv_norm_gain, w_kv_shared, b_forget, k_gain_shared, mem_norm_gain, w_mem_kv, mem_q_gain, mem_k_gain, w_out, w_router_group, b_router_group, w_router_expert, b_router_expert, w_gate, w_up, w_down):
    batch, seq, _ = x_prompt.shape
    dec_batch, dec_seq, _ = x_sample.shape
    n_phys = cache_k.shape[0]
    n_pages = page_table.shape[1]
    depth = norm1_gain.shape[0]
    assert depth == 2 and dec_seq <= SAMPLE_ROWS and seq % CHUNK == 0

    n_route = N_GROUPS + N_EXPERTS
    w_router = jnp.pad(jnp.concatenate([w_router_group, w_router_expert], axis=-1),
                       ((0, 0), (0, 0), (0, LANES - n_route)))
    b_router = jnp.pad(jnp.concatenate([b_router_group, b_router_expert], axis=-1),
                       ((0, 0), (0, LANES - n_route)))[:, None, :]
    w_f = w_kv_shared[:, 2 * MAIN_WIDTH:]
    forget = (jnp.pad(w_f, ((0, 0), (0, LANES - MAIN_HEADS))),
              jnp.pad(w_f.T, ((0, 2 * SUBLANES - MAIN_HEADS), (0, 0))),
              jnp.pad(b_forget, (0, LANES - MAIN_HEADS))[None, :],
              jnp.pad(b_forget, (0, 2 * SUBLANES - MAIN_HEADS))[:, None])
    common = dict(norm1_gain=norm1_gain, norm2_gain=norm2_gain, w_in_a=w_in_a, v_gain_a=v_gain_a, w_in_b=w_in_b,
                  q_gain_b=q_gain_b, kv_norm_gain=kv_norm_gain[None, :], w_kv_shared=w_kv_shared,
                  k_gain_shared=k_gain_shared[None, :], mem_q_gain=mem_q_gain, w_out=w_out,
                  w_router=w_router, b_router=b_router, w_gate=w_gate, w_up=w_up, w_down=w_down, forget=forget)

    t_p = batch * seq
    mem_tok = mem_prompt.reshape(batch * N_MEM, D_MODEL)
    mem_k_layers, mem_v_layers = [], []
    for l in range(depth):
        mk, mv = _norm_matmul(mem_tok, mem_norm_gain[l:l + 1], w_mem_kv, l,
                              [(MEM_WIDTH, "headnorm", 0), (MEM_WIDTH, "plain", 0)], mem_k_gain[l:l + 1])
        mem_k_layers.append(mk.reshape(batch, N_MEM, MEM_WIDTH))
        mem_v_layers.append(mv.reshape(batch, N_MEM, MEM_WIDTH))
    mem_k_p = jnp.stack(mem_k_layers)
    mem_v_p = jnp.stack(mem_v_layers)

    def fox_prompt(q, k, v, lf, lft):
        ct, c = _fox_cumsum(lft, lf, batch, seq)
        return _fox_prompt(q, k, v, ct, c, batch, seq)

    w_p = dict(common, w_mix=w_s_a[0], b_mix=b_s_a[0].T, gmlp_rows=CHUNK, gmlp_grp=CHUNK)
    y_p, k_p, v_p, lf_p, vrows_p = _trunk(x_prompt.reshape(t_p, D_MODEL), batch, seq, seq,
                                          mem_k_p, mem_v_p, fox_prompt, w_p, None)

    rows = SAMPLE_ROWS
    t_s = dec_batch * rows
    x_s = jnp.pad(x_sample, ((0, 0), (0, rows - dec_seq), (0, 0))).reshape(t_s, D_MODEL)
    pt_flat = page_table.reshape(dec_batch * n_pages)
    bias = _fox_past_bias(pt_flat, jnp.transpose(cache_logf, (2, 0, 1)), dec_batch, n_pages)
    cache_k_t = jnp.transpose(cache_k, (0, 2, 1, 3))
    cache_v_t = jnp.transpose(cache_v, (0, 2, 1, 3))

    def fox_sample(q, k, v, lf, lft):
        return _fox_sample(pt_flat, q, k, v, lf, bias, cache_k_t, cache_v_t, dec_batch, n_pages, dec_seq)

    w_s = dict(common, w_mix=jnp.tile(w_s_a[0][:, :rows, :rows], (1, dec_batch, dec_batch)),
               b_mix=jnp.tile(b_s_a[0][:, :rows].T, (dec_batch, 1)), gmlp_rows=t_s, gmlp_grp=rows)
    mem_k_s = cache_mem_k.reshape(depth, dec_batch, N_MEM, MEM_WIDTH)
    mem_v_s = cache_mem_v.reshape(depth, dec_batch, N_MEM, MEM_WIDTH)
    y_s, k_s, v_s, lf_s, vrows_s = _trunk(x_s, dec_batch, rows, dec_seq, mem_k_s, mem_v_s, fox_sample, w_s, HIGHEST)

    def unpad(a, *tail):
        return a.reshape((dec_batch, rows) + tail)[:, :dec_seq]

    return (y_p.reshape(batch, seq, D_MODEL),
            unpad(y_s, D_MODEL),
            k_p.reshape(batch, seq, MAIN_HEADS, HEAD_DIM),
            v_p.reshape(batch, seq, MAIN_HEADS, HEAD_DIM),
            lf_p[:, :MAIN_HEADS].reshape(batch, seq, MAIN_HEADS),
            unpad(k_s, MAIN_HEADS, HEAD_DIM),
            unpad(v_s, MAIN_HEADS, HEAD_DIM),
            unpad(lf_s[:, :MAIN_HEADS], MAIN_HEADS),
            mem_k_p.reshape(depth, batch, N_MEM, MEM_HEADS, HEAD_DIM),
            mem_v_p.reshape(depth, batch, N_MEM, MEM_HEADS, HEAD_DIM),
            vrows_p.reshape(batch, seq, MAIN_WIDTH)[:, -CHUNK:][None],
            unpad(vrows_s, MAIN_WIDTH)[None])
```

```python
import functools

import jax
import jax.numpy as jnp
from jax import lax
from jax.experimental import pallas as pl
from jax.experimental.pallas import tpu as pltpu

D_MODEL = 2048
HEAD_DIM = 128
MAIN_HEADS = 12
MEM_HEADS = 4
MAIN_WIDTH = MAIN_HEADS * HEAD_DIM
MEM_WIDTH = MEM_HEADS * HEAD_DIM
N_MEM = 256
N_GROUPS = 4
EXPERTS_PER_GROUP = 4
N_EXPERTS = N_GROUPS * EXPERTS_PER_GROUP
D_EXPERT = 512
CHUNK = 128
PAGE_SIZE = 128
EPS = 1e-6
ATTN_SCALE = HEAD_DIM ** -0.5

LANES = 128
SUBLANES = 8
VMEM_LIMIT_BYTES = 56 * 1024 * 1024
SAMPLE_ROWS = SUBLANES

F32 = jnp.float32
_NT = (((1,), (1,)), ((), ()))


def _params(n_axes):
    return pltpu.CompilerParams(dimension_semantics=("arbitrary",) * n_axes,
                                vmem_limit_bytes=VMEM_LIMIT_BYTES)


HIGHEST = lax.Precision.HIGHEST


def _dot(a, b, prec=None):
    return jnp.dot(a, b, preferred_element_type=F32, precision=prec)


def _dot_nt(a, b, prec=None):
    return lax.dot_general(a, b, _NT, preferred_element_type=F32, precision=prec)


def _split_hi_lo(x):
    hi = x.astype(jnp.bfloat16).astype(F32)
    return hi, x - hi


def _split3(x):
    hi = x.astype(jnp.bfloat16).astype(F32)
    r = x - hi
    mid = r.astype(jnp.bfloat16).astype(F32)
    lo = (r - mid).astype(jnp.bfloat16).astype(F32)
    return hi, mid, lo


def _dot3_right(a_exact, x):
    hi, mid, lo = _split3(x)
    return _dot(a_exact, hi) + _dot(a_exact, mid) + _dot(a_exact, lo)


def _dot3_left(x, b_exact):
    hi, mid, lo = _split3(x)
    return _dot(hi, b_exact) + _dot(mid, b_exact) + _dot(lo, b_exact)


def _head_norm(z, gain_row):
    return z * lax.rsqrt(jnp.mean(z * z, axis=-1, keepdims=True) + EPS) * gain_row


def _norm_matmul_kernel(*refs, segs, tn, with_forget, prec):
    it = iter(refs)
    x_ref, g_ref, w_ref, hg_ref = next(it), next(it), next(it), next(it)
    if with_forget:
        wf_ref, wft_ref, bfr_ref, bfc_ref = next(it), next(it), next(it), next(it)
    out_refs = [next(it) for _ in segs]
    if with_forget:
        lf_ref, lft_ref = next(it), next(it)
    xn_ref = next(it)
    j = pl.program_id(1)

    def project(xn, o_ref, kind, grow):
        z = _dot(xn, w_ref[...], prec)
        if kind == "plain":
            o_ref[...] = z
        elif kind == "gelu":
            o_ref[...] = jax.nn.gelu(z)
        else:
            gain = hg_ref[grow:grow + 1, :]
            for c in range(tn // HEAD_DIM):
                sl = slice(c * HEAD_DIM, (c + 1) * HEAD_DIM)
                o_ref[:, sl] = _head_norm(z[:, sl], gain)

    @pl.when(j == 0)
    def _first():
        x = x_ref[...]
        xn = x * lax.rsqrt(jnp.mean(x * x, axis=-1, keepdims=True) + EPS) * g_ref[...]
        xn_ref[...] = xn
        if with_forget:
            lf_ref[...] = jax.nn.log_sigmoid(_dot(xn, wf_ref[...], prec) + bfr_ref[...])
            lft_ref[...] = jax.nn.log_sigmoid(_dot_nt(wft_ref[...], xn, prec) + bfc_ref[...])
        project(xn, out_refs[0], segs[0][1], segs[0][2])

    lo = 0
    for (ncols, kind, grow), o_ref in zip(segs, out_refs):
        nt = ncols // tn

        @pl.when((j >= max(lo, 1)) & (j < lo + nt))
        def _rest(o_ref=o_ref, kind=kind, grow=grow):
            project(xn_ref[...], o_ref, kind, grow)

        lo += nt


def _norm_matmul(x, gain, w, layer, segs, head_gains, forget=None, prec=None):
    t = x.shape[0]
    tm = min(t, 1024)
    tn = 512
    n_tiles = sum(s[0] for s in segs) // tn
    with_forget = forget is not None
    in_specs = [
        pl.BlockSpec((tm, D_MODEL), lambda i, j: (i, 0)),
        pl.BlockSpec((1, D_MODEL), lambda i, j: (0, 0)),
        pl.BlockSpec((None, D_MODEL, tn), lambda i, j: (layer, 0, j)) if w.ndim == 3
        else pl.BlockSpec((D_MODEL, tn), lambda i, j: (0, j)),
        pl.BlockSpec(head_gains.shape, lambda i, j: (0, 0)),
    ]
    args = [x, gain, w, head_gains]
    if with_forget:
        wf, wft, bfr, bfc = forget
        in_specs += [pl.BlockSpec(wf.shape, lambda i, j: (0, 0)), pl.BlockSpec(wft.shape, lambda i, j: (0, 0)),
                     pl.BlockSpec(bfr.shape, lambda i, j: (0, 0)), pl.BlockSpec(bfc.shape, lambda i, j: (0, 0))]
        args += [wf, wft, bfr, bfc]
    out_shape, out_specs = [], []
    lo = 0
    for ncols, _, _ in segs:
        nt = ncols // tn
        out_shape.append(jax.ShapeDtypeStruct((t, ncols), F32))
        out_specs.append(pl.BlockSpec((tm, tn), lambda i, j, lo=lo, nt=nt: (i, jnp.clip(j - lo, 0, nt - 1))))
        lo += nt
    if with_forget:
        out_shape += [jax.ShapeDtypeStruct((t, LANES), F32), jax.ShapeDtypeStruct((2 * SUBLANES, t), F32)]
        out_specs += [pl.BlockSpec((tm, LANES), lambda i, j: (i, 0)),
                      pl.BlockSpec((2 * SUBLANES, tm), lambda i, j: (0, i))]
    return pl.pallas_call(
        functools.partial(_norm_matmul_kernel, segs=tuple(segs), tn=tn, with_forget=with_forget, prec=prec),
        grid=(t // tm, n_tiles),
        in_specs=in_specs,
        out_specs=out_specs,
        out_shape=out_shape,
        scratch_shapes=[pltpu.VMEM((tm, D_MODEL), F32)],
        compiler_params=_params(2),
    )(*args)


def _gmlp_kernel(u_ref, v_ref, vg_ref, w_ref, b_ref, y_ref, vn_ref, *, rows, grp, valid, prec):
    v = v_ref[...]
    vn = v * lax.rsqrt(jnp.mean(v * v, axis=-1, keepdims=True) + EPS) * vg_ref[...]
    vn_ref[...] = vn
    r = lax.broadcasted_iota(jnp.int32, (rows, rows), 0)
    c = lax.broadcasted_iota(jnp.int32, (rows, rows), 1)
    shift = grp.bit_length() - 1
    allowed = ((c & (grp - 1)) <= (r & (grp - 1))) & ((c & (grp - 1)) < valid)
    if grp < rows:
        allowed = allowed & ((r >> shift) == (c >> shift))
    for g in range(MAIN_HEADS):
        sl = slice(g * HEAD_DIM, (g + 1) * HEAD_DIM)
        w = jnp.where(allowed, w_ref[g], 0.0)
        mixed = _dot(w, vn[:, sl], prec) + b_ref[:, g:g + 1]
        y_ref[:, sl] = u_ref[:, sl] * mixed


def _gmlp(u, v, v_gain, w_mix, b_mix, rows, grp, valid, prec):
    t = u.shape[0]
    return pl.pallas_call(
        functools.partial(_gmlp_kernel, rows=rows, grp=grp, valid=valid, prec=prec),
        grid=(t // rows,),
        in_specs=[
            pl.BlockSpec((rows, MAIN_WIDTH), lambda i: (i, 0)),
            pl.BlockSpec((rows, MAIN_WIDTH), lambda i: (i, 0)),
            pl.BlockSpec((1, MAIN_WIDTH), lambda i: (0, 0)),
            pl.BlockSpec((MAIN_HEADS, rows, rows), lambda i: (0, 0, 0)),
            pl.BlockSpec((rows, MAIN_HEADS), lambda i: (0, 0)),
        ],
        out_specs=[pl.BlockSpec((rows, MAIN_WIDTH), lambda i: (i, 0)),
                   pl.BlockSpec((rows, MAIN_WIDTH), lambda i: (i, 0))],
        out_shape=[jax.ShapeDtypeStruct((t, MAIN_WIDTH), F32), jax.ShapeDtypeStruct((t, MAIN_WIDTH), F32)],
        compiler_params=_params(1),
    )(u, v, v_gain, w_mix, b_mix)


def _mem_attn_kernel(q_ref, k_ref, v_ref, o_ref, *, prec):
    for h in range(MEM_HEADS):
        sl = slice(h * HEAD_DIM, (h + 1) * HEAD_DIM)
        s = _dot_nt(q_ref[:, sl], k_ref[:, sl], prec) * ATTN_SCALE
        e = jnp.exp(s - jnp.max(s, axis=-1, keepdims=True))
        p = e / jnp.sum(e, axis=-1, keepdims=True)
        o_ref[:, sl] = _dot(p, v_ref[:, sl], prec)


def _mem_attn(q, k, v, layer, nb, rows_per_batch, prec):
    tq = min(rows_per_batch, 1024)
    nq = rows_per_batch // tq
    kv_spec = pl.BlockSpec((None, None, N_MEM, MEM_WIDTH), lambda b, i: (layer, b, 0, 0))
    return pl.pallas_call(
        functools.partial(_mem_attn_kernel, prec=prec),
        grid=(nb, nq),
        in_specs=[pl.BlockSpec((tq, MEM_WIDTH), lambda b, i: (b * nq + i, 0)), kv_spec, kv_spec],
        out_specs=pl.BlockSpec((tq, MEM_WIDTH), lambda b, i: (b * nq + i, 0)),
        out_shape=jax.ShapeDtypeStruct(q.shape, F32),
        compiler_params=_params(2),
    )(q, k, v)


def _out_proj_kernel(ym_ref, ymem_ref, w1_ref, w2_ref, h_ref, o_ref, *, prec):
    o_ref[...] = h_ref[...] + _dot(ym_ref[...], w1_ref[...], prec) + _dot(ymem_ref[...], w2_ref[...], prec)


def _out_proj(y_main, y_mem, w_out, layer, h, prec):
    t = h.shape[0]
    tm = min(t, 1024)
    tn = 512
    return pl.pallas_call(
        functools.partial(_out_proj_kernel, prec=prec),
        grid=(t // tm, D_MODEL // tn),
        in_specs=[
            pl.BlockSpec((tm, MAIN_WIDTH), lambda i, j: (i, 0)),
            pl.BlockSpec((tm, MEM_WIDTH), lambda i, j: (i, 0)),
            pl.BlockSpec((None, MAIN_WIDTH, tn), lambda i, j: (layer, 0, j)),
            pl.BlockSpec((None, MEM_WIDTH, tn), lambda i, j: (layer, MAIN_WIDTH // MEM_WIDTH, j)),
            pl.BlockSpec((tm, tn), lambda i, j: (i, j)),
        ],
        out_specs=pl.BlockSpec((tm, tn), lambda i, j: (i, j)),
        out_shape=jax.ShapeDtypeStruct((t, D_MODEL), F32),
        compiler_params=_params(2),
    )(y_main, y_mem, w_out, w_out, h)


def _router_kernel(h_ref, g_ref, wr_ref, br_ref, x2_ref, idx_ref, gate_ref, *, prec):
    x = h_ref[...]
    xn = x * lax.rsqrt(jnp.mean(x * x, axis=-1, keepdims=True) + EPS) * g_ref[...]
    x2_ref[...] = xn
    lg = _dot(xn, wr_ref[...], prec) + br_ref[...]
    lane = lax.broadcasted_iota(jnp.int32, lg.shape, 1)
    neg = -jnp.inf
    is_grp = lane < N_GROUPS
    gl = jnp.where(is_grp, lg, neg)
    ge = jnp.where(is_grp, jnp.exp(gl - jnp.max(gl, axis=-1, keepdims=True)), 0.0)
    pg = ge / jnp.sum(ge, axis=-1, keepdims=True)
    p_top = jnp.max(pg, axis=-1, keepdims=True)
    g_idx = jnp.min(jnp.where(is_grp & (pg == p_top), lane, LANES), axis=-1, keepdims=True)
    first = N_GROUPS + EXPERTS_PER_GROUP * g_idx
    in_grp = (lane >= first) & (lane < first + EXPERTS_PER_GROUP)
    e1 = jnp.max(jnp.where(in_grp, lg, neg), axis=-1, keepdims=True)
    i1 = jnp.min(jnp.where(in_grp & (lg == e1), lane, LANES), axis=-1, keepdims=True)
    rest = in_grp & (lane != i1)
    e2 = jnp.max(jnp.where(rest, lg, neg), axis=-1, keepdims=True)
    i2 = jnp.min(jnp.where(rest & (lg == e2), lane, LANES), axis=-1, keepdims=True)
    t2 = jnp.exp(e2 - e1)
    den = 1.0 + t2
    idx_ref[...] = jnp.where(lane == 0, i1 - N_GROUPS, jnp.where(lane == 1, i2 - N_GROUPS, 0))
    gate_ref[...] = jnp.where(lane == 0, p_top * (1.0 / den), jnp.where(lane == 1, p_top * (t2 / den), 0.0))


def _router(h, gain, wr, br, prec):
    t = h.shape[0]
    tm = min(t, 512)
    return pl.pallas_call(
        functools.partial(_router_kernel, prec=prec),
        grid=(t // tm,),
        in_specs=[pl.BlockSpec((tm, D_MODEL), lambda i: (i, 0)), pl.BlockSpec((1, D_MODEL), lambda i: (0, 0)),
                  pl.BlockSpec((D_MODEL, LANES), lambda i: (0, 0)), pl.BlockSpec((1, LANES), lambda i: (0, 0))],
        out_specs=[pl.BlockSpec((tm, D_MODEL), lambda i: (i, 0)), pl.BlockSpec((tm, LANES), lambda i: (i, 0)),
                   pl.BlockSpec((tm, LANES), lambda i: (i, 0))],
        out_shape=[jax.ShapeDtypeStruct((t, D_MODEL), F32), jax.ShapeDtypeStruct((t, LANES), jnp.int32),
                   jax.ShapeDtypeStruct((t, LANES), F32)],
        compiler_params=_params(1),
    )(h, gain, wr, br)


def _plan_kernel(e_ref, pos_ref, te_ref, nv_ref, rank_ref, *, t2, blk, tm):
    shift = tm.bit_length() - 1
    nblk = t2 // blk
    r = lax.broadcasted_iota(jnp.int32, (blk, blk), 0)
    c = lax.broadcasted_iota(jnp.int32, (blk, blk), 1)
    upper = (r <= c).astype(F32)
    sub = lax.broadcasted_iota(jnp.int32, (N_EXPERTS, blk), 0)
    carry = jnp.zeros((N_EXPERTS, 1), F32)
    for b in range(nblk):
        sl = slice(b * blk, (b + 1) * blk)
        oh = (sub == e_ref[:, sl]).astype(F32)
        cs = _dot(oh, upper) + carry
        rank_ref[:, sl] = jnp.sum(oh * (cs - 1.0), axis=0, keepdims=True)
        carry = cs[:, blk - 1:blk]
    counts = carry.astype(jnp.int32)
    padc = ((counts + (tm - 1)) >> shift) << shift
    sub1 = lax.broadcasted_iota(jnp.int32, (N_EXPERTS, 1), 0)
    off = jnp.zeros((N_EXPERTS, 1), jnp.int32)
    run = jnp.zeros((1, 1), jnp.int32)
    for e in range(N_EXPERTS):
        off = jnp.where(sub1 == e, run, off)
        run = run + padc[e:e + 1, :]
    ends = off + padc
    for b in range(nblk):
        sl = slice(b * blk, (b + 1) * blk)
        offv = jnp.sum(jnp.where(sub == e_ref[:, sl], off, 0), axis=0, keepdims=True)
        pos_ref[:, sl] = rank_ref[:, sl].astype(jnp.int32) + offv
    lane = lax.broadcasted_iota(jnp.int32, (N_EXPERTS, LANES), 1)
    lane1 = lax.broadcasted_iota(jnp.int32, (1, LANES), 1)
    nv = run >> shift
    te = jnp.sum((ends <= lane * tm).astype(jnp.int32), axis=0, keepdims=True)
    te_last = jnp.sum((ends <= (nv - 1) * tm).astype(jnp.int32), axis=0, keepdims=True)
    te_ref[...] = jnp.minimum(jnp.where(lane1 >= nv, te_last, te), N_EXPERTS - 1)
    nv_ref[...] = jnp.broadcast_to(nv, (1, LANES))


def _plan(e_flat, tm):
    t2 = e_flat.shape[1]
    blk = min(t2, 512)
    return pl.pallas_call(
        functools.partial(_plan_kernel, t2=t2, blk=blk, tm=tm),
        out_shape=[jax.ShapeDtypeStruct((1, t2), jnp.int32), jax.ShapeDtypeStruct((1, LANES), jnp.int32),
                   jax.ShapeDtypeStruct((1, LANES), jnp.int32)],
        scratch_shapes=[pltpu.VMEM((1, t2), F32)],
        compiler_params=pltpu.CompilerParams(vmem_limit_bytes=VMEM_LIMIT_BYTES),
    )(e_flat)


def _invert_kernel(pos_ref, src_ref, *, t, t2, n_rows):
    def zero(p, carry):
        src_ref[p] = 0
        return carry

    lax.fori_loop(0, n_rows, zero, 0, unroll=8)

    def put(i, carry):
        src_ref[pos_ref[i]] = jnp.where(i >= t, i - t, i)
        return carry

    lax.fori_loop(0, t2, put, 0, unroll=8)


def _invert(pos, t, n_rows):
    return pl.pallas_call(
        functools.partial(_invert_kernel, t=t, t2=pos.shape[0], n_rows=n_rows),
        grid_spec=pltpu.PrefetchScalarGridSpec(
            num_scalar_prefetch=1,
            grid=(1,),
            in_specs=[],
            out_specs=pl.BlockSpec(memory_space=pltpu.SMEM),
        ),
        out_shape=jax.ShapeDtypeStruct((n_rows,), jnp.int32),
        compiler_params=_params(1),
    )(pos)


def _moe_ffn_kernel(te_ref, nv_ref, src_ref, x_ref, wg_ref, wu_ref, wd_ref, y_ref, xbuf_ref, sems, *, tm, prec):
    i = pl.program_id(0)
    nv = nv_ref[0]
    slot = i % 2

    def start_tile(tile, dst_slot):
        base = tile * tm
        for r in range(tm):
            pltpu.make_async_copy(x_ref.at[pl.ds(src_ref[base + r], 1)], xbuf_ref.at[dst_slot, pl.ds(r, 1)],
                                  sems.at[dst_slot]).start()

    def wait_tile():
        pltpu.make_async_copy(xbuf_ref.at[slot], xbuf_ref.at[slot], sems.at[slot]).wait()

    def ffn():
        x = xbuf_ref[slot]
        hdn = jax.nn.silu(_dot(x, wg_ref[...], prec)) * _dot(x, wu_ref[...], prec)
        y_ref[...] = _dot(hdn, wd_ref[...], prec)

    @pl.when(i == 0)
    def _():
        start_tile(0, 0)

    @pl.when(i + 1 < nv)
    def _():
        wait_tile()
        start_tile(i + 1, 1 - slot)
        ffn()

    @pl.when(i + 1 == nv)
    def _():
        wait_tile()
        ffn()

    @pl.when(i >= nv)
    def _():
        y_ref[...] = jnp.zeros_like(y_ref)


def _moe_ffn(te, nv, src, x2, w_gate, w_up, w_down, layer, tm, prec):
    n_rows = src.shape[0]
    n_tiles = n_rows // tm
    w_in_spec = pl.BlockSpec((None, None, D_MODEL, D_EXPERT), lambda i, te, nv, src: (layer, te[i], 0, 0))
    return pl.pallas_call(
        functools.partial(_moe_ffn_kernel, tm=tm, prec=prec),
        grid_spec=pltpu.PrefetchScalarGridSpec(
            num_scalar_prefetch=3,
            grid=(n_tiles,),
            in_specs=[
                pl.BlockSpec(memory_space=pl.ANY),
                w_in_spec,
                w_in_spec,
                pl.BlockSpec((None, None, D_EXPERT, D_MODEL), lambda i, te, nv, src: (layer, te[i], 0, 0)),
            ],
            out_specs=pl.BlockSpec((tm, D_MODEL), lambda i, te, nv, src: (i, 0)),
            scratch_shapes=[pltpu.VMEM((2, tm, D_MODEL), F32), pltpu.SemaphoreType.DMA((2,))],
        ),
        out_shape=jax.ShapeDtypeStruct((n_rows, D_MODEL), F32),
        compiler_params=_params(1),
    )(te, nv, src, x2, w_gate, w_up, w_down)


def _combine_kernel(pos_ref, h_ref, g_ref, y_ref, o_ref, y0_ref, y1_ref, sems, *, t, tm):
    base = pl.program_id(0) * tm

    def copies(r):
        return (pltpu.make_async_copy(y_ref.at[pl.ds(pos_ref[base + r], 1)], y0_ref.at[pl.ds(r, 1)], sems.at[0]),
                pltpu.make_async_copy(y_ref.at[pl.ds(pos_ref[t + base + r], 1)], y1_ref.at[pl.ds(r, 1)], sems.at[1]))

    def start(r, carry):
        for cp in copies(r):
            cp.start()
        return carry

    def wait(r, carry):
        for cp in copies(r):
            cp.wait()
        return carry

    lax.fori_loop(0, tm, start, 0)
    lax.fori_loop(0, tm, wait, 0)
    g = g_ref[...]
    o_ref[...] = h_ref[...] + g[:, 0:1] * y0_ref[...] + g[:, 1:2] * y1_ref[...]


def _combine(pos, h, gates, y):
    t = h.shape[0]
    tm = min(t, 256)
    return pl.pallas_call(
        functools.partial(_combine_kernel, t=t, tm=tm),
        grid_spec=pltpu.PrefetchScalarGridSpec(
            num_scalar_prefetch=1,
            grid=(t // tm,),
            in_specs=[pl.BlockSpec((tm, D_MODEL), lambda i, pos: (i, 0)),
                      pl.BlockSpec((tm, LANES), lambda i, pos: (i, 0)),
                      pl.BlockSpec(memory_space=pl.ANY)],
            out_specs=pl.BlockSpec((tm, D_MODEL), lambda i, pos: (i, 0)),
            scratch_shapes=[pltpu.VMEM((tm, D_MODEL), F32), pltpu.VMEM((tm, D_MODEL), F32),
                            pltpu.SemaphoreType.DMA((2,))],
        ),
        out_shape=jax.ShapeDtypeStruct((t, D_MODEL), F32),
        compiler_params=_params(1),
    )(pos, h, gates, y)


def _hier_moe(h, gain, wr, br, w_gate, w_up, w_down, layer, prec):
    t = h.shape[0]
    tm = 256 if t >= 1024 else 16
    n_tiles = (2 * t) // tm + N_EXPERTS
    assert n_tiles <= LANES
    x2, idx, gates = _router(h, gain, wr, br, prec)
    e_flat = idx[:, :2].T.reshape(1, 2 * t)
    pos, te, nv = _plan(e_flat, tm)
    pos = pos.reshape(2 * t)
    te = te[0, :n_tiles]
    nv = nv[0, :1]
    src = _invert(pos, t, n_tiles * tm)
    y = _moe_ffn(te, nv, src, x2, w_gate, w_up, w_down, layer, tm, prec)
    return _combine(pos, h, gates, y)


def _fox_cumsum_kernel(lft_ref, lf_ref, ct_ref, c_ref, *, s, blk):
    r = lax.broadcasted_iota(jnp.int32, (blk, blk), 0)
    c = lax.broadcasted_iota(jnp.int32, (blk, blk), 1)
    upper = (r <= c).astype(F32)
    lower = (r >= c).astype(F32)
    carry_t = jnp.zeros((2 * SUBLANES, 1), F32)
    carry = jnp.zeros((1, LANES), F32)
    for b in range(s // blk):
        sl = slice(b * blk, (b + 1) * blk)
        ct = _dot3_left(lft_ref[:, sl], upper) + carry_t
        ct_ref[:, sl] = ct
        carry_t = ct[:, blk - 1:blk]
        cc = _dot3_right(lower, lf_ref[sl, :]) + carry
        c_ref[sl, :] = cc
        carry = cc[blk - 1:blk, :]


def _fox_cumsum(lft, lf, nb, s):
    t = lf.shape[0]
    blk = min(s, 256)
    return pl.pallas_call(
        functools.partial(_fox_cumsum_kernel, s=s, blk=blk),
        grid=(nb,),
        in_specs=[pl.BlockSpec((2 * SUBLANES, s), lambda b: (0, b)), pl.BlockSpec((s, LANES), lambda b: (b, 0))],
        out_specs=[pl.BlockSpec((2 * SUBLANES, s), lambda b: (0, b)), pl.BlockSpec((s, LANES), lambda b: (b, 0))],
        out_shape=[jax.ShapeDtypeStruct((2 * SUBLANES, t), F32), jax.ShapeDtypeStruct((t, LANES), F32)],
        compiler_params=_params(1),
    )(lft, lf)


def _fox_prompt_kernel(q_ref, k_ref, v_ref, ct_ref, c_ref, o_ref, *, s, tq):
    h = pl.program_id(1)
    lane = lax.broadcasted_iota(jnp.int32, (s, LANES), 1)
    c_col = jnp.sum(jnp.where(lane == h, c_ref[...], 0.0), axis=1, keepdims=True)
    c_row = ct_ref[pl.ds(h, 1), :]
    ri = lax.broadcasted_iota(jnp.int32, (tq, tq), 0)
    ci = lax.broadcasted_iota(jnp.int32, (tq, tq), 1)
    causal = ci <= ri
    for qi in range(s // tq):
        qs = slice(qi * tq, (qi + 1) * tq)
        q = q_ref[qs, :]
        m = jnp.full((tq, 1), -jnp.inf, F32)
        l = jnp.zeros((tq, 1), F32)
        acc = jnp.zeros((tq, HEAD_DIM), F32)
        for kj in range(qi + 1):
            ks = slice(kj * tq, (kj + 1) * tq)
            sc = lax.dot_general(q, k_ref[ks, :], _NT, preferred_element_type=F32) * ATTN_SCALE
            sc = sc + c_col[qs, :] - c_row[:, ks]
            if kj == qi:
                sc = jnp.where(causal, sc, -jnp.inf)
            m_new = jnp.maximum(m, jnp.max(sc, axis=-1, keepdims=True))
            alpha = jnp.exp(m - m_new)
            p = jnp.exp(sc - m_new)
            l = alpha * l + jnp.sum(p, axis=-1, keepdims=True)
            acc = alpha * acc + _dot(p, v_ref[ks, :])
            m = m_new
        o_ref[qs, :] = acc / l


def _fox_prompt(q, k, v, ct, c, nb, s):
    t = q.shape[0]
    tq = min(s, 512)
    head_spec = pl.BlockSpec((s, HEAD_DIM), lambda b, h: (b, h))
    return pl.pallas_call(
        functools.partial(_fox_prompt_kernel, s=s, tq=tq),
        grid=(nb, MAIN_HEADS),
        in_specs=[head_spec, head_spec, head_spec,
                  pl.BlockSpec((2 * SUBLANES, s), lambda b, h: (0, b)),
                  pl.BlockSpec((s, LANES), lambda b, h: (b, 0))],
        out_specs=head_spec,
        out_shape=jax.ShapeDtypeStruct((t, MAIN_WIDTH), F32),
        compiler_params=_params(2),
    )(q, k, v, ct, c)


def _fox_past_bias_kernel(pt_ref, lfc_ref, o_ref, buf_ref, sem, *, n_pages):
    b = pl.program_id(0)

    def page_copy(p):
        page = pt_ref[b * n_pages + p]
        return pltpu.make_async_copy(lfc_ref.at[:, pl.ds(page, 1), :], buf_ref.at[:, pl.ds(p, 1), :], sem)

    def start(p, carry):
        page_copy(p).start()
        return carry

    def wait(p, carry):
        page_copy(p).wait()
        return carry

    lax.fori_loop(0, n_pages, start, 0)
    r = lax.broadcasted_iota(jnp.int32, (PAGE_SIZE, PAGE_SIZE), 0)
    c = lax.broadcasted_iota(jnp.int32, (PAGE_SIZE, PAGE_SIZE), 1)
    after_in_page = (r > c).astype(F32)
    pr = lax.broadcasted_iota(jnp.int32, (n_pages, n_pages), 0)
    pc = lax.broadcasted_iota(jnp.int32, (n_pages, n_pages), 1)
    later_pages = (pc > pr).astype(F32)
    lax.fori_loop(0, n_pages, wait, 0)
    for h in range(MAIN_HEADS):
        lp = buf_ref[h]
        tot = jnp.broadcast_to(jnp.sum(lp, axis=1, keepdims=True), lp.shape)
        o_ref[h] = _dot3_left(lp, after_in_page) + _dot3_right(later_pages, tot)


def _fox_past_bias(page_table_flat, cache_logf_t, nb, n_pages):
    return pl.pallas_call(
        functools.partial(_fox_past_bias_kernel, n_pages=n_pages),
        grid_spec=pltpu.PrefetchScalarGridSpec(
            num_scalar_prefetch=1,
            grid=(nb,),
            in_specs=[pl.BlockSpec(memory_space=pl.ANY)],
            out_specs=pl.BlockSpec((None, MAIN_HEADS, n_pages, PAGE_SIZE), lambda b, pt: (b, 0, 0, 0)),
            scratch_shapes=[pltpu.VMEM((MAIN_HEADS, n_pages, PAGE_SIZE), F32), pltpu.SemaphoreType.DMA(())],
        ),
        out_shape=jax.ShapeDtypeStruct((nb, MAIN_HEADS, n_pages, PAGE_SIZE), F32),
        compiler_params=_params(1),
    )(page_table_flat, cache_logf_t)


def _fox_sample_kernel(*refs, ppb, n_steps, valid):
    pt_ref, q_ref, kn_ref, vn_ref, lf_ref, bias_ref = refs[:6]
    k_refs = refs[6:6 + ppb]
    v_refs = refs[6 + ppb:6 + 2 * ppb]
    o_ref, m_ref, l_ref, acc_ref = refs[6 + 2 * ppb:]
    g = pl.program_id(1)
    rows = SAMPLE_ROWS

    @pl.when(g == 0)
    def _():
        m_ref[...] = jnp.full(m_ref.shape, -jnp.inf, F32)
        l_ref[...] = jnp.zeros(l_ref.shape, F32)
        acc_ref[...] = jnp.zeros(acc_ref.shape, F32)

    r8 = lax.broadcasted_iota(jnp.int32, (rows, rows), 0)
    c8 = lax.broadcasted_iota(jnp.int32, (rows, rows), 1)
    new_mask = (c8 <= r8) & (c8 < valid)
    c_new = _dot3_right(new_mask.astype(F32), lf_ref[...])

    def qk3(q_hi, q_lo, k):
        k_hi, k_lo = _split_hi_lo(k)
        a = _dot_nt(jnp.concatenate([q_hi, q_lo], axis=0), k_hi)
        return a[:rows] + a[rows:] + _dot_nt(q_hi, k_lo)

    def pv3(p, v):
        p_hi, p_lo = _split_hi_lo(p)
        v_hi, v_lo = _split_hi_lo(v)
        a = _dot(jnp.concatenate([p_hi, p_lo], axis=0), v_hi)
        return a[:rows] + a[rows:] + _dot(p_hi, v_lo)

    all_scores = []
    for h in range(MAIN_HEADS):
        q_hi, q_lo = _split_hi_lo(q_ref[:, h * HEAD_DIM:(h + 1) * HEAD_DIM])
        cq = c_new[:, h:h + 1]
        all_scores.append([qk3(q_hi, q_lo, k_refs[u][h]) * ATTN_SCALE + cq + bias_ref[h, u:u + 1, :]
                           for u in range(ppb)])
    all_probs, all_alpha = [], []
    for h in range(MAIN_HEADS):
        m_old = m_ref[h, :, 0:1]
        m_new = m_old
        for sc in all_scores[h]:
            m_new = jnp.maximum(m_new, jnp.max(sc, axis=-1, keepdims=True))
        alpha = jnp.exp(m_old - m_new)
        probs = [jnp.exp(sc - m_new) for sc in all_scores[h]]
        l_new = alpha * l_ref[h, :, 0:1]
        for p in probs:
            l_new = l_new + jnp.sum(p, axis=-1, keepdims=True)
        m_ref[h] = jnp.broadcast_to(m_new, (rows, LANES))
        l_ref[h] = jnp.broadcast_to(l_new, (rows, LANES))
        all_probs.append(probs)
        all_alpha.append(alpha)
    for h in range(MAIN_HEADS):
        sl = slice(h * HEAD_DIM, (h + 1) * HEAD_DIM)
        acc = all_alpha[h] * acc_ref[:, sl]
        for u in range(ppb):
            acc = acc + pv3(all_probs[h][u], v_refs[u][h])
        acc_ref[:, sl] = acc

    @pl.when(g == n_steps - 1)
    def _():
        for h in range(MAIN_HEADS):
            sl = slice(h * HEAD_DIM, (h + 1) * HEAD_DIM)
            qh = q_ref[:, sl]
            cq = c_new[:, h:h + 1]
            cq_row = jnp.sum(jnp.where(r8 == c8, jnp.broadcast_to(cq, (rows, rows)), 0.0), axis=0, keepdims=True)
            sn = _dot_nt(qh, kn_ref[:, sl], HIGHEST) * ATTN_SCALE + cq - cq_row
            sn = jnp.where(new_mask, sn, -jnp.inf)
            m_old = m_ref[h, :, 0:1]
            m_new = jnp.maximum(m_old, jnp.max(sn, axis=-1, keepdims=True))
            alpha = jnp.exp(m_old - m_new)
            p = jnp.exp(sn - m_new)
            l_new = alpha * l_ref[h, :, 0:1] + jnp.sum(p, axis=-1, keepdims=True)
            acc = alpha * acc_ref[:, sl] + _dot(p, vn_ref[:, sl], HIGHEST)
            o_ref[:, sl] = acc / l_new


def _fox_sample(page_table_flat, q, k_new, v_new, lf, bias, cache_k_t, cache_v_t, nb, n_pages, valid):
    ppb = 8 if n_pages % 8 == 0 else n_pages
    n_steps = n_pages // ppb
    row_spec = pl.BlockSpec((SAMPLE_ROWS, MAIN_WIDTH), lambda b, g, pt: (b, 0))

    def page_spec(u):
        return pl.BlockSpec((None, MAIN_HEADS, PAGE_SIZE, HEAD_DIM),
                            lambda b, g, pt, u=u: (pt[b * n_pages + g * ppb + u], 0, 0, 0))

    in_specs = [row_spec, row_spec, row_spec,
                pl.BlockSpec((SAMPLE_ROWS, LANES), lambda b, g, pt: (b, 0)),
                pl.BlockSpec((None, MAIN_HEADS, ppb, PAGE_SIZE), lambda b, g, pt: (b, 0, g, 0))]
    in_specs += [page_spec(u) for u in range(ppb)] + [page_spec(u) for u in range(ppb)]
    return pl.pallas_call(
        functools.partial(_fox_sample_kernel, ppb=ppb, n_steps=n_steps, valid=valid),
        grid_spec=pltpu.PrefetchScalarGridSpec(
            num_scalar_prefetch=1,
            grid=(nb, n_steps),
            in_specs=in_specs,
            out_specs=row_spec,
            scratch_shapes=[pltpu.VMEM((MAIN_HEADS, SAMPLE_ROWS, LANES), F32),
                            pltpu.VMEM((MAIN_HEADS, SAMPLE_ROWS, LANES), F32),
                            pltpu.VMEM((SAMPLE_ROWS, MAIN_WIDTH), F32)],
        ),
        out_shape=jax.ShapeDtypeStruct(q.shape, F32),
        compiler_params=_params(2),
    )(page_table_flat, q, k_new, v_new, lf, bias, *([cache_k_t] * ppb), *([cache_v_t] * ppb))


def _trunk(x, nb, rows, valid, mem_k, mem_v, fox, w, prec):
    u, v, qm = _norm_matmul(x, w["norm1_gain"][0:1], w["w_in_a"], 0,
                            [(MAIN_WIDTH, "gelu", 0), (MAIN_WIDTH, "gelu", 0), (MEM_WIDTH, "headnorm", 0)],
                            w["mem_q_gain"][0:1], prec=prec)
    y_main, v_rows = _gmlp(u, v, w["v_gain_a"][0:1], w["w_mix"], w["b_mix"], w["gmlp_rows"], w["gmlp_grp"], valid,
                           prec)
    y_mem = _mem_attn(qm, mem_k, mem_v, 0, nb, rows, prec)
    h = _out_proj(y_main, y_mem, w["w_out"], 0, x, prec)
    h = _hier_moe(h, w["norm2_gain"][0:1], w["w_router"][0], w["b_router"][0], w["w_gate"], w["w_up"], w["w_down"],
                  0, prec)
    k, vv, lf, lft = _norm_matmul(h, w["kv_norm_gain"], w["w_kv_shared"], 0,
                                  [(MAIN_WIDTH, "headnorm", 0), (MAIN_WIDTH, "plain", 0)],
                                  w["k_gain_shared"], forget=w["forget"], prec=prec)
    q, qm = _norm_matmul(h, w["norm1_gain"][1:2], w["w_in_b"], 0,
                         [(MAIN_WIDTH, "headnorm", 0), (MEM_WIDTH, "headnorm", 1)],
                         jnp.concatenate([w["q_gain_b"][0:1], w["mem_q_gain"][1:2]], axis=0), prec=prec)
    y_main = fox(q, k, vv, lf, lft)
    y_mem = _mem_attn(qm, mem_k, mem_v, 1, nb, rows, prec)
    h = _out_proj(y_main, y_mem, w["w_out"], 1, h, prec)
    h = _hier_moe(h, w["norm2_gain"][1:2], w["w_router"][1], w["b_router"][1], w["w_gate"], w["w_up"], w["w_down"],
                  1, prec)
    return h, k, vv, lf, v_rows


def kernel(x_prompt, x_sample, cache_k, cache_v, cache_logf, cache_mem_k, cache_mem_v, page_table, mem_prompt, norm1_gain, norm2_gain, w_in_a, v_gain_a, w_s_a, b_s_a, w_in_b, q_gain_b, kv_norm_gain, w_kv_shared, b_forget, k_gain_shared, mem_norm_gain, w_mem_kv, mem_q_gain, mem_k_gain, w_out, w_router_group, b_router_group, w_router_expert, b_router_expert, w_gate, w_up, w_down):
    batch, seq, _ = x_prompt.shape
    dec_batch, dec_seq, _ = x_sample.shape
    n_phys = cache_k.shape[0]
    n_pages = page_table.shape[1]
    depth = norm1_gain.shape[0]
    assert depth == 2 and dec_seq <= SAMPLE_ROWS and seq % CHUNK == 0

    n_route = N_GROUPS + N_EXPERTS
    w_router = jnp.pad(jnp.concatenate([w_router_group, w_router_expert], axis=-1),
                       ((0, 0), (0, 0), (0, LANES - n_route)))
    b_router = jnp.pad(jnp.concatenate([b_router_group, b_router_expert], axis=-1),
                       ((0, 0), (0, LANES - n_route)))[:, None, :]
    w_f = w_kv_shared[:, 2 * MAIN_WIDTH:]
    forget = (jnp.pad(w_f, ((0, 0), (0, LANES - MAIN_HEADS))),
              jnp.pad(w_f.T, ((0, 2 * SUBLANES - MAIN_HEADS), (0, 0))),
              jnp.pad(b_forget, (0, LANES - MAIN_HEADS))[None, :],
              jnp.pad(b_forget, (0, 2 * SUBLANES - MAIN_HEADS))[:, None])
    common = dict(norm1_gain=norm1_gain, norm2_gain=norm2_gain, w_in_a=w_in_a, v_gain_a=v_gain_a, w_in_b=w_in_b,
                  q_gain_b=q_gain_b, kv_norm_gain=kv_norm_gain[None, :], w_kv_shared=w_kv_shared,
                  k_gain_shared=k_gain_shared[None, :], mem_q_gain=mem_q_gain, w_out=w_out,
                  w_router=w_router, b_router=b_router, w_gate=w_gate, w_up=w_up, w_down=w_down, forget=forget)

    t_p = batch * seq
    mem_tok = mem_prompt.reshape(batch * N_MEM, D_MODEL)
    mem_k_layers, mem_v_layers = [], []
    for l in range(depth):
        mk, mv = _norm_matmul(mem_tok, mem_norm_gain[l:l + 1], w_mem_kv, l,
                              [(MEM_WIDTH, "headnorm", 0), (MEM_WIDTH, "plain", 0)], mem_k_gain[l:l + 1])
        mem_k_layers.append(mk.reshape(batch, N_MEM, MEM_WIDTH))
        mem_v_layers.append(mv.reshape(batch, N_MEM, MEM_WIDTH))
    mem_k_p = jnp.stack(mem_k_layers)
    mem_v_p = jnp.stack(mem_v_layers)

    def fox_prompt(q, k, v, lf, lft):
        ct, c = _fox_cumsum(lft, lf, batch, seq)
        return _fox_prompt(q, k, v, ct, c, batch, seq)

    w_p = dict(common, w_mix=w_s_a[0], b_mix=b_s_a[0].T, gmlp_rows=CHUNK, gmlp_grp=CHUNK)
    y_p, k_p, v_p, lf_p, vrows_p = _trunk(x_prompt.reshape(t_p, D_MODEL), batch, seq, seq,
                                          mem_k_p, mem_v_p, fox_prompt, w_p, None)

    rows = SAMPLE_ROWS
    t_s = dec_batch * rows
    x_s = jnp.pad(x_sample, ((0, 0), (0, rows - dec_seq), (0, 0))).reshape(t_s, D_MODEL)
    pt_flat = page_table.reshape(dec_batch * n_pages)
    bias = _fox_past_bias(pt_flat, jnp.transpose(cache_logf, (2, 0, 1)), dec_batch, n_pages)
    cache_k_t = jnp.transpose(cache_k, (0, 2, 1, 3))
    cache_v_t = jnp.transpose(cache_v, (0, 2, 1, 3))

    def fox_sample(q, k, v, lf, lft):
        return _fox_sample(pt_flat, q, k, v, lf, bias, cache_k_t, cache_v_t, dec_batch, n_pages, dec_seq)

    w_s = dict(common, w_mix=jnp.tile(w_s_a[0][:, :rows, :rows], (1, dec_batch, dec_batch)),
               b_mix=jnp.tile(b_s_a[0][:, :rows].T, (dec_batch, 1)), gmlp_rows=t_s, gmlp_grp=rows)
    mem_k_s = cache_mem_k.reshape(depth, dec_batch, N_MEM, MEM_WIDTH)
    mem_v_s = cache_mem_v.reshape(depth, dec_batch, N_MEM, MEM_WIDTH)
    y_s, k_s, v_s, lf_s, vrows_s = _trunk(x_s, dec_batch, rows, dec_seq, mem_k_s, mem_v_s, fox_sample, w_s, HIGHEST)

    def unpad(a, *tail):
        return a.reshape((dec_batch, rows) + tail)[:, :dec_seq]

    return (y_p.reshape(batch, seq, D_MODEL),
            unpad(y_s, D_MODEL),
            k_p.reshape(batch, seq, MAIN_HEADS, HEAD_DIM),
            v_p.reshape(batch, seq, MAIN_HEADS, HEAD_DIM),
            lf_p[:, :MAIN_HEADS].reshape(batch, seq, MAIN_HEADS),
            unpad(k_s, MAIN_HEADS, HEAD_DIM),
            unpad(v_s, MAIN_HEADS, HEAD_DIM),
            unpad(lf_s[:, :MAIN_HEADS], MAIN_HEADS),
            mem_k_p.reshape(depth, batch, N_MEM, MEM_HEADS, HEAD_DIM),
            mem_v_p.reshape(depth, batch, N_MEM, MEM_HEADS, HEAD_DIM),
            vrows_p.reshape(batch, seq, MAIN_WIDTH)[:, -CHUNK:][None],
            unpad(vrows_s, MAIN_WIDTH)[None])
```

```python
import functools

import jax
import jax.numpy as jnp
from jax import lax
from jax.experimental import pallas as pl
from jax.experimental.pallas import tpu as pltpu

D_MODEL = 2048
HEAD_DIM = 128
MAIN_HEADS = 12
MEM_HEADS = 4
MAIN_WIDTH = MAIN_HEADS * HEAD_DIM
MEM_WIDTH = MEM_HEADS * HEAD_DIM
N_MEM = 256
N_GROUPS = 4
EXPERTS_PER_GROUP = 4
N_EXPERTS = N_GROUPS * EXPERTS_PER_GROUP
D_EXPERT = 512
CHUNK = 128
PAGE_SIZE = 128
EPS = 1e-6
ATTN_SCALE = HEAD_DIM ** -0.5

LANES = 128
SUBLANES = 8
VMEM_LIMIT_BYTES = 56 * 1024 * 1024
SAMPLE_ROWS = SUBLANES
N_CHUNKS = D_MODEL // LANES
ROW_PITCH = 24

F32 = jnp.float32
_NT = (((1,), (1,)), ((), ()))


def _params(n_axes):
    return pltpu.CompilerParams(dimension_semantics=("arbitrary",) * n_axes,
                                vmem_limit_bytes=VMEM_LIMIT_BYTES)


HIGHEST = lax.Precision.HIGHEST


def _dot(a, b, prec=None):
    return jnp.dot(a, b, preferred_element_type=F32, precision=prec)


def _dot_nt(a, b, prec=None):
    return lax.dot_general(a, b, _NT, preferred_element_type=F32, precision=prec)


def _split_hi_lo(x):
    hi = x.astype(jnp.bfloat16).astype(F32)
    return hi, x - hi


def _split3(x):
    hi = x.astype(jnp.bfloat16).astype(F32)
    r = x - hi
    mid = r.astype(jnp.bfloat16).astype(F32)
    lo = (r - mid).astype(jnp.bfloat16).astype(F32)
    return hi, mid, lo


def _dot3_right(a_exact, x):
    hi, mid, lo = _split3(x)
    return _dot(a_exact, hi) + _dot(a_exact, mid) + _dot(a_exact, lo)


def _dot3_left(x, b_exact):
    hi, mid, lo = _split3(x)
    return _dot(hi, b_exact) + _dot(mid, b_exact) + _dot(lo, b_exact)


def _store_chunk_rows(ref, x):
    rows = x.shape[0]
    for c in range(N_CHUNKS):
        ref[pl.ds(c, rows, stride=N_CHUNKS), :] = x[:, c * LANES:(c + 1) * LANES]


def _load_gathered_rows(ref, rows):
    return jnp.concatenate([ref[pl.ds(c, rows, stride=ROW_PITCH), :] for c in range(N_CHUNKS)], axis=1)


def _head_norm(z, gain_row):
    return z * lax.rsqrt(jnp.mean(z * z, axis=-1, keepdims=True) + EPS) * gain_row


def _norm_matmul_kernel(*refs, segs, tn, with_forget, prec):
    it = iter(refs)
    x_ref, g_ref, w_ref, hg_ref = next(it), next(it), next(it), next(it)
    if with_forget:
        wf_ref, wft_ref, bfr_ref, bfc_ref = next(it), next(it), next(it), next(it)
    out_refs = [next(it) for _ in segs]
    if with_forget:
        lf_ref, lft_ref = next(it), next(it)
    xn_ref = next(it)
    j = pl.program_id(1)

    def project(xn, o_ref, kind, grow):
        z = _dot(xn, w_ref[...], prec)
        if kind == "plain":
            o_ref[...] = z
        elif kind == "gelu":
            o_ref[...] = jax.nn.gelu(z)
        else:
            gain = hg_ref[grow:grow + 1, :]
            for c in range(tn // HEAD_DIM):
                sl = slice(c * HEAD_DIM, (c + 1) * HEAD_DIM)
                o_ref[:, sl] = _head_norm(z[:, sl], gain)

    @pl.when(j == 0)
    def _first():
        x = x_ref[...]
        xn = x * lax.rsqrt(jnp.mean(x * x, axis=-1, keepdims=True) + EPS) * g_ref[...]
        xn_ref[...] = xn
        if with_forget:
            lf_ref[...] = jax.nn.log_sigmoid(_dot(xn, wf_ref[...], prec) + bfr_ref[...])
            lft_ref[...] = jax.nn.log_sigmoid(_dot_nt(wft_ref[...], xn, prec) + bfc_ref[...])
        project(xn, out_refs[0], segs[0][1], segs[0][2])

    lo = 0
    for (ncols, kind, grow), o_ref in zip(segs, out_refs):
        nt = ncols // tn

        @pl.when((j >= max(lo, 1)) & (j < lo + nt))
        def _rest(o_ref=o_ref, kind=kind, grow=grow):
            project(xn_ref[...], o_ref, kind, grow)

        lo += nt


def _norm_matmul(x, gain, w, layer, segs, head_gains, forget=None, prec=None):
    t = x.shape[0]
    tm = min(t, 1024)
    tn = 512
    n_tiles = sum(s[0] for s in segs) // tn
    with_forget = forget is not None
    in_specs = [
        pl.BlockSpec((tm, D_MODEL), lambda i, j: (i, 0)),
        pl.BlockSpec((1, D_MODEL), lambda i, j: (0, 0)),
        pl.BlockSpec((None, D_MODEL, tn), lambda i, j: (layer, 0, j)) if w.ndim == 3
        else pl.BlockSpec((D_MODEL, tn), lambda i, j: (0, j)),
        pl.BlockSpec(head_gains.shape, lambda i, j: (0, 0)),
    ]
    args = [x, gain, w, head_gains]
    if with_forget:
        wf, wft, bfr, bfc = forget
        in_specs += [pl.BlockSpec(wf.shape, lambda i, j: (0, 0)), pl.BlockSpec(wft.shape, lambda i, j: (0, 0)),
                     pl.BlockSpec(bfr.shape, lambda i, j: (0, 0)), pl.BlockSpec(bfc.shape, lambda i, j: (0, 0))]
        args += [wf, wft, bfr, bfc]
    out_shape, out_specs = [], []
    lo = 0
    for ncols, _, _ in segs:
        nt = ncols // tn
        out_shape.append(jax.ShapeDtypeStruct((t, ncols), F32))
        out_specs.append(pl.BlockSpec((tm, tn), lambda i, j, lo=lo, nt=nt: (i, jnp.clip(j - lo, 0, nt - 1))))
        lo += nt
    if with_forget:
        out_shape += [jax.ShapeDtypeStruct((t, LANES), F32), jax.ShapeDtypeStruct((2 * SUBLANES, t), F32)]
        out_specs += [pl.BlockSpec((tm, LANES), lambda i, j: (i, 0)),
                      pl.BlockSpec((2 * SUBLANES, tm), lambda i, j: (0, i))]
    return pl.pallas_call(
        functools.partial(_norm_matmul_kernel, segs=tuple(segs), tn=tn, with_forget=with_forget, prec=prec),
        grid=(t // tm, n_tiles),
        in_specs=in_specs,
        out_specs=out_specs,
        out_shape=out_shape,
        scratch_shapes=[pltpu.VMEM((tm, D_MODEL), F32)],
        compiler_params=_params(2),
    )(*args)


def _gmlp_kernel(u_ref, v_ref, vg_ref, w_ref, b_ref, y_ref, vn_ref, *, rows, grp, valid, prec):
    v = v_ref[...]
    vn = v * lax.rsqrt(jnp.mean(v * v, axis=-1, keepdims=True) + EPS) * vg_ref[...]
    vn_ref[...] = vn
    r = lax.broadcasted_iota(jnp.int32, (rows, rows), 0)
    c = lax.broadcasted_iota(jnp.int32, (rows, rows), 1)
    shift = grp.bit_length() - 1
    allowed = ((c & (grp - 1)) <= (r & (grp - 1))) & ((c & (grp - 1)) < valid)
    if grp < rows:
        allowed = allowed & ((r >> shift) == (c >> shift))
    for g in range(MAIN_HEADS):
        sl = slice(g * HEAD_DIM, (g + 1) * HEAD_DIM)
        w = jnp.where(allowed, w_ref[g], 0.0)
        mixed = _dot(w, vn[:, sl], prec) + b_ref[:, g:g + 1]
        y_ref[:, sl] = u_ref[:, sl] * mixed


def _gmlp(u, v, v_gain, w_mix, b_mix, rows, grp, valid, prec):
    t = u.shape[0]
    return pl.pallas_call(
        functools.partial(_gmlp_kernel, rows=rows, grp=grp, valid=valid, prec=prec),
        grid=(t // rows,),
        in_specs=[
            pl.BlockSpec((rows, MAIN_WIDTH), lambda i: (i, 0)),
            pl.BlockSpec((rows, MAIN_WIDTH), lambda i: (i, 0)),
            pl.BlockSpec((1, MAIN_WIDTH), lambda i: (0, 0)),
            pl.BlockSpec((MAIN_HEADS, rows, rows), lambda i: (0, 0, 0)),
            pl.BlockSpec((rows, MAIN_HEADS), lambda i: (0, 0)),
        ],
        out_specs=[pl.BlockSpec((rows, MAIN_WIDTH), lambda i: (i, 0)),
                   pl.BlockSpec((rows, MAIN_WIDTH), lambda i: (i, 0))],
        out_shape=[jax.ShapeDtypeStruct((t, MAIN_WIDTH), F32), jax.ShapeDtypeStruct((t, MAIN_WIDTH), F32)],
        compiler_params=_params(1),
    )(u, v, v_gain, w_mix, b_mix)


def _mem_attn_kernel(q_ref, k_ref, v_ref, o_ref, *, prec):
    for h in range(MEM_HEADS):
        sl = slice(h * HEAD_DIM, (h + 1) * HEAD_DIM)
        s = _dot_nt(q_ref[:, sl], k_ref[:, sl], prec) * ATTN_SCALE
        e = jnp.exp(s - jnp.max(s, axis=-1, keepdims=True))
        p = e / jnp.sum(e, axis=-1, keepdims=True)
        o_ref[:, sl] = _dot(p, v_ref[:, sl], prec)


def _mem_attn(q, k, v, layer, nb, rows_per_batch, prec):
    tq = min(rows_per_batch, 1024)
    nq = rows_per_batch // tq
    kv_spec = pl.BlockSpec((None, None, N_MEM, MEM_WIDTH), lambda b, i: (layer, b, 0, 0))
    return pl.pallas_call(
        functools.partial(_mem_attn_kernel, prec=prec),
        grid=(nb, nq),
        in_specs=[pl.BlockSpec((tq, MEM_WIDTH), lambda b, i: (b * nq + i, 0)), kv_spec, kv_spec],
        out_specs=pl.BlockSpec((tq, MEM_WIDTH), lambda b, i: (b * nq + i, 0)),
        out_shape=jax.ShapeDtypeStruct(q.shape, F32),
        compiler_params=_params(2),
    )(q, k, v)


def _out_proj_kernel(ym_ref, ymem_ref, w1_ref, w2_ref, h_ref, o_ref, *, prec):
    o_ref[...] = h_ref[...] + _dot(ym_ref[...], w1_ref[...], prec) + _dot(ymem_ref[...], w2_ref[...], prec)


def _out_proj(y_main, y_mem, w_out, layer, h, prec):
    t = h.shape[0]
    tm = min(t, 1024)
    tn = 512
    return pl.pallas_call(
        functools.partial(_out_proj_kernel, prec=prec),
        grid=(t // tm, D_MODEL // tn),
        in_specs=[
            pl.BlockSpec((tm, MAIN_WIDTH), lambda i, j: (i, 0)),
            pl.BlockSpec((tm, MEM_WIDTH), lambda i, j: (i, 0)),
            pl.BlockSpec((None, MAIN_WIDTH, tn), lambda i, j: (layer, 0, j)),
            pl.BlockSpec((None, MEM_WIDTH, tn), lambda i, j: (layer, MAIN_WIDTH // MEM_WIDTH, j)),
            pl.BlockSpec((tm, tn), lambda i, j: (i, j)),
        ],
        out_specs=pl.BlockSpec((tm, tn), lambda i, j: (i, j)),
        out_shape=jax.ShapeDtypeStruct((t, D_MODEL), F32),
        compiler_params=_params(2),
    )(y_main, y_mem, w_out, w_out, h)


def _router_kernel(h_ref, g_ref, wr_ref, br_ref, x2_ref, idx_ref, gate_ref, *, prec):
    x = h_ref[...]
    xn = x * lax.rsqrt(jnp.mean(x * x, axis=-1, keepdims=True) + EPS) * g_ref[...]
    _store_chunk_rows(x2_ref, xn)
    lg = _dot(xn, wr_ref[...], prec) + br_ref[...]
    lane = lax.broadcasted_iota(jnp.int32, lg.shape, 1)
    neg = -jnp.inf
    is_grp = lane < N_GROUPS
    gl = jnp.where(is_grp, lg, neg)
    ge = jnp.where(is_grp, jnp.exp(gl - jnp.max(gl, axis=-1, keepdims=True)), 0.0)
    pg = ge / jnp.sum(ge, axis=-1, keepdims=True)
    p_top = jnp.max(pg, axis=-1, keepdims=True)
    g_idx = jnp.min(jnp.where(is_grp & (pg == p_top), lane, LANES), axis=-1, keepdims=True)
    first = N_GROUPS + EXPERTS_PER_GROUP * g_idx
    in_grp = (lane >= first) & (lane < first + EXPERTS_PER_GROUP)
    e1 = jnp.max(jnp.where(in_grp, lg, neg), axis=-1, keepdims=True)
    i1 = jnp.min(jnp.where(in_grp & (lg == e1), lane, LANES), axis=-1, keepdims=True)
    rest = in_grp & (lane != i1)
    e2 = jnp.max(jnp.where(rest, lg, neg), axis=-1, keepdims=True)
    i2 = jnp.min(jnp.where(rest & (lg == e2), lane, LANES), axis=-1, keepdims=True)
    t2 = jnp.exp(e2 - e1)
    den = 1.0 + t2
    idx_ref[...] = jnp.where(lane == 0, i1 - N_GROUPS, jnp.where(lane == 1, i2 - N_GROUPS, 0))
    gate_ref[...] = jnp.where(lane == 0, p_top * (1.0 / den), jnp.where(lane == 1, p_top * (t2 / den), 0.0))


def _router(h, gain, wr, br, prec):
    t = h.shape[0]
    tm = min(t, 512)
    return pl.pallas_call(
        functools.partial(_router_kernel, prec=prec),
        grid=(t // tm,),
        in_specs=[pl.BlockSpec((tm, D_MODEL), lambda i: (i, 0)), pl.BlockSpec((1, D_MODEL), lambda i: (0, 0)),
                  pl.BlockSpec((D_MODEL, LANES), lambda i: (0, 0)), pl.BlockSpec((1, LANES), lambda i: (0, 0))],
        out_specs=[pl.BlockSpec((tm * N_CHUNKS, LANES), lambda i: (i, 0)), pl.BlockSpec((tm, LANES), lambda i: (i, 0)),
                   pl.BlockSpec((tm, LANES), lambda i: (i, 0))],
        out_shape=[jax.ShapeDtypeStruct((t * N_CHUNKS, LANES), F32), jax.ShapeDtypeStruct((t, LANES), jnp.int32),
                   jax.ShapeDtypeStruct((t, LANES), F32)],
        compiler_params=_params(1),
    )(h, gain, wr, br)


def _plan_kernel(e_ref, pos_ref, te_ref, nv_ref, rank_ref, *, t2, blk, tm):
    shift = tm.bit_length() - 1
    nblk = t2 // blk
    r = lax.broadcasted_iota(jnp.int32, (blk, blk), 0)
    c = lax.broadcasted_iota(jnp.int32, (blk, blk), 1)
    upper = (r <= c).astype(F32)
    sub = lax.broadcasted_iota(jnp.int32, (N_EXPERTS, blk), 0)
    carry = jnp.zeros((N_EXPERTS, 1), F32)
    for b in range(nblk):
        sl = slice(b * blk, (b + 1) * blk)
        oh = (sub == e_ref[:, sl]).astype(F32)
        cs = _dot(oh, upper) + carry
        rank_ref[:, sl] = jnp.sum(oh * (cs - 1.0), axis=0, keepdims=True)
        carry = cs[:, blk - 1:blk]
    counts = carry.astype(jnp.int32)
    padc = ((counts + (tm - 1)) >> shift) << shift
    sub1 = lax.broadcasted_iota(jnp.int32, (N_EXPERTS, 1), 0)
    off = jnp.zeros((N_EXPERTS, 1), jnp.int32)
    run = jnp.zeros((1, 1), jnp.int32)
    for e in range(N_EXPERTS):
        off = jnp.where(sub1 == e, run, off)
        run = run + padc[e:e + 1, :]
    ends = off + padc
    for b in range(nblk):
        sl = slice(b * blk, (b + 1) * blk)
        offv = jnp.sum(jnp.where(sub == e_ref[:, sl], off, 0), axis=0, keepdims=True)
        pos_ref[:, sl] = rank_ref[:, sl].astype(jnp.int32) + offv
    lane = lax.broadcasted_iota(jnp.int32, (N_EXPERTS, LANES), 1)
    lane1 = lax.broadcasted_iota(jnp.int32, (1, LANES), 1)
    nv = run >> shift
    te = jnp.sum((ends <= lane * tm).astype(jnp.int32), axis=0, keepdims=True)
    te_last = jnp.sum((ends <= (nv - 1) * tm).astype(jnp.int32), axis=0, keepdims=True)
    te_ref[...] = jnp.minimum(jnp.where(lane1 >= nv, te_last, te), N_EXPERTS - 1)
    nv_ref[...] = jnp.broadcast_to(nv, (1, LANES))


def _plan(e_flat, tm):
    t2 = e_flat.shape[1]
    blk = min(t2, 512)
    return pl.pallas_call(
        functools.partial(_plan_kernel, t2=t2, blk=blk, tm=tm),
        out_shape=[jax.ShapeDtypeStruct((1, t2), jnp.int32), jax.ShapeDtypeStruct((1, LANES), jnp.int32),
                   jax.ShapeDtypeStruct((1, LANES), jnp.int32)],
        scratch_shapes=[pltpu.VMEM((1, t2), F32)],
        compiler_params=pltpu.CompilerParams(vmem_limit_bytes=VMEM_LIMIT_BYTES),
    )(e_flat)


def _invert_kernel(pos_ref, src_ref, *, t, t2, n_rows):
    def zero(p, carry):
        src_ref[p] = 0
        return carry

    lax.fori_loop(0, n_rows, zero, 0, unroll=8)

    def put(i, carry):
        src_ref[pos_ref[i]] = N_CHUNKS * jnp.where(i >= t, i - t, i)
        return carry

    lax.fori_loop(0, t2, put, 0, unroll=8)


def _invert(pos, t, n_rows):
    return pl.pallas_call(
        functools.partial(_invert_kernel, t=t, t2=pos.shape[0], n_rows=n_rows),
        grid_spec=pltpu.PrefetchScalarGridSpec(
            num_scalar_prefetch=1,
            grid=(1,),
            in_specs=[],
            out_specs=pl.BlockSpec(memory_space=pltpu.SMEM),
        ),
        out_shape=jax.ShapeDtypeStruct((n_rows,), jnp.int32),
        compiler_params=_params(1),
    )(pos)


def _moe_ffn_kernel(te_ref, nv_ref, src_ref, x_ref, wg_ref, wu_ref, wd_ref, y_ref, xbuf_ref, sems, *, tm, prec):
    i = pl.program_id(0)
    nv = nv_ref[0]
    slot = i % 2

    def start_tile(tile, dst_slot):
        base = tile * tm
        for r in range(tm):
            src = pl.multiple_of(src_ref[base + r], N_CHUNKS)
            pltpu.make_async_copy(x_ref.at[pl.ds(src, N_CHUNKS)],
                                  xbuf_ref.at[dst_slot, pl.ds(r * ROW_PITCH, N_CHUNKS)], sems.at[dst_slot]).start()

    def wait_tile():
        done = xbuf_ref.at[slot, pl.ds(0, tm * N_CHUNKS)]
        pltpu.make_async_copy(done, done, sems.at[slot]).wait()

    def ffn():
        x = _load_gathered_rows(xbuf_ref.at[slot], tm)
        hdn = jax.nn.silu(_dot(x, wg_ref[...], prec)) * _dot(x, wu_ref[...], prec)
        _store_chunk_rows(y_ref, _dot(hdn, wd_ref[...], prec))

    @pl.when(i == 0)
    def _():
        start_tile(0, 0)

    @pl.when(i + 1 < nv)
    def _():
        wait_tile()
        start_tile(i + 1, 1 - slot)
        ffn()

    @pl.when(i + 1 == nv)
    def _():
        wait_tile()
        ffn()

    @pl.when(i >= nv)
    def _():
        y_ref[...] = jnp.zeros_like(y_ref)


def _moe_ffn(te, nv, src, x2, w_gate, w_up, w_down, layer, tm, prec):
    n_rows = src.shape[0]
    n_tiles = n_rows // tm
    w_in_spec = pl.BlockSpec((None, None, D_MODEL, D_EXPERT), lambda i, te, nv, src: (layer, te[i], 0, 0))
    return pl.pallas_call(
        functools.partial(_moe_ffn_kernel, tm=tm, prec=prec),
        grid_spec=pltpu.PrefetchScalarGridSpec(
            num_scalar_prefetch=3,
            grid=(n_tiles,),
            in_specs=[
                pl.BlockSpec(memory_space=pl.ANY),
                w_in_spec,
                w_in_spec,
                pl.BlockSpec((None, None, D_EXPERT, D_MODEL), lambda i, te, nv, src: (layer, te[i], 0, 0)),
            ],
            out_specs=pl.BlockSpec((tm * N_CHUNKS, LANES), lambda i, te, nv, src: (i, 0)),
            scratch_shapes=[pltpu.VMEM((2, tm * ROW_PITCH, LANES), F32), pltpu.SemaphoreType.DMA((2,))],
        ),
        out_shape=jax.ShapeDtypeStruct((n_rows * N_CHUNKS, LANES), F32),
        compiler_params=_params(1),
    )(te, nv, src, x2, w_gate, w_up, w_down)


def _combine_kernel(pos_ref, h_ref, g_ref, y_ref, o_ref, y0_ref, y1_ref, sems, *, t, tm):
    base = pl.program_id(0) * tm
    for k, buf in enumerate((y0_ref, y1_ref)):
        for r in range(tm):
            src = pl.multiple_of(pos_ref[k * t + base + r] * N_CHUNKS, N_CHUNKS)
            pltpu.make_async_copy(y_ref.at[pl.ds(src, N_CHUNKS)], buf.at[pl.ds(r * ROW_PITCH, N_CHUNKS)],
                                  sems.at[k]).start()
    for k, buf in enumerate((y0_ref, y1_ref)):
        done = buf.at[pl.ds(0, tm * N_CHUNKS)]
        pltpu.make_async_copy(done, done, sems.at[k]).wait()
    g = g_ref[...]
    o_ref[...] = (h_ref[...] + g[:, 0:1] * _load_gathered_rows(y0_ref, tm)
                  + g[:, 1:2] * _load_gathered_rows(y1_ref, tm))


def _combine(pos, h, gates, y):
    t = h.shape[0]
    tm = min(t, 256)
    return pl.pallas_call(
        functools.partial(_combine_kernel, t=t, tm=tm),
        grid_spec=pltpu.PrefetchScalarGridSpec(
            num_scalar_prefetch=1,
            grid=(t // tm,),
            in_specs=[pl.BlockSpec((tm, D_MODEL), lambda i, pos: (i, 0)),
                      pl.BlockSpec((tm, LANES), lambda i, pos: (i, 0)),
                      pl.BlockSpec(memory_space=pl.ANY)],
            out_specs=pl.BlockSpec((tm, D_MODEL), lambda i, pos: (i, 0)),
            scratch_shapes=[pltpu.VMEM((tm * ROW_PITCH, LANES), F32), pltpu.VMEM((tm * ROW_PITCH, LANES), F32),
                            pltpu.SemaphoreType.DMA((2,))],
        ),
        out_shape=jax.ShapeDtypeStruct((t, D_MODEL), F32),
        compiler_params=_params(1),
    )(pos, h, gates, y)


def _hier_moe(h, gain, wr, br, w_gate, w_up, w_down, layer, prec):
    t = h.shape[0]
    tm = 256 if t >= 1024 else 16
    n_tiles = (2 * t) // tm + N_EXPERTS
    assert n_tiles <= LANES
    x2, idx, gates = _router(h, gain, wr, br, prec)
    e_flat = idx[:, :2].T.reshape(1, 2 * t)
    pos, te, nv = _plan(e_flat, tm)
    pos = pos.reshape(2 * t)
    te = te[0, :n_tiles]
    nv = nv[0, :1]
    src = _invert(pos, t, n_tiles * tm)
    y = _moe_ffn(te, nv, src, x2, w_gate, w_up, w_down, layer, tm, prec)
    return _combine(pos, h, gates, y)


def _fox_cumsum_kernel(lft_ref, lf_ref, ct_ref, c_ref, *, s, blk):
    r = lax.broadcasted_iota(jnp.int32, (blk, blk), 0)
    c = lax.broadcasted_iota(jnp.int32, (blk, blk), 1)
    upper = (r <= c).astype(F32)
    lower = (r >= c).astype(F32)
    carry_t = jnp.zeros((2 * SUBLANES, 1), F32)
    carry = jnp.zeros((1, LANES), F32)
    for b in range(s // blk):
        sl = slice(b * blk, (b + 1) * blk)
        ct = _dot3_left(lft_ref[:, sl], upper) + carry_t
        ct_ref[:, sl] = ct
        carry_t = ct[:, blk - 1:blk]
        cc = _dot3_right(lower, lf_ref[sl, :]) + carry
        c_ref[sl, :] = cc
        carry = cc[blk - 1:blk, :]


def _fox_cumsum(lft, lf, nb, s):
    t = lf.shape[0]
    blk = min(s, 256)
    return pl.pallas_call(
        functools.partial(_fox_cumsum_kernel, s=s, blk=blk),
        grid=(nb,),
        in_specs=[pl.BlockSpec((2 * SUBLANES, s), lambda b: (0, b)), pl.BlockSpec((s, LANES), lambda b: (b, 0))],
        out_specs=[pl.BlockSpec((2 * SUBLANES, s), lambda b: (0, b)), pl.BlockSpec((s, LANES), lambda b: (b, 0))],
        out_shape=[jax.ShapeDtypeStruct((2 * SUBLANES, t), F32), jax.ShapeDtypeStruct((t, LANES), F32)],
        compiler_params=_params(1),
    )(lft, lf)


def _fox_prompt_kernel(q_ref, k_ref, v_ref, ct_ref, c_ref, o_ref, *, s, tq):
    h = pl.program_id(1)
    lane = lax.broadcasted_iota(jnp.int32, (s, LANES), 1)
    c_col = jnp.sum(jnp.where(lane == h, c_ref[...], 0.0), axis=1, keepdims=True)
    c_row = ct_ref[pl.ds(h, 1), :]
    ri = lax.broadcasted_iota(jnp.int32, (tq, tq), 0)
    ci = lax.broadcasted_iota(jnp.int32, (tq, tq), 1)
    causal = ci <= ri
    for qi in range(s // tq):
        qs = slice(qi * tq, (qi + 1) * tq)
        q = q_ref[qs, :]
        m = jnp.full((tq, 1), -jnp.inf, F32)
        l = jnp.zeros((tq, 1), F32)
        acc = jnp.zeros((tq, HEAD_DIM), F32)
        for kj in range(qi + 1):
            ks = slice(kj * tq, (kj + 1) * tq)
            sc = lax.dot_general(q, k_ref[ks, :], _NT, preferred_element_type=F32) * ATTN_SCALE
            sc = sc + c_col[qs, :] - c_row[:, ks]
            if kj == qi:
                sc = jnp.where(causal, sc, -jnp.inf)
            m_new = jnp.maximum(m, jnp.max(sc, axis=-1, keepdims=True))
            alpha = jnp.exp(m - m_new)
            p = jnp.exp(sc - m_new)
            l = alpha * l + jnp.sum(p, axis=-1, keepdims=True)
            acc = alpha * acc + _dot(p, v_ref[ks, :])
            m = m_new
        o_ref[qs, :] = acc / l


def _fox_prompt(q, k, v, ct, c, nb, s):
    t = q.shape[0]
    tq = min(s, 512)
    head_spec = pl.BlockSpec((s, HEAD_DIM), lambda b, h: (b, h))
    return pl.pallas_call(
        functools.partial(_fox_prompt_kernel, s=s, tq=tq),
        grid=(nb, MAIN_HEADS),
        in_specs=[head_spec, head_spec, head_spec,
                  pl.BlockSpec((2 * SUBLANES, s), lambda b, h: (0, b)),
                  pl.BlockSpec((s, LANES), lambda b, h: (b, 0))],
        out_specs=head_spec,
        out_shape=jax.ShapeDtypeStruct((t, MAIN_WIDTH), F32),
        compiler_params=_params(2),
    )(q, k, v, ct, c)


def _fox_past_bias_kernel(pt_ref, lfc_ref, o_ref, buf_ref, sem, *, n_pages):
    b = pl.program_id(0)

    def page_copy(p):
        page = pt_ref[b * n_pages + p]
        return pltpu.make_async_copy(lfc_ref.at[:, pl.ds(page, 1), :], buf_ref.at[:, pl.ds(p, 1), :], sem)

    def start(p, carry):
        page_copy(p).start()
        return carry

    def wait(p, carry):
        page_copy(p).wait()
        return carry

    lax.fori_loop(0, n_pages, start, 0)
    r = lax.broadcasted_iota(jnp.int32, (PAGE_SIZE, PAGE_SIZE), 0)
    c = lax.broadcasted_iota(jnp.int32, (PAGE_SIZE, PAGE_SIZE), 1)
    after_in_page = (r > c).astype(F32)
    pr = lax.broadcasted_iota(jnp.int32, (n_pages, n_pages), 0)
    pc = lax.broadcasted_iota(jnp.int32, (n_pages, n_pages), 1)
    later_pages = (pc > pr).astype(F32)
    lax.fori_loop(0, n_pages, wait, 0)
    for h in range(MAIN_HEADS):
        lp = buf_ref[h]
        tot = jnp.broadcast_to(jnp.sum(lp, axis=1, keepdims=True), lp.shape)
        o_ref[h] = _dot3_left(lp, after_in_page) + _dot3_right(later_pages, tot)


def _fox_past_bias(page_table_flat, cache_logf_t, nb, n_pages):
    return pl.pallas_call(
        functools.partial(_fox_past_bias_kernel, n_pages=n_pages),
        grid_spec=pltpu.PrefetchScalarGridSpec(
            num_scalar_prefetch=1,
            grid=(nb,),
            in_specs=[pl.BlockSpec(memory_space=pl.ANY)],
            out_specs=pl.BlockSpec((None, MAIN_HEADS, n_pages, PAGE_SIZE), lambda b, pt: (b, 0, 0, 0)),
            scratch_shapes=[pltpu.VMEM((MAIN_HEADS, n_pages, PAGE_SIZE), F32), pltpu.SemaphoreType.DMA(())],
        ),
        out_shape=jax.ShapeDtypeStruct((nb, MAIN_HEADS, n_pages, PAGE_SIZE), F32),
        compiler_params=_params(1),
    )(page_table_flat, cache_logf_t)


def _fox_sample_kernel(*refs, ppb, n_steps, valid):
    pt_ref, q_ref, kn_ref, vn_ref, lf_ref, bias_ref = refs[:6]
    k_refs = refs[6:6 + ppb]
    v_refs = refs[6 + ppb:6 + 2 * ppb]
    o_ref, m_ref, l_ref, acc_ref = refs[6 + 2 * ppb:]
    g = pl.program_id(1)
    rows = SAMPLE_ROWS

    @pl.when(g == 0)
    def _():
        m_ref[...] = jnp.full(m_ref.shape, -jnp.inf, F32)
        l_ref[...] = jnp.zeros(l_ref.shape, F32)
        acc_ref[...] = jnp.zeros(acc_ref.shape, F32)

    r8 = lax.broadcasted_iota(jnp.int32, (rows, rows), 0)
    c8 = lax.broadcasted_iota(jnp.int32, (rows, rows), 1)
    new_mask = (c8 <= r8) & (c8 < valid)
    c_new = _dot3_right(new_mask.astype(F32), lf_ref[...])

    def qk3(q_hi, q_lo, k):
        k_hi, k_lo = _split_hi_lo(k)
        a = _dot_nt(jnp.concatenate([q_hi, q_lo], axis=0), k_hi)
        return a[:rows] + a[rows:] + _dot_nt(q_hi, k_lo)

    def pv3(p, v):
        p_hi, p_lo = _split_hi_lo(p)
        v_hi, v_lo = _split_hi_lo(v)
        a = _dot(jnp.concatenate([p_hi, p_lo], axis=0), v_hi)
        return a[:rows] + a[rows:] + _dot(p_hi, v_lo)

    all_scores = []
    for h in range(MAIN_HEADS):
        q_hi, q_lo = _split_hi_lo(q_ref[:, h * HEAD_DIM:(h + 1) * HEAD_DIM])
        cq = c_new[:, h:h + 1]
        all_scores.append([qk3(q_hi, q_lo, k_refs[u][h]) * ATTN_SCALE + cq + bias_ref[h, u:u + 1, :]
                           for u in range(ppb)])
    all_probs, all_alpha = [], []
    for h in range(MAIN_HEADS):
        m_old = m_ref[h, :, 0:1]
        m_new = m_old
        for sc in all_scores[h]:
            m_new = jnp.maximum(m_new, jnp.max(sc, axis=-1, keepdims=True))
        alpha = jnp.exp(m_old - m_new)
        probs = [jnp.exp(sc - m_new) for sc in all_scores[h]]
        l_new = alpha * l_ref[h, :, 0:1]
        for p in probs:
            l_new = l_new + jnp.sum(p, axis=-1, keepdims=True)
        m_ref[h] = jnp.broadcast_to(m_new, (rows, LANES))
        l_ref[h] = jnp.broadcast_to(l_new, (rows, LANES))
        all_probs.append(probs)
        all_alpha.append(alpha)
    for h in range(MAIN_HEADS):
        sl = slice(h * HEAD_DIM, (h + 1) * HEAD_DIM)
        acc = all_alpha[h] * acc_ref[:, sl]
        for u in range(ppb):
            acc = acc + pv3(all_probs[h][u], v_refs[u][h])
        acc_ref[:, sl] = acc

    @pl.when(g == n_steps - 1)
    def _():
        for h in range(MAIN_HEADS):
            sl = slice(h * HEAD_DIM, (h + 1) * HEAD_DIM)
            qh = q_ref[:, sl]
            cq = c_new[:, h:h + 1]
            cq_row = jnp.sum(jnp.where(r8 == c8, jnp.broadcast_to(cq, (rows, rows)), 0.0), axis=0, keepdims=True)
            sn = _dot_nt(qh, kn_ref[:, sl], HIGHEST) * ATTN_SCALE + cq - cq_row
            sn = jnp.where(new_mask, sn, -jnp.inf)
            m_old = m_ref[h, :, 0:1]
            m_new = jnp.maximum(m_old, jnp.max(sn, axis=-1, keepdims=True))
            alpha = jnp.exp(m_old - m_new)
            p = jnp.exp(sn - m_new)
            l_new = alpha * l_ref[h, :, 0:1] + jnp.sum(p, axis=-1, keepdims=True)
            acc = alpha * acc_ref[:, sl] + _dot(p, vn_ref[:, sl], HIGHEST)
            o_ref[:, sl] = acc / l_new


def _fox_sample(page_table_flat, q, k_new, v_new, lf, bias, cache_k_t, cache_v_t, nb, n_pages, valid):
    ppb = 8 if n_pages % 8 == 0 else n_pages
    n_steps = n_pages // ppb
    row_spec = pl.BlockSpec((SAMPLE_ROWS, MAIN_WIDTH), lambda b, g, pt: (b, 0))

    def page_spec(u):
        return pl.BlockSpec((None, MAIN_HEADS, PAGE_SIZE, HEAD_DIM),
                            lambda b, g, pt, u=u: (pt[b * n_pages + g * ppb + u], 0, 0, 0))

    in_specs = [row_spec, row_spec, row_spec,
                pl.BlockSpec((SAMPLE_ROWS, LANES), lambda b, g, pt: (b, 0)),
                pl.BlockSpec((None, MAIN_HEADS, ppb, PAGE_SIZE), lambda b, g, pt: (b, 0, g, 0))]
    in_specs += [page_spec(u) for u in range(ppb)] + [page_spec(u) for u in range(ppb)]
    return pl.pallas_call(
        functools.partial(_fox_sample_kernel, ppb=ppb, n_steps=n_steps, valid=valid),
        grid_spec=pltpu.PrefetchScalarGridSpec(
            num_scalar_prefetch=1,
            grid=(nb, n_steps),
            in_specs=in_specs,
            out_specs=row_spec,
            scratch_shapes=[pltpu.VMEM((MAIN_HEADS, SAMPLE_ROWS, LANES), F32),
                            pltpu.VMEM((MAIN_HEADS, SAMPLE_ROWS, LANES), F32),
                            pltpu.VMEM((SAMPLE_ROWS, MAIN_WIDTH), F32)],
        ),
        out_shape=jax.ShapeDtypeStruct(q.shape, F32),
        compiler_params=_params(2),
    )(page_table_flat, q, k_new, v_new, lf, bias, *([cache_k_t] * ppb), *([cache_v_t] * ppb))


def _trunk(x, nb, rows, valid, mem_k, mem_v, fox, w, prec):
    u, v, qm = _norm_matmul(x, w["norm1_gain"][0:1], w["w_in_a"], 0,
                            [(MAIN_WIDTH, "gelu", 0), (MAIN_WIDTH, "gelu", 0), (MEM_WIDTH, "headnorm", 0)],
                            w["mem_q_gain"][0:1], prec=prec)
    y_main, v_rows = _gmlp(u, v, w["v_gain_a"][0:1], w["w_mix"], w["b_mix"], w["gmlp_rows"], w["gmlp_grp"], valid,
                           prec)
    y_mem = _mem_attn(qm, mem_k, mem_v, 0, nb, rows, prec)
    h = _out_proj(y_main, y_mem, w["w_out"], 0, x, prec)
    h = _hier_moe(h, w["norm2_gain"][0:1], w["w_router"][0], w["b_router"][0], w["w_gate"], w["w_up"], w["w_down"],
                  0, prec)
    k, vv, lf, lft = _norm_matmul(h, w["kv_norm_gain"], w["w_kv_shared"], 0,
                                  [(MAIN_WIDTH, "headnorm", 0), (MAIN_WIDTH, "plain", 0)],
                                  w["k_gain_shared"], forget=w["forget"], prec=prec)
    q, qm = _norm_matmul(h, w["norm1_gain"][1:2], w["w_in_b"], 0,
                         [(MAIN_WIDTH, "headnorm", 0), (MEM_WIDTH, "headnorm", 1)],
                         jnp.concatenate([w["q_gain_b"][0:1], w["mem_q_gain"][1:2]], axis=0), prec=prec)
    y_main = fox(q, k, vv, lf, lft)
    y_mem = _mem_attn(qm, mem_k, mem_v, 1, nb, rows, prec)
    h = _out_proj(y_main, y_mem, w["w_out"], 1, h, prec)
    h = _hier_moe(h, w["norm2_gain"][1:2], w["w_router"][1], w["b_router"][1], w["w_gate"], w["w_up"], w["w_down"],
                  1, prec)
    return h, k, vv, lf, v_rows


def kernel(x_prompt, x_sample, cache_k, cache_v, cache_logf, cache_mem_k, cache_mem_v, page_table, mem_prompt, norm1_gain, norm2_gain, w_in_a, v_gain_a, w_s_a, b_s_a, w_in_b, q_gain_b, kv_norm_gain, w_kv_shared, b_forget, k_gain_shared, mem_norm_gain, w_mem_kv, mem_q_gain, mem_k_gain, w_out, w_router_group, b_router_group, w_router_expert, b_router_expert, w_gate, w_up, w_down):
    batch, seq, _ = x_prompt.shape
    dec_batch, dec_seq, _ = x_sample.shape
    n_phys = cache_k.shape[0]
    n_pages = page_table.shape[1]
    depth = norm1_gain.shape[0]
    assert depth == 2 and dec_seq <= SAMPLE_ROWS and seq % CHUNK == 0

    n_route = N_GROUPS + N_EXPERTS
    w_router = jnp.pad(jnp.concatenate([w_router_group, w_router_expert], axis=-1),
                       ((0, 0), (0, 0), (0, LANES - n_route)))
    b_router = jnp.pad(jnp.concatenate([b_router_group, b_router_expert], axis=-1),
                       ((0, 0), (0, LANES - n_route)))[:, None, :]
    w_f = w_kv_shared[:, 2 * MAIN_WIDTH:]
    forget = (jnp.pad(w_f, ((0, 0), (0, LANES - MAIN_HEADS))),
              jnp.pad(w_f.T, ((0, 2 * SUBLANES - MAIN_HEADS), (0, 0))),
              jnp.pad(b_forget, (0, LANES - MAIN_HEADS))[None, :],
              jnp.pad(b_forget, (0, 2 * SUBLANES - MAIN_HEADS))[:, None])
    common = dict(norm1_gain=norm1_gain, norm2_gain=norm2_gain, w_in_a=w_in_a, v_gain_a=v_gain_a, w_in_b=w_in_b,
                  q_gain_b=q_gain_b, kv_norm_gain=kv_norm_gain[None, :], w_kv_shared=w_kv_shared,
                  k_gain_shared=k_gain_shared[None, :], mem_q_gain=mem_q_gain, w_out=w_out,
                  w_router=w_router, b_router=b_router, w_gate=w_gate, w_up=w_up, w_down=w_down, forget=forget)

    t_p = batch * seq
    mem_tok = mem_prompt.reshape(batch * N_MEM, D_MODEL)
    mem_k_layers, mem_v_layers = [], []
    for l in range(depth):
        mk, mv = _norm_matmul(mem_tok, mem_norm_gain[l:l + 1], w_mem_kv, l,
                              [(MEM_WIDTH, "headnorm", 0), (MEM_WIDTH, "plain", 0)], mem_k_gain[l:l + 1])
        mem_k_layers.append(mk.reshape(batch, N_MEM, MEM_WIDTH))
        mem_v_layers.append(mv.reshape(batch, N_MEM, MEM_WIDTH))
    mem_k_p = jnp.stack(mem_k_layers)
    mem_v_p = jnp.stack(mem_v_layers)

    def fox_prompt(q, k, v, lf, lft):
        ct, c = _fox_cumsum(lft, lf, batch, seq)
        return _fox_prompt(q, k, v, ct, c, batch, seq)

    w_p = dict(common, w_mix=w_s_a[0], b_mix=b_s_a[0].T, gmlp_rows=CHUNK, gmlp_grp=CHUNK)
    y_p, k_p, v_p, lf_p, vrows_p = _trunk(x_prompt.reshape(t_p, D_MODEL), batch, seq, seq,
                                          mem_k_p, mem_v_p, fox_prompt, w_p, None)

    rows = SAMPLE_ROWS
    t_s = dec_batch * rows
    x_s = jnp.pad(x_sample, ((0, 0), (0, rows - dec_seq), (0, 0))).reshape(t_s, D_MODEL)
    pt_flat = page_table.reshape(dec_batch * n_pages)
    bias = _fox_past_bias(pt_flat, jnp.transpose(cache_logf, (2, 0, 1)), dec_batch, n_pages)
    cache_k_t = jnp.transpose(cache_k, (0, 2, 1, 3))
    cache_v_t = jnp.transpose(cache_v, (0, 2, 1, 3))

    def fox_sample(q, k, v, lf, lft):
        return _fox_sample(pt_flat, q, k, v, lf, bias, cache_k_t, cache_v_t, dec_batch, n_pages, dec_seq)

    w_s = dict(common, w_mix=jnp.tile(w_s_a[0][:, :rows, :rows], (1, dec_batch, dec_batch)),
               b_mix=jnp.tile(b_s_a[0][:, :rows].T, (dec_batch, 1)), gmlp_rows=t_s, gmlp_grp=rows)
    mem_k_s = cache_mem_k.reshape(depth, dec_batch, N_MEM, MEM_WIDTH)
    mem_v_s = cache_mem_v.reshape(depth, dec_batch, N_MEM, MEM_WIDTH)
    y_s, k_s, v_s, lf_s, vrows_s = _trunk(x_s, dec_batch, rows, dec_seq, mem_k_s, mem_v_s, fox_sample, w_s, HIGHEST)

    def unpad(a, *tail):
        return a.reshape((dec_batch, rows) + tail)[:, :dec_seq]

    return (y_p.reshape(batch, seq, D_MODEL),
            unpad(y_s, D_MODEL),
            k_p.reshape(batch, seq, MAIN_HEADS, HEAD_DIM),
            v_p.reshape(batch, seq, MAIN_HEADS, HEAD_DIM),
            lf_p[:, :MAIN_HEADS].reshape(batch, seq, MAIN_HEADS),
            unpad(k_s, MAIN_HEADS, HEAD_DIM),
            unpad(v_s, MAIN_HEADS, HEAD_DIM),
            unpad(lf_s[:, :MAIN_HEADS], MAIN_HEADS),
            mem_k_p.reshape(depth, batch, N_MEM, MEM_HEADS, HEAD_DIM),
            mem_v_p.reshape(depth, batch, N_MEM, MEM_HEADS, HEAD_DIM),
            vrows_p.reshape(batch, seq, MAIN_WIDTH)[:, -CHUNK:][None],
            unpad(vrows_s, MAIN_WIDTH)[None])
```

```python
import functools

import jax
import jax.numpy as jnp
from jax import lax
from jax.experimental import pallas as pl
from jax.experimental.pallas import tpu as pltpu

D_MODEL = 2048
HEAD_DIM = 128
MAIN_HEADS = 12
MEM_HEADS = 4
MAIN_WIDTH = MAIN_HEADS * HEAD_DIM
MEM_WIDTH = MEM_HEADS * HEAD_DIM
N_MEM = 256
N_GROUPS = 4
EXPERTS_PER_GROUP = 4
N_EXPERTS = N_GROUPS * EXPERTS_PER_GROUP
D_EXPERT = 512
CHUNK = 128
PAGE_SIZE = 128
EPS = 1e-6
ATTN_SCALE = HEAD_DIM ** -0.5

LANES = 128
SUBLANES = 8
VMEM_LIMIT_BYTES = 56 * 1024 * 1024
SAMPLE_ROWS = SUBLANES
N_CHUNKS = D_MODEL // LANES
ROW_PITCH = 24

F32 = jnp.float32
_NT = (((1,), (1,)), ((), ()))


def _params(n_axes):
    return pltpu.CompilerParams(dimension_semantics=("arbitrary",) * n_axes,
                                vmem_limit_bytes=VMEM_LIMIT_BYTES)


HIGHEST = lax.Precision.HIGHEST


def _dot(a, b, prec=None):
    return jnp.dot(a, b, preferred_element_type=F32, precision=prec)


def _dot_nt(a, b, prec=None):
    return lax.dot_general(a, b, _NT, preferred_element_type=F32, precision=prec)


def _split_hi_lo(x):
    hi = x.astype(jnp.bfloat16).astype(F32)
    return hi, x - hi


def _split3(x):
    hi = x.astype(jnp.bfloat16).astype(F32)
    r = x - hi
    mid = r.astype(jnp.bfloat16).astype(F32)
    lo = (r - mid).astype(jnp.bfloat16).astype(F32)
    return hi, mid, lo


def _dot3_right(a_exact, x):
    hi, mid, lo = _split3(x)
    return _dot(a_exact, hi) + _dot(a_exact, mid) + _dot(a_exact, lo)


def _dot3_left(x, b_exact):
    hi, mid, lo = _split3(x)
    return _dot(hi, b_exact) + _dot(mid, b_exact) + _dot(lo, b_exact)


def _store_chunk_rows(ref, x):
    rows = x.shape[0]
    for c in range(N_CHUNKS):
        ref[pl.ds(c, rows, stride=N_CHUNKS), :] = x[:, c * LANES:(c + 1) * LANES]


def _load_gathered_rows(ref, rows):
    return jnp.concatenate([ref[pl.ds(c, rows, stride=ROW_PITCH), :] for c in range(N_CHUNKS)], axis=1)


def _head_norm(z, gain_row):
    return z * lax.rsqrt(jnp.mean(z * z, axis=-1, keepdims=True) + EPS) * gain_row


def _norm_matmul_kernel(*refs, segs, tn, with_forget, prec):
    it = iter(refs)
    x_ref, g_ref, w_ref, hg_ref = next(it), next(it), next(it), next(it)
    if with_forget:
        wf_ref, wft_ref, bfr_ref, bfc_ref = next(it), next(it), next(it), next(it)
    out_refs = [next(it) for _ in segs]
    if with_forget:
        lf_ref, lft_ref = next(it), next(it)
    xn_ref = next(it)
    j = pl.program_id(1)

    def project(xn, o_ref, kind, grow):
        z = _dot(xn, w_ref[...], prec)
        if kind == "plain":
            o_ref[...] = z
        elif kind == "gelu":
            o_ref[...] = jax.nn.gelu(z)
        else:
            gain = hg_ref[grow:grow + 1, :]
            for c in range(tn // HEAD_DIM):
                sl = slice(c * HEAD_DIM, (c + 1) * HEAD_DIM)
                o_ref[:, sl] = _head_norm(z[:, sl], gain)

    @pl.when(j == 0)
    def _first():
        x = x_ref[...]
        xn = x * lax.rsqrt(jnp.mean(x * x, axis=-1, keepdims=True) + EPS) * g_ref[...]
        xn_ref[...] = xn
        if with_forget:
            lf_ref[...] = jax.nn.log_sigmoid(_dot(xn, wf_ref[...], prec) + bfr_ref[...])
            lft_ref[...] = jax.nn.log_sigmoid(_dot_nt(wft_ref[...], xn, prec) + bfc_ref[...])
        project(xn, out_refs[0], segs[0][1], segs[0][2])

    lo = 0
    for (ncols, kind, grow), o_ref in zip(segs, out_refs):
        nt = ncols // tn

        @pl.when((j >= max(lo, 1)) & (j < lo + nt))
        def _rest(o_ref=o_ref, kind=kind, grow=grow):
            project(xn_ref[...], o_ref, kind, grow)

        lo += nt


def _norm_matmul(x, gain, w, layer, segs, head_gains, forget=None, prec=None):
    t = x.shape[0]
    tm = min(t, 1024)
    tn = 512
    n_tiles = sum(s[0] for s in segs) // tn
    with_forget = forget is not None
    in_specs = [
        pl.BlockSpec((tm, D_MODEL), lambda i, j: (i, 0)),
        pl.BlockSpec((1, D_MODEL), lambda i, j: (0, 0)),
        pl.BlockSpec((None, D_MODEL, tn), lambda i, j: (layer, 0, j)) if w.ndim == 3
        else pl.BlockSpec((D_MODEL, tn), lambda i, j: (0, j)),
        pl.BlockSpec(head_gains.shape, lambda i, j: (0, 0)),
    ]
    args = [x, gain, w, head_gains]
    if with_forget:
        wf, wft, bfr, bfc = forget
        in_specs += [pl.BlockSpec(wf.shape, lambda i, j: (0, 0)), pl.BlockSpec(wft.shape, lambda i, j: (0, 0)),
                     pl.BlockSpec(bfr.shape, lambda i, j: (0, 0)), pl.BlockSpec(bfc.shape, lambda i, j: (0, 0))]
        args += [wf, wft, bfr, bfc]
    out_shape, out_specs = [], []
    lo = 0
    for ncols, _, _ in segs:
        nt = ncols // tn
        out_shape.append(jax.ShapeDtypeStruct((t, ncols), F32))
        out_specs.append(pl.BlockSpec((tm, tn), lambda i, j, lo=lo, nt=nt: (i, jnp.clip(j - lo, 0, nt - 1))))
        lo += nt
    if with_forget:
        out_shape += [jax.ShapeDtypeStruct((t, LANES), F32), jax.ShapeDtypeStruct((2 * SUBLANES, t), F32)]
        out_specs += [pl.BlockSpec((tm, LANES), lambda i, j: (i, 0)),
                      pl.BlockSpec((2 * SUBLANES, tm), lambda i, j: (0, i))]
    return pl.pallas_call(
        functools.partial(_norm_matmul_kernel, segs=tuple(segs), tn=tn, with_forget=with_forget, prec=prec),
        grid=(t // tm, n_tiles),
        in_specs=in_specs,
        out_specs=out_specs,
        out_shape=out_shape,
        scratch_shapes=[pltpu.VMEM((tm, D_MODEL), F32)],
        compiler_params=_params(2),
    )(*args)


def _gmlp_kernel(u_ref, v_ref, vg_ref, w_ref, b_ref, y_ref, vn_ref, *, rows, grp, valid, prec):
    v = v_ref[...]
    vn = v * lax.rsqrt(jnp.mean(v * v, axis=-1, keepdims=True) + EPS) * vg_ref[...]
    vn_ref[...] = vn
    r = lax.broadcasted_iota(jnp.int32, (rows, rows), 0)
    c = lax.broadcasted_iota(jnp.int32, (rows, rows), 1)
    shift = grp.bit_length() - 1
    allowed = ((c & (grp - 1)) <= (r & (grp - 1))) & ((c & (grp - 1)) < valid)
    if grp < rows:
        allowed = allowed & ((r >> shift) == (c >> shift))
    for g in range(MAIN_HEADS):
        sl = slice(g * HEAD_DIM, (g + 1) * HEAD_DIM)
        w = jnp.where(allowed, w_ref[g], 0.0)
        mixed = _dot(w, vn[:, sl], prec) + b_ref[:, g:g + 1]
        y_ref[:, sl] = u_ref[:, sl] * mixed


def _gmlp(u, v, v_gain, w_mix, b_mix, rows, grp, valid, prec):
    t = u.shape[0]
    return pl.pallas_call(
        functools.partial(_gmlp_kernel, rows=rows, grp=grp, valid=valid, prec=prec),
        grid=(t // rows,),
        in_specs=[
            pl.BlockSpec((rows, MAIN_WIDTH), lambda i: (i, 0)),
            pl.BlockSpec((rows, MAIN_WIDTH), lambda i: (i, 0)),
            pl.BlockSpec((1, MAIN_WIDTH), lambda i: (0, 0)),
            pl.BlockSpec((MAIN_HEADS, rows, rows), lambda i: (0, 0, 0)),
            pl.BlockSpec((rows, MAIN_HEADS), lambda i: (0, 0)),
        ],
        out_specs=[pl.BlockSpec((rows, MAIN_WIDTH), lambda i: (i, 0)),
                   pl.BlockSpec((rows, MAIN_WIDTH), lambda i: (i, 0))],
        out_shape=[jax.ShapeDtypeStruct((t, MAIN_WIDTH), F32), jax.ShapeDtypeStruct((t, MAIN_WIDTH), F32)],
        compiler_params=_params(1),
    )(u, v, v_gain, w_mix, b_mix)


def _mem_attn_kernel(q_ref, k_ref, v_ref, o_ref, *, prec):
    for h in range(MEM_HEADS):
        sl = slice(h * HEAD_DIM, (h + 1) * HEAD_DIM)
        s = _dot_nt(q_ref[:, sl], k_ref[:, sl], prec) * ATTN_SCALE
        e = jnp.exp(s - jnp.max(s, axis=-1, keepdims=True))
        p = e / jnp.sum(e, axis=-1, keepdims=True)
        o_ref[:, sl] = _dot(p, v_ref[:, sl], prec)


def _mem_attn(q, k, v, layer, nb, rows_per_batch, prec):
    tq = min(rows_per_batch, 1024)
    nq = rows_per_batch // tq
    kv_spec = pl.BlockSpec((None, None, N_MEM, MEM_WIDTH), lambda b, i: (layer, b, 0, 0))
    return pl.pallas_call(
        functools.partial(_mem_attn_kernel, prec=prec),
        grid=(nb, nq),
        in_specs=[pl.BlockSpec((tq, MEM_WIDTH), lambda b, i: (b * nq + i, 0)), kv_spec, kv_spec],
        out_specs=pl.BlockSpec((tq, MEM_WIDTH), lambda b, i: (b * nq + i, 0)),
        out_shape=jax.ShapeDtypeStruct(q.shape, F32),
        compiler_params=_params(2),
    )(q, k, v)


def _out_proj_kernel(ym_ref, ymem_ref, w1_ref, w2_ref, h_ref, o_ref, *, prec):
    o_ref[...] = h_ref[...] + _dot(ym_ref[...], w1_ref[...], prec) + _dot(ymem_ref[...], w2_ref[...], prec)


def _out_proj(y_main, y_mem, w_out, layer, h, prec):
    t = h.shape[0]
    tm = min(t, 1024)
    tn = 512
    return pl.pallas_call(
        functools.partial(_out_proj_kernel, prec=prec),
        grid=(t // tm, D_MODEL // tn),
        in_specs=[
            pl.BlockSpec((tm, MAIN_WIDTH), lambda i, j: (i, 0)),
            pl.BlockSpec((tm, MEM_WIDTH), lambda i, j: (i, 0)),
            pl.BlockSpec((None, MAIN_WIDTH, tn), lambda i, j: (layer, 0, j)),
            pl.BlockSpec((None, MEM_WIDTH, tn), lambda i, j: (layer, MAIN_WIDTH // MEM_WIDTH, j)),
            pl.BlockSpec((tm, tn), lambda i, j: (i, j)),
        ],
        out_specs=pl.BlockSpec((tm, tn), lambda i, j: (i, j)),
        out_shape=jax.ShapeDtypeStruct((t, D_MODEL), F32),
        compiler_params=_params(2),
    )(y_main, y_mem, w_out, w_out, h)


def _router_kernel(h_ref, g_ref, wr_ref, br_ref, x2_ref, idx_ref, gate_ref, *, prec):
    x = h_ref[...]
    xn = x * lax.rsqrt(jnp.mean(x * x, axis=-1, keepdims=True) + EPS) * g_ref[...]
    _store_chunk_rows(x2_ref, xn)
    lg = _dot(xn, wr_ref[...], prec) + br_ref[...]
    lane = lax.broadcasted_iota(jnp.int32, lg.shape, 1)
    neg = -jnp.inf
    is_grp = lane < N_GROUPS
    gl = jnp.where(is_grp, lg, neg)
    ge = jnp.where(is_grp, jnp.exp(gl - jnp.max(gl, axis=-1, keepdims=True)), 0.0)
    pg = ge / jnp.sum(ge, axis=-1, keepdims=True)
    p_top = jnp.max(pg, axis=-1, keepdims=True)
    g_idx = jnp.min(jnp.where(is_grp & (pg == p_top), lane, LANES), axis=-1, keepdims=True)
    first = N_GROUPS + EXPERTS_PER_GROUP * g_idx
    in_grp = (lane >= first) & (lane < first + EXPERTS_PER_GROUP)
    e1 = jnp.max(jnp.where(in_grp, lg, neg), axis=-1, keepdims=True)
    i1 = jnp.min(jnp.where(in_grp & (lg == e1), lane, LANES), axis=-1, keepdims=True)
    rest = in_grp & (lane != i1)
    e2 = jnp.max(jnp.where(rest, lg, neg), axis=-1, keepdims=True)
    i2 = jnp.min(jnp.where(rest & (lg == e2), lane, LANES), axis=-1, keepdims=True)
    t2 = jnp.exp(e2 - e1)
    den = 1.0 + t2
    idx_ref[...] = jnp.where(lane == 0, i1 - N_GROUPS, jnp.where(lane == 1, i2 - N_GROUPS, 0))
    gate_ref[...] = jnp.where(lane == 0, p_top * (1.0 / den), jnp.where(lane == 1, p_top * (t2 / den), 0.0))


def _router(h, gain, wr, br, prec):
    t = h.shape[0]
    tm = min(t, 512)
    return pl.pallas_call(
        functools.partial(_router_kernel, prec=prec),
        grid=(t // tm,),
        in_specs=[pl.BlockSpec((tm, D_MODEL), lambda i: (i, 0)), pl.BlockSpec((1, D_MODEL), lambda i: (0, 0)),
                  pl.BlockSpec((D_MODEL, LANES), lambda i: (0, 0)), pl.BlockSpec((1, LANES), lambda i: (0, 0))],
        out_specs=[pl.BlockSpec((tm * N_CHUNKS, LANES), lambda i: (i, 0)), pl.BlockSpec((tm, LANES), lambda i: (i, 0)),
                   pl.BlockSpec((tm, LANES), lambda i: (i, 0))],
        out_shape=[jax.ShapeDtypeStruct((t * N_CHUNKS, LANES), F32), jax.ShapeDtypeStruct((t, LANES), jnp.int32),
                   jax.ShapeDtypeStruct((t, LANES), F32)],
        compiler_params=_params(1),
    )(h, gain, wr, br)


def _plan_kernel(e_ref, pos_ref, te_ref, nv_ref, rank_ref, *, t2, blk, tm):
    shift = tm.bit_length() - 1
    nblk = t2 // blk
    r = lax.broadcasted_iota(jnp.int32, (blk, blk), 0)
    c = lax.broadcasted_iota(jnp.int32, (blk, blk), 1)
    upper = (r <= c).astype(F32)
    sub = lax.broadcasted_iota(jnp.int32, (N_EXPERTS, blk), 0)
    carry = jnp.zeros((N_EXPERTS, 1), F32)
    for b in range(nblk):
        sl = slice(b * blk, (b + 1) * blk)
        oh = (sub == e_ref[:, sl]).astype(F32)
        cs = _dot(oh, upper) + carry
        rank_ref[:, sl] = jnp.sum(oh * (cs - 1.0), axis=0, keepdims=True)
        carry = cs[:, blk - 1:blk]
    counts = carry.astype(jnp.int32)
    padc = ((counts + (tm - 1)) >> shift) << shift
    sub1 = lax.broadcasted_iota(jnp.int32, (N_EXPERTS, 1), 0)
    off = jnp.zeros((N_EXPERTS, 1), jnp.int32)
    run = jnp.zeros((1, 1), jnp.int32)
    for e in range(N_EXPERTS):
        off = jnp.where(sub1 == e, run, off)
        run = run + padc[e:e + 1, :]
    ends = off + padc
    for b in range(nblk):
        sl = slice(b * blk, (b + 1) * blk)
        offv = jnp.sum(jnp.where(sub == e_ref[:, sl], off, 0), axis=0, keepdims=True)
        pos_ref[:, sl] = rank_ref[:, sl].astype(jnp.int32) + offv
    lane = lax.broadcasted_iota(jnp.int32, (N_EXPERTS, LANES), 1)
    lane1 = lax.broadcasted_iota(jnp.int32, (1, LANES), 1)
    nv = run >> shift
    te = jnp.sum((ends <= lane * tm).astype(jnp.int32), axis=0, keepdims=True)
    te_last = jnp.sum((ends <= (nv - 1) * tm).astype(jnp.int32), axis=0, keepdims=True)
    te_ref[...] = jnp.minimum(jnp.where(lane1 >= nv, te_last, te), N_EXPERTS - 1)
    nv_ref[...] = jnp.broadcast_to(nv, (1, LANES))


def _plan(e_flat, tm):
    t2 = e_flat.shape[1]
    blk = min(t2, 512)
    return pl.pallas_call(
        functools.partial(_plan_kernel, t2=t2, blk=blk, tm=tm),
        out_shape=[jax.ShapeDtypeStruct((1, t2), jnp.int32), jax.ShapeDtypeStruct((1, LANES), jnp.int32),
                   jax.ShapeDtypeStruct((1, LANES), jnp.int32)],
        scratch_shapes=[pltpu.VMEM((1, t2), F32)],
        compiler_params=pltpu.CompilerParams(vmem_limit_bytes=VMEM_LIMIT_BYTES),
    )(e_flat)


def _invert_kernel(pos_ref, src_ref, *, t, t2, n_rows):
    def zero(p, carry):
        src_ref[p] = 0
        return carry

    lax.fori_loop(0, n_rows, zero, 0, unroll=8)

    def put(i, carry):
        src_ref[pos_ref[i]] = N_CHUNKS * jnp.where(i >= t, i - t, i)
        return carry

    lax.fori_loop(0, t2, put, 0, unroll=8)


def _invert(pos, t, n_rows):
    return pl.pallas_call(
        functools.partial(_invert_kernel, t=t, t2=pos.shape[0], n_rows=n_rows),
        grid_spec=pltpu.PrefetchScalarGridSpec(
            num_scalar_prefetch=1,
            grid=(1,),
            in_specs=[],
            out_specs=pl.BlockSpec(memory_space=pltpu.SMEM),
        ),
        out_shape=jax.ShapeDtypeStruct((n_rows,), jnp.int32),
        compiler_params=_params(1),
    )(pos)


def _moe_ffn_kernel(te_ref, nv_ref, src_ref, x_ref, wg_hbm, wu_hbm, wd_hbm, y_ref,
                    xbuf_ref, wg_ref, wu_ref, wd_ref, wslot_ref, sems, wsems, *, tm, layer, prec):
    i = pl.program_id(0)
    nv = nv_ref[0]
    slot = i % 2
    expert = te_ref[i]

    def start_tile(tile, dst_slot):
        base = tile * tm
        for r in range(tm):
            src = pl.multiple_of(src_ref[base + r], N_CHUNKS)
            pltpu.make_async_copy(x_ref.at[pl.ds(src, N_CHUNKS)],
                                  xbuf_ref.at[dst_slot, pl.ds(r * ROW_PITCH, N_CHUNKS)],
                                  sems.at[dst_slot]).start(priority=r % 2)

    def wait_tile():
        done = xbuf_ref.at[slot, pl.ds(0, tm * N_CHUNKS)]
        pltpu.make_async_copy(done, done, sems.at[slot]).wait()

    def weight_copies(e, ws):
        return (pltpu.make_async_copy(wg_hbm.at[layer, e], wg_ref.at[ws], wsems.at[ws]),
                pltpu.make_async_copy(wu_hbm.at[layer, e], wu_ref.at[ws], wsems.at[ws]),
                pltpu.make_async_copy(wd_hbm.at[layer, e], wd_ref.at[ws], wsems.at[ws]))

    @pl.when(i == 0)
    def _():
        start_tile(0, 0)
        wslot_ref[0] = 1
        for cp in weight_copies(expert, 0):
            cp.start()

    is_first = (i < nv) & ((i == 0) | (te_ref[jnp.maximum(i - 1, 0)] != expert))

    @pl.when(is_first)
    def _():
        ws = 1 - wslot_ref[0]
        wslot_ref[0] = ws
        for cp in weight_copies(expert, ws):
            cp.wait()
        nxt = lax.while_loop(lambda j: (j < nv) & (te_ref[jnp.minimum(j, nv - 1)] == expert), lambda j: j + 1, i + 1)

        @pl.when(nxt < nv)
        def _():
            for cp in weight_copies(te_ref[nxt], 1 - ws):
                cp.start()

    def ffn():
        ws = wslot_ref[0]
        x = _load_gathered_rows(xbuf_ref.at[slot], tm)
        hdn = jax.nn.silu(_dot(x, wg_ref[ws], prec)) * _dot(x, wu_ref[ws], prec)
        _store_chunk_rows(y_ref, _dot(hdn, wd_ref[ws], prec))

    @pl.when(i + 1 < nv)
    def _():
        wait_tile()
        start_tile(i + 1, 1 - slot)
        ffn()

    @pl.when(i + 1 == nv)
    def _():
        wait_tile()
        ffn()

    @pl.when(i >= nv)
    def _():
        y_ref[...] = jnp.zeros_like(y_ref)


def _moe_ffn(te, nv, src, x2, w_gate, w_up, w_down, layer, tm, prec):
    n_rows = src.shape[0]
    n_tiles = n_rows // tm
    any_spec = pl.BlockSpec(memory_space=pl.ANY)
    return pl.pallas_call(
        functools.partial(_moe_ffn_kernel, tm=tm, layer=layer, prec=prec),
        grid_spec=pltpu.PrefetchScalarGridSpec(
            num_scalar_prefetch=3,
            grid=(n_tiles,),
            in_specs=[any_spec, any_spec, any_spec, any_spec],
            out_specs=pl.BlockSpec((tm * N_CHUNKS, LANES), lambda i, te, nv, src: (i, 0)),
            scratch_shapes=[pltpu.VMEM((2, tm * ROW_PITCH, LANES), F32),
                            pltpu.VMEM((2, D_MODEL, D_EXPERT), F32), pltpu.VMEM((2, D_MODEL, D_EXPERT), F32),
                            pltpu.VMEM((2, D_EXPERT, D_MODEL), F32), pltpu.SMEM((1,), jnp.int32),
                            pltpu.SemaphoreType.DMA((2,)), pltpu.SemaphoreType.DMA((2,))],
        ),
        out_shape=jax.ShapeDtypeStruct((n_rows * N_CHUNKS, LANES), F32),
        compiler_params=_params(1),
    )(te, nv, src, x2, w_gate, w_up, w_down)


def _combine_kernel(pos_ref, h_ref, g_ref, y_ref, o_ref, y0_ref, y1_ref, sems, *, t, tm):
    base = pl.program_id(0) * tm
    for k, buf in enumerate((y0_ref, y1_ref)):
        for r in range(tm):
            src = pl.multiple_of(pos_ref[k * t + base + r] * N_CHUNKS, N_CHUNKS)
            pltpu.make_async_copy(y_ref.at[pl.ds(src, N_CHUNKS)], buf.at[pl.ds(r * ROW_PITCH, N_CHUNKS)],
                                  sems.at[k]).start(priority=k)
    for k, buf in enumerate((y0_ref, y1_ref)):
        done = buf.at[pl.ds(0, tm * N_CHUNKS)]
        pltpu.make_async_copy(done, done, sems.at[k]).wait()
    g = g_ref[...]
    o_ref[...] = (h_ref[...] + g[:, 0:1] * _load_gathered_rows(y0_ref, tm)
                  + g[:, 1:2] * _load_gathered_rows(y1_ref, tm))


def _combine(pos, h, gates, y):
    t = h.shape[0]
    tm = min(t, 256)
    return pl.pallas_call(
        functools.partial(_combine_kernel, t=t, tm=tm),
        grid_spec=pltpu.PrefetchScalarGridSpec(
            num_scalar_prefetch=1,
            grid=(t // tm,),
            in_specs=[pl.BlockSpec((tm, D_MODEL), lambda i, pos: (i, 0)),
                      pl.BlockSpec((tm, LANES), lambda i, pos: (i, 0)),
                      pl.BlockSpec(memory_space=pl.ANY)],
            out_specs=pl.BlockSpec((tm, D_MODEL), lambda i, pos: (i, 0)),
            scratch_shapes=[pltpu.VMEM((tm * ROW_PITCH, LANES), F32), pltpu.VMEM((tm * ROW_PITCH, LANES), F32),
                            pltpu.SemaphoreType.DMA((2,))],
        ),
        out_shape=jax.ShapeDtypeStruct((t, D_MODEL), F32),
        compiler_params=_params(1),
    )(pos, h, gates, y)


def _hier_moe(h, gain, wr, br, w_gate, w_up, w_down, layer, prec):
    t = h.shape[0]
    tm = 256 if t >= 1024 else 16
    n_tiles = (2 * t) // tm + N_EXPERTS
    assert n_tiles <= LANES
    x2, idx, gates = _router(h, gain, wr, br, prec)
    e_flat = idx[:, :2].T.reshape(1, 2 * t)
    pos, te, nv = _plan(e_flat, tm)
    pos = pos.reshape(2 * t)
    te = te[0, :n_tiles]
    nv = nv[0, :1]
    src = _invert(pos, t, n_tiles * tm)
    y = _moe_ffn(te, nv, src, x2, w_gate, w_up, w_down, layer, tm, prec)
    return _combine(pos, h, gates, y)


def _fox_cumsum_kernel(lft_ref, lf_ref, ct_ref, c_ref, *, s, blk):
    r = lax.broadcasted_iota(jnp.int32, (blk, blk), 0)
    c = lax.broadcasted_iota(jnp.int32, (blk, blk), 1)
    upper = (r <= c).astype(F32)
    lower = (r >= c).astype(F32)
    carry_t = jnp.zeros((2 * SUBLANES, 1), F32)
    carry = jnp.zeros((1, LANES), F32)
    for b in range(s // blk):
        sl = slice(b * blk, (b + 1) * blk)
        ct = _dot3_left(lft_ref[:, sl], upper) + carry_t
        ct_ref[:, sl] = ct
        carry_t = ct[:, blk - 1:blk]
        cc = _dot3_right(lower, lf_ref[sl, :]) + carry
        c_ref[sl, :] = cc
        carry = cc[blk - 1:blk, :]


def _fox_cumsum(lft, lf, nb, s):
    t = lf.shape[0]
    blk = min(s, 256)
    return pl.pallas_call(
        functools.partial(_fox_cumsum_kernel, s=s, blk=blk),
        grid=(nb,),
        in_specs=[pl.BlockSpec((2 * SUBLANES, s), lambda b: (0, b)), pl.BlockSpec((s, LANES), lambda b: (b, 0))],
        out_specs=[pl.BlockSpec((2 * SUBLANES, s), lambda b: (0, b)), pl.BlockSpec((s, LANES), lambda b: (b, 0))],
        out_shape=[jax.ShapeDtypeStruct((2 * SUBLANES, t), F32), jax.ShapeDtypeStruct((t, LANES), F32)],
        compiler_params=_params(1),
    )(lft, lf)


def _fox_prompt_kernel(q_ref, k_ref, v_ref, ct_ref, c_ref, o_ref, *, s, tq):
    h = pl.program_id(1)
    lane = lax.broadcasted_iota(jnp.int32, (s, LANES), 1)
    c_col = jnp.sum(jnp.where(lane == h, c_ref[...], 0.0), axis=1, keepdims=True)
    c_row = ct_ref[pl.ds(h, 1), :]
    ri = lax.broadcasted_iota(jnp.int32, (tq, tq), 0)
    ci = lax.broadcasted_iota(jnp.int32, (tq, tq), 1)
    causal = ci <= ri
    for qi in range(s // tq):
        qs = slice(qi * tq, (qi + 1) * tq)
        q = q_ref[qs, :]
        m = jnp.full((tq, 1), -jnp.inf, F32)
        l = jnp.zeros((tq, 1), F32)
        acc = jnp.zeros((tq, HEAD_DIM), F32)
        for kj in range(qi + 1):
            ks = slice(kj * tq, (kj + 1) * tq)
            sc = lax.dot_general(q, k_ref[ks, :], _NT, preferred_element_type=F32) * ATTN_SCALE
            sc = sc + c_col[qs, :] - c_row[:, ks]
            if kj == qi:
                sc = jnp.where(causal, sc, -jnp.inf)
            m_new = jnp.maximum(m, jnp.max(sc, axis=-1, keepdims=True))
            alpha = jnp.exp(m - m_new)
            p = jnp.exp(sc - m_new)
            l = alpha * l + jnp.sum(p, axis=-1, keepdims=True)
            acc = alpha * acc + _dot(p, v_ref[ks, :])
            m = m_new
        o_ref[qs, :] = acc / l


def _fox_prompt(q, k, v, ct, c, nb, s):
    t = q.shape[0]
    tq = min(s, 512)
    head_spec = pl.BlockSpec((s, HEAD_DIM), lambda b, h: (b, h))
    return pl.pallas_call(
        functools.partial(_fox_prompt_kernel, s=s, tq=tq),
        grid=(nb, MAIN_HEADS),
        in_specs=[head_spec, head_spec, head_spec,
                  pl.BlockSpec((2 * SUBLANES, s), lambda b, h: (0, b)),
                  pl.BlockSpec((s, LANES), lambda b, h: (b, 0))],
        out_specs=head_spec,
        out_shape=jax.ShapeDtypeStruct((t, MAIN_WIDTH), F32),
        compiler_params=_params(2),
    )(q, k, v, ct, c)


def _fox_past_bias_kernel(pt_ref, lfc_ref, o_ref, buf_ref, sem, *, n_pages):
    b = pl.program_id(0)

    def page_copy(p):
        page = pt_ref[b * n_pages + p]
        return pltpu.make_async_copy(lfc_ref.at[:, pl.ds(page, 1), :], buf_ref.at[:, pl.ds(p, 1), :], sem)

    def start(p, carry):
        page_copy(p).start()
        return carry

    def wait(p, carry):
        page_copy(p).wait()
        return carry

    lax.fori_loop(0, n_pages, start, 0)
    r = lax.broadcasted_iota(jnp.int32, (PAGE_SIZE, PAGE_SIZE), 0)
    c = lax.broadcasted_iota(jnp.int32, (PAGE_SIZE, PAGE_SIZE), 1)
    after_in_page = (r > c).astype(F32)
    pr = lax.broadcasted_iota(jnp.int32, (n_pages, n_pages), 0)
    pc = lax.broadcasted_iota(jnp.int32, (n_pages, n_pages), 1)
    later_pages = (pc > pr).astype(F32)
    lax.fori_loop(0, n_pages, wait, 0)
    for h in range(MAIN_HEADS):
        lp = buf_ref[h]
        tot = jnp.broadcast_to(jnp.sum(lp, axis=1, keepdims=True), lp.shape)
        o_ref[h] = _dot3_left(lp, after_in_page) + _dot3_right(later_pages, tot)


def _fox_past_bias(page_table_flat, cache_logf_t, nb, n_pages):
    return pl.pallas_call(
        functools.partial(_fox_past_bias_kernel, n_pages=n_pages),
        grid_spec=pltpu.PrefetchScalarGridSpec(
            num_scalar_prefetch=1,
            grid=(nb,),
            in_specs=[pl.BlockSpec(memory_space=pl.ANY)],
            out_specs=pl.BlockSpec((None, MAIN_HEADS, n_pages, PAGE_SIZE), lambda b, pt: (b, 0, 0, 0)),
            scratch_shapes=[pltpu.VMEM((MAIN_HEADS, n_pages, PAGE_SIZE), F32), pltpu.SemaphoreType.DMA(())],
        ),
        out_shape=jax.ShapeDtypeStruct((nb, MAIN_HEADS, n_pages, PAGE_SIZE), F32),
        compiler_params=_params(1),
    )(page_table_flat, cache_logf_t)


def _fox_sample_kernel(*refs, ppb, n_steps, valid):
    pt_ref, q_ref, kn_ref, vn_ref, lf_ref, bias_ref = refs[:6]
    k_refs = refs[6:6 + ppb]
    v_refs = refs[6 + ppb:6 + 2 * ppb]
    o_ref, m_ref, l_ref, acc_ref = refs[6 + 2 * ppb:]
    g = pl.program_id(1)
    rows = SAMPLE_ROWS

    @pl.when(g == 0)
    def _():
        m_ref[...] = jnp.full(m_ref.shape, -jnp.inf, F32)
        l_ref[...] = jnp.zeros(l_ref.shape, F32)
        acc_ref[...] = jnp.zeros(acc_ref.shape, F32)

    r8 = lax.broadcasted_iota(jnp.int32, (rows, rows), 0)
    c8 = lax.broadcasted_iota(jnp.int32, (rows, rows), 1)
    new_mask = (c8 <= r8) & (c8 < valid)
    c_new = _dot3_right(new_mask.astype(F32), lf_ref[...])

    def qk3(q_hi, q_lo, k):
        k_hi, k_lo = _split_hi_lo(k)
        a = _dot_nt(jnp.concatenate([q_hi, q_lo], axis=0), k_hi)
        return a[:rows] + a[rows:] + _dot_nt(q_hi, k_lo)

    def pv3(p, v):
        p_hi, p_lo = _split_hi_lo(p)
        v_hi, v_lo = _split_hi_lo(v)
        a = _dot(jnp.concatenate([p_hi, p_lo], axis=0), v_hi)
        return a[:rows] + a[rows:] + _dot(p_hi, v_lo)

    all_scores = []
    for h in range(MAIN_HEADS):
        q_hi, q_lo = _split_hi_lo(q_ref[:, h * HEAD_DIM:(h + 1) * HEAD_DIM])
        cq = c_new[:, h:h + 1]
        all_scores.append([qk3(q_hi, q_lo, k_refs[u][h]) * ATTN_SCALE + cq + bias_ref[h, u:u + 1, :]
                           for u in range(ppb)])
    all_probs, all_alpha = [], []
    for h in range(MAIN_HEADS):
        m_old = m_ref[h, :, 0:1]
        m_new = m_old
        for sc in all_scores[h]:
            m_new = jnp.maximum(m_new, jnp.max(sc, axis=-1, keepdims=True))
        alpha = jnp.exp(m_old - m_new)
        probs = [jnp.exp(sc - m_new) for sc in all_scores[h]]
        l_new = alpha * l_ref[h, :, 0:1]
        for p in probs:
            l_new = l_new + jnp.sum(p, axis=-1, keepdims=True)
        m_ref[h] = jnp.broadcast_to(m_new, (rows, LANES))
        l_ref[h] = jnp.broadcast_to(l_new, (rows, LANES))
        all_probs.append(probs)
        all_alpha.append(alpha)
    for h in range(MAIN_HEADS):
        sl = slice(h * HEAD_DIM, (h + 1) * HEAD_DIM)
        acc = all_alpha[h] * acc_ref[:, sl]
        for u in range(ppb):
            acc = acc + pv3(all_probs[h][u], v_refs[u][h])
        acc_ref[:, sl] = acc

    @pl.when(g == n_steps - 1)
    def _():
        for h in range(MAIN_HEADS):
            sl = slice(h * HEAD_DIM, (h + 1) * HEAD_DIM)
            qh = q_ref[:, sl]
            cq = c_new[:, h:h + 1]
            cq_row = jnp.sum(jnp.where(r8 == c8, jnp.broadcast_to(cq, (rows, rows)), 0.0), axis=0, keepdims=True)
            sn = _dot_nt(qh, kn_ref[:, sl], HIGHEST) * ATTN_SCALE + cq - cq_row
            sn = jnp.where(new_mask, sn, -jnp.inf)
            m_old = m_ref[h, :, 0:1]
            m_new = jnp.maximum(m_old, jnp.max(sn, axis=-1, keepdims=True))
            alpha = jnp.exp(m_old - m_new)
            p = jnp.exp(sn - m_new)
            l_new = alpha * l_ref[h, :, 0:1] + jnp.sum(p, axis=-1, keepdims=True)
            acc = alpha * acc_ref[:, sl] + _dot(p, vn_ref[:, sl], HIGHEST)
            o_ref[:, sl] = acc / l_new


def _fox_sample(page_table_flat, q, k_new, v_new, lf, bias, cache_k_t, cache_v_t, nb, n_pages, valid):
    ppb = 8 if n_pages % 8 == 0 else n_pages
    n_steps = n_pages // ppb
    row_spec = pl.BlockSpec((SAMPLE_ROWS, MAIN_WIDTH), lambda b, g, pt: (b, 0))

    def page_spec(u):
        return pl.BlockSpec((None, MAIN_HEADS, PAGE_SIZE, HEAD_DIM),
                            lambda b, g, pt, u=u: (pt[b * n_pages + g * ppb + u], 0, 0, 0))

    in_specs = [row_spec, row_spec, row_spec,
                pl.BlockSpec((SAMPLE_ROWS, LANES), lambda b, g, pt: (b, 0)),
                pl.BlockSpec((None, MAIN_HEADS, ppb, PAGE_SIZE), lambda b, g, pt: (b, 0, g, 0))]
    in_specs += [page_spec(u) for u in range(ppb)] + [page_spec(u) for u in range(ppb)]
    return pl.pallas_call(
        functools.partial(_fox_sample_kernel, ppb=ppb, n_steps=n_steps, valid=valid),
        grid_spec=pltpu.PrefetchScalarGridSpec(
            num_scalar_prefetch=1,
            grid=(nb, n_steps),
            in_specs=in_specs,
            out_specs=row_spec,
            scratch_shapes=[pltpu.VMEM((MAIN_HEADS, SAMPLE_ROWS, LANES), F32),
                            pltpu.VMEM((MAIN_HEADS, SAMPLE_ROWS, LANES), F32),
                            pltpu.VMEM((SAMPLE_ROWS, MAIN_WIDTH), F32)],
        ),
        out_shape=jax.ShapeDtypeStruct(q.shape, F32),
        compiler_params=_params(2),
    )(page_table_flat, q, k_new, v_new, lf, bias, *([cache_k_t] * ppb), *([cache_v_t] * ppb))


def _trunk(x, nb, rows, valid, mem_k, mem_v, fox, w, prec):
    u, v, qm = _norm_matmul(x, w["norm1_gain"][0:1], w["w_in_a"], 0,
                            [(MAIN_WIDTH, "gelu", 0), (MAIN_WIDTH, "gelu", 0), (MEM_WIDTH, "headnorm", 0)],
                            w["mem_q_gain"][0:1], prec=prec)
    y_main, v_rows = _gmlp(u, v, w["v_gain_a"][0:1], w["w_mix"], w["b_mix"], w["gmlp_rows"], w["gmlp_grp"], valid,
                           prec)
    y_mem = _mem_attn(qm, mem_k, mem_v, 0, nb, rows, prec)
    h = _out_proj(y_main, y_mem, w["w_out"], 0, x, prec)
    h = _hier_moe(h, w["norm2_gain"][0:1], w["w_router"][0], w["b_router"][0], w["w_gate"], w["w_up"], w["w_down"],
                  0, prec)
    k, vv, lf, lft = _norm_matmul(h, w["kv_norm_gain"], w["w_kv_shared"], 0,
                                  [(MAIN_WIDTH, "headnorm", 0), (MAIN_WIDTH, "plain", 0)],
                                  w["k_gain_shared"], forget=w["forget"], prec=prec)
    q, qm = _norm_matmul(h, w["norm1_gain"][1:2], w["w_in_b"], 0,
                         [(MAIN_WIDTH, "headnorm", 0), (MEM_WIDTH, "headnorm", 1)],
                         jnp.concatenate([w["q_gain_b"][0:1], w["mem_q_gain"][1:2]], axis=0), prec=prec)
    y_main = fox(q, k, vv, lf, lft)
    y_mem = _mem_attn(qm, mem_k, mem_v, 1, nb, rows, prec)
    h = _out_proj(y_main, y_mem, w["w_out"], 1, h, prec)
    h = _hier_moe(h, w["norm2_gain"][1:2], w["w_router"][1], w["b_router"][1], w["w_gate"], w["w_up"], w["w_down"],
                  1, prec)
    return h, k, vv, lf, v_rows


def kernel(x_prompt, x_sample, cache_k, cache_v, cache_logf, cache_mem_k, cache_mem_v, page_table, mem_prompt, norm1_gain, norm2_gain, w_in_a, v_gain_a, w_s_a, b_s_a, w_in_b, q_gain_b, kv_norm_gain, w_kv_shared, b_forget, k_gain_shared, mem_norm_gain, w_mem_kv, mem_q_gain, mem_k_gain, w_out, w_router_group, b_router_group, w_router_expert, b_router_expert, w_gate, w_up, w_down):
    batch, seq, _ = x_prompt.shape
    dec_batch, dec_seq, _ = x_sample.shape
    n_phys = cache_k.shape[0]
    n_pages = page_table.shape[1]
    depth = norm1_gain.shape[0]
    assert depth == 2 and dec_seq <= SAMPLE_ROWS and seq % CHUNK == 0

    n_route = N_GROUPS + N_EXPERTS
    w_router = jnp.pad(jnp.concatenate([w_router_group, w_router_expert], axis=-1),
                       ((0, 0), (0, 0), (0, LANES - n_route)))
    b_router = jnp.pad(jnp.concatenate([b_router_group, b_router_expert], axis=-1),
                       ((0, 0), (0, LANES - n_route)))[:, None, :]
    w_f = w_kv_shared[:, 2 * MAIN_WIDTH:]
    forget = (jnp.pad(w_f, ((0, 0), (0, LANES - MAIN_HEADS))),
              jnp.pad(w_f.T, ((0, 2 * SUBLANES - MAIN_HEADS), (0, 0))),
              jnp.pad(b_forget, (0, LANES - MAIN_HEADS))[None, :],
              jnp.pad(b_forget, (0, 2 * SUBLANES - MAIN_HEADS))[:, None])
    common = dict(norm1_gain=norm1_gain, norm2_gain=norm2_gain, w_in_a=w_in_a, v_gain_a=v_gain_a, w_in_b=w_in_b,
                  q_gain_b=q_gain_b, kv_norm_gain=kv_norm_gain[None, :], w_kv_shared=w_kv_shared,
                  k_gain_shared=k_gain_shared[None, :], mem_q_gain=mem_q_gain, w_out=w_out,
                  w_router=w_router, b_router=b_router, w_gate=w_gate, w_up=w_up, w_down=w_down, forget=forget)

    t_p = batch * seq
    mem_tok = mem_prompt.reshape(batch * N_MEM, D_MODEL)
    mem_k_layers, mem_v_layers = [], []
    for l in range(depth):
        mk, mv = _norm_matmul(mem_tok, mem_norm_gain[l:l + 1], w_mem_kv, l,
                              [(MEM_WIDTH, "headnorm", 0), (MEM_WIDTH, "plain", 0)], mem_k_gain[l:l + 1])
        mem_k_layers.append(mk.reshape(batch, N_MEM, MEM_WIDTH))
        mem_v_layers.append(mv.reshape(batch, N_MEM, MEM_WIDTH))
    mem_k_p = jnp.stack(mem_k_layers)
    mem_v_p = jnp.stack(mem_v_layers)

    def fox_prompt(q, k, v, lf, lft):
        ct, c = _fox_cumsum(lft, lf, batch, seq)
        return _fox_prompt(q, k, v, ct, c, batch, seq)

    w_p = dict(common, w_mix=w_s_a[0], b_mix=b_s_a[0].T, gmlp_rows=CHUNK, gmlp_grp=CHUNK)
    y_p, k_p, v_p, lf_p, vrows_p = _trunk(x_prompt.reshape(t_p, D_MODEL), batch, seq, seq,
                                          mem_k_p, mem_v_p, fox_prompt, w_p, None)

    rows = SAMPLE_ROWS
    t_s = dec_batch * rows
    x_s = jnp.pad(x_sample, ((0, 0), (0, rows - dec_seq), (0, 0))).reshape(t_s, D_MODEL)
    pt_flat = page_table.reshape(dec_batch * n_pages)
    bias = _fox_past_bias(pt_flat, jnp.transpose(cache_logf, (2, 0, 1)), dec_batch, n_pages)
    cache_k_t = jnp.transpose(cache_k, (0, 2, 1, 3))
    cache_v_t = jnp.transpose(cache_v, (0, 2, 1, 3))

    def fox_sample(q, k, v, lf, lft):
        return _fox_sample(pt_flat, q, k, v, lf, bias, cache_k_t, cache_v_t, dec_batch, n_pages, dec_seq)

    w_s = dict(common, w_mix=jnp.tile(w_s_a[0][:, :rows, :rows], (1, dec_batch, dec_batch)),
               b_mix=jnp.tile(b_s_a[0][:, :rows].T, (dec_batch, 1)), gmlp_rows=t_s, gmlp_grp=rows)
    mem_k_s = cache_mem_k.reshape(depth, dec_batch, N_MEM, MEM_WIDTH)
    mem_v_s = cache_mem_v.reshape(depth, dec_batch, N_MEM, MEM_WIDTH)
    y_s, k_s, v_s, lf_s, vrows_s = _trunk(x_s, dec_batch, rows, dec_seq, mem_k_s, mem_v_s, fox_sample, w_s, HIGHEST)

    def unpad(a, *tail):
        return a.reshape((dec_batch, rows) + tail)[:, :dec_seq]

    return (y_p.reshape(batch, seq, D_MODEL),
            unpad(y_s, D_MODEL),
            k_p.reshape(batch, seq, MAIN_HEADS, HEAD_DIM),
            v_p.reshape(batch, seq, MAIN_HEADS, HEAD_DIM),
            lf_p[:, :MAIN_HEADS].reshape(batch, seq, MAIN_HEADS),
            unpad(k_s, MAIN_HEADS, HEAD_DIM),
            unpad(v_s, MAIN_HEADS, HEAD_DIM),
            unpad(lf_s[:, :MAIN_HEADS], MAIN_HEADS),
            mem_k_p.reshape(depth, batch, N_MEM, MEM_HEADS, HEAD_DIM),
            mem_v_p.reshape(depth, batch, N_MEM, MEM_HEADS, HEAD_DIM),
            vrows_p.reshape(batch, seq, MAIN_WIDTH)[:, -CHUNK:][None],
            unpad(vrows_s, MAIN_WIDTH)[None])
```

```python
import functools

import jax
import jax.numpy as jnp
from jax import lax
from jax.experimental import pallas as pl
from jax.experimental.pallas import tpu as pltpu

D_MODEL = 2048
HEAD_DIM = 128
MAIN_HEADS = 12
MEM_HEADS = 4
MAIN_WIDTH = MAIN_HEADS * HEAD_DIM
MEM_WIDTH = MEM_HEADS * HEAD_DIM
N_MEM = 256
N_GROUPS = 4
EXPERTS_PER_GROUP = 4
N_EXPERTS = N_GROUPS * EXPERTS_PER_GROUP
D_EXPERT = 512
CHUNK = 128
PAGE_SIZE = 128
EPS = 1e-6
ATTN_SCALE = HEAD_DIM ** -0.5

LANES = 128
SUBLANES = 8
VMEM_LIMIT_BYTES = 56 * 1024 * 1024
SAMPLE_ROWS = SUBLANES
N_CHUNKS = D_MODEL // LANES
ROW_PITCH = 24

F32 = jnp.float32
_NT = (((1,), (1,)), ((), ()))


def _params(n_axes):
    return pltpu.CompilerParams(dimension_semantics=("arbitrary",) * n_axes,
                                vmem_limit_bytes=VMEM_LIMIT_BYTES)


HIGHEST = lax.Precision.HIGHEST


def _dot(a, b, prec=None):
    return jnp.dot(a, b, preferred_element_type=F32, precision=prec)


def _dot_nt(a, b, prec=None):
    return lax.dot_general(a, b, _NT, preferred_element_type=F32, precision=prec)


def _matmul_only_dtype(prec):
    return jnp.bfloat16 if prec is None else F32


def _split_hi_lo(x):
    hi = x.astype(jnp.bfloat16).astype(F32)
    return hi, x - hi


def _split3(x):
    hi = x.astype(jnp.bfloat16).astype(F32)
    r = x - hi
    mid = r.astype(jnp.bfloat16).astype(F32)
    lo = (r - mid).astype(jnp.bfloat16).astype(F32)
    return hi, mid, lo


def _dot3_right(a_exact, x):
    hi, mid, lo = _split3(x)
    return _dot(a_exact, hi) + _dot(a_exact, mid) + _dot(a_exact, lo)


def _dot3_left(x, b_exact):
    hi, mid, lo = _split3(x)
    return _dot(hi, b_exact) + _dot(mid, b_exact) + _dot(lo, b_exact)


def _store_chunk_rows(ref, x):
    rows = x.shape[0]
    for c in range(N_CHUNKS):
        ref[pl.ds(c, rows, stride=N_CHUNKS), :] = x[:, c * LANES:(c + 1) * LANES]


def _load_gathered_rows(ref, rows):
    return jnp.concatenate([ref[pl.ds(c, rows, stride=ROW_PITCH), :] for c in range(N_CHUNKS)], axis=1)


def _head_norm(z, gain_row):
    return z * lax.rsqrt(jnp.mean(z * z, axis=-1, keepdims=True) + EPS) * gain_row


def _norm_matmul_kernel(*refs, segs, tn, with_forget, prec):
    it = iter(refs)
    x_ref, g_ref, w_ref, hg_ref = next(it), next(it), next(it), next(it)
    if with_forget:
        wf_ref, wft_ref, bfr_ref, bfc_ref = next(it), next(it), next(it), next(it)
    out_refs = [next(it) for _ in segs]
    if with_forget:
        lf_ref, lft_ref = next(it), next(it)
    xn_ref = next(it)
    j = pl.program_id(1)

    def project(xn, o_ref, kind, grow):
        z = _dot(xn, w_ref[...], prec)
        if kind == "plain":
            o_ref[...] = z
        elif kind == "gelu":
            o_ref[...] = jax.nn.gelu(z)
        else:
            gain = hg_ref[grow:grow + 1, :]
            for c in range(tn // HEAD_DIM):
                sl = slice(c * HEAD_DIM, (c + 1) * HEAD_DIM)
                o_ref[:, sl] = _head_norm(z[:, sl], gain)

    @pl.when(j == 0)
    def _first():
        x = x_ref[...]
        xn = x * lax.rsqrt(jnp.mean(x * x, axis=-1, keepdims=True) + EPS) * g_ref[...]
        xn_ref[...] = xn
        if with_forget:
            lf_ref[...] = jax.nn.log_sigmoid(_dot(xn, wf_ref[...], prec) + bfr_ref[...])
            lft_ref[...] = jax.nn.log_sigmoid(_dot_nt(wft_ref[...], xn, prec) + bfc_ref[...])
        project(xn, out_refs[0], segs[0][1], segs[0][2])

    lo = 0
    for (ncols, kind, grow), o_ref in zip(segs, out_refs):
        nt = ncols // tn

        @pl.when((j >= max(lo, 1)) & (j < lo + nt))
        def _rest(o_ref=o_ref, kind=kind, grow=grow):
            project(xn_ref[...], o_ref, kind, grow)

        lo += nt


def _norm_matmul(x, gain, w, layer, segs, head_gains, forget=None, prec=None):
    t = x.shape[0]
    tm = min(t, 1024)
    tn = 512
    n_tiles = sum(s[0] for s in segs) // tn
    with_forget = forget is not None
    in_specs = [
        pl.BlockSpec((tm, D_MODEL), lambda i, j: (i, 0)),
        pl.BlockSpec((1, D_MODEL), lambda i, j: (0, 0)),
        pl.BlockSpec((None, D_MODEL, tn), lambda i, j: (layer, 0, j)) if w.ndim == 3
        else pl.BlockSpec((D_MODEL, tn), lambda i, j: (0, j)),
        pl.BlockSpec(head_gains.shape, lambda i, j: (0, 0)),
    ]
    args = [x, gain, w, head_gains]
    if with_forget:
        wf, wft, bfr, bfc = forget
        in_specs += [pl.BlockSpec(wf.shape, lambda i, j: (0, 0)), pl.BlockSpec(wft.shape, lambda i, j: (0, 0)),
                     pl.BlockSpec(bfr.shape, lambda i, j: (0, 0)), pl.BlockSpec(bfc.shape, lambda i, j: (0, 0))]
        args += [wf, wft, bfr, bfc]
    out_shape, out_specs = [], []
    lo = 0
    for ncols, _, _ in segs:
        nt = ncols // tn
        out_shape.append(jax.ShapeDtypeStruct((t, ncols), F32))
        out_specs.append(pl.BlockSpec((tm, tn), lambda i, j, lo=lo, nt=nt: (i, jnp.clip(j - lo, 0, nt - 1))))
        lo += nt
    if with_forget:
        out_shape += [jax.ShapeDtypeStruct((t, LANES), F32), jax.ShapeDtypeStruct((2 * SUBLANES, t), F32)]
        out_specs += [pl.BlockSpec((tm, LANES), lambda i, j: (i, 0)),
                      pl.BlockSpec((2 * SUBLANES, tm), lambda i, j: (0, i))]
    return pl.pallas_call(
        functools.partial(_norm_matmul_kernel, segs=tuple(segs), tn=tn, with_forget=with_forget, prec=prec),
        grid=(t // tm, n_tiles),
        in_specs=in_specs,
        out_specs=out_specs,
        out_shape=out_shape,
        scratch_shapes=[pltpu.VMEM((tm, D_MODEL), F32)],
        compiler_params=_params(2),
    )(*args)


def _gmlp_kernel(u_ref, v_ref, vg_ref, w_ref, b_ref, y_ref, vn_ref, *, rows, grp, valid, prec):
    v = v_ref[...]
    vn = v * lax.rsqrt(jnp.mean(v * v, axis=-1, keepdims=True) + EPS) * vg_ref[...]
    vn_ref[...] = vn
    r = lax.broadcasted_iota(jnp.int32, (rows, rows), 0)
    c = lax.broadcasted_iota(jnp.int32, (rows, rows), 1)
    shift = grp.bit_length() - 1
    allowed = ((c & (grp - 1)) <= (r & (grp - 1))) & ((c & (grp - 1)) < valid)
    if grp < rows:
        allowed = allowed & ((r >> shift) == (c >> shift))
    for g in range(MAIN_HEADS):
        sl = slice(g * HEAD_DIM, (g + 1) * HEAD_DIM)
        w = jnp.where(allowed, w_ref[g], 0.0)
        mixed = _dot(w, vn[:, sl], prec) + b_ref[:, g:g + 1]
        y_ref[:, sl] = (u_ref[:, sl] * mixed).astype(y_ref.dtype)


def _gmlp(u, v, v_gain, w_mix, b_mix, rows, grp, valid, prec):
    t = u.shape[0]
    return pl.pallas_call(
        functools.partial(_gmlp_kernel, rows=rows, grp=grp, valid=valid, prec=prec),
        grid=(t // rows,),
        in_specs=[
            pl.BlockSpec((rows, MAIN_WIDTH), lambda i: (i, 0)),
            pl.BlockSpec((rows, MAIN_WIDTH), lambda i: (i, 0)),
            pl.BlockSpec((1, MAIN_WIDTH), lambda i: (0, 0)),
            pl.BlockSpec((MAIN_HEADS, rows, rows), lambda i: (0, 0, 0)),
            pl.BlockSpec((rows, MAIN_HEADS), lambda i: (0, 0)),
        ],
        out_specs=[pl.BlockSpec((rows, MAIN_WIDTH), lambda i: (i, 0)),
                   pl.BlockSpec((rows, MAIN_WIDTH), lambda i: (i, 0))],
        out_shape=[jax.ShapeDtypeStruct((t, MAIN_WIDTH), _matmul_only_dtype(prec)),
                   jax.ShapeDtypeStruct((t, MAIN_WIDTH), F32)],
        compiler_params=_params(1),
    )(u, v, v_gain, w_mix, b_mix)


def _mem_attn_kernel(q_ref, k_ref, v_ref, o_ref, *, prec):
    for h in range(MEM_HEADS):
        sl = slice(h * HEAD_DIM, (h + 1) * HEAD_DIM)
        s = _dot_nt(q_ref[:, sl], k_ref[:, sl], prec) * ATTN_SCALE
        e = jnp.exp(s - jnp.max(s, axis=-1, keepdims=True))
        p = e / jnp.sum(e, axis=-1, keepdims=True)
        o_ref[:, sl] = _dot(p, v_ref[:, sl], prec).astype(o_ref.dtype)


def _mem_attn(q, k, v, layer, nb, rows_per_batch, prec):
    tq = min(rows_per_batch, 1024)
    nq = rows_per_batch // tq
    kv_spec = pl.BlockSpec((None, None, N_MEM, MEM_WIDTH), lambda b, i: (layer, b, 0, 0))
    return pl.pallas_call(
        functools.partial(_mem_attn_kernel, prec=prec),
        grid=(nb, nq),
        in_specs=[pl.BlockSpec((tq, MEM_WIDTH), lambda b, i: (b * nq + i, 0)), kv_spec, kv_spec],
        out_specs=pl.BlockSpec((tq, MEM_WIDTH), lambda b, i: (b * nq + i, 0)),
        out_shape=jax.ShapeDtypeStruct(q.shape, _matmul_only_dtype(prec)),
        compiler_params=_params(2),
    )(q, k, v)


def _out_proj_kernel(ym_ref, ymem_ref, w1_ref, w2_ref, h_ref, o_ref, *, prec):
    w1 = w1_ref[...].astype(ym_ref.dtype)
    w2 = w2_ref[...].astype(ymem_ref.dtype)
    o_ref[...] = h_ref[...] + _dot(ym_ref[...], w1, prec) + _dot(ymem_ref[...], w2, prec)


def _out_proj(y_main, y_mem, w_out, layer, h, prec):
    t = h.shape[0]
    tm = min(t, 2048 if y_main.dtype == jnp.bfloat16 else 1024)
    tn = 512
    return pl.pallas_call(
        functools.partial(_out_proj_kernel, prec=prec),
        grid=(t // tm, D_MODEL // tn),
        in_specs=[
            pl.BlockSpec((tm, MAIN_WIDTH), lambda i, j: (i, 0)),
            pl.BlockSpec((tm, MEM_WIDTH), lambda i, j: (i, 0)),
            pl.BlockSpec((None, MAIN_WIDTH, tn), lambda i, j: (layer, 0, j)),
            pl.BlockSpec((None, MEM_WIDTH, tn), lambda i, j: (layer, MAIN_WIDTH // MEM_WIDTH, j)),
            pl.BlockSpec((tm, tn), lambda i, j: (i, j)),
        ],
        out_specs=pl.BlockSpec((tm, tn), lambda i, j: (i, j)),
        out_shape=jax.ShapeDtypeStruct((t, D_MODEL), F32),
        compiler_params=_params(2),
    )(y_main, y_mem, w_out, w_out, h)


def _router_kernel(h_ref, g_ref, wr_ref, br_ref, x2_ref, idx_ref, gate_ref, *, prec):
    x = h_ref[...]
    xn = x * lax.rsqrt(jnp.mean(x * x, axis=-1, keepdims=True) + EPS) * g_ref[...]
    _store_chunk_rows(x2_ref, xn)
    lg = _dot(xn, wr_ref[...], prec) + br_ref[...]
    lane = lax.broadcasted_iota(jnp.int32, lg.shape, 1)
    neg = -jnp.inf
    is_grp = lane < N_GROUPS
    gl = jnp.where(is_grp, lg, neg)
    ge = jnp.where(is_grp, jnp.exp(gl - jnp.max(gl, axis=-1, keepdims=True)), 0.0)
    pg = ge / jnp.sum(ge, axis=-1, keepdims=True)
    p_top = jnp.max(pg, axis=-1, keepdims=True)
    g_idx = jnp.min(jnp.where(is_grp & (pg == p_top), lane, LANES), axis=-1, keepdims=True)
    first = N_GROUPS + EXPERTS_PER_GROUP * g_idx
    in_grp = (lane >= first) & (lane < first + EXPERTS_PER_GROUP)
    e1 = jnp.max(jnp.where(in_grp, lg, neg), axis=-1, keepdims=True)
    i1 = jnp.min(jnp.where(in_grp & (lg == e1), lane, LANES), axis=-1, keepdims=True)
    rest = in_grp & (lane != i1)
    e2 = jnp.max(jnp.where(rest, lg, neg), axis=-1, keepdims=True)
    i2 = jnp.min(jnp.where(rest & (lg == e2), lane, LANES), axis=-1, keepdims=True)
    t2 = jnp.exp(e2 - e1)
    den = 1.0 + t2
    idx_ref[...] = jnp.where(lane == 0, i1 - N_GROUPS, jnp.where(lane == 1, i2 - N_GROUPS, 0))
    gate_ref[...] = jnp.where(lane == 0, p_top * (1.0 / den), jnp.where(lane == 1, p_top * (t2 / den), 0.0))


def _router(h, gain, wr, br, prec):
    t = h.shape[0]
    tm = min(t, 512)
    return pl.pallas_call(
        functools.partial(_router_kernel, prec=prec),
        grid=(t // tm,),
        in_specs=[pl.BlockSpec((tm, D_MODEL), lambda i: (i, 0)), pl.BlockSpec((1, D_MODEL), lambda i: (0, 0)),
                  pl.BlockSpec((D_MODEL, LANES), lambda i: (0, 0)), pl.BlockSpec((1, LANES), lambda i: (0, 0))],
        out_specs=[pl.BlockSpec((tm * N_CHUNKS, LANES), lambda i: (i, 0)), pl.BlockSpec((tm, LANES), lambda i: (i, 0)),
                   pl.BlockSpec((tm, LANES), lambda i: (i, 0))],
        out_shape=[jax.ShapeDtypeStruct((t * N_CHUNKS, LANES), F32), jax.ShapeDtypeStruct((t, LANES), jnp.int32),
                   jax.ShapeDtypeStruct((t, LANES), F32)],
        compiler_params=_params(1),
    )(h, gain, wr, br)


def _plan_kernel(e_ref, pos_ref, te_ref, nv_ref, rank_ref, *, t2, blk, tm):
    shift = tm.bit_length() - 1
    nblk = t2 // blk
    r = lax.broadcasted_iota(jnp.int32, (blk, blk), 0)
    c = lax.broadcasted_iota(jnp.int32, (blk, blk), 1)
    upper = (r <= c).astype(F32)
    sub = lax.broadcasted_iota(jnp.int32, (N_EXPERTS, blk), 0)
    carry = jnp.zeros((N_EXPERTS, 1), F32)
    for b in range(nblk):
        sl = slice(b * blk, (b + 1) * blk)
        oh = (sub == e_ref[:, sl]).astype(F32)
        cs = _dot(oh, upper) + carry
        rank_ref[:, sl] = jnp.sum(oh * (cs - 1.0), axis=0, keepdims=True)
        carry = cs[:, blk - 1:blk]
    counts = carry.astype(jnp.int32)
    padc = ((counts + (tm - 1)) >> shift) << shift
    sub1 = lax.broadcasted_iota(jnp.int32, (N_EXPERTS, 1), 0)
    off = jnp.zeros((N_EXPERTS, 1), jnp.int32)
    run = jnp.zeros((1, 1), jnp.int32)
    for e in range(N_EXPERTS):
        off = jnp.where(sub1 == e, run, off)
        run = run + padc[e:e + 1, :]
    ends = off + padc
    for b in range(nblk):
        sl = slice(b * blk, (b + 1) * blk)
        offv = jnp.sum(jnp.where(sub == e_ref[:, sl], off, 0), axis=0, keepdims=True)
        pos_ref[:, sl] = rank_ref[:, sl].astype(jnp.int32) + offv
    lane = lax.broadcasted_iota(jnp.int32, (N_EXPERTS, LANES), 1)
    lane1 = lax.broadcasted_iota(jnp.int32, (1, LANES), 1)
    nv = run >> shift
    te = jnp.sum((ends <= lane * tm).astype(jnp.int32), axis=0, keepdims=True)
    te_last = jnp.sum((ends <= (nv - 1) * tm).astype(jnp.int32), axis=0, keepdims=True)
    te_ref[...] = jnp.minimum(jnp.where(lane1 >= nv, te_last, te), N_EXPERTS - 1)
    nv_ref[...] = jnp.broadcast_to(nv, (1, LANES))


def _plan(e_flat, tm):
    t2 = e_flat.shape[1]
    blk = min(t2, 512)
    return pl.pallas_call(
        functools.partial(_plan_kernel, t2=t2, blk=blk, tm=tm),
        out_shape=[jax.ShapeDtypeStruct((1, t2), jnp.int32), jax.ShapeDtypeStruct((1, LANES), jnp.int32),
                   jax.ShapeDtypeStruct((1, LANES), jnp.int32)],
        scratch_shapes=[pltpu.VMEM((1, t2), F32)],
        compiler_params=pltpu.CompilerParams(vmem_limit_bytes=VMEM_LIMIT_BYTES),
    )(e_flat)


def _invert_kernel(pos_ref, src_ref, *, t, t2, n_rows):
    def zero(p, carry):
        src_ref[p] = 0
        return carry

    lax.fori_loop(0, n_rows, zero, 0, unroll=32)

    def put(i, carry):
        src_ref[pos_ref[i]] = N_CHUNKS * jnp.where(i >= t, i - t, i)
        return carry

    lax.fori_loop(0, t2, put, 0, unroll=32)


def _invert(pos, t, n_rows):
    return pl.pallas_call(
        functools.partial(_invert_kernel, t=t, t2=pos.shape[0], n_rows=n_rows),
        grid_spec=pltpu.PrefetchScalarGridSpec(
            num_scalar_prefetch=1,
            grid=(1,),
            in_specs=[],
            out_specs=pl.BlockSpec(memory_space=pltpu.SMEM),
        ),
        out_shape=jax.ShapeDtypeStruct((n_rows,), jnp.int32),
        compiler_params=_params(1),
    )(pos)


def _moe_ffn_kernel(te_ref, nv_ref, src_ref, x_ref, wg_hbm, wu_hbm, wd_hbm, y_ref,
                    xbuf_ref, wg_ref, wu_ref, wd_ref, wslot_ref, sems, wsems, *, tm, layer, prec):
    i = pl.program_id(0)
    nv = nv_ref[0]
    slot = i % 2
    expert = te_ref[i]

    def start_tile(tile, dst_slot):
        base = tile * tm
        for r in range(tm):
            src = pl.multiple_of(src_ref[base + r], N_CHUNKS)
            pltpu.make_async_copy(x_ref.at[pl.ds(src, N_CHUNKS)],
                                  xbuf_ref.at[dst_slot, pl.ds(r * ROW_PITCH, N_CHUNKS)],
                                  sems.at[dst_slot]).start(priority=r % 2)

    def wait_tile():
        done = xbuf_ref.at[slot, pl.ds(0, tm * N_CHUNKS)]
        pltpu.make_async_copy(done, done, sems.at[slot]).wait()

    def weight_copies(e, ws):
        return (pltpu.make_async_copy(wg_hbm.at[layer, e], wg_ref.at[ws], wsems.at[ws]),
                pltpu.make_async_copy(wu_hbm.at[layer, e], wu_ref.at[ws], wsems.at[ws]),
                pltpu.make_async_copy(wd_hbm.at[layer, e], wd_ref.at[ws], wsems.at[ws]))

    @pl.when(i == 0)
    def _():
        start_tile(0, 0)
        wslot_ref[0] = 1
        for cp in weight_copies(expert, 0):
            cp.start()

    is_first = (i < nv) & ((i == 0) | (te_ref[jnp.maximum(i - 1, 0)] != expert))

    @pl.when(is_first)
    def _():
        ws = 1 - wslot_ref[0]
        wslot_ref[0] = ws
        for cp in weight_copies(expert, ws):
            cp.wait()
        nxt = lax.while_loop(lambda j: (j < nv) & (te_ref[jnp.minimum(j, nv - 1)] == expert), lambda j: j + 1, i + 1)

        @pl.when(nxt < nv)
        def _():
            for cp in weight_copies(te_ref[nxt], 1 - ws):
                cp.start()

    def ffn():
        ws = wslot_ref[0]
        x = _load_gathered_rows(xbuf_ref.at[slot], tm)
        hdn = jax.nn.silu(_dot(x, wg_ref[ws], prec)) * _dot(x, wu_ref[ws], prec)
        _store_chunk_rows(y_ref, _dot(hdn, wd_ref[ws], prec))

    @pl.when(i + 1 < nv)
    def _():
        wait_tile()
        start_tile(i + 1, 1 - slot)
        ffn()

    @pl.when(i + 1 == nv)
    def _():
        wait_tile()
        ffn()

    @pl.when(i >= nv)
    def _():
        y_ref[...] = jnp.zeros_like(y_ref)


def _moe_ffn(te, nv, src, x2, w_gate, w_up, w_down, layer, tm, prec):
    n_rows = src.shape[0]
    n_tiles = n_rows // tm
    any_spec = pl.BlockSpec(memory_space=pl.ANY)
    return pl.pallas_call(
        functools.partial(_moe_ffn_kernel, tm=tm, layer=layer, prec=prec),
        grid_spec=pltpu.PrefetchScalarGridSpec(
            num_scalar_prefetch=3,
            grid=(n_tiles,),
            in_specs=[any_spec, any_spec, any_spec, any_spec],
            out_specs=pl.BlockSpec((tm * N_CHUNKS, LANES), lambda i, te, nv, src: (i, 0)),
            scratch_shapes=[pltpu.VMEM((2, tm * ROW_PITCH, LANES), F32),
                            pltpu.VMEM((2, D_MODEL, D_EXPERT), F32), pltpu.VMEM((2, D_MODEL, D_EXPERT), F32),
                            pltpu.VMEM((2, D_EXPERT, D_MODEL), F32), pltpu.SMEM((1,), jnp.int32),
                            pltpu.SemaphoreType.DMA((2,)), pltpu.SemaphoreType.DMA((2,))],
        ),
        out_shape=jax.ShapeDtypeStruct((n_rows * N_CHUNKS, LANES), F32),
        compiler_params=_params(1),
    )(te, nv, src, x2, w_gate, w_up, w_down)


def _combine_kernel(pos_ref, h_ref, g_ref, y_ref, o_ref, y0_ref, y1_ref, sems, *, t, tm):
    base = pl.program_id(0) * tm
    for k, buf in enumerate((y0_ref, y1_ref)):
        for r in range(tm):
            src = pl.multiple_of(pos_ref[k * t + base + r] * N_CHUNKS, N_CHUNKS)
            pltpu.make_async_copy(y_ref.at[pl.ds(src, N_CHUNKS)], buf.at[pl.ds(r * ROW_PITCH, N_CHUNKS)],
                                  sems.at[k]).start(priority=k)
    for k, buf in enumerate((y0_ref, y1_ref)):
        done = buf.at[pl.ds(0, tm * N_CHUNKS)]
        pltpu.make_async_copy(done, done, sems.at[k]).wait()
    g = g_ref[...]
    o_ref[...] = (h_ref[...] + g[:, 0:1] * _load_gathered_rows(y0_ref, tm)
                  + g[:, 1:2] * _load_gathered_rows(y1_ref, tm))


def _combine(pos, h, gates, y):
    t = h.shape[0]
    tm = min(t, 256)
    return pl.pallas_call(
        functools.partial(_combine_kernel, t=t, tm=tm),
        grid_spec=pltpu.PrefetchScalarGridSpec(
            num_scalar_prefetch=1,
            grid=(t // tm,),
            in_specs=[pl.BlockSpec((tm, D_MODEL), lambda i, pos: (i, 0)),
                      pl.BlockSpec((tm, LANES), lambda i, pos: (i, 0)),
                      pl.BlockSpec(memory_space=pl.ANY)],
            out_specs=pl.BlockSpec((tm, D_MODEL), lambda i, pos: (i, 0)),
            scratch_shapes=[pltpu.VMEM((tm * ROW_PITCH, LANES), F32), pltpu.VMEM((tm * ROW_PITCH, LANES), F32),
                            pltpu.SemaphoreType.DMA((2,))],
        ),
        out_shape=jax.ShapeDtypeStruct((t, D_MODEL), F32),
        compiler_params=_params(1),
    )(pos, h, gates, y)


def _hier_moe(h, gain, wr, br, w_gate, w_up, w_down, layer, prec):
    t = h.shape[0]
    tm = 256 if t >= 1024 else 16
    n_tiles = (2 * t) // tm + N_EXPERTS
    assert n_tiles <= LANES
    x2, idx, gates = _router(h, gain, wr, br, prec)
    e_flat = idx[:, :2].T.reshape(1, 2 * t)
    pos, te, nv = _plan(e_flat, tm)
    pos = pos.reshape(2 * t)
    te = te[0, :n_tiles]
    nv = nv[0, :1]
    src = _invert(pos, t, n_tiles * tm)
    y = _moe_ffn(te, nv, src, x2, w_gate, w_up, w_down, layer, tm, prec)
    return _combine(pos, h, gates, y)


def _fox_cumsum_kernel(lft_ref, lf_ref, ct_ref, c_ref, *, s, blk):
    r = lax.broadcasted_iota(jnp.int32, (blk, blk), 0)
    c = lax.broadcasted_iota(jnp.int32, (blk, blk), 1)
    upper = (r <= c).astype(F32)
    lower = (r >= c).astype(F32)
    carry_t = jnp.zeros((2 * SUBLANES, 1), F32)
    carry = jnp.zeros((1, LANES), F32)
    for b in range(s // blk):
        sl = slice(b * blk, (b + 1) * blk)
        ct = _dot3_left(lft_ref[:, sl], upper) + carry_t
        ct_ref[:, sl] = ct
        carry_t = ct[:, blk - 1:blk]
        cc = _dot3_right(lower, lf_ref[sl, :]) + carry
        c_ref[sl, :] = cc
        carry = cc[blk - 1:blk, :]


def _fox_cumsum(lft, lf, nb, s):
    t = lf.shape[0]
    blk = min(s, 256)
    return pl.pallas_call(
        functools.partial(_fox_cumsum_kernel, s=s, blk=blk),
        grid=(nb,),
        in_specs=[pl.BlockSpec((2 * SUBLANES, s), lambda b: (0, b)), pl.BlockSpec((s, LANES), lambda b: (b, 0))],
        out_specs=[pl.BlockSpec((2 * SUBLANES, s), lambda b: (0, b)), pl.BlockSpec((s, LANES), lambda b: (b, 0))],
        out_shape=[jax.ShapeDtypeStruct((2 * SUBLANES, t), F32), jax.ShapeDtypeStruct((t, LANES), F32)],
        compiler_params=_params(1),
    )(lft, lf)


def _fox_prompt_kernel(q_ref, k_ref, v_ref, ct_ref, c_ref, o_ref, *, s, tq):
    h = pl.program_id(1)
    lane = lax.broadcasted_iota(jnp.int32, (s, LANES), 1)
    c_col = jnp.sum(jnp.where(lane == h, c_ref[...], 0.0), axis=1, keepdims=True)
    c_row = ct_ref[pl.ds(h, 1), :]
    ri = lax.broadcasted_iota(jnp.int32, (tq, tq), 0)
    ci = lax.broadcasted_iota(jnp.int32, (tq, tq), 1)
    causal = ci <= ri
    for qi in range(s // tq):
        qs = slice(qi * tq, (qi + 1) * tq)
        q = q_ref[qs, :]
        m = jnp.full((tq, 1), -jnp.inf, F32)
        l = jnp.zeros((tq, 1), F32)
        acc = jnp.zeros((tq, HEAD_DIM), F32)
        for kj in range(qi + 1):
            ks = slice(kj * tq, (kj + 1) * tq)
            sc = lax.dot_general(q, k_ref[ks, :], _NT, preferred_element_type=F32) * ATTN_SCALE
            sc = sc + c_col[qs, :] - c_row[:, ks]
            if kj == qi:
                sc = jnp.where(causal, sc, -jnp.inf)
            m_new = jnp.maximum(m, jnp.max(sc, axis=-1, keepdims=True))
            alpha = jnp.exp(m - m_new)
            p = jnp.exp(sc - m_new)
            l = alpha * l + jnp.sum(p, axis=-1, keepdims=True)
            acc = alpha * acc + _dot(p, v_ref[ks, :])
            m = m_new
        o_ref[qs, :] = (acc / l).astype(o_ref.dtype)


def _fox_prompt(q, k, v, ct, c, nb, s):
    t = q.shape[0]
    tq = min(s, 512)
    head_spec = pl.BlockSpec((s, HEAD_DIM), lambda b, h: (b, h))
    return pl.pallas_call(
        functools.partial(_fox_prompt_kernel, s=s, tq=tq),
        grid=(nb, MAIN_HEADS),
        in_specs=[head_spec, head_spec, head_spec,
                  pl.BlockSpec((2 * SUBLANES, s), lambda b, h: (0, b)),
                  pl.BlockSpec((s, LANES), lambda b, h: (b, 0))],
        out_specs=head_spec,
        out_shape=jax.ShapeDtypeStruct((t, MAIN_WIDTH), _matmul_only_dtype(None)),
        compiler_params=_params(2),
    )(q, k, v, ct, c)


def _fox_past_bias_kernel(pt_ref, lfc_ref, o_ref, buf_ref, sem, *, n_pages):
    b = pl.program_id(0)

    def page_copy(p):
        page = pt_ref[b * n_pages + p]
        return pltpu.make_async_copy(lfc_ref.at[:, pl.ds(page, 1), :], buf_ref.at[:, pl.ds(p, 1), :], sem)

    def start(p, carry):
        page_copy(p).start()
        return carry

    def wait(p, carry):
        page_copy(p).wait()
        return carry

    lax.fori_loop(0, n_pages, start, 0)
    r = lax.broadcasted_iota(jnp.int32, (PAGE_SIZE, PAGE_SIZE), 0)
    c = lax.broadcasted_iota(jnp.int32, (PAGE_SIZE, PAGE_SIZE), 1)
    after_in_page = (r > c).astype(F32)
    pr = lax.broadcasted_iota(jnp.int32, (n_pages, n_pages), 0)
    pc = lax.broadcasted_iota(jnp.int32, (n_pages, n_pages), 1)
    later_pages = (pc > pr).astype(F32)
    lax.fori_loop(0, n_pages, wait, 0)
    for h in range(MAIN_HEADS):
        lp = buf_ref[h]
        tot = jnp.broadcast_to(jnp.sum(lp, axis=1, keepdims=True), lp.shape)
        o_ref[h] = _dot3_left(lp, after_in_page) + _dot3_right(later_pages, tot)


def _fox_past_bias(page_table_flat, cache_logf_t, nb, n_pages):
    return pl.pallas_call(
        functools.partial(_fox_past_bias_kernel, n_pages=n_pages),
        grid_spec=pltpu.PrefetchScalarGridSpec(
            num_scalar_prefetch=1,
            grid=(nb,),
            in_specs=[pl.BlockSpec(memory_space=pl.ANY)],
            out_specs=pl.BlockSpec((None, MAIN_HEADS, n_pages, PAGE_SIZE), lambda b, pt: (b, 0, 0, 0)),
            scratch_shapes=[pltpu.VMEM((MAIN_HEADS, n_pages, PAGE_SIZE), F32), pltpu.SemaphoreType.DMA(())],
        ),
        out_shape=jax.ShapeDtypeStruct((nb, MAIN_HEADS, n_pages, PAGE_SIZE), F32),
        compiler_params=_params(1),
    )(page_table_flat, cache_logf_t)


def _fox_sample_kernel(*refs, ppb, n_steps, valid):
    pt_ref, q_ref, kn_ref, vn_ref, lf_ref, bias_ref = refs[:6]
    k_refs = refs[6:6 + ppb]
    v_refs = refs[6 + ppb:6 + 2 * ppb]
    o_ref, m_ref, l_ref, acc_ref = refs[6 + 2 * ppb:]
    g = pl.program_id(1)
    rows = SAMPLE_ROWS

    @pl.when(g == 0)
    def _():
        m_ref[...] = jnp.full(m_ref.shape, -jnp.inf, F32)
        l_ref[...] = jnp.zeros(l_ref.shape, F32)
        acc_ref[...] = jnp.zeros(acc_ref.shape, F32)

    r8 = lax.broadcasted_iota(jnp.int32, (rows, rows), 0)
    c8 = lax.broadcasted_iota(jnp.int32, (rows, rows), 1)
    new_mask = (c8 <= r8) & (c8 < valid)
    c_new = _dot3_right(new_mask.astype(F32), lf_ref[...])

    def qk3(q_hi, q_lo, k):
        k_hi, k_lo = _split_hi_lo(k)
        a = _dot_nt(jnp.concatenate([q_hi, q_lo], axis=0), k_hi)
        return a[:rows] + a[rows:] + _dot_nt(q_hi, k_lo)

    def pv3(p, v):
        p_hi, p_lo = _split_hi_lo(p)
        a = _dot(jnp.concatenate([p_hi, p_lo], axis=0), v)
        return a[:rows] + a[rows:]

    all_scores = []
    for h in range(MAIN_HEADS):
        q_hi, q_lo = _split_hi_lo(q_ref[:, h * HEAD_DIM:(h + 1) * HEAD_DIM])
        cq = c_new[:, h:h + 1]
        all_scores.append([qk3(q_hi, q_lo, k_refs[u][h]) * ATTN_SCALE + cq + bias_ref[h, u:u + 1, :]
                           for u in range(ppb)])
    all_probs, all_alpha = [], []
    for h in range(MAIN_HEADS):
        m_old = m_ref[h, :, 0:1]
        m_new = m_old
        for sc in all_scores[h]:
            m_new = jnp.maximum(m_new, jnp.max(sc, axis=-1, keepdims=True))
        alpha = jnp.exp(m_old - m_new)
        probs = [jnp.exp(sc - m_new) for sc in all_scores[h]]
        l_new = alpha * l_ref[h, :, 0:1]
        for p in probs:
            l_new = l_new + jnp.sum(p, axis=-1, keepdims=True)
        m_ref[h] = jnp.broadcast_to(m_new, (rows, LANES))
        l_ref[h] = jnp.broadcast_to(l_new, (rows, LANES))
        all_probs.append(probs)
        all_alpha.append(alpha)
    for h in range(MAIN_HEADS):
        sl = slice(h * HEAD_DIM, (h + 1) * HEAD_DIM)
        acc = all_alpha[h] * acc_ref[:, sl]
        for u in range(ppb):
            acc = acc + pv3(all_probs[h][u], v_refs[u][h])
        acc_ref[:, sl] = acc

    @pl.when(g == n_steps - 1)
    def _():
        for h in range(MAIN_HEADS):
            sl = slice(h * HEAD_DIM, (h + 1) * HEAD_DIM)
            qh = q_ref[:, sl]
            cq = c_new[:, h:h + 1]
            cq_row = jnp.sum(jnp.where(r8 == c8, jnp.broadcast_to(cq, (rows, rows)), 0.0), axis=0, keepdims=True)
            sn = _dot_nt(qh, kn_ref[:, sl], HIGHEST) * ATTN_SCALE + cq - cq_row
            sn = jnp.where(new_mask, sn, -jnp.inf)
            m_old = m_ref[h, :, 0:1]
            m_new = jnp.maximum(m_old, jnp.max(sn, axis=-1, keepdims=True))
            alpha = jnp.exp(m_old - m_new)
            p = jnp.exp(sn - m_new)
            l_new = alpha * l_ref[h, :, 0:1] + jnp.sum(p, axis=-1, keepdims=True)
            acc = alpha * acc_ref[:, sl] + _dot(p, vn_ref[:, sl], HIGHEST)
            o_ref[:, sl] = acc / l_new


def _fox_sample(page_table_flat, q, k_new, v_new, lf, bias, cache_k_t, cache_v_t, nb, n_pages, valid):
    ppb = 8 if n_pages % 8 == 0 else n_pages
    n_steps = n_pages // ppb
    row_spec = pl.BlockSpec((SAMPLE_ROWS, MAIN_WIDTH), lambda b, g, pt: (b, 0))

    def page_spec(u):
        return pl.BlockSpec((None, MAIN_HEADS, PAGE_SIZE, HEAD_DIM),
                            lambda b, g, pt, u=u: (pt[b * n_pages + g * ppb + u], 0, 0, 0))

    in_specs = [row_spec, row_spec, row_spec,
                pl.BlockSpec((SAMPLE_ROWS, LANES), lambda b, g, pt: (b, 0)),
                pl.BlockSpec((None, MAIN_HEADS, ppb, PAGE_SIZE), lambda b, g, pt: (b, 0, g, 0))]
    in_specs += [page_spec(u) for u in range(ppb)] + [page_spec(u) for u in range(ppb)]
    return pl.pallas_call(
        functools.partial(_fox_sample_kernel, ppb=ppb, n_steps=n_steps, valid=valid),
        grid_spec=pltpu.PrefetchScalarGridSpec(
            num_scalar_prefetch=1,
            grid=(nb, n_steps),
            in_specs=in_specs,
            out_specs=row_spec,
            scratch_shapes=[pltpu.VMEM((MAIN_HEADS, SAMPLE_ROWS, LANES), F32),
                            pltpu.VMEM((MAIN_HEADS, SAMPLE_ROWS, LANES), F32),
                            pltpu.VMEM((SAMPLE_ROWS, MAIN_WIDTH), F32)],
        ),
        out_shape=jax.ShapeDtypeStruct(q.shape, F32),
        compiler_params=_params(2),
    )(page_table_flat, q, k_new, v_new, lf, bias, *([cache_k_t] * ppb), *([cache_v_t] * ppb))


def _trunk(x, nb, rows, valid, mem_k, mem_v, fox, w, prec):
    u, v, qm = _norm_matmul(x, w["norm1_gain"][0:1], w["w_in_a"], 0,
                            [(MAIN_WIDTH, "gelu", 0), (MAIN_WIDTH, "gelu", 0), (MEM_WIDTH, "headnorm", 0)],
                            w["mem_q_gain"][0:1], prec=prec)
    y_main, v_rows = _gmlp(u, v, w["v_gain_a"][0:1], w["w_mix"], w["b_mix"], w["gmlp_rows"], w["gmlp_grp"], valid,
                           prec)
    y_mem = _mem_attn(qm, mem_k, mem_v, 0, nb, rows, prec)
    h = _out_proj(y_main, y_mem, w["w_out"], 0, x, prec)
    h = _hier_moe(h, w["norm2_gain"][0:1], w["w_router"][0], w["b_router"][0], w["w_gate"], w["w_up"], w["w_down"],
                  0, prec)
    k, vv, lf, lft = _norm_matmul(h, w["kv_norm_gain"], w["w_kv_shared"], 0,
                                  [(MAIN_WIDTH, "headnorm", 0), (MAIN_WIDTH, "plain", 0)],
                                  w["k_gain_shared"], forget=w["forget"], prec=prec)
    q, qm = _norm_matmul(h, w["norm1_gain"][1:2], w["w_in_b"], 0,
                         [(MAIN_WIDTH, "headnorm", 0), (MEM_WIDTH, "headnorm", 1)],
                         jnp.concatenate([w["q_gain_b"][0:1], w["mem_q_gain"][1:2]], axis=0), prec=prec)
    y_main = fox(q, k, vv, lf, lft)
    y_mem = _mem_attn(qm, mem_k, mem_v, 1, nb, rows, prec)
    h = _out_proj(y_main, y_mem, w["w_out"], 1, h, prec)
    h = _hier_moe(h, w["norm2_gain"][1:2], w["w_router"][1], w["b_router"][1], w["w_gate"], w["w_up"], w["w_down"],
                  1, prec)
    return h, k, vv, lf, v_rows


def kernel(x_prompt, x_sample, cache_k, cache_v, cache_logf, cache_mem_k, cache_mem_v, page_table, mem_prompt, norm1_gain, norm2_gain, w_in_a, v_gain_a, w_s_a, b_s_a, w_in_b, q_gain_b, kv_norm_gain, w_kv_shared, b_forget, k_gain_shared, mem_norm_gain, w_mem_kv, mem_q_gain, mem_k_gain, w_out, w_router_group, b_router_group, w_router_expert, b_router_expert, w_gate, w_up, w_down):
    batch, seq, _ = x_prompt.shape
    dec_batch, dec_seq, _ = x_sample.shape
    n_phys = cache_k.shape[0]
    n_pages = page_table.shape[1]
    depth = norm1_gain.shape[0]
    assert depth == 2 and dec_seq <= SAMPLE_ROWS and seq % CHUNK == 0

    n_route = N_GROUPS + N_EXPERTS
    w_router = jnp.pad(jnp.concatenate([w_router_group, w_router_expert], axis=-1),
                       ((0, 0), (0, 0), (0, LANES - n_route)))
    b_router = jnp.pad(jnp.concatenate([b_router_group, b_router_expert], axis=-1),
                       ((0, 0), (0, LANES - n_route)))[:, None, :]
    w_f = w_kv_shared[:, 2 * MAIN_WIDTH:]
    forget = (jnp.pad(w_f, ((0, 0), (0, LANES - MAIN_HEADS))),
              jnp.pad(w_f.T, ((0, 2 * SUBLANES - MAIN_HEADS), (0, 0))),
              jnp.pad(b_forget, (0, LANES - MAIN_HEADS))[None, :],
              jnp.pad(b_forget, (0, 2 * SUBLANES - MAIN_HEADS))[:, None])
    common = dict(norm1_gain=norm1_gain, norm2_gain=norm2_gain, w_in_a=w_in_a, v_gain_a=v_gain_a, w_in_b=w_in_b,
                  q_gain_b=q_gain_b, kv_norm_gain=kv_norm_gain[None, :], w_kv_shared=w_kv_shared,
                  k_gain_shared=k_gain_shared[None, :], mem_q_gain=mem_q_gain, w_out=w_out,
                  w_router=w_router, b_router=b_router, w_gate=w_gate, w_up=w_up, w_down=w_down, forget=forget)

    t_p = batch * seq
    mem_tok = mem_prompt.reshape(batch * N_MEM, D_MODEL)
    mem_k_layers, mem_v_layers = [], []
    for l in range(depth):
        mk, mv = _norm_matmul(mem_tok, mem_norm_gain[l:l + 1], w_mem_kv, l,
                              [(MEM_WIDTH, "headnorm", 0), (MEM_WIDTH, "plain", 0)], mem_k_gain[l:l + 1])
        mem_k_layers.append(mk.reshape(batch, N_MEM, MEM_WIDTH))
        mem_v_layers.append(mv.reshape(batch, N_MEM, MEM_WIDTH))
    mem_k_p = jnp.stack(mem_k_layers)
    mem_v_p = jnp.stack(mem_v_layers)

    def fox_prompt(q, k, v, lf, lft):
        ct, c = _fox_cumsum(lft, lf, batch, seq)
        return _fox_prompt(q, k, v, ct, c, batch, seq)

    w_p = dict(common, w_mix=w_s_a[0], b_mix=b_s_a[0].T, gmlp_rows=CHUNK, gmlp_grp=CHUNK)
    y_p, k_p, v_p, lf_p, vrows_p = _trunk(x_prompt.reshape(t_p, D_MODEL), batch, seq, seq,
                                          mem_k_p, mem_v_p, fox_prompt, w_p, None)

    rows = SAMPLE_ROWS
    t_s = dec_batch * rows
    x_s = jnp.pad(x_sample, ((0, 0), (0, rows - dec_seq), (0, 0))).reshape(t_s, D_MODEL)
    pt_flat = page_table.reshape(dec_batch * n_pages)
    bias = _fox_past_bias(pt_flat, jnp.transpose(cache_logf, (2, 0, 1)), dec_batch, n_pages)
    cache_k_t = jnp.transpose(cache_k, (0, 2, 1, 3))
    cache_v_t = jnp.transpose(cache_v, (0, 2, 1, 3))

    def fox_sample(q, k, v, lf, lft):
        return _fox_sample(pt_flat, q, k, v, lf, bias, cache_k_t, cache_v_t, dec_batch, n_pages, dec_seq)

    w_s = dict(common, w_mix=jnp.tile(w_s_a[0][:, :rows, :rows], (1, dec_batch, dec_batch)),
               b_mix=jnp.tile(b_s_a[0][:, :rows].T, (dec_batch, 1)), gmlp_rows=t_s, gmlp_grp=rows)
    mem_k_s = cache_mem_k.reshape(depth, dec_batch, N_MEM, MEM_WIDTH)
    mem_v_s = cache_mem_v.reshape(depth, dec_batch, N_MEM, MEM_WIDTH)
    y_s, k_s, v_s, lf_s, vrows_s = _trunk(x_s, dec_batch, rows, dec_seq, mem_k_s, mem_v_s, fox_sample, w_s, HIGHEST)

    def unpad(a, *tail):
        return a.reshape((dec_batch, rows) + tail)[:, :dec_seq]

    return (y_p.reshape(batch, seq, D_MODEL),
            unpad(y_s, D_MODEL),
            k_p.reshape(batch, seq, MAIN_HEADS, HEAD_DIM),
            v_p.reshape(batch, seq, MAIN_HEADS, HEAD_DIM),
            lf_p[:, :MAIN_HEADS].reshape(batch, seq, MAIN_HEADS),
            unpad(k_s, MAIN_HEADS, HEAD_DIM),
            unpad(v_s, MAIN_HEADS, HEAD_DIM),
            unpad(lf_s[:, :MAIN_HEADS], MAIN_HEADS),
            mem_k_p.reshape(depth, batch, N_MEM, MEM_HEADS, HEAD_DIM),
            mem_v_p.reshape(depth, batch, N_MEM, MEM_HEADS, HEAD_DIM),
            vrows_p.reshape(batch, seq, MAIN_WIDTH)[:, -CHUNK:][None],
            unpad(vrows_s, MAIN_WIDTH)[None])
```

```python
import functools

import jax
import jax.numpy as jnp
from jax import lax
from jax.experimental import pallas as pl
from jax.experimental.pallas import tpu as pltpu

D_MODEL = 2048
HEAD_DIM = 128
MAIN_HEADS = 12
MEM_HEADS = 4
MAIN_WIDTH = MAIN_HEADS * HEAD_DIM
MEM_WIDTH = MEM_HEADS * HEAD_DIM
N_MEM = 256
N_GROUPS = 4
EXPERTS_PER_GROUP = 4
N_EXPERTS = N_GROUPS * EXPERTS_PER_GROUP
D_EXPERT = 512
CHUNK = 128
PAGE_SIZE = 128
EPS = 1e-6
ATTN_SCALE = HEAD_DIM ** -0.5

LANES = 128
SUBLANES = 8
VMEM_LIMIT_BYTES = 56 * 1024 * 1024
SAMPLE_ROWS = SUBLANES
N_CHUNKS = D_MODEL // LANES
ROW_PITCH = 24

F32 = jnp.float32
_NT = (((1,), (1,)), ((), ()))


def _params(n_axes):
    return pltpu.CompilerParams(dimension_semantics=("arbitrary",) * n_axes,
                                vmem_limit_bytes=VMEM_LIMIT_BYTES)


HIGHEST = lax.Precision.HIGHEST


SPLIT3 = "split3"


def _dot(a, b, prec=None):
    if prec == SPLIT3:
        return _dot_split3(a, b, _dot)
    return jnp.dot(a, b, preferred_element_type=F32, precision=prec)


def _dot_nt(a, b, prec=None):
    if prec == SPLIT3:
        return _dot_split3(a, b, _dot_nt)
    return lax.dot_general(a, b, _NT, preferred_element_type=F32, precision=prec)


def _dot_tn(a, b, prec=None):
    dims = (((0,), (1,)), ((), ()))
    if prec == SPLIT3:
        a_hi, a_lo = _split_hi_lo(a)
        b_hi, b_lo = _split_hi_lo(b)
        return _dot_tn(a_hi, b_hi) + _dot_tn(a_lo, b_hi) + _dot_tn(a_hi, b_lo)
    return lax.dot_general(a, b, dims, preferred_element_type=F32, precision=prec)


def _dot_split3(a, b, dot):
    a_hi, a_lo = _split_hi_lo(a)
    b_hi, b_lo = _split_hi_lo(b)
    m = a.shape[0]
    s = dot(jnp.concatenate([a_hi, a_lo], axis=0), b_hi)
    return s[:m] + s[m:] + dot(a_hi, b_lo)


def _matmul_only_dtype(prec):
    return jnp.bfloat16 if prec is None else F32


def _split_hi_lo(x):
    hi = x.astype(jnp.bfloat16).astype(F32)
    return hi, x - hi


def _split3(x):
    hi = x.astype(jnp.bfloat16).astype(F32)
    r = x - hi
    mid = r.astype(jnp.bfloat16).astype(F32)
    lo = (r - mid).astype(jnp.bfloat16).astype(F32)
    return hi, mid, lo


def _dot3_right(a_exact, x):
    hi, mid, lo = _split3(x)
    return _dot(a_exact, hi) + _dot(a_exact, mid) + _dot(a_exact, lo)


def _dot3_left(x, b_exact):
    hi, mid, lo = _split3(x)
    return _dot(hi, b_exact) + _dot(mid, b_exact) + _dot(lo, b_exact)


def _store_chunk_rows(ref, x):
    rows = x.shape[0]
    for c in range(N_CHUNKS):
        ref[pl.ds(c, rows, stride=N_CHUNKS), :] = x[:, c * LANES:(c + 1) * LANES]


def _load_gathered_rows(ref, rows):
    return jnp.concatenate([ref[pl.ds(c, rows, stride=ROW_PITCH), :] for c in range(N_CHUNKS)], axis=1)


def _head_norm(z, gain_row):
    return z * lax.rsqrt(jnp.mean(z * z, axis=-1, keepdims=True) + EPS) * gain_row


def _norm_matmul_kernel(*refs, segs, tn, with_forget, prec):
    it = iter(refs)
    x_ref, g_ref, w_ref, hg_ref = next(it), next(it), next(it), next(it)
    if with_forget:
        wf_ref, bfr_ref, bfc_ref = next(it), next(it), next(it)
    out_refs = [next(it) for _ in segs]
    if with_forget:
        lf_ref, lft_ref = next(it), next(it)
    xn_ref = next(it)
    j = pl.program_id(1)

    def project(xn, o_ref, kind, grow):
        z = _dot(xn, w_ref[...], prec)
        if kind == "plain":
            o_ref[...] = z
        elif kind == "gelu":
            o_ref[...] = jax.nn.gelu(z)
        else:
            gain = hg_ref[grow:grow + 1, :]
            for c in range(tn // HEAD_DIM):
                sl = slice(c * HEAD_DIM, (c + 1) * HEAD_DIM)
                o_ref[:, sl] = _head_norm(z[:, sl], gain)

    @pl.when(j == 0)
    def _first():
        x = x_ref[...]
        xn = x * lax.rsqrt(jnp.mean(x * x, axis=-1, keepdims=True) + EPS) * g_ref[...]
        xn_ref[...] = xn
        if with_forget:
            lf_ref[...] = jax.nn.log_sigmoid(_dot(xn, wf_ref[...], prec) + bfr_ref[...])
            zt = _dot_tn(wf_ref[...], xn, prec)[:2 * SUBLANES, :]
            lft_ref[...] = jax.nn.log_sigmoid(zt + bfc_ref[...])
        project(xn, out_refs[0], segs[0][1], segs[0][2])

    lo = 0
    for (ncols, kind, grow), o_ref in zip(segs, out_refs):
        nt = ncols // tn

        @pl.when((j >= max(lo, 1)) & (j < lo + nt))
        def _rest(o_ref=o_ref, kind=kind, grow=grow):
            project(xn_ref[...], o_ref, kind, grow)

        lo += nt


def _norm_matmul(x, gain, w, layer, segs, head_gains, forget=None, prec=None):
    t = x.shape[0]
    tm = min(t, 1024)
    tn = 512
    n_tiles = sum(s[0] for s in segs) // tn
    with_forget = forget is not None
    in_specs = [
        pl.BlockSpec((tm, D_MODEL), lambda i, j: (i, 0)),
        pl.BlockSpec((1, D_MODEL), lambda i, j: (0, 0)),
        pl.BlockSpec((None, D_MODEL, tn), lambda i, j: (layer, 0, j)) if w.ndim == 3
        else pl.BlockSpec((D_MODEL, tn), lambda i, j: (0, j)),
        pl.BlockSpec(head_gains.shape, lambda i, j: (0, 0)),
    ]
    args = [x, gain, w, head_gains]
    if with_forget:
        wf, bfr, bfc = forget
        in_specs += [pl.BlockSpec(wf.shape, lambda i, j: (0, 0)),
                     pl.BlockSpec(bfr.shape, lambda i, j: (0, 0)), pl.BlockSpec(bfc.shape, lambda i, j: (0, 0))]
        args += [wf, bfr, bfc]
    out_shape, out_specs = [], []
    lo = 0
    for ncols, _, _ in segs:
        nt = ncols // tn
        out_shape.append(jax.ShapeDtypeStruct((t, ncols), F32))
        out_specs.append(pl.BlockSpec((tm, tn), lambda i, j, lo=lo, nt=nt: (i, jnp.clip(j - lo, 0, nt - 1))))
        lo += nt
    if with_forget:
        out_shape += [jax.ShapeDtypeStruct((t, LANES), F32), jax.ShapeDtypeStruct((2 * SUBLANES, t), F32)]
        out_specs += [pl.BlockSpec((tm, LANES), lambda i, j: (i, 0)),
                      pl.BlockSpec((2 * SUBLANES, tm), lambda i, j: (0, i))]
    return pl.pallas_call(
        functools.partial(_norm_matmul_kernel, segs=tuple(segs), tn=tn, with_forget=with_forget, prec=prec),
        grid=(t // tm, n_tiles),
        in_specs=in_specs,
        out_specs=out_specs,
        out_shape=out_shape,
        scratch_shapes=[pltpu.VMEM((tm, D_MODEL), F32)],
        compiler_params=_params(2),
    )(*args)


def _gmlp_kernel(u_ref, v_ref, vg_ref, w_ref, b_ref, y_ref, vn_ref, *, rows, grp, valid, prec):
    v = v_ref[...]
    vn = v * lax.rsqrt(jnp.mean(v * v, axis=-1, keepdims=True) + EPS) * vg_ref[...]
    vn_ref[...] = vn
    r = lax.broadcasted_iota(jnp.int32, (rows, rows), 0)
    c = lax.broadcasted_iota(jnp.int32, (rows, rows), 1)
    shift = grp.bit_length() - 1
    allowed = ((c & (grp - 1)) <= (r & (grp - 1))) & ((c & (grp - 1)) < valid)
    if grp < rows:
        allowed = allowed & ((r >> shift) == (c >> shift))
    for g in range(MAIN_HEADS):
        sl = slice(g * HEAD_DIM, (g + 1) * HEAD_DIM)
        w = jnp.where(allowed, w_ref[g], 0.0)
        mixed = _dot(w, vn[:, sl], prec) + b_ref[:, g:g + 1]
        y_ref[:, sl] = (u_ref[:, sl] * mixed).astype(y_ref.dtype)


def _gmlp(u, v, v_gain, w_mix, b_mix, rows, grp, valid, prec):
    t = u.shape[0]
    return pl.pallas_call(
        functools.partial(_gmlp_kernel, rows=rows, grp=grp, valid=valid, prec=prec),
        grid=(t // rows,),
        in_specs=[
            pl.BlockSpec((rows, MAIN_WIDTH), lambda i: (i, 0)),
            pl.BlockSpec((rows, MAIN_WIDTH), lambda i: (i, 0)),
            pl.BlockSpec((1, MAIN_WIDTH), lambda i: (0, 0)),
            pl.BlockSpec((MAIN_HEADS, rows, rows), lambda i: (0, 0, 0)),
            pl.BlockSpec((rows, MAIN_HEADS), lambda i: (0, 0)),
        ],
        out_specs=[pl.BlockSpec((rows, MAIN_WIDTH), lambda i: (i, 0)),
                   pl.BlockSpec((rows, MAIN_WIDTH), lambda i: (i, 0))],
        out_shape=[jax.ShapeDtypeStruct((t, MAIN_WIDTH), _matmul_only_dtype(prec)),
                   jax.ShapeDtypeStruct((t, MAIN_WIDTH), F32)],
        compiler_params=_params(1),
    )(u, v, v_gain, w_mix, b_mix)


def _mem_attn_kernel(q_ref, k_ref, v_ref, o_ref, *, prec):
    for h in range(MEM_HEADS):
        sl = slice(h * HEAD_DIM, (h + 1) * HEAD_DIM)
        s = _dot_nt(q_ref[:, sl], k_ref[:, sl], prec) * ATTN_SCALE
        e = jnp.exp(s - jnp.max(s, axis=-1, keepdims=True))
        p = e / jnp.sum(e, axis=-1, keepdims=True)
        o_ref[:, sl] = _dot(p, v_ref[:, sl], prec).astype(o_ref.dtype)


def _mem_attn(q, k, v, layer, nb, rows_per_batch, prec):
    tq = min(rows_per_batch, 1024)
    nq = rows_per_batch // tq
    kv_spec = pl.BlockSpec((None, None, N_MEM, MEM_WIDTH), lambda b, i: (layer, b, 0, 0))
    return pl.pallas_call(
        functools.partial(_mem_attn_kernel, prec=prec),
        grid=(nb, nq),
        in_specs=[pl.BlockSpec((tq, MEM_WIDTH), lambda b, i: (b * nq + i, 0)), kv_spec, kv_spec],
        out_specs=pl.BlockSpec((tq, MEM_WIDTH), lambda b, i: (b * nq + i, 0)),
        out_shape=jax.ShapeDtypeStruct(q.shape, _matmul_only_dtype(prec)),
        compiler_params=_params(2),
    )(q, k, v)


def _out_proj_kernel(ym_ref, ymem_ref, w1_ref, w2_ref, h_ref, o_ref, *, prec):
    w1 = w1_ref[...].astype(ym_ref.dtype)
    w2 = w2_ref[...].astype(ymem_ref.dtype)
    o_ref[...] = h_ref[...] + _dot(ym_ref[...], w1, prec) + _dot(ymem_ref[...], w2, prec)


def _out_proj(y_main, y_mem, w_out, layer, h, prec):
    t = h.shape[0]
    tm = min(t, 2048 if y_main.dtype == jnp.bfloat16 else 1024)
    tn = 512
    return pl.pallas_call(
        functools.partial(_out_proj_kernel, prec=prec),
        grid=(t // tm, D_MODEL // tn),
        in_specs=[
            pl.BlockSpec((tm, MAIN_WIDTH), lambda i, j: (i, 0)),
            pl.BlockSpec((tm, MEM_WIDTH), lambda i, j: (i, 0)),
            pl.BlockSpec((None, MAIN_WIDTH, tn), lambda i, j: (layer, 0, j)),
            pl.BlockSpec((None, MEM_WIDTH, tn), lambda i, j: (layer, MAIN_WIDTH // MEM_WIDTH, j)),
            pl.BlockSpec((tm, tn), lambda i, j: (i, j)),
        ],
        out_specs=pl.BlockSpec((tm, tn), lambda i, j: (i, j)),
        out_shape=jax.ShapeDtypeStruct((t, D_MODEL), F32),
        compiler_params=_params(2),
    )(y_main, y_mem, w_out, w_out, h)


def _router_kernel(h_ref, g_ref, wr_ref, br_ref, x2_ref, idx_ref, gate_ref, *, prec):
    x = h_ref[...]
    xn = x * lax.rsqrt(jnp.mean(x * x, axis=-1, keepdims=True) + EPS) * g_ref[...]
    _store_chunk_rows(x2_ref, xn)
    lg = _dot(xn, wr_ref[...], prec) + br_ref[...]
    lane = lax.broadcasted_iota(jnp.int32, lg.shape, 1)
    neg = -jnp.inf
    is_grp = lane < N_GROUPS
    gl = jnp.where(is_grp, lg, neg)
    ge = jnp.where(is_grp, jnp.exp(gl - jnp.max(gl, axis=-1, keepdims=True)), 0.0)
    pg = ge / jnp.sum(ge, axis=-1, keepdims=True)
    p_top = jnp.max(pg, axis=-1, keepdims=True)
    g_idx = jnp.min(jnp.where(is_grp & (pg == p_top), lane, LANES), axis=-1, keepdims=True)
    first = N_GROUPS + EXPERTS_PER_GROUP * g_idx
    in_grp = (lane >= first) & (lane < first + EXPERTS_PER_GROUP)
    e1 = jnp.max(jnp.where(in_grp, lg, neg), axis=-1, keepdims=True)
    i1 = jnp.min(jnp.where(in_grp & (lg == e1), lane, LANES), axis=-1, keepdims=True)
    rest = in_grp & (lane != i1)
    e2 = jnp.max(jnp.where(rest, lg, neg), axis=-1, keepdims=True)
    i2 = jnp.min(jnp.where(rest & (lg == e2), lane, LANES), axis=-1, keepdims=True)
    t2 = jnp.exp(e2 - e1)
    den = 1.0 + t2
    idx_ref[...] = jnp.where(lane == 0, i1 - N_GROUPS, jnp.where(lane == 1, i2 - N_GROUPS, 0))
    gate_ref[...] = jnp.where(lane == 0, p_top * (1.0 / den), jnp.where(lane == 1, p_top * (t2 / den), 0.0))


def _router(h, gain, wr, br, prec):
    t = h.shape[0]
    tm = min(t, 512)
    return pl.pallas_call(
        functools.partial(_router_kernel, prec=prec),
        grid=(t // tm,),
        in_specs=[pl.BlockSpec((tm, D_MODEL), lambda i: (i, 0)), pl.BlockSpec((1, D_MODEL), lambda i: (0, 0)),
                  pl.BlockSpec((D_MODEL, LANES), lambda i: (0, 0)), pl.BlockSpec((1, LANES), lambda i: (0, 0))],
        out_specs=[pl.BlockSpec((tm * N_CHUNKS, LANES), lambda i: (i, 0)), pl.BlockSpec((tm, LANES), lambda i: (i, 0)),
                   pl.BlockSpec((tm, LANES), lambda i: (i, 0))],
        out_shape=[jax.ShapeDtypeStruct((t * N_CHUNKS, LANES), F32), jax.ShapeDtypeStruct((t, LANES), jnp.int32),
                   jax.ShapeDtypeStruct((t, LANES), F32)],
        compiler_params=_params(1),
    )(h, gain, wr, br)


def _plan_kernel(e_ref, pos_ref, te_ref, nv_ref, rank_ref, *, t2, blk, tm):
    shift = tm.bit_length() - 1
    nblk = t2 // blk
    r = lax.broadcasted_iota(jnp.int32, (blk, blk), 0)
    c = lax.broadcasted_iota(jnp.int32, (blk, blk), 1)
    upper = (r <= c).astype(F32)
    sub = lax.broadcasted_iota(jnp.int32, (N_EXPERTS, blk), 0)
    carry = jnp.zeros((N_EXPERTS, 1), F32)
    for b in range(nblk):
        sl = slice(b * blk, (b + 1) * blk)
        oh = (sub == e_ref[:, sl]).astype(F32)
        cs = _dot(oh, upper) + carry
        rank_ref[:, sl] = jnp.sum(oh * (cs - 1.0), axis=0, keepdims=True)
        carry = cs[:, blk - 1:blk]
    counts = carry.astype(jnp.int32)
    padc = ((counts + (tm - 1)) >> shift) << shift
    sub1 = lax.broadcasted_iota(jnp.int32, (N_EXPERTS, 1), 0)
    off = jnp.zeros((N_EXPERTS, 1), jnp.int32)
    run = jnp.zeros((1, 1), jnp.int32)
    for e in range(N_EXPERTS):
        off = jnp.where(sub1 == e, run, off)
        run = run + padc[e:e + 1, :]
    ends = off + padc
    for b in range(nblk):
        sl = slice(b * blk, (b + 1) * blk)
        offv = jnp.sum(jnp.where(sub == e_ref[:, sl], off, 0), axis=0, keepdims=True)
        pos_ref[:, sl] = rank_ref[:, sl].astype(jnp.int32) + offv
    lane = lax.broadcasted_iota(jnp.int32, (N_EXPERTS, LANES), 1)
    lane1 = lax.broadcasted_iota(jnp.int32, (1, LANES), 1)
    nv = run >> shift
    te = jnp.sum((ends <= lane * tm).astype(jnp.int32), axis=0, keepdims=True)
    te_last = jnp.sum((ends <= (nv - 1) * tm).astype(jnp.int32), axis=0, keepdims=True)
    te_ref[...] = jnp.minimum(jnp.where(lane1 >= nv, te_last, te), N_EXPERTS - 1)
    nv_ref[...] = jnp.broadcast_to(nv, (1, LANES))


def _plan(e_flat, tm):
    t2 = e_flat.shape[1]
    blk = min(t2, 512)
    return pl.pallas_call(
        functools.partial(_plan_kernel, t2=t2, blk=blk, tm=tm),
        out_shape=[jax.ShapeDtypeStruct((1, t2), jnp.int32), jax.ShapeDtypeStruct((1, LANES), jnp.int32),
                   jax.ShapeDtypeStruct((1, LANES), jnp.int32)],
        scratch_shapes=[pltpu.VMEM((1, t2), F32)],
        compiler_params=pltpu.CompilerParams(vmem_limit_bytes=VMEM_LIMIT_BYTES),
    )(e_flat)


def _invert_kernel(pos_ref, src_ref, *, t, t2, n_rows):
    def zero(p, carry):
        src_ref[p] = 0
        return carry

    lax.fori_loop(0, n_rows, zero, 0, unroll=32)

    def put(i, carry):
        src_ref[pos_ref[i]] = N_CHUNKS * jnp.where(i >= t, i - t, i)
        return carry

    lax.fori_loop(0, t2, put, 0, unroll=32)


def _invert(pos, t, n_rows):
    return pl.pallas_call(
        functools.partial(_invert_kernel, t=t, t2=pos.shape[0], n_rows=n_rows),
        grid_spec=pltpu.PrefetchScalarGridSpec(
            num_scalar_prefetch=1,
            grid=(1,),
            in_specs=[],
            out_specs=pl.BlockSpec(memory_space=pltpu.SMEM),
        ),
        out_shape=jax.ShapeDtypeStruct((n_rows,), jnp.int32),
        compiler_params=_params(1),
    )(pos)


def _moe_ffn_kernel(te_ref, nv_ref, src_ref, x_ref, wg_hbm, wu_hbm, wd_hbm, y_ref,
                    xbuf_ref, wg_ref, wu_ref, wd_ref, wslot_ref, sems, wsems, *, tm, layer, prec):
    i = pl.program_id(0)
    nv = nv_ref[0]
    slot = i % 2
    expert = te_ref[i]

    def start_tile(tile, dst_slot):
        base = tile * tm
        for r in range(tm):
            src = pl.multiple_of(src_ref[base + r], N_CHUNKS)
            pltpu.make_async_copy(x_ref.at[pl.ds(src, N_CHUNKS)],
                                  xbuf_ref.at[dst_slot, pl.ds(r * ROW_PITCH, N_CHUNKS)],
                                  sems.at[dst_slot]).start(priority=r % 2)

    def wait_tile():
        done = xbuf_ref.at[slot, pl.ds(0, tm * N_CHUNKS)]
        pltpu.make_async_copy(done, done, sems.at[slot]).wait()

    def weight_copies(e, ws):
        return (pltpu.make_async_copy(wg_hbm.at[layer, e], wg_ref.at[ws], wsems.at[ws]),
                pltpu.make_async_copy(wu_hbm.at[layer, e], wu_ref.at[ws], wsems.at[ws]),
                pltpu.make_async_copy(wd_hbm.at[layer, e], wd_ref.at[ws], wsems.at[ws]))

    @pl.when(i == 0)
    def _():
        start_tile(0, 0)
        wslot_ref[0] = 1
        for cp in weight_copies(expert, 0):
            cp.start()

    is_first = (i < nv) & ((i == 0) | (te_ref[jnp.maximum(i - 1, 0)] != expert))

    @pl.when(is_first)
    def _():
        ws = 1 - wslot_ref[0]
        wslot_ref[0] = ws
        for cp in weight_copies(expert, ws):
            cp.wait()
        nxt = lax.while_loop(lambda j: (j < nv) & (te_ref[jnp.minimum(j, nv - 1)] == expert), lambda j: j + 1, i + 1)

        @pl.when(nxt < nv)
        def _():
            for cp in weight_copies(te_ref[nxt], 1 - ws):
                cp.start()

    def ffn():
        ws = wslot_ref[0]
        x = _load_gathered_rows(xbuf_ref.at[slot], tm)
        hdn = jax.nn.silu(_dot(x, wg_ref[ws], prec)) * _dot(x, wu_ref[ws], prec)
        _store_chunk_rows(y_ref, _dot(hdn, wd_ref[ws], prec))

    @pl.when(i + 1 < nv)
    def _():
        wait_tile()
        start_tile(i + 1, 1 - slot)
        ffn()

    @pl.when(i + 1 == nv)
    def _():
        wait_tile()
        ffn()

    @pl.when(i >= nv)
    def _():
        y_ref[...] = jnp.zeros_like(y_ref)


def _moe_ffn(te, nv, src, x2, w_gate, w_up, w_down, layer, tm, prec):
    n_rows = src.shape[0]
    n_tiles = n_rows // tm
    any_spec = pl.BlockSpec(memory_space=pl.ANY)
    return pl.pallas_call(
        functools.partial(_moe_ffn_kernel, tm=tm, layer=layer, prec=prec),
        grid_spec=pltpu.PrefetchScalarGridSpec(
            num_scalar_prefetch=3,
            grid=(n_tiles,),
            in_specs=[any_spec, any_spec, any_spec, any_spec],
            out_specs=pl.BlockSpec((tm * N_CHUNKS, LANES), lambda i, te, nv, src: (i, 0)),
            scratch_shapes=[pltpu.VMEM((2, tm * ROW_PITCH, LANES), F32),
                            pltpu.VMEM((2, D_MODEL, D_EXPERT), F32), pltpu.VMEM((2, D_MODEL, D_EXPERT), F32),
                            pltpu.VMEM((2, D_EXPERT, D_MODEL), F32), pltpu.SMEM((1,), jnp.int32),
                            pltpu.SemaphoreType.DMA((2,)), pltpu.SemaphoreType.DMA((2,))],
        ),
        out_shape=jax.ShapeDtypeStruct((n_rows * N_CHUNKS, LANES), F32),
        compiler_params=_params(1),
    )(te, nv, src, x2, w_gate, w_up, w_down)


def _combine_kernel(pos_ref, h_ref, g_ref, y_ref, o_ref, y0_ref, y1_ref, sems, *, t, tm):
    base = pl.program_id(0) * tm
    for k, buf in enumerate((y0_ref, y1_ref)):
        for r in range(tm):
            src = pl.multiple_of(pos_ref[k * t + base + r] * N_CHUNKS, N_CHUNKS)
            pltpu.make_async_copy(y_ref.at[pl.ds(src, N_CHUNKS)], buf.at[pl.ds(r * ROW_PITCH, N_CHUNKS)],
                                  sems.at[k]).start(priority=k)
    for k, buf in enumerate((y0_ref, y1_ref)):
        done = buf.at[pl.ds(0, tm * N_CHUNKS)]
        pltpu.make_async_copy(done, done, sems.at[k]).wait()
    g = g_ref[...]
    o_ref[...] = (h_ref[...] + g[:, 0:1] * _load_gathered_rows(y0_ref, tm)
                  + g[:, 1:2] * _load_gathered_rows(y1_ref, tm))


def _combine(pos, h, gates, y):
    t = h.shape[0]
    tm = min(t, 256)
    return pl.pallas_call(
        functools.partial(_combine_kernel, t=t, tm=tm),
        grid_spec=pltpu.PrefetchScalarGridSpec(
            num_scalar_prefetch=1,
            grid=(t // tm,),
            in_specs=[pl.BlockSpec((tm, D_MODEL), lambda i, pos: (i, 0)),
                      pl.BlockSpec((tm, LANES), lambda i, pos: (i, 0)),
                      pl.BlockSpec(memory_space=pl.ANY)],
            out_specs=pl.BlockSpec((tm, D_MODEL), lambda i, pos: (i, 0)),
            scratch_shapes=[pltpu.VMEM((tm * ROW_PITCH, LANES), F32), pltpu.VMEM((tm * ROW_PITCH, LANES), F32),
                            pltpu.SemaphoreType.DMA((2,))],
        ),
        out_shape=jax.ShapeDtypeStruct((t, D_MODEL), F32),
        compiler_params=_params(1),
    )(pos, h, gates, y)


def _hier_moe(h, gain, wr, br, w_gate, w_up, w_down, layer, prec):
    t = h.shape[0]
    tm = 256 if t >= 1024 else 16
    n_tiles = (2 * t) // tm + N_EXPERTS
    assert n_tiles <= LANES
    x2, idx, gates = _router(h, gain, wr, br, prec)
    e_flat = idx[:, :2].T.reshape(1, 2 * t)
    pos, te, nv = _plan(e_flat, tm)
    pos = pos.reshape(2 * t)
    te = te[0, :n_tiles]
    nv = nv[0, :1]
    src = _invert(pos, t, n_tiles * tm)
    y = _moe_ffn(te, nv, src, x2, w_gate, w_up, w_down, layer, tm, prec)
    return _combine(pos, h, gates, y)


def _fox_cumsum_kernel(lft_ref, lf_ref, ct_ref, c_ref, *, s, blk):
    r = lax.broadcasted_iota(jnp.int32, (blk, blk), 0)
    c = lax.broadcasted_iota(jnp.int32, (blk, blk), 1)
    upper = (r <= c).astype(F32)
    lower = (r >= c).astype(F32)
    carry_t = jnp.zeros((2 * SUBLANES, 1), F32)
    carry = jnp.zeros((1, LANES), F32)
    for b in range(s // blk):
        sl = slice(b * blk, (b + 1) * blk)
        ct = _dot3_left(lft_ref[:, sl], upper) + carry_t
        ct_ref[:, sl] = ct
        carry_t = ct[:, blk - 1:blk]
        cc = _dot3_right(lower, lf_ref[sl, :]) + carry
        c_ref[sl, :] = cc
        carry = cc[blk - 1:blk, :]


def _fox_cumsum(lft, lf, nb, s):
    t = lf.shape[0]
    blk = min(s, 256)
    return pl.pallas_call(
        functools.partial(_fox_cumsum_kernel, s=s, blk=blk),
        grid=(nb,),
        in_specs=[pl.BlockSpec((2 * SUBLANES, s), lambda b: (0, b)), pl.BlockSpec((s, LANES), lambda b: (b, 0))],
        out_specs=[pl.BlockSpec((2 * SUBLANES, s), lambda b: (0, b)), pl.BlockSpec((s, LANES), lambda b: (b, 0))],
        out_shape=[jax.ShapeDtypeStruct((2 * SUBLANES, t), F32), jax.ShapeDtypeStruct((t, LANES), F32)],
        compiler_params=_params(1),
    )(lft, lf)


def _fox_prompt_kernel(q_ref, k_ref, v_ref, ct_ref, c_ref, o_ref, *, s, tq):
    h = pl.program_id(1)
    lane = lax.broadcasted_iota(jnp.int32, (s, LANES), 1)
    c_col = jnp.sum(jnp.where(lane == h, c_ref[...], 0.0), axis=1, keepdims=True)
    c_row = ct_ref[pl.ds(h, 1), :]
    ri = lax.broadcasted_iota(jnp.int32, (tq, tq), 0)
    ci = lax.broadcasted_iota(jnp.int32, (tq, tq), 1)
    causal = ci <= ri
    for qi in range(s // tq):
        qs = slice(qi * tq, (qi + 1) * tq)
        q = q_ref[qs, :]
        m = jnp.full((tq, 1), -jnp.inf, F32)
        l = jnp.zeros((tq, 1), F32)
        acc = jnp.zeros((tq, HEAD_DIM), F32)
        for kj in range(qi + 1):
            ks = slice(kj * tq, (kj + 1) * tq)
            sc = lax.dot_general(q, k_ref[ks, :], _NT, preferred_element_type=F32) * ATTN_SCALE
            sc = sc + c_col[qs, :] - c_row[:, ks]
            if kj == qi:
                sc = jnp.where(causal, sc, -jnp.inf)
            m_new = jnp.maximum(m, jnp.max(sc, axis=-1, keepdims=True))
            alpha = jnp.exp(m - m_new)
            p = jnp.exp(sc - m_new)
            l = alpha * l + jnp.sum(p, axis=-1, keepdims=True)
            acc = alpha * acc + _dot(p, v_ref[ks, :])
            m = m_new
        o_ref[qs, :] = (acc / l).astype(o_ref.dtype)


def _fox_prompt(q, k, v, ct, c, nb, s):
    t = q.shape[0]
    tq = min(s, 512)
    head_spec = pl.BlockSpec((s, HEAD_DIM), lambda b, h: (b, h))
    return pl.pallas_call(
        functools.partial(_fox_prompt_kernel, s=s, tq=tq),
        grid=(nb, MAIN_HEADS),
        in_specs=[head_spec, head_spec, head_spec,
                  pl.BlockSpec((2 * SUBLANES, s), lambda b, h: (0, b)),
                  pl.BlockSpec((s, LANES), lambda b, h: (b, 0))],
        out_specs=head_spec,
        out_shape=jax.ShapeDtypeStruct((t, MAIN_WIDTH), _matmul_only_dtype(None)),
        compiler_params=_params(2),
    )(q, k, v, ct, c)


def _fox_past_bias_kernel(pt_ref, lfc_ref, o_ref, buf_ref, sem, *, n_pages):
    b = pl.program_id(0)

    def page_copy(p):
        page = pt_ref[b * n_pages + p]
        return pltpu.make_async_copy(lfc_ref.at[:, pl.ds(page, 1), :], buf_ref.at[:, pl.ds(p, 1), :], sem)

    def start(p, carry):
        page_copy(p).start()
        return carry

    def wait(p, carry):
        page_copy(p).wait()
        return carry

    lax.fori_loop(0, n_pages, start, 0)
    r = lax.broadcasted_iota(jnp.int32, (PAGE_SIZE, PAGE_SIZE), 0)
    c = lax.broadcasted_iota(jnp.int32, (PAGE_SIZE, PAGE_SIZE), 1)
    after_in_page = (r > c).astype(F32)
    pr = lax.broadcasted_iota(jnp.int32, (n_pages, n_pages), 0)
    pc = lax.broadcasted_iota(jnp.int32, (n_pages, n_pages), 1)
    later_pages = (pc > pr).astype(F32)
    lax.fori_loop(0, n_pages, wait, 0)
    for h in range(MAIN_HEADS):
        lp = buf_ref[h]
        tot = jnp.broadcast_to(jnp.sum(lp, axis=1, keepdims=True), lp.shape)
        o_ref[h] = _dot3_left(lp, after_in_page) + _dot3_right(later_pages, tot)


def _fox_past_bias(page_table_flat, cache_logf_t, nb, n_pages):
    return pl.pallas_call(
        functools.partial(_fox_past_bias_kernel, n_pages=n_pages),
        grid_spec=pltpu.PrefetchScalarGridSpec(
            num_scalar_prefetch=1,
            grid=(nb,),
            in_specs=[pl.BlockSpec(memory_space=pl.ANY)],
            out_specs=pl.BlockSpec((None, MAIN_HEADS, n_pages, PAGE_SIZE), lambda b, pt: (b, 0, 0, 0)),
            scratch_shapes=[pltpu.VMEM((MAIN_HEADS, n_pages, PAGE_SIZE), F32), pltpu.SemaphoreType.DMA(())],
        ),
        out_shape=jax.ShapeDtypeStruct((nb, MAIN_HEADS, n_pages, PAGE_SIZE), F32),
        compiler_params=_params(1),
    )(page_table_flat, cache_logf_t)


def _fox_sample_kernel(*refs, ppb, n_steps, valid):
    pt_ref, q_ref, kn_ref, vn_ref, lf_ref, bias_ref = refs[:6]
    k_refs = refs[6:6 + ppb]
    v_refs = refs[6 + ppb:6 + 2 * ppb]
    o_ref, m_ref, l_ref, acc_ref = refs[6 + 2 * ppb:]
    g = pl.program_id(1)
    rows = SAMPLE_ROWS

    @pl.when(g == 0)
    def _():
        m_ref[...] = jnp.full(m_ref.shape, -jnp.inf, F32)
        l_ref[...] = jnp.zeros(l_ref.shape, F32)
        acc_ref[...] = jnp.zeros(acc_ref.shape, F32)

    r8 = lax.broadcasted_iota(jnp.int32, (rows, rows), 0)
    c8 = lax.broadcasted_iota(jnp.int32, (rows, rows), 1)
    new_mask = (c8 <= r8) & (c8 < valid)
    c_new = _dot3_right(new_mask.astype(F32), lf_ref[...])

    def qk3(q_hi, q_lo, k):
        a = _dot_nt(jnp.concatenate([q_hi, q_lo], axis=0), k)
        return a[:rows] + a[rows:]

    def pv3(p, v):
        p_hi, p_lo = _split_hi_lo(p)
        a = _dot(jnp.concatenate([p_hi, p_lo], axis=0), v)
        return a[:rows] + a[rows:]

    all_scores = []
    for h in range(MAIN_HEADS):
        q_hi, q_lo = _split_hi_lo(q_ref[:, h * HEAD_DIM:(h + 1) * HEAD_DIM])
        cq = c_new[:, h:h + 1]
        all_scores.append([qk3(q_hi, q_lo, k_refs[u][h]) * ATTN_SCALE + cq + bias_ref[h, u:u + 1, :]
                           for u in range(ppb)])
    all_probs, all_alpha = [], []
    for h in range(MAIN_HEADS):
        m_old = m_ref[h, :, 0:1]
        m_new = m_old
        for sc in all_scores[h]:
            m_new = jnp.maximum(m_new, jnp.max(sc, axis=-1, keepdims=True))
        alpha = jnp.exp(m_old - m_new)
        probs = [jnp.exp(sc - m_new) for sc in all_scores[h]]
        l_new = alpha * l_ref[h, :, 0:1]
        for p in probs:
            l_new = l_new + jnp.sum(p, axis=-1, keepdims=True)
        m_ref[h] = jnp.broadcast_to(m_new, (rows, LANES))
        l_ref[h] = jnp.broadcast_to(l_new, (rows, LANES))
        all_probs.append(probs)
        all_alpha.append(alpha)
    for h in range(MAIN_HEADS):
        sl = slice(h * HEAD_DIM, (h + 1) * HEAD_DIM)
        acc = all_alpha[h] * acc_ref[:, sl]
        for u in range(ppb):
            acc = acc + pv3(all_probs[h][u], v_refs[u][h])
        acc_ref[:, sl] = acc

    @pl.when(g == n_steps - 1)
    def _():
        for h in range(MAIN_HEADS):
            sl = slice(h * HEAD_DIM, (h + 1) * HEAD_DIM)
            qh = q_ref[:, sl]
            cq = c_new[:, h:h + 1]
            cq_row = jnp.sum(jnp.where(r8 == c8, jnp.broadcast_to(cq, (rows, rows)), 0.0), axis=0, keepdims=True)
            sn = _dot_nt(qh, kn_ref[:, sl], HIGHEST) * ATTN_SCALE + cq - cq_row
            sn = jnp.where(new_mask, sn, -jnp.inf)
            m_old = m_ref[h, :, 0:1]
            m_new = jnp.maximum(m_old, jnp.max(sn, axis=-1, keepdims=True))
            alpha = jnp.exp(m_old - m_new)
            p = jnp.exp(sn - m_new)
            l_new = alpha * l_ref[h, :, 0:1] + jnp.sum(p, axis=-1, keepdims=True)
            acc = alpha * acc_ref[:, sl] + _dot(p, vn_ref[:, sl], HIGHEST)
            o_ref[:, sl] = acc / l_new


def _fox_sample(page_table_flat, q, k_new, v_new, lf, bias, cache_k_t, cache_v_t, nb, n_pages, valid):
    ppb = 8 if n_pages % 8 == 0 else n_pages
    n_steps = n_pages // ppb
    row_spec = pl.BlockSpec((SAMPLE_ROWS, MAIN_WIDTH), lambda b, g, pt: (b, 0))

    def page_spec(u):
        return pl.BlockSpec((None, MAIN_HEADS, PAGE_SIZE, HEAD_DIM),
                            lambda b, g, pt, u=u: (pt[b * n_pages + g * ppb + u], 0, 0, 0))

    in_specs = [row_spec, row_spec, row_spec,
                pl.BlockSpec((SAMPLE_ROWS, LANES), lambda b, g, pt: (b, 0)),
                pl.BlockSpec((None, MAIN_HEADS, ppb, PAGE_SIZE), lambda b, g, pt: (b, 0, g, 0))]
    in_specs += [page_spec(u) for u in range(ppb)] + [page_spec(u) for u in range(ppb)]
    return pl.pallas_call(
        functools.partial(_fox_sample_kernel, ppb=ppb, n_steps=n_steps, valid=valid),
        grid_spec=pltpu.PrefetchScalarGridSpec(
            num_scalar_prefetch=1,
            grid=(nb, n_steps),
            in_specs=in_specs,
            out_specs=row_spec,
            scratch_shapes=[pltpu.VMEM((MAIN_HEADS, SAMPLE_ROWS, LANES), F32),
                            pltpu.VMEM((MAIN_HEADS, SAMPLE_ROWS, LANES), F32),
                            pltpu.VMEM((SAMPLE_ROWS, MAIN_WIDTH), F32)],
        ),
        out_shape=jax.ShapeDtypeStruct(q.shape, F32),
        compiler_params=_params(2),
    )(page_table_flat, q, k_new, v_new, lf, bias, *([cache_k_t] * ppb), *([cache_v_t] * ppb))


def _trunk(x, nb, rows, valid, mem_k, mem_v, fox, w, prec):
    u, v, qm = _norm_matmul(x, w["norm1_gain"][0:1], w["w_in_a"], 0,
                            [(MAIN_WIDTH, "gelu", 0), (MAIN_WIDTH, "gelu", 0), (MEM_WIDTH, "headnorm", 0)],
                            w["mem_q_gain"][0:1], prec=prec)
    y_main, v_rows = _gmlp(u, v, w["v_gain_a"][0:1], w["w_mix"], w["b_mix"], w["gmlp_rows"], w["gmlp_grp"], valid,
                           prec)
    y_mem = _mem_attn(qm, mem_k, mem_v, 0, nb, rows, prec)
    h = _out_proj(y_main, y_mem, w["w_out"], 0, x, prec)
    h = _hier_moe(h, w["norm2_gain"][0:1], w["w_router"][0], w["b_router"][0], w["w_gate"], w["w_up"], w["w_down"],
                  0, prec)
    k, vv, lf, lft = _norm_matmul(h, w["kv_norm_gain"], w["w_kv_shared"], 0,
                                  [(MAIN_WIDTH, "headnorm", 0), (MAIN_WIDTH, "plain", 0)],
                                  w["k_gain_shared"], forget=w["forget"], prec=prec)
    q, qm = _norm_matmul(h, w["norm1_gain"][1:2], w["w_in_b"], 0,
                         [(MAIN_WIDTH, "headnorm", 0), (MEM_WIDTH, "headnorm", 1)],
                         jnp.concatenate([w["q_gain_b"][0:1], w["mem_q_gain"][1:2]], axis=0), prec=prec)
    y_main = fox(q, k, vv, lf, lft)
    y_mem = _mem_attn(qm, mem_k, mem_v, 1, nb, rows, prec)
    h = _out_proj(y_main, y_mem, w["w_out"], 1, h, prec)
    h = _hier_moe(h, w["norm2_gain"][1:2], w["w_router"][1], w["b_router"][1], w["w_gate"], w["w_up"], w["w_down"],
                  1, prec)
    return h, k, vv, lf, v_rows


def kernel(x_prompt, x_sample, cache_k, cache_v, cache_logf, cache_mem_k, cache_mem_v, page_table, mem_prompt, norm1_gain, norm2_gain, w_in_a, v_gain_a, w_s_a, b_s_a, w_in_b, q_gain_b, kv_norm_gain, w_kv_shared, b_forget, k_gain_shared, mem_norm_gain, w_mem_kv, mem_q_gain, mem_k_gain, w_out, w_router_group, b_router_group, w_router_expert, b_router_expert, w_gate, w_up, w_down):
    batch, seq, _ = x_prompt.shape
    dec_batch, dec_seq, _ = x_sample.shape
    n_phys = cache_k.shape[0]
    n_pages = page_table.shape[1]
    depth = norm1_gain.shape[0]
    assert depth == 2 and dec_seq <= SAMPLE_ROWS and seq % CHUNK == 0

    n_route = N_GROUPS + N_EXPERTS
    w_router = jnp.pad(jnp.concatenate([w_router_group, w_router_expert], axis=-1),
                       ((0, 0), (0, 0), (0, LANES - n_route)))
    b_router = jnp.pad(jnp.concatenate([b_router_group, b_router_expert], axis=-1),
                       ((0, 0), (0, LANES - n_route)))[:, None, :]
    w_f = w_kv_shared[:, 2 * MAIN_WIDTH:]
    forget = (jnp.pad(w_f, ((0, 0), (0, LANES - MAIN_HEADS))),
              jnp.pad(b_forget, (0, LANES - MAIN_HEADS))[None, :],
              jnp.pad(b_forget, (0, 2 * SUBLANES - MAIN_HEADS))[:, None])
    common = dict(norm1_gain=norm1_gain, norm2_gain=norm2_gain, w_in_a=w_in_a, v_gain_a=v_gain_a, w_in_b=w_in_b,
                  q_gain_b=q_gain_b, kv_norm_gain=kv_norm_gain[None, :], w_kv_shared=w_kv_shared,
                  k_gain_shared=k_gain_shared[None, :], mem_q_gain=mem_q_gain, w_out=w_out,
                  w_router=w_router, b_router=b_router, w_gate=w_gate, w_up=w_up, w_down=w_down, forget=forget)

    t_p = batch * seq
    mem_tok = mem_prompt.reshape(batch * N_MEM, D_MODEL)
    mem_k_layers, mem_v_layers = [], []
    for l in range(depth):
        mk, mv = _norm_matmul(mem_tok, mem_norm_gain[l:l + 1], w_mem_kv, l,
                              [(MEM_WIDTH, "headnorm", 0), (MEM_WIDTH, "plain", 0)], mem_k_gain[l:l + 1])
        mem_k_layers.append(mk.reshape(batch, N_MEM, MEM_WIDTH))
        mem_v_layers.append(mv.reshape(batch, N_MEM, MEM_WIDTH))
    mem_k_p = jnp.stack(mem_k_layers)
    mem_v_p = jnp.stack(mem_v_layers)

    def fox_prompt(q, k, v, lf, lft):
        ct, c = _fox_cumsum(lft, lf, batch, seq)
        return _fox_prompt(q, k, v, ct, c, batch, seq)

    w_p = dict(common, w_mix=w_s_a[0], b_mix=b_s_a[0].T, gmlp_rows=CHUNK, gmlp_grp=CHUNK)
    y_p, k_p, v_p, lf_p, vrows_p = _trunk(x_prompt.reshape(t_p, D_MODEL), batch, seq, seq,
                                          mem_k_p, mem_v_p, fox_prompt, w_p, None)

    rows = SAMPLE_ROWS
    t_s = dec_batch * rows
    x_s = jnp.pad(x_sample, ((0, 0), (0, rows - dec_seq), (0, 0))).reshape(t_s, D_MODEL)
    pt_flat = page_table.reshape(dec_batch * n_pages)
    bias = _fox_past_bias(pt_flat, jnp.transpose(cache_logf, (2, 0, 1)), dec_batch, n_pages)
    cache_k_t = jnp.transpose(cache_k, (0, 2, 1, 3))
    cache_v_t = jnp.transpose(cache_v, (0, 2, 1, 3))

    def fox_sample(q, k, v, lf, lft):
        return _fox_sample(pt_flat, q, k, v, lf, bias, cache_k_t, cache_v_t, dec_batch, n_pages, dec_seq)

    w_s = dict(common, w_mix=jnp.tile(w_s_a[0][:, :rows, :rows], (1, dec_batch, dec_batch)),
               b_mix=jnp.tile(b_s_a[0][:, :rows].T, (dec_batch, 1)), gmlp_rows=t_s, gmlp_grp=rows)
    mem_k_s = cache_mem_k.reshape(depth, dec_batch, N_MEM, MEM_WIDTH)
    mem_v_s = cache_mem_v.reshape(depth, dec_batch, N_MEM, MEM_WIDTH)
    y_s, k_s, v_s, lf_s, vrows_s = _trunk(x_s, dec_batch, rows, dec_seq, mem_k_s, mem_v_s, fox_sample, w_s, SPLIT3)

    def unpad(a, *tail):
        return a.reshape((dec_batch, rows) + tail)[:, :dec_seq]

    return (y_p.reshape(batch, seq, D_MODEL),
            unpad(y_s, D_MODEL),
            k_p.reshape(batch, seq, MAIN_HEADS, HEAD_DIM),
            v_p.reshape(batch, seq, MAIN_HEADS, HEAD_DIM),
            lf_p[:, :MAIN_HEADS].reshape(batch, seq, MAIN_HEADS),
            unpad(k_s, MAIN_HEADS, HEAD_DIM),
            unpad(v_s, MAIN_HEADS, HEAD_DIM),
            unpad(lf_s[:, :MAIN_HEADS], MAIN_HEADS),
            mem_k_p.reshape(depth, batch, N_MEM, MEM_HEADS, HEAD_DIM),
            mem_v_p.reshape(depth, batch, N_MEM, MEM_HEADS, HEAD_DIM),
            vrows_p.reshape(batch, seq, MAIN_WIDTH)[:, -CHUNK:][None],
            unpad(vrows_s, MAIN_WIDTH)[None])
```

```python
import functools

import jax
import jax.numpy as jnp
from jax import lax
from jax.experimental import pallas as pl
from jax.experimental.pallas import tpu as pltpu

D_MODEL = 2048
HEAD_DIM = 128
MAIN_HEADS = 12
MEM_HEADS = 4
MAIN_WIDTH = MAIN_HEADS * HEAD_DIM
MEM_WIDTH = MEM_HEADS * HEAD_DIM
N_MEM = 256
N_GROUPS = 4
EXPERTS_PER_GROUP = 4
N_EXPERTS = N_GROUPS * EXPERTS_PER_GROUP
D_EXPERT = 512
CHUNK = 128
PAGE_SIZE = 128
EPS = 1e-6
ATTN_SCALE = HEAD_DIM ** -0.5

LANES = 128
SUBLANES = 8
VMEM_LIMIT_BYTES = 56 * 1024 * 1024
SAMPLE_ROWS = SUBLANES
N_CHUNKS = D_MODEL // LANES
ROW_PITCH = 24

F32 = jnp.float32
_NT = (((1,), (1,)), ((), ()))


def _params(n_axes):
    return pltpu.CompilerParams(dimension_semantics=("arbitrary",) * n_axes,
                                vmem_limit_bytes=VMEM_LIMIT_BYTES)


HIGHEST = lax.Precision.HIGHEST


SPLIT3 = "split3"


def _dot(a, b, prec=None):
    if prec == SPLIT3:
        return _dot_split3(a, b, _dot)
    return jnp.dot(a, b, preferred_element_type=F32, precision=prec)


def _dot_nt(a, b, prec=None):
    if prec == SPLIT3:
        return _dot_split3(a, b, _dot_nt)
    return lax.dot_general(a, b, _NT, preferred_element_type=F32, precision=prec)


def _dot_tn(a, b, prec=None):
    dims = (((0,), (1,)), ((), ()))
    if prec == SPLIT3:
        a_hi, a_lo = _split_hi_lo(a)
        b_hi, b_lo = _split_hi_lo(b)
        return _dot_tn(a_hi, b_hi) + _dot_tn(a_lo, b_hi) + _dot_tn(a_hi, b_lo)
    return lax.dot_general(a, b, dims, preferred_element_type=F32, precision=prec)


def _dot_split3(a, b, dot):
    a_hi, a_lo = _split_hi_lo(a)
    b_hi, b_lo = _split_hi_lo(b)
    m = a.shape[0]
    s = dot(jnp.concatenate([a_hi, a_lo], axis=0), b_hi)
    return s[:m] + s[m:] + dot(a_hi, b_lo)


def _matmul_only_dtype(prec):
    return jnp.bfloat16 if prec is None else F32


def _split_hi_lo(x):
    hi = x.astype(jnp.bfloat16).astype(F32)
    return hi, x - hi


def _split3(x):
    hi = x.astype(jnp.bfloat16).astype(F32)
    r = x - hi
    mid = r.astype(jnp.bfloat16).astype(F32)
    lo = (r - mid).astype(jnp.bfloat16).astype(F32)
    return hi, mid, lo


def _dot3_right(a_exact, x):
    hi, mid, lo = _split3(x)
    return _dot(a_exact, hi) + _dot(a_exact, mid) + _dot(a_exact, lo)


def _dot3_left(x, b_exact):
    hi, mid, lo = _split3(x)
    return _dot(hi, b_exact) + _dot(mid, b_exact) + _dot(lo, b_exact)


def _store_chunk_rows(ref, x):
    rows = x.shape[0]
    for c in range(N_CHUNKS):
        ref[pl.ds(c, rows, stride=N_CHUNKS), :] = x[:, c * LANES:(c + 1) * LANES]


def _load_gathered_rows(ref, rows):
    return jnp.concatenate([ref[pl.ds(c, rows, stride=ROW_PITCH), :] for c in range(N_CHUNKS)], axis=1)


def _head_norm(z, gain_row):
    return z * lax.rsqrt(jnp.mean(z * z, axis=-1, keepdims=True) + EPS) * gain_row


def _norm_matmul_kernel(*refs, segs, tn, with_forget, prec):
    it = iter(refs)
    x_ref, g_ref, w_ref, hg_ref = next(it), next(it), next(it), next(it)
    if with_forget:
        wf_ref, bfr_ref, bfc_ref = next(it), next(it), next(it)
    out_refs = [next(it) for _ in segs]
    if with_forget:
        lf_ref, lft_ref = next(it), next(it)
    xn_ref = next(it)
    j = pl.program_id(1)

    def project(xn, o_ref, kind, grow):
        z = _dot(xn, w_ref[...], prec)
        if kind == "plain":
            o_ref[...] = z
        elif kind == "gelu":
            o_ref[...] = jax.nn.gelu(z)
        else:
            gain = hg_ref[grow:grow + 1, :]
            for c in range(tn // HEAD_DIM):
                sl = slice(c * HEAD_DIM, (c + 1) * HEAD_DIM)
                o_ref[:, sl] = _head_norm(z[:, sl], gain).astype(o_ref.dtype)

    @pl.when(j == 0)
    def _first():
        x = x_ref[...]
        xn = x * lax.rsqrt(jnp.mean(x * x, axis=-1, keepdims=True) + EPS) * g_ref[...]
        xn_ref[...] = xn
        if with_forget:
            lf_ref[...] = jax.nn.log_sigmoid(_dot(xn, wf_ref[...], prec) + bfr_ref[...])
            zt = _dot_tn(wf_ref[...], xn, prec)[:2 * SUBLANES, :]
            lft_ref[...] = jax.nn.log_sigmoid(zt + bfc_ref[...])
        project(xn, out_refs[0], segs[0][1], segs[0][2])

    lo = 0
    for (ncols, kind, grow), o_ref in zip(segs, out_refs):
        nt = ncols // tn

        @pl.when((j >= max(lo, 1)) & (j < lo + nt))
        def _rest(o_ref=o_ref, kind=kind, grow=grow):
            project(xn_ref[...], o_ref, kind, grow)

        lo += nt


def _norm_matmul(x, gain, w, layer, segs, head_gains, forget=None, prec=None):
    t = x.shape[0]
    tm = min(t, 1024)
    tn = 512
    n_tiles = sum(s[0] for s in segs) // tn
    with_forget = forget is not None
    in_specs = [
        pl.BlockSpec((tm, D_MODEL), lambda i, j: (i, 0)),
        pl.BlockSpec((1, D_MODEL), lambda i, j: (0, 0)),
        pl.BlockSpec((None, D_MODEL, tn), lambda i, j: (layer, 0, j)) if w.ndim == 3
        else pl.BlockSpec((D_MODEL, tn), lambda i, j: (0, j)),
        pl.BlockSpec(head_gains.shape, lambda i, j: (0, 0)),
    ]
    args = [x, gain, w, head_gains]
    if with_forget:
        wf, bfr, bfc = forget
        in_specs += [pl.BlockSpec(wf.shape, lambda i, j: (0, 0)),
                     pl.BlockSpec(bfr.shape, lambda i, j: (0, 0)), pl.BlockSpec(bfc.shape, lambda i, j: (0, 0))]
        args += [wf, bfr, bfc]
    out_shape, out_specs = [], []
    lo = 0
    for ncols, kind, _ in segs:
        nt = ncols // tn
        out_shape.append(jax.ShapeDtypeStruct((t, ncols), _matmul_only_dtype(prec) if kind == "headnorm_mm" else F32))
        out_specs.append(pl.BlockSpec((tm, tn), lambda i, j, lo=lo, nt=nt: (i, jnp.clip(j - lo, 0, nt - 1))))
        lo += nt
    if with_forget:
        out_shape += [jax.ShapeDtypeStruct((t, LANES), F32), jax.ShapeDtypeStruct((2 * SUBLANES, t), F32)]
        out_specs += [pl.BlockSpec((tm, LANES), lambda i, j: (i, 0)),
                      pl.BlockSpec((2 * SUBLANES, tm), lambda i, j: (0, i))]
    return pl.pallas_call(
        functools.partial(_norm_matmul_kernel, segs=tuple(segs), tn=tn, with_forget=with_forget, prec=prec),
        grid=(t // tm, n_tiles),
        in_specs=in_specs,
        out_specs=out_specs,
        out_shape=out_shape,
        scratch_shapes=[pltpu.VMEM((tm, D_MODEL), F32)],
        compiler_params=_params(2),
    )(*args)


def _gmlp_kernel(u_ref, v_ref, vg_ref, w_ref, b_ref, y_ref, vn_ref, *, rows, grp, valid, prec):
    v = v_ref[...]
    vn = v * lax.rsqrt(jnp.mean(v * v, axis=-1, keepdims=True) + EPS) * vg_ref[...]
    vn_ref[...] = vn
    r = lax.broadcasted_iota(jnp.int32, (rows, rows), 0)
    c = lax.broadcasted_iota(jnp.int32, (rows, rows), 1)
    shift = grp.bit_length() - 1
    allowed = ((c & (grp - 1)) <= (r & (grp - 1))) & ((c & (grp - 1)) < valid)
    if grp < rows:
        allowed = allowed & ((r >> shift) == (c >> shift))
    for g in range(MAIN_HEADS):
        sl = slice(g * HEAD_DIM, (g + 1) * HEAD_DIM)
        w = jnp.where(allowed, w_ref[g], 0.0)
        mixed = _dot(w, vn[:, sl], prec) + b_ref[:, g:g + 1]
        y_ref[:, sl] = (u_ref[:, sl] * mixed).astype(y_ref.dtype)


def _gmlp(u, v, v_gain, w_mix, b_mix, rows, grp, valid, prec):
    t = u.shape[0]
    return pl.pallas_call(
        functools.partial(_gmlp_kernel, rows=rows, grp=grp, valid=valid, prec=prec),
        grid=(t // rows,),
        in_specs=[
            pl.BlockSpec((rows, MAIN_WIDTH), lambda i: (i, 0)),
            pl.BlockSpec((rows, MAIN_WIDTH), lambda i: (i, 0)),
            pl.BlockSpec((1, MAIN_WIDTH), lambda i: (0, 0)),
            pl.BlockSpec((MAIN_HEADS, rows, rows), lambda i: (0, 0, 0)),
            pl.BlockSpec((rows, MAIN_HEADS), lambda i: (0, 0)),
        ],
        out_specs=[pl.BlockSpec((rows, MAIN_WIDTH), lambda i: (i, 0)),
                   pl.BlockSpec((rows, MAIN_WIDTH), lambda i: (i, 0))],
        out_shape=[jax.ShapeDtypeStruct((t, MAIN_WIDTH), _matmul_only_dtype(prec)),
                   jax.ShapeDtypeStruct((t, MAIN_WIDTH), F32)],
        compiler_params=_params(1),
    )(u, v, v_gain, w_mix, b_mix)


def _mem_attn_kernel(q_ref, k_ref, v_ref, o_ref, *, prec):
    for h in range(MEM_HEADS):
        sl = slice(h * HEAD_DIM, (h + 1) * HEAD_DIM)
        s = _dot_nt(q_ref[:, sl].astype(F32), k_ref[:, sl], prec) * ATTN_SCALE
        e = jnp.exp(s - jnp.max(s, axis=-1, keepdims=True))
        p = e / jnp.sum(e, axis=-1, keepdims=True)
        o_ref[:, sl] = _dot(p, v_ref[:, sl], prec).astype(o_ref.dtype)


def _mem_attn(q, k, v, layer, nb, rows_per_batch, prec):
    tq = min(rows_per_batch, 1024)
    nq = rows_per_batch // tq
    kv_spec = pl.BlockSpec((None, None, N_MEM, MEM_WIDTH), lambda b, i: (layer, b, 0, 0))
    return pl.pallas_call(
        functools.partial(_mem_attn_kernel, prec=prec),
        grid=(nb, nq),
        in_specs=[pl.BlockSpec((tq, MEM_WIDTH), lambda b, i: (b * nq + i, 0)), kv_spec, kv_spec],
        out_specs=pl.BlockSpec((tq, MEM_WIDTH), lambda b, i: (b * nq + i, 0)),
        out_shape=jax.ShapeDtypeStruct(q.shape, _matmul_only_dtype(prec)),
        compiler_params=_params(2),
    )(q, k, v)


def _out_proj_kernel(ym_ref, ymem_ref, w1_ref, w2_ref, h_ref, o_ref, *, prec):
    w1 = w1_ref[...].astype(ym_ref.dtype)
    w2 = w2_ref[...].astype(ymem_ref.dtype)
    o_ref[...] = h_ref[...] + _dot(ym_ref[...], w1, prec) + _dot(ymem_ref[...], w2, prec)


def _out_proj(y_main, y_mem, w_out, layer, h, prec):
    t = h.shape[0]
    tm = min(t, 2048 if y_main.dtype == jnp.bfloat16 else 1024)
    tn = 512
    return pl.pallas_call(
        functools.partial(_out_proj_kernel, prec=prec),
        grid=(t // tm, D_MODEL // tn),
        in_specs=[
            pl.BlockSpec((tm, MAIN_WIDTH), lambda i, j: (i, 0)),
            pl.BlockSpec((tm, MEM_WIDTH), lambda i, j: (i, 0)),
            pl.BlockSpec((None, MAIN_WIDTH, tn), lambda i, j: (layer, 0, j)),
            pl.BlockSpec((None, MEM_WIDTH, tn), lambda i, j: (layer, MAIN_WIDTH // MEM_WIDTH, j)),
            pl.BlockSpec((tm, tn), lambda i, j: (i, j)),
        ],
        out_specs=pl.BlockSpec((tm, tn), lambda i, j: (i, j)),
        out_shape=jax.ShapeDtypeStruct((t, D_MODEL), F32),
        compiler_params=_params(2),
    )(y_main, y_mem, w_out, w_out, h)


def _router_kernel(h_ref, g_ref, wr_ref, br_ref, x2_ref, idx_ref, gate_ref, *, prec):
    x = h_ref[...]
    xn = x * lax.rsqrt(jnp.mean(x * x, axis=-1, keepdims=True) + EPS) * g_ref[...]
    _store_chunk_rows(x2_ref, xn)
    lg = _dot(xn, wr_ref[...], prec) + br_ref[...]
    lane = lax.broadcasted_iota(jnp.int32, lg.shape, 1)
    neg = -jnp.inf
    is_grp = lane < N_GROUPS
    gl = jnp.where(is_grp, lg, neg)
    ge = jnp.where(is_grp, jnp.exp(gl - jnp.max(gl, axis=-1, keepdims=True)), 0.0)
    pg = ge / jnp.sum(ge, axis=-1, keepdims=True)
    p_top = jnp.max(pg, axis=-1, keepdims=True)
    g_idx = jnp.min(jnp.where(is_grp & (pg == p_top), lane, LANES), axis=-1, keepdims=True)
    first = N_GROUPS + EXPERTS_PER_GROUP * g_idx
    in_grp = (lane >= first) & (lane < first + EXPERTS_PER_GROUP)
    e1 = jnp.max(jnp.where(in_grp, lg, neg), axis=-1, keepdims=True)
    i1 = jnp.min(jnp.where(in_grp & (lg == e1), lane, LANES), axis=-1, keepdims=True)
    rest = in_grp & (lane != i1)
    e2 = jnp.max(jnp.where(rest, lg, neg), axis=-1, keepdims=True)
    i2 = jnp.min(jnp.where(rest & (lg == e2), lane, LANES), axis=-1, keepdims=True)
    t2 = jnp.exp(e2 - e1)
    den = 1.0 + t2
    idx_ref[...] = jnp.where(lane == 0, i1 - N_GROUPS, jnp.where(lane == 1, i2 - N_GROUPS, 0))
    gate_ref[...] = jnp.where(lane == 0, p_top * (1.0 / den), jnp.where(lane == 1, p_top * (t2 / den), 0.0))


def _router(h, gain, wr, br, prec):
    t = h.shape[0]
    tm = min(t, 512)
    return pl.pallas_call(
        functools.partial(_router_kernel, prec=prec),
        grid=(t // tm,),
        in_specs=[pl.BlockSpec((tm, D_MODEL), lambda i: (i, 0)), pl.BlockSpec((1, D_MODEL), lambda i: (0, 0)),
                  pl.BlockSpec((D_MODEL, LANES), lambda i: (0, 0)), pl.BlockSpec((1, LANES), lambda i: (0, 0))],
        out_specs=[pl.BlockSpec((tm * N_CHUNKS, LANES), lambda i: (i, 0)), pl.BlockSpec((tm, LANES), lambda i: (i, 0)),
                   pl.BlockSpec((tm, LANES), lambda i: (i, 0))],
        out_shape=[jax.ShapeDtypeStruct((t * N_CHUNKS, LANES), F32), jax.ShapeDtypeStruct((t, LANES), jnp.int32),
                   jax.ShapeDtypeStruct((t, LANES), F32)],
        compiler_params=_params(1),
    )(h, gain, wr, br)


def _plan_kernel(e_ref, pos_ref, te_ref, nv_ref, rank_ref, *, t2, blk, tm):
    shift = tm.bit_length() - 1
    nblk = t2 // blk
    r = lax.broadcasted_iota(jnp.int32, (blk, blk), 0)
    c = lax.broadcasted_iota(jnp.int32, (blk, blk), 1)
    upper = (r <= c).astype(F32)
    sub = lax.broadcasted_iota(jnp.int32, (N_EXPERTS, blk), 0)
    carry = jnp.zeros((N_EXPERTS, 1), F32)
    for b in range(nblk):
        sl = slice(b * blk, (b + 1) * blk)
        oh = (sub == e_ref[:, sl]).astype(F32)
        cs = _dot(oh, upper) + carry
        rank_ref[:, sl] = jnp.sum(oh * (cs - 1.0), axis=0, keepdims=True)
        carry = cs[:, blk - 1:blk]
    counts = carry.astype(jnp.int32)
    padc = ((counts + (tm - 1)) >> shift) << shift
    sub1 = lax.broadcasted_iota(jnp.int32, (N_EXPERTS, 1), 0)
    off = jnp.zeros((N_EXPERTS, 1), jnp.int32)
    run = jnp.zeros((1, 1), jnp.int32)
    for e in range(N_EXPERTS):
        off = jnp.where(sub1 == e, run, off)
        run = run + padc[e:e + 1, :]
    ends = off + padc
    for b in range(nblk):
        sl = slice(b * blk, (b + 1) * blk)
        offv = jnp.sum(jnp.where(sub == e_ref[:, sl], off, 0), axis=0, keepdims=True)
        pos_ref[:, sl] = rank_ref[:, sl].astype(jnp.int32) + offv
    lane = lax.broadcasted_iota(jnp.int32, (N_EXPERTS, LANES), 1)
    lane1 = lax.broadcasted_iota(jnp.int32, (1, LANES), 1)
    nv = run >> shift
    te = jnp.sum((ends <= lane * tm).astype(jnp.int32), axis=0, keepdims=True)
    te_last = jnp.sum((ends <= (nv - 1) * tm).astype(jnp.int32), axis=0, keepdims=True)
    te_ref[...] = jnp.minimum(jnp.where(lane1 >= nv, te_last, te), N_EXPERTS - 1)
    nv_ref[...] = jnp.broadcast_to(nv, (1, LANES))


def _plan(e_flat, tm):
    t2 = e_flat.shape[1]
    blk = min(t2, 512)
    return pl.pallas_call(
        functools.partial(_plan_kernel, t2=t2, blk=blk, tm=tm),
        out_shape=[jax.ShapeDtypeStruct((1, t2), jnp.int32), jax.ShapeDtypeStruct((1, LANES), jnp.int32),
                   jax.ShapeDtypeStruct((1, LANES), jnp.int32)],
        scratch_shapes=[pltpu.VMEM((1, t2), F32)],
        compiler_params=pltpu.CompilerParams(vmem_limit_bytes=VMEM_LIMIT_BYTES),
    )(e_flat)


def _invert_kernel(pos_ref, te_ref, nv_ref, src_ref, *, t, tm, n_tiles):
    nv = nv_ref[0]

    def zero_tile(j, carry):
        last_of_expert = (j >= nv - 1) | (te_ref[jnp.minimum(j + 1, n_tiles - 1)] != te_ref[j])

        @pl.when(last_of_expert)
        def _():
            def zero(r, c):
                src_ref[j * tm + r] = 0
                return c
            lax.fori_loop(0, tm, zero, 0, unroll=min(tm, 32))

        return carry

    lax.fori_loop(0, n_tiles, zero_tile, 0)
    for k in range(2):
        def put(tok, carry, k=k):
            src_ref[pos_ref[k * t + tok]] = N_CHUNKS * tok
            return carry

        lax.fori_loop(0, t, put, 0, unroll=32)


def _invert(pos, te, nv, t, tm):
    n_tiles = te.shape[0]
    return pl.pallas_call(
        functools.partial(_invert_kernel, t=t, tm=tm, n_tiles=n_tiles),
        grid_spec=pltpu.PrefetchScalarGridSpec(
            num_scalar_prefetch=3,
            grid=(1,),
            in_specs=[],
            out_specs=pl.BlockSpec(memory_space=pltpu.SMEM),
        ),
        out_shape=jax.ShapeDtypeStruct((n_tiles * tm,), jnp.int32),
        compiler_params=_params(1),
    )(pos, te, nv)


def _moe_ffn_kernel(te_ref, nv_ref, src_ref, x_ref, wg_hbm, wu_hbm, wd_hbm, y_ref,
                    xbuf_ref, wg_ref, wu_ref, wd_ref, wslot_ref, sems, wsems, *, tm, layer, prec):
    i = pl.program_id(0)
    nv = nv_ref[0]
    slot = i % 2
    expert = te_ref[i]

    def start_tile(tile, dst_slot):
        base = tile * tm
        for r in range(tm):
            src = pl.multiple_of(src_ref[base + r], N_CHUNKS)
            pltpu.make_async_copy(x_ref.at[pl.ds(src, N_CHUNKS)],
                                  xbuf_ref.at[dst_slot, pl.ds(r * ROW_PITCH, N_CHUNKS)],
                                  sems.at[dst_slot]).start(priority=r % 2)

    def wait_tile():
        done = xbuf_ref.at[slot, pl.ds(0, tm * N_CHUNKS)]
        pltpu.make_async_copy(done, done, sems.at[slot]).wait()

    def weight_copies(e, ws):
        return (pltpu.make_async_copy(wg_hbm.at[layer, e], wg_ref.at[ws], wsems.at[ws]),
                pltpu.make_async_copy(wu_hbm.at[layer, e], wu_ref.at[ws], wsems.at[ws]),
                pltpu.make_async_copy(wd_hbm.at[layer, e], wd_ref.at[ws], wsems.at[ws]))

    @pl.when(i == 0)
    def _():
        start_tile(0, 0)
        wslot_ref[0] = 1
        for cp in weight_copies(expert, 0):
            cp.start()

    is_first = (i < nv) & ((i == 0) | (te_ref[jnp.maximum(i - 1, 0)] != expert))

    @pl.when(is_first)
    def _():
        ws = 1 - wslot_ref[0]
        wslot_ref[0] = ws
        for cp in weight_copies(expert, ws):
            cp.wait()
        nxt = lax.while_loop(lambda j: (j < nv) & (te_ref[jnp.minimum(j, nv - 1)] == expert), lambda j: j + 1, i + 1)

        @pl.when(nxt < nv)
        def _():
            for cp in weight_copies(te_ref[nxt], 1 - ws):
                cp.start()

    def ffn():
        ws = wslot_ref[0]
        x = _load_gathered_rows(xbuf_ref.at[slot], tm)
        hdn = jax.nn.silu(_dot(x, wg_ref[ws], prec)) * _dot(x, wu_ref[ws], prec)
        _store_chunk_rows(y_ref, _dot(hdn, wd_ref[ws], prec))

    @pl.when(i + 1 < nv)
    def _():
        wait_tile()
        start_tile(i + 1, 1 - slot)
        ffn()

    @pl.when(i + 1 == nv)
    def _():
        wait_tile()
        ffn()

    @pl.when(i >= nv)
    def _():
        y_ref[...] = jnp.zeros_like(y_ref)


def _moe_ffn(te, nv, src, x2, w_gate, w_up, w_down, layer, tm, prec):
    n_rows = src.shape[0]
    n_tiles = n_rows // tm
    any_spec = pl.BlockSpec(memory_space=pl.ANY)
    return pl.pallas_call(
        functools.partial(_moe_ffn_kernel, tm=tm, layer=layer, prec=prec),
        grid_spec=pltpu.PrefetchScalarGridSpec(
            num_scalar_prefetch=3,
            grid=(n_tiles,),
            in_specs=[any_spec, any_spec, any_spec, any_spec],
            out_specs=pl.BlockSpec((tm * N_CHUNKS, LANES), lambda i, te, nv, src: (i, 0)),
            scratch_shapes=[pltpu.VMEM((2, tm * ROW_PITCH, LANES), F32),
                            pltpu.VMEM((2, D_MODEL, D_EXPERT), F32), pltpu.VMEM((2, D_MODEL, D_EXPERT), F32),
                            pltpu.VMEM((2, D_EXPERT, D_MODEL), F32), pltpu.SMEM((1,), jnp.int32),
                            pltpu.SemaphoreType.DMA((2,)), pltpu.SemaphoreType.DMA((2,))],
        ),
        out_shape=jax.ShapeDtypeStruct((n_rows * N_CHUNKS, LANES), F32),
        compiler_params=_params(1),
    )(te, nv, src, x2, w_gate, w_up, w_down)


def _combine_kernel(pos_ref, h_ref, g_ref, y_ref, o_ref, y0_ref, y1_ref, sems, *, t, tm):
    base = pl.program_id(0) * tm
    for k, buf in enumerate((y0_ref, y1_ref)):
        for r in range(tm):
            src = pl.multiple_of(pos_ref[k * t + base + r] * N_CHUNKS, N_CHUNKS)
            pltpu.make_async_copy(y_ref.at[pl.ds(src, N_CHUNKS)], buf.at[pl.ds(r * ROW_PITCH, N_CHUNKS)],
                                  sems.at[k]).start(priority=k)
    for k, buf in enumerate((y0_ref, y1_ref)):
        done = buf.at[pl.ds(0, tm * N_CHUNKS)]
        pltpu.make_async_copy(done, done, sems.at[k]).wait()
    g = g_ref[...]
    o_ref[...] = (h_ref[...] + g[:, 0:1] * _load_gathered_rows(y0_ref, tm)
                  + g[:, 1:2] * _load_gathered_rows(y1_ref, tm))


def _combine(pos, h, gates, y):
    t = h.shape[0]
    tm = min(t, 256)
    return pl.pallas_call(
        functools.partial(_combine_kernel, t=t, tm=tm),
        grid_spec=pltpu.PrefetchScalarGridSpec(
            num_scalar_prefetch=1,
            grid=(t // tm,),
            in_specs=[pl.BlockSpec((tm, D_MODEL), lambda i, pos: (i, 0)),
                      pl.BlockSpec((tm, LANES), lambda i, pos: (i, 0)),
                      pl.BlockSpec(memory_space=pl.ANY)],
            out_specs=pl.BlockSpec((tm, D_MODEL), lambda i, pos: (i, 0)),
            scratch_shapes=[pltpu.VMEM((tm * ROW_PITCH, LANES), F32), pltpu.VMEM((tm * ROW_PITCH, LANES), F32),
                            pltpu.SemaphoreType.DMA((2,))],
        ),
        out_shape=jax.ShapeDtypeStruct((t, D_MODEL), F32),
        compiler_params=_params(1),
    )(pos, h, gates, y)


def _hier_moe(h, gain, wr, br, w_gate, w_up, w_down, layer, prec):
    t = h.shape[0]
    tm = 256 if t >= 1024 else 16
    n_tiles = (2 * t) // tm + N_EXPERTS
    assert n_tiles <= LANES
    x2, idx, gates = _router(h, gain, wr, br, prec)
    e_flat = idx[:, :2].T.reshape(1, 2 * t)
    pos, te, nv = _plan(e_flat, tm)
    pos = pos.reshape(2 * t)
    te = te[0, :n_tiles]
    nv = nv[0, :1]
    src = _invert(pos, te, nv, t, tm)
    y = _moe_ffn(te, nv, src, x2, w_gate, w_up, w_down, layer, tm, prec)
    return _combine(pos, h, gates, y)


def _fox_cumsum_kernel(lft_ref, lf_ref, ct_ref, c_ref, *, s, blk):
    r = lax.broadcasted_iota(jnp.int32, (blk, blk), 0)
    c = lax.broadcasted_iota(jnp.int32, (blk, blk), 1)
    upper = (r <= c).astype(F32)
    lower = (r >= c).astype(F32)
    carry_t = jnp.zeros((2 * SUBLANES, 1), F32)
    carry = jnp.zeros((1, LANES), F32)
    for b in range(s // blk):
        sl = slice(b * blk, (b + 1) * blk)
        ct = _dot3_left(lft_ref[:, sl], upper) + carry_t
        ct_ref[:, sl] = ct
        carry_t = ct[:, blk - 1:blk]
        cc = _dot3_right(lower, lf_ref[sl, :]) + carry
        c_ref[sl, :] = cc
        carry = cc[blk - 1:blk, :]


def _fox_cumsum(lft, lf, nb, s):
    t = lf.shape[0]
    blk = min(s, 256)
    return pl.pallas_call(
        functools.partial(_fox_cumsum_kernel, s=s, blk=blk),
        grid=(nb,),
        in_specs=[pl.BlockSpec((2 * SUBLANES, s), lambda b: (0, b)), pl.BlockSpec((s, LANES), lambda b: (b, 0))],
        out_specs=[pl.BlockSpec((2 * SUBLANES, s), lambda b: (0, b)), pl.BlockSpec((s, LANES), lambda b: (b, 0))],
        out_shape=[jax.ShapeDtypeStruct((2 * SUBLANES, t), F32), jax.ShapeDtypeStruct((t, LANES), F32)],
        compiler_params=_params(1),
    )(lft, lf)


def _fox_prompt_kernel(q_ref, k_ref, v_ref, ct_ref, c_ref, o_ref, *, s, tq):
    h = pl.program_id(1)
    lane = lax.broadcasted_iota(jnp.int32, (s, LANES), 1)
    c_col = jnp.sum(jnp.where(lane == h, c_ref[...], 0.0), axis=1, keepdims=True)
    c_row = ct_ref[pl.ds(h, 1), :]
    ri = lax.broadcasted_iota(jnp.int32, (tq, tq), 0)
    ci = lax.broadcasted_iota(jnp.int32, (tq, tq), 1)
    causal = ci <= ri
    for qi in range(s // tq):
        qs = slice(qi * tq, (qi + 1) * tq)
        q = q_ref[qs, :].astype(F32)
        m = jnp.full((tq, 1), -jnp.inf, F32)
        l = jnp.zeros((tq, 1), F32)
        acc = jnp.zeros((tq, HEAD_DIM), F32)
        for kj in range(qi + 1):
            ks = slice(kj * tq, (kj + 1) * tq)
            sc = lax.dot_general(q, k_ref[ks, :], _NT, preferred_element_type=F32) * ATTN_SCALE
            sc = sc + c_col[qs, :] - c_row[:, ks]
            if kj == qi:
                sc = jnp.where(causal, sc, -jnp.inf)
            m_new = jnp.maximum(m, jnp.max(sc, axis=-1, keepdims=True))
            alpha = jnp.exp(m - m_new)
            p = jnp.exp(sc - m_new)
            l = alpha * l + jnp.sum(p, axis=-1, keepdims=True)
            acc = alpha * acc + _dot(p, v_ref[ks, :])
            m = m_new
        o_ref[qs, :] = (acc / l).astype(o_ref.dtype)


def _fox_prompt(q, k, v, ct, c, nb, s):
    t = q.shape[0]
    tq = min(s, 512)
    head_spec = pl.BlockSpec((s, HEAD_DIM), lambda b, h: (b, h))
    return pl.pallas_call(
        functools.partial(_fox_prompt_kernel, s=s, tq=tq),
        grid=(nb, MAIN_HEADS),
        in_specs=[head_spec, head_spec, head_spec,
                  pl.BlockSpec((2 * SUBLANES, s), lambda b, h: (0, b)),
                  pl.BlockSpec((s, LANES), lambda b, h: (b, 0))],
        out_specs=head_spec,
        out_shape=jax.ShapeDtypeStruct((t, MAIN_WIDTH), _matmul_only_dtype(None)),
        compiler_params=_params(2),
    )(q, k, v, ct, c)


def _fox_past_bias_kernel(pt_ref, lfc_ref, o_ref, buf_ref, sem, *, n_pages):
    b = pl.program_id(0)

    def page_copy(p):
        page = pt_ref[b * n_pages + p]
        return pltpu.make_async_copy(lfc_ref.at[:, pl.ds(page, 1), :], buf_ref.at[:, pl.ds(p, 1), :], sem)

    def start(p, carry):
        page_copy(p).start()
        return carry

    def wait(p, carry):
        page_copy(p).wait()
        return carry

    lax.fori_loop(0, n_pages, start, 0)
    r = lax.broadcasted_iota(jnp.int32, (PAGE_SIZE, PAGE_SIZE), 0)
    c = lax.broadcasted_iota(jnp.int32, (PAGE_SIZE, PAGE_SIZE), 1)
    after_in_page = (r > c).astype(F32)
    pr = lax.broadcasted_iota(jnp.int32, (n_pages, n_pages), 0)
    pc = lax.broadcasted_iota(jnp.int32, (n_pages, n_pages), 1)
    later_pages = (pc > pr).astype(F32)
    lax.fori_loop(0, n_pages, wait, 0)
    for h in range(MAIN_HEADS):
        lp = buf_ref[h]
        tot = jnp.broadcast_to(jnp.sum(lp, axis=1, keepdims=True), lp.shape)
        o_ref[h] = _dot3_left(lp, after_in_page) + _dot3_right(later_pages, tot)


def _fox_past_bias(page_table_flat, cache_logf_t, nb, n_pages):
    return pl.pallas_call(
        functools.partial(_fox_past_bias_kernel, n_pages=n_pages),
        grid_spec=pltpu.PrefetchScalarGridSpec(
            num_scalar_prefetch=1,
            grid=(nb,),
            in_specs=[pl.BlockSpec(memory_space=pl.ANY)],
            out_specs=pl.BlockSpec((None, MAIN_HEADS, n_pages, PAGE_SIZE), lambda b, pt: (b, 0, 0, 0)),
            scratch_shapes=[pltpu.VMEM((MAIN_HEADS, n_pages, PAGE_SIZE), F32), pltpu.SemaphoreType.DMA(())],
        ),
        out_shape=jax.ShapeDtypeStruct((nb, MAIN_HEADS, n_pages, PAGE_SIZE), F32),
        compiler_params=_params(1),
    )(page_table_flat, cache_logf_t)


def _fox_sample_kernel(*refs, ppb, n_steps, valid):
    pt_ref, q_ref, kn_ref, vn_ref, lf_ref, bias_ref = refs[:6]
    k_refs = refs[6:6 + ppb]
    v_refs = refs[6 + ppb:6 + 2 * ppb]
    o_ref, m_ref, l_ref, acc_ref = refs[6 + 2 * ppb:]
    g = pl.program_id(1)
    rows = SAMPLE_ROWS

    @pl.when(g == 0)
    def _():
        m_ref[...] = jnp.full(m_ref.shape, -jnp.inf, F32)
        l_ref[...] = jnp.zeros(l_ref.shape, F32)
        acc_ref[...] = jnp.zeros(acc_ref.shape, F32)

    r8 = lax.broadcasted_iota(jnp.int32, (rows, rows), 0)
    c8 = lax.broadcasted_iota(jnp.int32, (rows, rows), 1)
    new_mask = (c8 <= r8) & (c8 < valid)
    c_new = _dot3_right(new_mask.astype(F32), lf_ref[...])

    def qk3(q_hi, q_lo, k):
        a = _dot_nt(jnp.concatenate([q_hi, q_lo], axis=0), k)
        return a[:rows] + a[rows:]

    def pv3(p, v):
        p_hi, p_lo = _split_hi_lo(p)
        a = _dot(jnp.concatenate([p_hi, p_lo], axis=0), v)
        return a[:rows] + a[rows:]

    all_scores = []
    for h in range(MAIN_HEADS):
        q_hi, q_lo = _split_hi_lo(q_ref[:, h * HEAD_DIM:(h + 1) * HEAD_DIM])
        cq = c_new[:, h:h + 1]
        all_scores.append([qk3(q_hi, q_lo, k_refs[u][h]) * ATTN_SCALE + cq + bias_ref[h, u:u + 1, :]
                           for u in range(ppb)])
    all_probs, all_alpha = [], []
    for h in range(MAIN_HEADS):
        m_old = m_ref[h, :, 0:1]
        m_new = m_old
        for sc in all_scores[h]:
            m_new = jnp.maximum(m_new, jnp.max(sc, axis=-1, keepdims=True))
        alpha = jnp.exp(m_old - m_new)
        probs = [jnp.exp(sc - m_new) for sc in all_scores[h]]
        l_new = alpha * l_ref[h, :, 0:1]
        for p in probs:
            l_new = l_new + jnp.sum(p, axis=-1, keepdims=True)
        m_ref[h] = jnp.broadcast_to(m_new, (rows, LANES))
        l_ref[h] = jnp.broadcast_to(l_new, (rows, LANES))
        all_probs.append(probs)
        all_alpha.append(alpha)
    for h in range(MAIN_HEADS):
        sl = slice(h * HEAD_DIM, (h + 1) * HEAD_DIM)
        acc = all_alpha[h] * acc_ref[:, sl]
        for u in range(ppb):
            acc = acc + pv3(all_probs[h][u], v_refs[u][h])
        acc_ref[:, sl] = acc

    @pl.when(g == n_steps - 1)
    def _():
        for h in range(MAIN_HEADS):
            sl = slice(h * HEAD_DIM, (h + 1) * HEAD_DIM)
            qh = q_ref[:, sl]
            cq = c_new[:, h:h + 1]
            cq_row = jnp.sum(jnp.where(r8 == c8, jnp.broadcast_to(cq, (rows, rows)), 0.0), axis=0, keepdims=True)
            sn = _dot_nt(qh, kn_ref[:, sl], HIGHEST) * ATTN_SCALE + cq - cq_row
            sn = jnp.where(new_mask, sn, -jnp.inf)
            m_old = m_ref[h, :, 0:1]
            m_new = jnp.maximum(m_old, jnp.max(sn, axis=-1, keepdims=True))
            alpha = jnp.exp(m_old - m_new)
            p = jnp.exp(sn - m_new)
            l_new = alpha * l_ref[h, :, 0:1] + jnp.sum(p, axis=-1, keepdims=True)
            acc = alpha * acc_ref[:, sl] + _dot(p, vn_ref[:, sl], HIGHEST)
            o_ref[:, sl] = acc / l_new


def _fox_sample(page_table_flat, q, k_new, v_new, lf, bias, cache_k_t, cache_v_t, nb, n_pages, valid):
    ppb = 8 if n_pages % 8 == 0 else n_pages
    n_steps = n_pages // ppb
    row_spec = pl.BlockSpec((SAMPLE_ROWS, MAIN_WIDTH), lambda b, g, pt: (b, 0))

    def page_spec(u):
        return pl.BlockSpec((None, MAIN_HEADS, PAGE_SIZE, HEAD_DIM),
                            lambda b, g, pt, u=u: (pt[b * n_pages + g * ppb + u], 0, 0, 0))

    in_specs = [row_spec, row_spec, row_spec,
                pl.BlockSpec((SAMPLE_ROWS, LANES), lambda b, g, pt: (b, 0)),
                pl.BlockSpec((None, MAIN_HEADS, ppb, PAGE_SIZE), lambda b, g, pt: (b, 0, g, 0))]
    in_specs += [page_spec(u) for u in range(ppb)] + [page_spec(u) for u in range(ppb)]
    return pl.pallas_call(
        functools.partial(_fox_sample_kernel, ppb=ppb, n_steps=n_steps, valid=valid),
        grid_spec=pltpu.PrefetchScalarGridSpec(
            num_scalar_prefetch=1,
            grid=(nb, n_steps),
            in_specs=in_specs,
            out_specs=row_spec,
            scratch_shapes=[pltpu.VMEM((MAIN_HEADS, SAMPLE_ROWS, LANES), F32),
                            pltpu.VMEM((MAIN_HEADS, SAMPLE_ROWS, LANES), F32),
                            pltpu.VMEM((SAMPLE_ROWS, MAIN_WIDTH), F32)],
        ),
        out_shape=jax.ShapeDtypeStruct(q.shape, F32),
        compiler_params=_params(2),
    )(page_table_flat, q, k_new, v_new, lf, bias, *([cache_k_t] * ppb), *([cache_v_t] * ppb))


def _trunk(x, nb, rows, valid, mem_k, mem_v, fox, w, prec):
    u, v, qm = _norm_matmul(x, w["norm1_gain"][0:1], w["w_in_a"], 0,
                            [(MAIN_WIDTH, "gelu", 0), (MAIN_WIDTH, "gelu", 0), (MEM_WIDTH, "headnorm_mm", 0)],
                            w["mem_q_gain"][0:1], prec=prec)
    y_main, v_rows = _gmlp(u, v, w["v_gain_a"][0:1], w["w_mix"], w["b_mix"], w["gmlp_rows"], w["gmlp_grp"], valid,
                           prec)
    y_mem = _mem_attn(qm, mem_k, mem_v, 0, nb, rows, prec)
    h = _out_proj(y_main, y_mem, w["w_out"], 0, x, prec)
    h = _hier_moe(h, w["norm2_gain"][0:1], w["w_router"][0], w["b_router"][0], w["w_gate"], w["w_up"], w["w_down"],
                  0, prec)
    k, vv, lf, lft = _norm_matmul(h, w["kv_norm_gain"], w["w_kv_shared"], 0,
                                  [(MAIN_WIDTH, "headnorm", 0), (MAIN_WIDTH, "plain", 0)],
                                  w["k_gain_shared"], forget=w["forget"], prec=prec)
    q, qm = _norm_matmul(h, w["norm1_gain"][1:2], w["w_in_b"], 0,
                         [(MAIN_WIDTH, "headnorm_mm", 0), (MEM_WIDTH, "headnorm_mm", 1)],
                         jnp.concatenate([w["q_gain_b"][0:1], w["mem_q_gain"][1:2]], axis=0), prec=prec)
    y_main = fox(q, k, vv, lf, lft)
    y_mem = _mem_attn(qm, mem_k, mem_v, 1, nb, rows, prec)
    h = _out_proj(y_main, y_mem, w["w_out"], 1, h, prec)
    h = _hier_moe(h, w["norm2_gain"][1:2], w["w_router"][1], w["b_router"][1], w["w_gate"], w["w_up"], w["w_down"],
                  1, prec)
    return h, k, vv, lf, v_rows


def kernel(x_prompt, x_sample, cache_k, cache_v, cache_logf, cache_mem_k, cache_mem_v, page_table, mem_prompt, norm1_gain, norm2_gain, w_in_a, v_gain_a, w_s_a, b_s_a, w_in_b, q_gain_b, kv_norm_gain, w_kv_shared, b_forget, k_gain_shared, mem_norm_gain, w_mem_kv, mem_q_gain, mem_k_gain, w_out, w_router_group, b_router_group, w_router_expert, b_router_expert, w_gate, w_up, w_down):
    batch, seq, _ = x_prompt.shape
    dec_batch, dec_seq, _ = x_sample.shape
    n_phys = cache_k.shape[0]
    n_pages = page_table.shape[1]
    depth = norm1_gain.shape[0]
    assert depth == 2 and dec_seq <= SAMPLE_ROWS and seq % CHUNK == 0

    n_route = N_GROUPS + N_EXPERTS
    w_router = jnp.pad(jnp.concatenate([w_router_group, w_router_expert], axis=-1),
                       ((0, 0), (0, 0), (0, LANES - n_route)))
    b_router = jnp.pad(jnp.concatenate([b_router_group, b_router_expert], axis=-1),
                       ((0, 0), (0, LANES - n_route)))[:, None, :]
    w_f = w_kv_shared[:, 2 * MAIN_WIDTH:]
    forget = (jnp.pad(w_f, ((0, 0), (0, LANES - MAIN_HEADS))),
              jnp.pad(b_forget, (0, LANES - MAIN_HEADS))[None, :],
              jnp.pad(b_forget, (0, 2 * SUBLANES - MAIN_HEADS))[:, None])
    common = dict(norm1_gain=norm1_gain, norm2_gain=norm2_gain, w_in_a=w_in_a, v_gain_a=v_gain_a, w_in_b=w_in_b,
                  q_gain_b=q_gain_b, kv_norm_gain=kv_norm_gain[None, :], w_kv_shared=w_kv_shared,
                  k_gain_shared=k_gain_shared[None, :], mem_q_gain=mem_q_gain, w_out=w_out,
                  w_router=w_router, b_router=b_router, w_gate=w_gate, w_up=w_up, w_down=w_down, forget=forget)

    t_p = batch * seq
    mem_tok = mem_prompt.reshape(batch * N_MEM, D_MODEL)
    mem_k_layers, mem_v_layers = [], []
    for l in range(depth):
        mk, mv = _norm_matmul(mem_tok, mem_norm_gain[l:l + 1], w_mem_kv, l,
                              [(MEM_WIDTH, "headnorm", 0), (MEM_WIDTH, "plain", 0)], mem_k_gain[l:l + 1])
        mem_k_layers.append(mk.reshape(batch, N_MEM, MEM_WIDTH))
        mem_v_layers.append(mv.reshape(batch, N_MEM, MEM_WIDTH))
    mem_k_p = jnp.stack(mem_k_layers)
    mem_v_p = jnp.stack(mem_v_layers)

    def fox_prompt(q, k, v, lf, lft):
        ct, c = _fox_cumsum(lft, lf, batch, seq)
        return _fox_prompt(q, k, v, ct, c, batch, seq)

    w_p = dict(common, w_mix=w_s_a[0], b_mix=b_s_a[0].T, gmlp_rows=CHUNK, gmlp_grp=CHUNK)
    y_p, k_p, v_p, lf_p, vrows_p = _trunk(x_prompt.reshape(t_p, D_MODEL), batch, seq, seq,
                                          mem_k_p, mem_v_p, fox_prompt, w_p, None)

    rows = SAMPLE_ROWS
    t_s = dec_batch * rows
    x_s = jnp.pad(x_sample, ((0, 0), (0, rows - dec_seq), (0, 0))).reshape(t_s, D_MODEL)
    pt_flat = page_table.reshape(dec_batch * n_pages)
    bias = _fox_past_bias(pt_flat, jnp.transpose(cache_logf, (2, 0, 1)), dec_batch, n_pages)
    cache_k_t = jnp.transpose(cache_k, (0, 2, 1, 3))
    cache_v_t = jnp.transpose(cache_v, (0, 2, 1, 3))

    def fox_sample(q, k, v, lf, lft):
        return _fox_sample(pt_flat, q, k, v, lf, bias, cache_k_t, cache_v_t, dec_batch, n_pages, dec_seq)

    w_s = dict(common, w_mix=jnp.tile(w_s_a[0][:, :rows, :rows], (1, dec_batch, dec_batch)),
               b_mix=jnp.tile(b_s_a[0][:, :rows].T, (dec_batch, 1)), gmlp_rows=t_s, gmlp_grp=rows)
    mem_k_s = cache_mem_k.reshape(depth, dec_batch, N_MEM, MEM_WIDTH)
    mem_v_s = cache_mem_v.reshape(depth, dec_batch, N_MEM, MEM_WIDTH)
    y_s, k_s, v_s, lf_s, vrows_s = _trunk(x_s, dec_batch, rows, dec_seq, mem_k_s, mem_v_s, fox_sample, w_s, SPLIT3)

    def unpad(a, *tail):
        return a.reshape((dec_batch, rows) + tail)[:, :dec_seq]

    return (y_p.reshape(batch, seq, D_MODEL),
            unpad(y_s, D_MODEL),
            k_p.reshape(batch, seq, MAIN_HEADS, HEAD_DIM),
            v_p.reshape(batch, seq, MAIN_HEADS, HEAD_DIM),
            lf_p[:, :MAIN_HEADS].reshape(batch, seq, MAIN_HEADS),
            unpad(k_s, MAIN_HEADS, HEAD_DIM),
            unpad(v_s, MAIN_HEADS, HEAD_DIM),
            unpad(lf_s[:, :MAIN_HEADS], MAIN_HEADS),
            mem_k_p.reshape(depth, batch, N_MEM, MEM_HEADS, HEAD_DIM),
            mem_v_p.reshape(depth, batch, N_MEM, MEM_HEADS, HEAD_DIM),
            vrows_p.reshape(batch, seq, MAIN_WIDTH)[:, -CHUNK:][None],
            unpad(vrows_s, MAIN_WIDTH)[None])
```

```python
import functools

import jax
import jax.numpy as jnp
from jax import lax
from jax.experimental import pallas as pl
from jax.experimental.pallas import tpu as pltpu

D_MODEL = 2048
HEAD_DIM = 128
MAIN_HEADS = 12
MEM_HEADS = 4
MAIN_WIDTH = MAIN_HEADS * HEAD_DIM
MEM_WIDTH = MEM_HEADS * HEAD_DIM
N_MEM = 256
N_GROUPS = 4
EXPERTS_PER_GROUP = 4
N_EXPERTS = N_GROUPS * EXPERTS_PER_GROUP
D_EXPERT = 512
CHUNK = 128
PAGE_SIZE = 128
EPS = 1e-6
ATTN_SCALE = HEAD_DIM ** -0.5
LOG2E = 1.4426950408889634

LANES = 128
SUBLANES = 8
VMEM_LIMIT_BYTES = 56 * 1024 * 1024
SAMPLE_ROWS = SUBLANES
N_CHUNKS = D_MODEL // LANES
ROW_PITCH = 24

F32 = jnp.float32
_NT = (((1,), (1,)), ((), ()))


def _params(n_axes):
    return pltpu.CompilerParams(dimension_semantics=("arbitrary",) * n_axes,
                                vmem_limit_bytes=VMEM_LIMIT_BYTES)


HIGHEST = lax.Precision.HIGHEST


SPLIT3 = "split3"


def _dot(a, b, prec=None):
    if prec == SPLIT3:
        return _dot_split3(a, b, _dot)
    return jnp.dot(a, b, preferred_element_type=F32, precision=prec)


def _dot_nt(a, b, prec=None):
    if prec == SPLIT3:
        return _dot_split3(a, b, _dot_nt)
    return lax.dot_general(a, b, _NT, preferred_element_type=F32, precision=prec)


def _dot_tn(a, b, prec=None):
    dims = (((0,), (1,)), ((), ()))
    if prec == SPLIT3:
        a_hi, a_lo = _split_hi_lo(a)
        b_hi, b_lo = _split_hi_lo(b)
        return _dot_tn(a_hi, b_hi) + _dot_tn(a_lo, b_hi) + _dot_tn(a_hi, b_lo)
    return lax.dot_general(a, b, dims, preferred_element_type=F32, precision=prec)


def _dot_split3(a, b, dot):
    a_hi, a_lo = _split_hi_lo(a)
    b_hi, b_lo = _split_hi_lo(b)
    m = a.shape[0]
    s = dot(jnp.concatenate([a_hi, a_lo], axis=0), b_hi)
    return s[:m] + s[m:] + dot(a_hi, b_lo)


def _matmul_only_dtype(prec):
    return jnp.bfloat16 if prec is None else F32


def _split_hi_lo(x):
    hi = x.astype(jnp.bfloat16).astype(F32)
    return hi, x - hi


def _split3(x):
    hi = x.astype(jnp.bfloat16).astype(F32)
    r = x - hi
    mid = r.astype(jnp.bfloat16).astype(F32)
    lo = (r - mid).astype(jnp.bfloat16).astype(F32)
    return hi, mid, lo


def _dot3_right(a_exact, x):
    hi, mid, lo = _split3(x)
    return _dot(a_exact, hi) + _dot(a_exact, mid) + _dot(a_exact, lo)


def _dot3_left(x, b_exact):
    hi, mid, lo = _split3(x)
    return _dot(hi, b_exact) + _dot(mid, b_exact) + _dot(lo, b_exact)


def _store_chunk_rows(ref, x):
    rows = x.shape[0]
    for c in range(N_CHUNKS):
        ref[pl.ds(c, rows, stride=N_CHUNKS), :] = x[:, c * LANES:(c + 1) * LANES]


def _load_gathered_rows(ref, rows):
    return jnp.concatenate([ref[pl.ds(c, rows, stride=ROW_PITCH), :] for c in range(N_CHUNKS)], axis=1)


def _head_norm(z, gain_row):
    return z * lax.rsqrt(jnp.mean(z * z, axis=-1, keepdims=True) + EPS) * gain_row


def _norm_matmul_kernel(*refs, segs, tn, with_forget, prec):
    it = iter(refs)
    x_ref, g_ref, w_ref, hg_ref = next(it), next(it), next(it), next(it)
    if with_forget:
        wf_ref, bfr_ref, bfc_ref = next(it), next(it), next(it)
    out_refs = [next(it) for _ in segs]
    if with_forget:
        lf_ref, lft_ref = next(it), next(it)
    xn_ref = next(it)
    j = pl.program_id(1)

    def project(xn, o_ref, kind, grow):
        z = _dot(xn, w_ref[...], prec)
        if kind == "plain":
            o_ref[...] = z
        elif kind == "gelu":
            o_ref[...] = jax.nn.gelu(z)
        else:
            gain = hg_ref[grow:grow + 1, :]
            for c in range(tn // HEAD_DIM):
                sl = slice(c * HEAD_DIM, (c + 1) * HEAD_DIM)
                o_ref[:, sl] = _head_norm(z[:, sl], gain).astype(o_ref.dtype)

    @pl.when(j == 0)
    def _first():
        x = x_ref[...]
        xn = x * lax.rsqrt(jnp.mean(x * x, axis=-1, keepdims=True) + EPS) * g_ref[...]
        xn_ref[...] = xn
        if with_forget:
            lf_ref[...] = jax.nn.log_sigmoid(_dot(xn, wf_ref[...], prec) + bfr_ref[...])
            zt = _dot_tn(wf_ref[...], xn, prec)[:2 * SUBLANES, :]
            lft_ref[...] = jax.nn.log_sigmoid(zt + bfc_ref[...])
        project(xn, out_refs[0], segs[0][1], segs[0][2])

    lo = 0
    for (ncols, kind, grow), o_ref in zip(segs, out_refs):
        nt = ncols // tn

        @pl.when((j >= max(lo, 1)) & (j < lo + nt))
        def _rest(o_ref=o_ref, kind=kind, grow=grow):
            project(xn_ref[...], o_ref, kind, grow)

        lo += nt


def _norm_matmul(x, gain, w, layer, segs, head_gains, forget=None, prec=None):
    t = x.shape[0]
    tm = min(t, 1024)
    tn = 512
    n_tiles = sum(s[0] for s in segs) // tn
    with_forget = forget is not None
    in_specs = [
        pl.BlockSpec((tm, D_MODEL), lambda i, j: (i, 0)),
        pl.BlockSpec((1, D_MODEL), lambda i, j: (0, 0)),
        pl.BlockSpec((None, D_MODEL, tn), lambda i, j: (layer, 0, j)) if w.ndim == 3
        else pl.BlockSpec((D_MODEL, tn), lambda i, j: (0, j)),
        pl.BlockSpec(head_gains.shape, lambda i, j: (0, 0)),
    ]
    args = [x, gain, w, head_gains]
    if with_forget:
        wf, bfr, bfc = forget
        in_specs += [pl.BlockSpec(wf.shape, lambda i, j: (0, 0)),
                     pl.BlockSpec(bfr.shape, lambda i, j: (0, 0)), pl.BlockSpec(bfc.shape, lambda i, j: (0, 0))]
        args += [wf, bfr, bfc]
    out_shape, out_specs = [], []
    lo = 0
    for ncols, kind, _ in segs:
        nt = ncols // tn
        out_shape.append(jax.ShapeDtypeStruct((t, ncols), _matmul_only_dtype(prec) if kind == "headnorm_mm" else F32))
        out_specs.append(pl.BlockSpec((tm, tn), lambda i, j, lo=lo, nt=nt: (i, jnp.clip(j - lo, 0, nt - 1))))
        lo += nt
    if with_forget:
        out_shape += [jax.ShapeDtypeStruct((t, LANES), F32), jax.ShapeDtypeStruct((2 * SUBLANES, t), F32)]
        out_specs += [pl.BlockSpec((tm, LANES), lambda i, j: (i, 0)),
                      pl.BlockSpec((2 * SUBLANES, tm), lambda i, j: (0, i))]
    return pl.pallas_call(
        functools.partial(_norm_matmul_kernel, segs=tuple(segs), tn=tn, with_forget=with_forget, prec=prec),
        grid=(t // tm, n_tiles),
        in_specs=in_specs,
        out_specs=out_specs,
        out_shape=out_shape,
        scratch_shapes=[pltpu.VMEM((tm, D_MODEL), F32)],
        compiler_params=_params(2),
    )(*args)


def _gmlp_kernel(u_ref, v_ref, vg_ref, w_ref, b_ref, y_ref, vn_ref, *, rows, grp, valid, prec):
    v = v_ref[...]
    vn = v * lax.rsqrt(jnp.mean(v * v, axis=-1, keepdims=True) + EPS) * vg_ref[...]
    vn_ref[...] = vn
    r = lax.broadcasted_iota(jnp.int32, (rows, rows), 0)
    c = lax.broadcasted_iota(jnp.int32, (rows, rows), 1)
    shift = grp.bit_length() - 1
    allowed = ((c & (grp - 1)) <= (r & (grp - 1))) & ((c & (grp - 1)) < valid)
    if grp < rows:
        allowed = allowed & ((r >> shift) == (c >> shift))
    for g in range(MAIN_HEADS):
        sl = slice(g * HEAD_DIM, (g + 1) * HEAD_DIM)
        w = jnp.where(allowed, w_ref[g], 0.0)
        mixed = _dot(w, vn[:, sl], prec) + b_ref[:, g:g + 1]
        y_ref[:, sl] = (u_ref[:, sl] * mixed).astype(y_ref.dtype)


def _gmlp(u, v, v_gain, w_mix, b_mix, rows, grp, valid, blocks_per_seq, prec):
    t = u.shape[0]
    return pl.pallas_call(
        functools.partial(_gmlp_kernel, rows=rows, grp=grp, valid=valid, prec=prec),
        grid=(t // rows,),
        in_specs=[
            pl.BlockSpec((rows, MAIN_WIDTH), lambda i: (i, 0)),
            pl.BlockSpec((rows, MAIN_WIDTH), lambda i: (i, 0)),
            pl.BlockSpec((1, MAIN_WIDTH), lambda i: (0, 0)),
            pl.BlockSpec((MAIN_HEADS, rows, rows), lambda i: (0, 0, 0)),
            pl.BlockSpec((rows, MAIN_HEADS), lambda i: (0, 0)),
        ],
        out_specs=[pl.BlockSpec((rows, MAIN_WIDTH), lambda i: (i, 0)),
                   pl.BlockSpec((rows, MAIN_WIDTH), lambda i: (i // blocks_per_seq, 0))],
        out_shape=[jax.ShapeDtypeStruct((t, MAIN_WIDTH), _matmul_only_dtype(prec)),
                   jax.ShapeDtypeStruct((t // blocks_per_seq, MAIN_WIDTH), F32)],
        compiler_params=_params(1),
    )(u, v, v_gain, w_mix, b_mix)


def _mem_attn_kernel(q_ref, k_ref, v_ref, o_ref, *, prec):
    for h in range(MEM_HEADS):
        sl = slice(h * HEAD_DIM, (h + 1) * HEAD_DIM)
        s = _dot_nt(q_ref[:, sl].astype(F32), k_ref[:, sl], prec) * ATTN_SCALE
        e = jnp.exp(s - jnp.max(s, axis=-1, keepdims=True))
        p = e / jnp.sum(e, axis=-1, keepdims=True)
        o_ref[:, sl] = _dot(p, v_ref[:, sl], prec).astype(o_ref.dtype)


def _mem_attn(q, k, v, layer, nb, rows_per_batch, prec):
    tq = min(rows_per_batch, 1024)
    nq = rows_per_batch // tq
    kv_spec = pl.BlockSpec((None, None, N_MEM, MEM_WIDTH), lambda b, i: (layer, b, 0, 0))
    return pl.pallas_call(
        functools.partial(_mem_attn_kernel, prec=prec),
        grid=(nb, nq),
        in_specs=[pl.BlockSpec((tq, MEM_WIDTH), lambda b, i: (b * nq + i, 0)), kv_spec, kv_spec],
        out_specs=pl.BlockSpec((tq, MEM_WIDTH), lambda b, i: (b * nq + i, 0)),
        out_shape=jax.ShapeDtypeStruct(q.shape, _matmul_only_dtype(prec)),
        compiler_params=_params(2),
    )(q, k, v)


def _out_proj_kernel(ym_ref, ymem_ref, w1_ref, w2_ref, h_ref, o_ref, *, prec):
    w1 = w1_ref[...].astype(ym_ref.dtype)
    w2 = w2_ref[...].astype(ymem_ref.dtype)
    o_ref[...] = h_ref[...] + _dot(ym_ref[...], w1, prec) + _dot(ymem_ref[...], w2, prec)


def _out_proj(y_main, y_mem, w_out, layer, h, prec):
    t = h.shape[0]
    tm = min(t, 2048 if y_main.dtype == jnp.bfloat16 else 1024)
    tn = 512
    return pl.pallas_call(
        functools.partial(_out_proj_kernel, prec=prec),
        grid=(t // tm, D_MODEL // tn),
        in_specs=[
            pl.BlockSpec((tm, MAIN_WIDTH), lambda i, j: (i, 0)),
            pl.BlockSpec((tm, MEM_WIDTH), lambda i, j: (i, 0)),
            pl.BlockSpec((None, MAIN_WIDTH, tn), lambda i, j: (layer, 0, j)),
            pl.BlockSpec((None, MEM_WIDTH, tn), lambda i, j: (layer, MAIN_WIDTH // MEM_WIDTH, j)),
            pl.BlockSpec((tm, tn), lambda i, j: (i, j)),
        ],
        out_specs=pl.BlockSpec((tm, tn), lambda i, j: (i, j)),
        out_shape=jax.ShapeDtypeStruct((t, D_MODEL), F32),
        compiler_params=_params(2),
    )(y_main, y_mem, w_out, w_out, h)


def _router_kernel(h_ref, g_ref, wr_ref, br_ref, x2_ref, idx_ref, gate_ref, *, prec):
    x = h_ref[...]
    xn = x * lax.rsqrt(jnp.mean(x * x, axis=-1, keepdims=True) + EPS) * g_ref[...]
    _store_chunk_rows(x2_ref, xn)
    lg = _dot(xn, wr_ref[...], prec) + br_ref[...]
    lane = lax.broadcasted_iota(jnp.int32, lg.shape, 1)
    neg = -jnp.inf
    is_grp = lane < N_GROUPS
    gl = jnp.where(is_grp, lg, neg)
    ge = jnp.where(is_grp, jnp.exp(gl - jnp.max(gl, axis=-1, keepdims=True)), 0.0)
    pg = ge / jnp.sum(ge, axis=-1, keepdims=True)
    p_top = jnp.max(pg, axis=-1, keepdims=True)
    g_idx = jnp.min(jnp.where(is_grp & (pg == p_top), lane, LANES), axis=-1, keepdims=True)
    first = N_GROUPS + EXPERTS_PER_GROUP * g_idx
    in_grp = (lane >= first) & (lane < first + EXPERTS_PER_GROUP)
    e1 = jnp.max(jnp.where(in_grp, lg, neg), axis=-1, keepdims=True)
    i1 = jnp.min(jnp.where(in_grp & (lg == e1), lane, LANES), axis=-1, keepdims=True)
    rest = in_grp & (lane != i1)
    e2 = jnp.max(jnp.where(rest, lg, neg), axis=-1, keepdims=True)
    i2 = jnp.min(jnp.where(rest & (lg == e2), lane, LANES), axis=-1, keepdims=True)
    t2 = jnp.exp(e2 - e1)
    den = 1.0 + t2
    idx_ref[...] = jnp.where(lane == 0, i1 - N_GROUPS, jnp.where(lane == 1, i2 - N_GROUPS, 0))
    gate_ref[...] = jnp.where(lane == 0, p_top * (1.0 / den), jnp.where(lane == 1, p_top * (t2 / den), 0.0))


def _router(h, gain, wr, br, prec):
    t = h.shape[0]
    tm = min(t, 512)
    return pl.pallas_call(
        functools.partial(_router_kernel, prec=prec),
        grid=(t // tm,),
        in_specs=[pl.BlockSpec((tm, D_MODEL), lambda i: (i, 0)), pl.BlockSpec((1, D_MODEL), lambda i: (0, 0)),
                  pl.BlockSpec((D_MODEL, LANES), lambda i: (0, 0)), pl.BlockSpec((1, LANES), lambda i: (0, 0))],
        out_specs=[pl.BlockSpec((tm * N_CHUNKS, LANES), lambda i: (i, 0)), pl.BlockSpec((tm, LANES), lambda i: (i, 0)),
                   pl.BlockSpec((tm, LANES), lambda i: (i, 0))],
        out_shape=[jax.ShapeDtypeStruct((t * N_CHUNKS, LANES), F32), jax.ShapeDtypeStruct((t, LANES), jnp.int32),
                   jax.ShapeDtypeStruct((t, LANES), F32)],
        compiler_params=_params(1),
    )(h, gain, wr, br)


def _plan_kernel(e_ref, pos_ref, te_ref, nv_ref, rank_ref, *, t2, blk, tm):
    shift = tm.bit_length() - 1
    nblk = t2 // blk
    r = lax.broadcasted_iota(jnp.int32, (blk, blk), 0)
    c = lax.broadcasted_iota(jnp.int32, (blk, blk), 1)
    upper = (r <= c).astype(F32)
    sub = lax.broadcasted_iota(jnp.int32, (N_EXPERTS, blk), 0)
    carry = jnp.zeros((N_EXPERTS, 1), F32)
    for b in range(nblk):
        sl = slice(b * blk, (b + 1) * blk)
        oh = (sub == e_ref[:, sl]).astype(F32)
        cs = _dot(oh, upper) + carry
        rank_ref[:, sl] = jnp.sum(oh * (cs - 1.0), axis=0, keepdims=True)
        carry = cs[:, blk - 1:blk]
    counts = carry.astype(jnp.int32)
    padc = ((counts + (tm - 1)) >> shift) << shift
    sub1 = lax.broadcasted_iota(jnp.int32, (N_EXPERTS, 1), 0)
    off = jnp.zeros((N_EXPERTS, 1), jnp.int32)
    run = jnp.zeros((1, 1), jnp.int32)
    for e in range(N_EXPERTS):
        off = jnp.where(sub1 == e, run, off)
        run = run + padc[e:e + 1, :]
    ends = off + padc
    for b in range(nblk):
        sl = slice(b * blk, (b + 1) * blk)
        offv = jnp.sum(jnp.where(sub == e_ref[:, sl], off, 0), axis=0, keepdims=True)
        pos_ref[:, sl] = rank_ref[:, sl].astype(jnp.int32) + offv
    lane = lax.broadcasted_iota(jnp.int32, (N_EXPERTS, LANES), 1)
    lane1 = lax.broadcasted_iota(jnp.int32, (1, LANES), 1)
    nv = run >> shift
    te = jnp.sum((ends <= lane * tm).astype(jnp.int32), axis=0, keepdims=True)
    te_last = jnp.sum((ends <= (nv - 1) * tm).astype(jnp.int32), axis=0, keepdims=True)
    te_ref[...] = jnp.minimum(jnp.where(lane1 >= nv, te_last, te), N_EXPERTS - 1)
    nv_ref[...] = jnp.broadcast_to(nv, (1, LANES))


def _plan(e_flat, tm):
    t2 = e_flat.shape[1]
    blk = min(t2, 512)
    return pl.pallas_call(
        functools.partial(_plan_kernel, t2=t2, blk=blk, tm=tm),
        out_shape=[jax.ShapeDtypeStruct((1, t2), jnp.int32), jax.ShapeDtypeStruct((1, LANES), jnp.int32),
                   jax.ShapeDtypeStruct((1, LANES), jnp.int32)],
        scratch_shapes=[pltpu.VMEM((1, t2), F32)],
        compiler_params=pltpu.CompilerParams(vmem_limit_bytes=VMEM_LIMIT_BYTES),
    )(e_flat)


def _invert_kernel(pos_ref, te_ref, nv_ref, src_ref, *, t, tm, n_tiles):
    nv = nv_ref[0]

    def zero_tile(j, carry):
        last_of_expert = (j >= nv - 1) | (te_ref[jnp.minimum(j + 1, n_tiles - 1)] != te_ref[j])

        @pl.when(last_of_expert)
        def _():
            def zero(r, c):
                src_ref[j * tm + r] = 0
                return c
            lax.fori_loop(0, tm, zero, 0, unroll=min(tm, 32))

        return carry

    lax.fori_loop(0, n_tiles, zero_tile, 0)
    for k in range(2):
        def put(tok, carry, k=k):
            src_ref[pos_ref[k * t + tok]] = N_CHUNKS * tok
            return carry

        lax.fori_loop(0, t, put, 0, unroll=32)


def _invert(pos, te, nv, t, tm):
    n_tiles = te.shape[0]
    return pl.pallas_call(
        functools.partial(_invert_kernel, t=t, tm=tm, n_tiles=n_tiles),
        grid_spec=pltpu.PrefetchScalarGridSpec(
            num_scalar_prefetch=3,
            grid=(1,),
            in_specs=[],
            out_specs=pl.BlockSpec(memory_space=pltpu.SMEM),
        ),
        out_shape=jax.ShapeDtypeStruct((n_tiles * tm,), jnp.int32),
        compiler_params=_params(1),
    )(pos, te, nv)


def _moe_ffn_kernel(te_ref, nv_ref, src_ref, x_ref, wg_hbm, wu_hbm, wd_hbm, y_ref,
                    xbuf_ref, wg_ref, wu_ref, wd_ref, wslot_ref, sems, wsems, *, tm, layer, prec):
    i = pl.program_id(0)
    nv = nv_ref[0]
    slot = i % 2
    expert = te_ref[i]

    def start_tile(tile, dst_slot):
        base = tile * tm
        for r in range(tm):
            src = pl.multiple_of(src_ref[base + r], N_CHUNKS)
            pltpu.make_async_copy(x_ref.at[pl.ds(src, N_CHUNKS)],
                                  xbuf_ref.at[dst_slot, pl.ds(r * ROW_PITCH, N_CHUNKS)],
                                  sems.at[dst_slot]).start(priority=r % 2)

    def wait_tile():
        done = xbuf_ref.at[slot, pl.ds(0, tm * N_CHUNKS)]
        pltpu.make_async_copy(done, done, sems.at[slot]).wait()

    def weight_copies(e, ws):
        return (pltpu.make_async_copy(wg_hbm.at[layer, e], wg_ref.at[ws], wsems.at[ws]),
                pltpu.make_async_copy(wu_hbm.at[layer, e], wu_ref.at[ws], wsems.at[ws]),
                pltpu.make_async_copy(wd_hbm.at[layer, e], wd_ref.at[ws], wsems.at[ws]))

    @pl.when(i == 0)
    def _():
        start_tile(0, 0)
        wslot_ref[0] = 1
        for cp in weight_copies(expert, 0):
            cp.start()

    is_first = (i < nv) & ((i == 0) | (te_ref[jnp.maximum(i - 1, 0)] != expert))

    @pl.when(is_first)
    def _():
        ws = 1 - wslot_ref[0]
        wslot_ref[0] = ws
        for cp in weight_copies(expert, ws):
            cp.wait()
        nxt = lax.while_loop(lambda j: (j < nv) & (te_ref[jnp.minimum(j, nv - 1)] == expert), lambda j: j + 1, i + 1)

        @pl.when(nxt < nv)
        def _():
            for cp in weight_copies(te_ref[nxt], 1 - ws):
                cp.start()

    def ffn():
        ws = wslot_ref[0]
        x = _load_gathered_rows(xbuf_ref.at[slot], tm)
        hdn = jax.nn.silu(_dot(x, wg_ref[ws], prec)) * _dot(x, wu_ref[ws], prec)
        _store_chunk_rows(y_ref, _dot(hdn, wd_ref[ws], prec))

    @pl.when(i + 1 < nv)
    def _():
        wait_tile()
        start_tile(i + 1, 1 - slot)
        ffn()

    @pl.when(i + 1 == nv)
    def _():
        wait_tile()
        ffn()

    @pl.when(i >= nv)
    def _():
        y_ref[...] = jnp.zeros_like(y_ref)


def _moe_ffn(te, nv, src, x2, w_gate, w_up, w_down, layer, tm, prec):
    n_rows = src.shape[0]
    n_tiles = n_rows // tm
    any_spec = pl.BlockSpec(memory_space=pl.ANY)
    return pl.pallas_call(
        functools.partial(_moe_ffn_kernel, tm=tm, layer=layer, prec=prec),
        grid_spec=pltpu.PrefetchScalarGridSpec(
            num_scalar_prefetch=3,
            grid=(n_tiles,),
            in_specs=[any_spec, any_spec, any_spec, any_spec],
            out_specs=pl.BlockSpec((tm * N_CHUNKS, LANES), lambda i, te, nv, src: (i, 0)),
            scratch_shapes=[pltpu.VMEM((2, tm * ROW_PITCH, LANES), F32),
                            pltpu.VMEM((2, D_MODEL, D_EXPERT), F32), pltpu.VMEM((2, D_MODEL, D_EXPERT), F32),
                            pltpu.VMEM((2, D_EXPERT, D_MODEL), F32), pltpu.SMEM((1,), jnp.int32),
                            pltpu.SemaphoreType.DMA((2,)), pltpu.SemaphoreType.DMA((2,))],
        ),
        out_shape=jax.ShapeDtypeStruct((n_rows * N_CHUNKS, LANES), F32),
        compiler_params=_params(1),
    )(te, nv, src, x2, w_gate, w_up, w_down)


def _combine_kernel(pos_ref, h_ref, g_ref, y_ref, o_ref, y0_ref, y1_ref, sems, *, t, tm):
    base = pl.program_id(0) * tm
    for k, buf in enumerate((y0_ref, y1_ref)):
        for r in range(tm):
            src = pl.multiple_of(pos_ref[k * t + base + r] * N_CHUNKS, N_CHUNKS)
            pltpu.make_async_copy(y_ref.at[pl.ds(src, N_CHUNKS)], buf.at[pl.ds(r * ROW_PITCH, N_CHUNKS)],
                                  sems.at[k]).start(priority=k)
    for k, buf in enumerate((y0_ref, y1_ref)):
        done = buf.at[pl.ds(0, tm * N_CHUNKS)]
        pltpu.make_async_copy(done, done, sems.at[k]).wait()
    g = g_ref[...]
    o_ref[...] = (h_ref[...] + g[:, 0:1] * _load_gathered_rows(y0_ref, tm)
                  + g[:, 1:2] * _load_gathered_rows(y1_ref, tm))


def _combine(pos, h, gates, y):
    t = h.shape[0]
    tm = min(t, 256)
    return pl.pallas_call(
        functools.partial(_combine_kernel, t=t, tm=tm),
        grid_spec=pltpu.PrefetchScalarGridSpec(
            num_scalar_prefetch=1,
            grid=(t // tm,),
            in_specs=[pl.BlockSpec((tm, D_MODEL), lambda i, pos: (i, 0)),
                      pl.BlockSpec((tm, LANES), lambda i, pos: (i, 0)),
                      pl.BlockSpec(memory_space=pl.ANY)],
            out_specs=pl.BlockSpec((tm, D_MODEL), lambda i, pos: (i, 0)),
            scratch_shapes=[pltpu.VMEM((tm * ROW_PITCH, LANES), F32), pltpu.VMEM((tm * ROW_PITCH, LANES), F32),
                            pltpu.SemaphoreType.DMA((2,))],
        ),
        out_shape=jax.ShapeDtypeStruct((t, D_MODEL), F32),
        compiler_params=_params(1),
    )(pos, h, gates, y)


def _hier_moe(h, gain, wr, br, w_gate, w_up, w_down, layer, prec):
    t = h.shape[0]
    tm = 256 if t >= 1024 else 16
    n_tiles = (2 * t) // tm + N_EXPERTS
    assert n_tiles <= LANES
    x2, idx, gates = _router(h, gain, wr, br, prec)
    e_flat = idx[:, :2].T.reshape(1, 2 * t)
    pos, te, nv = _plan(e_flat, tm)
    pos = pos.reshape(2 * t)
    te = te[0, :n_tiles]
    nv = nv[0, :1]
    src = _invert(pos, te, nv, t, tm)
    y = _moe_ffn(te, nv, src, x2, w_gate, w_up, w_down, layer, tm, prec)
    return _combine(pos, h, gates, y)


def _fox_cumsum_kernel(lft_ref, lf_ref, ct_ref, c_ref, *, s, blk):
    r = lax.broadcasted_iota(jnp.int32, (blk, blk), 0)
    c = lax.broadcasted_iota(jnp.int32, (blk, blk), 1)
    upper = (r <= c).astype(F32)
    lower = (r >= c).astype(F32)
    carry_t = jnp.zeros((2 * SUBLANES, 1), F32)
    carry = jnp.zeros((1, LANES), F32)
    for b in range(s // blk):
        sl = slice(b * blk, (b + 1) * blk)
        ct = _dot3_left(lft_ref[:, sl], upper) + carry_t
        ct_ref[:, sl] = ct
        carry_t = ct[:, blk - 1:blk]
        cc = _dot3_right(lower, lf_ref[sl, :]) + carry
        c_ref[sl, :] = cc
        carry = cc[blk - 1:blk, :]


def _fox_cumsum(lft, lf, nb, s):
    t = lf.shape[0]
    blk = min(s, 256)
    return pl.pallas_call(
        functools.partial(_fox_cumsum_kernel, s=s, blk=blk),
        grid=(nb,),
        in_specs=[pl.BlockSpec((2 * SUBLANES, s), lambda b: (0, b)), pl.BlockSpec((s, LANES), lambda b: (b, 0))],
        out_specs=[pl.BlockSpec((2 * SUBLANES, s), lambda b: (0, b)), pl.BlockSpec((s, LANES), lambda b: (b, 0))],
        out_shape=[jax.ShapeDtypeStruct((2 * SUBLANES, t), F32), jax.ShapeDtypeStruct((t, LANES), F32)],
        compiler_params=_params(1),
    )(lft, lf)


def _fox_prompt_kernel(q_ref, k_ref, v_ref, ct_ref, c_ref, o_ref, *, s, tq):
    h = pl.program_id(1)
    lane = lax.broadcasted_iota(jnp.int32, (s, LANES), 1)
    c_col = jnp.sum(jnp.where(lane == h, c_ref[...], 0.0), axis=1, keepdims=True) * LOG2E
    c_row = ct_ref[pl.ds(h, 1), :] * LOG2E
    ri = lax.broadcasted_iota(jnp.int32, (tq, tq), 0)
    ci = lax.broadcasted_iota(jnp.int32, (tq, tq), 1)
    causal = ci <= ri
    for qi in range(s // tq):
        qs = slice(qi * tq, (qi + 1) * tq)
        q = q_ref[qs, :].astype(F32)
        cq = c_col[qs, :]
        m = jnp.full((tq, 1), -jnp.inf, F32)
        l = jnp.zeros((tq, 1), F32)
        acc = jnp.zeros((tq, HEAD_DIM), F32)
        for kj in range(qi + 1):
            ks = slice(kj * tq, (kj + 1) * tq)
            sc = lax.dot_general(q, k_ref[ks, :], _NT, preferred_element_type=F32) - c_row[:, ks]
            if kj == qi:
                sc = jnp.where(causal, sc, -jnp.inf)
            m_new = jnp.maximum(m, jnp.max(sc, axis=-1, keepdims=True) + cq)
            alpha = jnp.exp2(m - m_new)
            p = jnp.exp2(sc - (m_new - cq))
            l = alpha * l + jnp.sum(p, axis=-1, keepdims=True)
            acc = alpha * acc + _dot(p, v_ref[ks, :])
            m = m_new
        o_ref[qs, :] = (acc / l).astype(o_ref.dtype)


def _fox_prompt(q, k, v, ct, c, nb, s):
    t = q.shape[0]
    tq = min(s, 512)
    head_spec = pl.BlockSpec((s, HEAD_DIM), lambda b, h: (b, h))
    return pl.pallas_call(
        functools.partial(_fox_prompt_kernel, s=s, tq=tq),
        grid=(nb, MAIN_HEADS),
        in_specs=[head_spec, head_spec, head_spec,
                  pl.BlockSpec((2 * SUBLANES, s), lambda b, h: (0, b)),
                  pl.BlockSpec((s, LANES), lambda b, h: (b, 0))],
        out_specs=head_spec,
        out_shape=jax.ShapeDtypeStruct((t, MAIN_WIDTH), _matmul_only_dtype(None)),
        compiler_params=_params(2),
    )(q, k, v, ct, c)


def _fox_past_bias_kernel(pt_ref, lfc_ref, o_ref, buf_ref, sem, *, n_pages):
    b = pl.program_id(0)

    def page_copy(p):
        page = pt_ref[b * n_pages + p]
        return pltpu.make_async_copy(lfc_ref.at[:, pl.ds(page, 1), :], buf_ref.at[:, pl.ds(p, 1), :], sem)

    def start(p, carry):
        page_copy(p).start()
        return carry

    def wait(p, carry):
        page_copy(p).wait()
        return carry

    lax.fori_loop(0, n_pages, start, 0)
    r = lax.broadcasted_iota(jnp.int32, (PAGE_SIZE, PAGE_SIZE), 0)
    c = lax.broadcasted_iota(jnp.int32, (PAGE_SIZE, PAGE_SIZE), 1)
    after_in_page = (r > c).astype(F32)
    pr = lax.broadcasted_iota(jnp.int32, (n_pages, n_pages), 0)
    pc = lax.broadcasted_iota(jnp.int32, (n_pages, n_pages), 1)
    later_pages = (pc > pr).astype(F32)
    lax.fori_loop(0, n_pages, wait, 0)
    for h in range(MAIN_HEADS):
        lp = buf_ref[h]
        tot = jnp.broadcast_to(jnp.sum(lp, axis=1, keepdims=True), lp.shape)
        o_ref[h] = _dot3_left(lp, after_in_page) + _dot3_right(later_pages, tot)


def _fox_past_bias(page_table_flat, cache_logf_t, nb, n_pages):
    return pl.pallas_call(
        functools.partial(_fox_past_bias_kernel, n_pages=n_pages),
        grid_spec=pltpu.PrefetchScalarGridSpec(
            num_scalar_prefetch=1,
            grid=(nb,),
            in_specs=[pl.BlockSpec(memory_space=pl.ANY)],
            out_specs=pl.BlockSpec((None, MAIN_HEADS, n_pages, PAGE_SIZE), lambda b, pt: (b, 0, 0, 0)),
            scratch_shapes=[pltpu.VMEM((MAIN_HEADS, n_pages, PAGE_SIZE), F32), pltpu.SemaphoreType.DMA(())],
        ),
        out_shape=jax.ShapeDtypeStruct((nb, MAIN_HEADS, n_pages, PAGE_SIZE), F32),
        compiler_params=_params(1),
    )(page_table_flat, cache_logf_t)


def _fox_sample_kernel(*refs, ppb, n_steps, valid):
    pt_ref, q_ref, kn_ref, vn_ref, lf_ref, bias_ref = refs[:6]
    k_refs = refs[6:6 + ppb]
    v_refs = refs[6 + ppb:6 + 2 * ppb]
    o_ref, m_ref, l_ref, acc_ref = refs[6 + 2 * ppb:]
    g = pl.program_id(1)
    rows = SAMPLE_ROWS

    @pl.when(g == 0)
    def _():
        m_ref[...] = jnp.full(m_ref.shape, -jnp.inf, F32)
        l_ref[...] = jnp.zeros(l_ref.shape, F32)
        acc_ref[...] = jnp.zeros(acc_ref.shape, F32)

    r8 = lax.broadcasted_iota(jnp.int32, (rows, rows), 0)
    c8 = lax.broadcasted_iota(jnp.int32, (rows, rows), 1)
    new_mask = (c8 <= r8) & (c8 < valid)
    c_new = _dot3_right(new_mask.astype(F32), lf_ref[...])

    def qk3(q_hi, q_lo, k):
        a = _dot_nt(jnp.concatenate([q_hi, q_lo], axis=0), k)
        return a[:rows] + a[rows:]

    def pv3(p, v):
        p_hi, p_lo = _split_hi_lo(p)
        a = _dot(jnp.concatenate([p_hi, p_lo], axis=0), v)
        return a[:rows] + a[rows:]

    all_scores = []
    for h in range(MAIN_HEADS):
        q_hi, q_lo = _split_hi_lo(q_ref[:, h * HEAD_DIM:(h + 1) * HEAD_DIM])
        cq = c_new[:, h:h + 1]
        all_scores.append([qk3(q_hi, q_lo, k_refs[u][h]) + cq + bias_ref[h, u:u + 1, :]
                           for u in range(ppb)])
    all_probs, all_alpha = [], []
    for h in range(MAIN_HEADS):
        m_old = m_ref[h, :, 0:1]
        m_new = m_old
        for sc in all_scores[h]:
            m_new = jnp.maximum(m_new, jnp.max(sc, axis=-1, keepdims=True))
        alpha = jnp.exp(m_old - m_new)
        probs = [jnp.exp(sc - m_new) for sc in all_scores[h]]
        l_new = alpha * l_ref[h, :, 0:1]
        for p in probs:
            l_new = l_new + jnp.sum(p, axis=-1, keepdims=True)
        m_ref[h] = jnp.broadcast_to(m_new, (rows, LANES))
        l_ref[h] = jnp.broadcast_to(l_new, (rows, LANES))
        all_probs.append(probs)
        all_alpha.append(alpha)
    for h in range(MAIN_HEADS):
        sl = slice(h * HEAD_DIM, (h + 1) * HEAD_DIM)
        acc = all_alpha[h] * acc_ref[:, sl]
        for u in range(ppb):
            acc = acc + pv3(all_probs[h][u], v_refs[u][h])
        acc_ref[:, sl] = acc

    @pl.when(g == n_steps - 1)
    def _():
        for h in range(MAIN_HEADS):
            sl = slice(h * HEAD_DIM, (h + 1) * HEAD_DIM)
            qh = q_ref[:, sl]
            cq = c_new[:, h:h + 1]
            cq_row = jnp.sum(jnp.where(r8 == c8, jnp.broadcast_to(cq, (rows, rows)), 0.0), axis=0, keepdims=True)
            sn = _dot_nt(qh, kn_ref[:, sl], HIGHEST) + cq - cq_row
            sn = jnp.where(new_mask, sn, -jnp.inf)
            m_old = m_ref[h, :, 0:1]
            m_new = jnp.maximum(m_old, jnp.max(sn, axis=-1, keepdims=True))
            alpha = jnp.exp(m_old - m_new)
            p = jnp.exp(sn - m_new)
            l_new = alpha * l_ref[h, :, 0:1] + jnp.sum(p, axis=-1, keepdims=True)
            acc = alpha * acc_ref[:, sl] + _dot(p, vn_ref[:, sl], HIGHEST)
            o_ref[:, sl] = acc / l_new


def _fox_sample(page_table_flat, q, k_new, v_new, lf, bias, cache_k_t, cache_v_t, nb, n_pages, valid):
    ppb = 8 if n_pages % 8 == 0 else n_pages
    n_steps = n_pages // ppb
    row_spec = pl.BlockSpec((SAMPLE_ROWS, MAIN_WIDTH), lambda b, g, pt: (b, 0))

    def page_spec(u):
        return pl.BlockSpec((None, MAIN_HEADS, PAGE_SIZE, HEAD_DIM),
                            lambda b, g, pt, u=u: (pt[b * n_pages + g * ppb + u], 0, 0, 0))

    in_specs = [row_spec, row_spec, row_spec,
                pl.BlockSpec((SAMPLE_ROWS, LANES), lambda b, g, pt: (b, 0)),
                pl.BlockSpec((None, MAIN_HEADS, ppb, PAGE_SIZE), lambda b, g, pt: (b, 0, g, 0))]
    in_specs += [page_spec(u) for u in range(ppb)] + [page_spec(u) for u in range(ppb)]
    return pl.pallas_call(
        functools.partial(_fox_sample_kernel, ppb=ppb, n_steps=n_steps, valid=valid),
        grid_spec=pltpu.PrefetchScalarGridSpec(
            num_scalar_prefetch=1,
            grid=(nb, n_steps),
            in_specs=in_specs,
            out_specs=row_spec,
            scratch_shapes=[pltpu.VMEM((MAIN_HEADS, SAMPLE_ROWS, LANES), F32),
                            pltpu.VMEM((MAIN_HEADS, SAMPLE_ROWS, LANES), F32),
                            pltpu.VMEM((SAMPLE_ROWS, MAIN_WIDTH), F32)],
        ),
        out_shape=jax.ShapeDtypeStruct(q.shape, F32),
        compiler_params=_params(2),
    )(page_table_flat, q, k_new, v_new, lf, bias, *([cache_k_t] * ppb), *([cache_v_t] * ppb))


def _trunk(x, nb, rows, valid, mem_k, mem_v, fox, w, prec):
    u, v, qm = _norm_matmul(x, w["norm1_gain"][0:1], w["w_in_a"], 0,
                            [(MAIN_WIDTH, "gelu", 0), (MAIN_WIDTH, "gelu", 0), (MEM_WIDTH, "headnorm_mm", 0)],
                            w["mem_q_gain"][0:1], prec=prec)
    y_main, v_rows = _gmlp(u, v, w["v_gain_a"][0:1], w["w_mix"], w["b_mix"], w["gmlp_rows"], w["gmlp_grp"], valid,
                           w["gmlp_blocks_per_seq"], prec)
    y_mem = _mem_attn(qm, mem_k, mem_v, 0, nb, rows, prec)
    h = _out_proj(y_main, y_mem, w["w_out"], 0, x, prec)
    h = _hier_moe(h, w["norm2_gain"][0:1], w["w_router"][0], w["b_router"][0], w["w_gate"], w["w_up"], w["w_down"],
                  0, prec)
    k, vv, lf, lft = _norm_matmul(h, w["kv_norm_gain"], w["w_kv_shared"], 0,
                                  [(MAIN_WIDTH, "headnorm", 0), (MAIN_WIDTH, "plain", 0)],
                                  w["k_gain_shared"], forget=w["forget"], prec=prec)
    q, qm = _norm_matmul(h, w["norm1_gain"][1:2], w["w_in_b"], 0,
                         [(MAIN_WIDTH, "headnorm_mm", 0), (MEM_WIDTH, "headnorm_mm", 1)],
                         jnp.concatenate([w["q_gain_b"][0:1] * w["q_logit_scale"], w["mem_q_gain"][1:2]], axis=0),
                         prec=prec)
    y_main = fox(q, k, vv, lf, lft)
    y_mem = _mem_attn(qm, mem_k, mem_v, 1, nb, rows, prec)
    h = _out_proj(y_main, y_mem, w["w_out"], 1, h, prec)
    h = _hier_moe(h, w["norm2_gain"][1:2], w["w_router"][1], w["b_router"][1], w["w_gate"], w["w_up"], w["w_down"],
                  1, prec)
    return h, k, vv, lf, v_rows


def kernel(x_prompt, x_sample, cache_k, cache_v, cache_logf, cache_mem_k, cache_mem_v, page_table, mem_prompt, norm1_gain, norm2_gain, w_in_a, v_gain_a, w_s_a, b_s_a, w_in_b, q_gain_b, kv_norm_gain, w_kv_shared, b_forget, k_gain_shared, mem_norm_gain, w_mem_kv, mem_q_gain, mem_k_gain, w_out, w_router_group, b_router_group, w_router_expert, b_router_expert, w_gate, w_up, w_down):
    batch, seq, _ = x_prompt.shape
    dec_batch, dec_seq, _ = x_sample.shape
    n_phys = cache_k.shape[0]
    n_pages = page_table.shape[1]
    depth = norm1_gain.shape[0]
    assert depth == 2 and dec_seq <= SAMPLE_ROWS and seq % CHUNK == 0

    n_route = N_GROUPS + N_EXPERTS
    w_router = jnp.pad(jnp.concatenate([w_router_group, w_router_expert], axis=-1),
                       ((0, 0), (0, 0), (0, LANES - n_route)))
    b_router = jnp.pad(jnp.concatenate([b_router_group, b_router_expert], axis=-1),
                       ((0, 0), (0, LANES - n_route)))[:, None, :]
    w_f = w_kv_shared[:, 2 * MAIN_WIDTH:]
    forget = (jnp.pad(w_f, ((0, 0), (0, LANES - MAIN_HEADS))),
              jnp.pad(b_forget, (0, LANES - MAIN_HEADS))[None, :],
              jnp.pad(b_forget, (0, 2 * SUBLANES - MAIN_HEADS))[:, None])
    common = dict(norm1_gain=norm1_gain, norm2_gain=norm2_gain, w_in_a=w_in_a, v_gain_a=v_gain_a, w_in_b=w_in_b,
                  q_gain_b=q_gain_b, kv_norm_gain=kv_norm_gain[None, :], w_kv_shared=w_kv_shared,
                  k_gain_shared=k_gain_shared[None, :], mem_q_gain=mem_q_gain, w_out=w_out,
                  w_router=w_router, b_router=b_router, w_gate=w_gate, w_up=w_up, w_down=w_down, forget=forget)

    t_p = batch * seq
    mem_tok = mem_prompt.reshape(batch * N_MEM, D_MODEL)
    mem_k_layers, mem_v_layers = [], []
    for l in range(depth):
        mk, mv = _norm_matmul(mem_tok, mem_norm_gain[l:l + 1], w_mem_kv, l,
                              [(MEM_WIDTH, "headnorm", 0), (MEM_WIDTH, "plain", 0)], mem_k_gain[l:l + 1])
        mem_k_layers.append(mk.reshape(batch, N_MEM, MEM_WIDTH))
        mem_v_layers.append(mv.reshape(batch, N_MEM, MEM_WIDTH))
    mem_k_p = jnp.stack(mem_k_layers)
    mem_v_p = jnp.stack(mem_v_layers)

    def fox_prompt(q, k, v, lf, lft):
        ct, c = _fox_cumsum(lft, lf, batch, seq)
        return _fox_prompt(q, k, v, ct, c, batch, seq)

    w_p = dict(common, w_mix=w_s_a[0], b_mix=b_s_a[0].T, gmlp_rows=CHUNK, gmlp_grp=CHUNK,
               gmlp_blocks_per_seq=seq // CHUNK, q_logit_scale=ATTN_SCALE * LOG2E)
    y_p, k_p, v_p, lf_p, vrows_p = _trunk(x_prompt.reshape(t_p, D_MODEL), batch, seq, seq,
                                          mem_k_p, mem_v_p, fox_prompt, w_p, None)

    rows = SAMPLE_ROWS
    t_s = dec_batch * rows
    x_s = jnp.pad(x_sample, ((0, 0), (0, rows - dec_seq), (0, 0))).reshape(t_s, D_MODEL)
    pt_flat = page_table.reshape(dec_batch * n_pages)
    bias = _fox_past_bias(pt_flat, jnp.transpose(cache_logf, (2, 0, 1)), dec_batch, n_pages)
    cache_k_t = jnp.transpose(cache_k, (0, 2, 1, 3))
    cache_v_t = jnp.transpose(cache_v, (0, 2, 1, 3))

    def fox_sample(q, k, v, lf, lft):
        return _fox_sample(pt_flat, q, k, v, lf, bias, cache_k_t, cache_v_t, dec_batch, n_pages, dec_seq)

    w_s = dict(common, w_mix=jnp.tile(w_s_a[0][:, :rows, :rows], (1, dec_batch, dec_batch)),
               b_mix=jnp.tile(b_s_a[0][:, :rows].T, (dec_batch, 1)), gmlp_rows=t_s, gmlp_grp=rows,
               gmlp_blocks_per_seq=1, q_logit_scale=ATTN_SCALE)
    mem_k_s = cache_mem_k.reshape(depth, dec_batch, N_MEM, MEM_WIDTH)
    mem_v_s = cache_mem_v.reshape(depth, dec_batch, N_MEM, MEM_WIDTH)
    y_s, k_s, v_s, lf_s, vrows_s = _trunk(x_s, dec_batch, rows, dec_seq, mem_k_s, mem_v_s, fox_sample, w_s, SPLIT3)

    def unpad(a, *tail):
        return a.reshape((dec_batch, rows) + tail)[:, :dec_seq]

    return (y_p.reshape(batch, seq, D_MODEL),
            unpad(y_s, D_MODEL),
            k_p.reshape(batch, seq, MAIN_HEADS, HEAD_DIM),
            v_p.reshape(batch, seq, MAIN_HEADS, HEAD_DIM),
            lf_p[:, :MAIN_HEADS].reshape(batch, seq, MAIN_HEADS),
            unpad(k_s, MAIN_HEADS, HEAD_DIM),
            unpad(v_s, MAIN_HEADS, HEAD_DIM),
            unpad(lf_s[:, :MAIN_HEADS], MAIN_HEADS),
            mem_k_p.reshape(depth, batch, N_MEM, MEM_HEADS, HEAD_DIM),
            mem_v_p.reshape(depth, batch, N_MEM, MEM_HEADS, HEAD_DIM),
            vrows_p.reshape(batch, CHUNK, MAIN_WIDTH)[None],
            unpad(vrows_s, MAIN_WIDTH)[None])
```

```python
import functools

import jax
import jax.numpy as jnp
from jax import lax
from jax.experimental import pallas as pl
from jax.experimental.pallas import tpu as pltpu

D_MODEL = 2048
HEAD_DIM = 128
MAIN_HEADS = 12
MEM_HEADS = 4
MAIN_WIDTH = MAIN_HEADS * HEAD_DIM
MEM_WIDTH = MEM_HEADS * HEAD_DIM
N_MEM = 256
N_GROUPS = 4
EXPERTS_PER_GROUP = 4
N_EXPERTS = N_GROUPS * EXPERTS_PER_GROUP
D_EXPERT = 512
CHUNK = 128
PAGE_SIZE = 128
EPS = 1e-6
ATTN_SCALE = HEAD_DIM ** -0.5
LOG2E = 1.4426950408889634

LANES = 128
SUBLANES = 8
VMEM_LIMIT_BYTES = 56 * 1024 * 1024
SAMPLE_ROWS = SUBLANES
N_CHUNKS = D_MODEL // LANES
ROW_PITCH = 24

F32 = jnp.float32
_NT = (((1,), (1,)), ((), ()))


def _params(n_axes):
    return pltpu.CompilerParams(dimension_semantics=("arbitrary",) * n_axes,
                                vmem_limit_bytes=VMEM_LIMIT_BYTES)


HIGHEST = lax.Precision.HIGHEST


SPLIT3 = "split3"


def _dot(a, b, prec=None):
    if prec == SPLIT3:
        return _dot_split3(a, b, _dot)
    return jnp.dot(a, b, preferred_element_type=F32, precision=prec)


def _dot_nt(a, b, prec=None):
    if prec == SPLIT3:
        return _dot_split3(a, b, _dot_nt)
    return lax.dot_general(a, b, _NT, preferred_element_type=F32, precision=prec)


def _dot_tn(a, b, prec=None):
    dims = (((0,), (1,)), ((), ()))
    if prec == SPLIT3:
        a_hi, a_lo = _split_hi_lo(a)
        b_hi, b_lo = _split_hi_lo(b)
        return _dot_tn(a_hi, b_hi) + _dot_tn(a_lo, b_hi) + _dot_tn(a_hi, b_lo)
    return lax.dot_general(a, b, dims, preferred_element_type=F32, precision=prec)


def _dot_split3(a, b, dot):
    a_hi, a_lo = _split_hi_lo(a)
    b_hi, b_lo = _split_hi_lo(b)
    m = a.shape[0]
    s = dot(jnp.concatenate([a_hi, a_lo], axis=0), b_hi)
    return s[:m] + s[m:] + dot(a_hi, b_lo)


def _matmul_only_dtype(prec):
    return jnp.bfloat16 if prec is None else F32


def _split_hi_lo(x):
    hi = x.astype(jnp.bfloat16).astype(F32)
    return hi, x - hi


def _split3(x):
    hi = x.astype(jnp.bfloat16).astype(F32)
    r = x - hi
    mid = r.astype(jnp.bfloat16).astype(F32)
    lo = (r - mid).astype(jnp.bfloat16).astype(F32)
    return hi, mid, lo


def _dot3_right(a_exact, x):
    hi, mid, lo = _split3(x)
    return _dot(a_exact, hi) + _dot(a_exact, mid) + _dot(a_exact, lo)


def _dot3_left(x, b_exact):
    hi, mid, lo = _split3(x)
    return _dot(hi, b_exact) + _dot(mid, b_exact) + _dot(lo, b_exact)


def _store_chunk_rows(ref, x):
    rows = x.shape[0]
    for c in range(N_CHUNKS):
        ref[pl.ds(c, rows, stride=N_CHUNKS), :] = x[:, c * LANES:(c + 1) * LANES]


def _load_gathered_rows(ref, rows):
    return jnp.concatenate([ref[pl.ds(c, rows, stride=ROW_PITCH), :] for c in range(N_CHUNKS)], axis=1)


def _head_norm(z, gain_row):
    return z * lax.rsqrt(jnp.mean(z * z, axis=-1, keepdims=True) + EPS) * gain_row


def _norm_matmul_kernel(*refs, segs, tn, with_forget, prec):
    it = iter(refs)
    x_ref, g_ref, w_ref, hg_ref = next(it), next(it), next(it), next(it)
    if with_forget:
        wf_ref, bfr_ref, bfc_ref = next(it), next(it), next(it)
    out_refs = [next(it) for _ in segs]
    if with_forget:
        lf_ref, lft_ref = next(it), next(it)
    xn_ref = next(it)
    j = pl.program_id(1)

    def project(xn, o_ref, kind, grow):
        z = _dot(xn, w_ref[...], prec)
        if kind == "plain":
            o_ref[...] = z
        elif kind in ("gelu", "gelu_mm"):
            o_ref[...] = jax.nn.gelu(z).astype(o_ref.dtype)
        else:
            gain = hg_ref[grow:grow + 1, :]
            for c in range(tn // HEAD_DIM):
                sl = slice(c * HEAD_DIM, (c + 1) * HEAD_DIM)
                o_ref[:, sl] = _head_norm(z[:, sl], gain).astype(o_ref.dtype)

    @pl.when(j == 0)
    def _first():
        x = x_ref[...]
        xn = x * lax.rsqrt(jnp.mean(x * x, axis=-1, keepdims=True) + EPS) * g_ref[...]
        xn_ref[...] = xn
        if with_forget:
            lf_ref[...] = jax.nn.log_sigmoid(_dot(xn, wf_ref[...], prec) + bfr_ref[...])
            zt = _dot_tn(wf_ref[...], xn, prec)[:2 * SUBLANES, :]
            lft_ref[...] = jax.nn.log_sigmoid(zt + bfc_ref[...])
        project(xn, out_refs[0], segs[0][1], segs[0][2])

    lo = 0
    for (ncols, kind, grow), o_ref in zip(segs, out_refs):
        nt = ncols // tn

        @pl.when((j >= max(lo, 1)) & (j < lo + nt))
        def _rest(o_ref=o_ref, kind=kind, grow=grow):
            project(xn_ref[...], o_ref, kind, grow)

        lo += nt


def _norm_matmul(x, gain, w, layer, segs, head_gains, forget=None, prec=None):
    t = x.shape[0]
    tm = min(t, 1024)
    tn = 512
    n_tiles = sum(s[0] for s in segs) // tn
    with_forget = forget is not None
    in_specs = [
        pl.BlockSpec((tm, D_MODEL), lambda i, j: (i, 0)),
        pl.BlockSpec((1, D_MODEL), lambda i, j: (0, 0)),
        pl.BlockSpec((None, D_MODEL, tn), lambda i, j: (layer, 0, j)) if w.ndim == 3
        else pl.BlockSpec((D_MODEL, tn), lambda i, j: (0, j)),
        pl.BlockSpec(head_gains.shape, lambda i, j: (0, 0)),
    ]
    args = [x, gain, w, head_gains]
    if with_forget:
        wf, bfr, bfc = forget
        in_specs += [pl.BlockSpec(wf.shape, lambda i, j: (0, 0)),
                     pl.BlockSpec(bfr.shape, lambda i, j: (0, 0)), pl.BlockSpec(bfc.shape, lambda i, j: (0, 0))]
        args += [wf, bfr, bfc]
    out_shape, out_specs = [], []
    lo = 0
    for ncols, kind, _ in segs:
        nt = ncols // tn
        out_shape.append(jax.ShapeDtypeStruct((t, ncols), _matmul_only_dtype(prec) if kind.endswith("_mm") else F32))
        out_specs.append(pl.BlockSpec((tm, tn), lambda i, j, lo=lo, nt=nt: (i, jnp.clip(j - lo, 0, nt - 1))))
        lo += nt
    if with_forget:
        out_shape += [jax.ShapeDtypeStruct((t, LANES), F32), jax.ShapeDtypeStruct((2 * SUBLANES, t), F32)]
        out_specs += [pl.BlockSpec((tm, LANES), lambda i, j: (i, 0)),
                      pl.BlockSpec((2 * SUBLANES, tm), lambda i, j: (0, i))]
    return pl.pallas_call(
        functools.partial(_norm_matmul_kernel, segs=tuple(segs), tn=tn, with_forget=with_forget, prec=prec),
        grid=(t // tm, n_tiles),
        in_specs=in_specs,
        out_specs=out_specs,
        out_shape=out_shape,
        scratch_shapes=[pltpu.VMEM((tm, D_MODEL), F32)],
        compiler_params=_params(2),
    )(*args)


def _gmlp_kernel(u_ref, v_ref, vg_ref, w_ref, b_ref, y_ref, vn_ref, *, rows, grp, valid, prec):
    v = v_ref[...]
    vn = v * lax.rsqrt(jnp.mean(v * v, axis=-1, keepdims=True) + EPS) * vg_ref[...]
    vn_ref[...] = vn
    r = lax.broadcasted_iota(jnp.int32, (rows, rows), 0)
    c = lax.broadcasted_iota(jnp.int32, (rows, rows), 1)
    shift = grp.bit_length() - 1
    allowed = ((c & (grp - 1)) <= (r & (grp - 1))) & ((c & (grp - 1)) < valid)
    if grp < rows:
        allowed = allowed & ((r >> shift) == (c >> shift))
    for g in range(MAIN_HEADS):
        sl = slice(g * HEAD_DIM, (g + 1) * HEAD_DIM)
        w = jnp.where(allowed, w_ref[g], 0.0)
        mixed = _dot(w, vn[:, sl], prec) + b_ref[:, g:g + 1]
        y_ref[:, sl] = (u_ref[:, sl] * mixed).astype(y_ref.dtype)


def _gmlp(u, v, v_gain, w_mix, b_mix, rows, grp, valid, blocks_per_seq, prec):
    t = u.shape[0]
    return pl.pallas_call(
        functools.partial(_gmlp_kernel, rows=rows, grp=grp, valid=valid, prec=prec),
        grid=(t // rows,),
        in_specs=[
            pl.BlockSpec((rows, MAIN_WIDTH), lambda i: (i, 0)),
            pl.BlockSpec((rows, MAIN_WIDTH), lambda i: (i, 0)),
            pl.BlockSpec((1, MAIN_WIDTH), lambda i: (0, 0)),
            pl.BlockSpec((MAIN_HEADS, rows, rows), lambda i: (0, 0, 0)),
            pl.BlockSpec((rows, MAIN_HEADS), lambda i: (0, 0)),
        ],
        out_specs=[pl.BlockSpec((rows, MAIN_WIDTH), lambda i: (i, 0)),
                   pl.BlockSpec((rows, MAIN_WIDTH), lambda i: (i // blocks_per_seq, 0))],
        out_shape=[jax.ShapeDtypeStruct((t, MAIN_WIDTH), _matmul_only_dtype(prec)),
                   jax.ShapeDtypeStruct((t // blocks_per_seq, MAIN_WIDTH), F32)],
        compiler_params=_params(1),
    )(u, v, v_gain, w_mix, b_mix)


def _mem_attn_kernel(q_ref, k_ref, v_ref, o_ref, *, prec):
    for h in range(MEM_HEADS):
        sl = slice(h * HEAD_DIM, (h + 1) * HEAD_DIM)
        s = _dot_nt(q_ref[:, sl].astype(F32), k_ref[:, sl], prec) * ATTN_SCALE
        e = jnp.exp(s - jnp.max(s, axis=-1, keepdims=True))
        p = e / jnp.sum(e, axis=-1, keepdims=True)
        o_ref[:, sl] = _dot(p, v_ref[:, sl], prec).astype(o_ref.dtype)


def _mem_attn(q, k, v, layer, nb, rows_per_batch, prec):
    tq = min(rows_per_batch, 1024)
    nq = rows_per_batch // tq
    kv_spec = pl.BlockSpec((None, None, N_MEM, MEM_WIDTH), lambda b, i: (layer, b, 0, 0))
    return pl.pallas_call(
        functools.partial(_mem_attn_kernel, prec=prec),
        grid=(nb, nq),
        in_specs=[pl.BlockSpec((tq, MEM_WIDTH), lambda b, i: (b * nq + i, 0)), kv_spec, kv_spec],
        out_specs=pl.BlockSpec((tq, MEM_WIDTH), lambda b, i: (b * nq + i, 0)),
        out_shape=jax.ShapeDtypeStruct(q.shape, _matmul_only_dtype(prec)),
        compiler_params=_params(2),
    )(q, k, v)


def _out_proj_kernel(ym_ref, ymem_ref, w1_ref, w2_ref, h_ref, o_ref, *, prec):
    w1 = w1_ref[...].astype(ym_ref.dtype)
    w2 = w2_ref[...].astype(ymem_ref.dtype)
    o_ref[...] = h_ref[...] + _dot(ym_ref[...], w1, prec) + _dot(ymem_ref[...], w2, prec)


def _out_proj(y_main, y_mem, w_out, layer, h, prec):
    t = h.shape[0]
    tm = min(t, 2048 if y_main.dtype == jnp.bfloat16 else 1024)
    tn = 512
    return pl.pallas_call(
        functools.partial(_out_proj_kernel, prec=prec),
        grid=(t // tm, D_MODEL // tn),
        in_specs=[
            pl.BlockSpec((tm, MAIN_WIDTH), lambda i, j: (i, 0)),
            pl.BlockSpec((tm, MEM_WIDTH), lambda i, j: (i, 0)),
            pl.BlockSpec((None, MAIN_WIDTH, tn), lambda i, j: (layer, 0, j)),
            pl.BlockSpec((None, MEM_WIDTH, tn), lambda i, j: (layer, MAIN_WIDTH // MEM_WIDTH, j)),
            pl.BlockSpec((tm, tn), lambda i, j: (i, j)),
        ],
        out_specs=pl.BlockSpec((tm, tn), lambda i, j: (i, j)),
        out_shape=jax.ShapeDtypeStruct((t, D_MODEL), F32),
        compiler_params=_params(2),
    )(y_main, y_mem, w_out, w_out, h)


def _router_kernel(h_ref, g_ref, wr_ref, br_ref, x2_ref, idx_ref, gate_ref, *, prec):
    x = h_ref[...]
    xn = x * lax.rsqrt(jnp.mean(x * x, axis=-1, keepdims=True) + EPS) * g_ref[...]
    _store_chunk_rows(x2_ref, xn)
    lg = _dot(xn, wr_ref[...], prec) + br_ref[...]
    lane = lax.broadcasted_iota(jnp.int32, lg.shape, 1)
    neg = -jnp.inf
    is_grp = lane < N_GROUPS
    gl = jnp.where(is_grp, lg, neg)
    ge = jnp.where(is_grp, jnp.exp(gl - jnp.max(gl, axis=-1, keepdims=True)), 0.0)
    pg = ge / jnp.sum(ge, axis=-1, keepdims=True)
    p_top = jnp.max(pg, axis=-1, keepdims=True)
    g_idx = jnp.min(jnp.where(is_grp & (pg == p_top), lane, LANES), axis=-1, keepdims=True)
    first = N_GROUPS + EXPERTS_PER_GROUP * g_idx
    in_grp = (lane >= first) & (lane < first + EXPERTS_PER_GROUP)
    e1 = jnp.max(jnp.where(in_grp, lg, neg), axis=-1, keepdims=True)
    i1 = jnp.min(jnp.where(in_grp & (lg == e1), lane, LANES), axis=-1, keepdims=True)
    rest = in_grp & (lane != i1)
    e2 = jnp.max(jnp.where(rest, lg, neg), axis=-1, keepdims=True)
    i2 = jnp.min(jnp.where(rest & (lg == e2), lane, LANES), axis=-1, keepdims=True)
    t2 = jnp.exp(e2 - e1)
    den = 1.0 + t2
    idx_ref[...] = jnp.where(lane == 0, i1 - N_GROUPS, jnp.where(lane == 1, i2 - N_GROUPS, 0))
    gate_ref[...] = jnp.where(lane == 0, p_top * (1.0 / den), jnp.where(lane == 1, p_top * (t2 / den), 0.0))


def _router(h, gain, wr, br, prec):
    t = h.shape[0]
    tm = min(t, 512)
    return pl.pallas_call(
        functools.partial(_router_kernel, prec=prec),
        grid=(t // tm,),
        in_specs=[pl.BlockSpec((tm, D_MODEL), lambda i: (i, 0)), pl.BlockSpec((1, D_MODEL), lambda i: (0, 0)),
                  pl.BlockSpec((D_MODEL, LANES), lambda i: (0, 0)), pl.BlockSpec((1, LANES), lambda i: (0, 0))],
        out_specs=[pl.BlockSpec((tm * N_CHUNKS, LANES), lambda i: (i, 0)), pl.BlockSpec((tm, LANES), lambda i: (i, 0)),
                   pl.BlockSpec((tm, LANES), lambda i: (i, 0))],
        out_shape=[jax.ShapeDtypeStruct((t * N_CHUNKS, LANES), F32), jax.ShapeDtypeStruct((t, LANES), jnp.int32),
                   jax.ShapeDtypeStruct((t, LANES), F32)],
        compiler_params=_params(1),
    )(h, gain, wr, br)


def _plan_kernel(e_ref, pos_ref, te_ref, nv_ref, rank_ref, *, t2, blk, tm):
    shift = tm.bit_length() - 1
    nblk = t2 // blk
    r = lax.broadcasted_iota(jnp.int32, (blk, blk), 0)
    c = lax.broadcasted_iota(jnp.int32, (blk, blk), 1)
    upper = (r <= c).astype(F32)
    sub = lax.broadcasted_iota(jnp.int32, (N_EXPERTS, blk), 0)
    carry = jnp.zeros((N_EXPERTS, 1), F32)
    for b in range(nblk):
        sl = slice(b * blk, (b + 1) * blk)
        oh = (sub == e_ref[:, sl]).astype(F32)
        cs = _dot(oh, upper) + carry
        rank_ref[:, sl] = jnp.sum(oh * (cs - 1.0), axis=0, keepdims=True)
        carry = cs[:, blk - 1:blk]
    counts = carry.astype(jnp.int32)
    padc = ((counts + (tm - 1)) >> shift) << shift
    sub1 = lax.broadcasted_iota(jnp.int32, (N_EXPERTS, 1), 0)
    off = jnp.zeros((N_EXPERTS, 1), jnp.int32)
    run = jnp.zeros((1, 1), jnp.int32)
    for e in range(N_EXPERTS):
        off = jnp.where(sub1 == e, run, off)
        run = run + padc[e:e + 1, :]
    ends = off + padc
    for b in range(nblk):
        sl = slice(b * blk, (b + 1) * blk)
        offv = jnp.sum(jnp.where(sub == e_ref[:, sl], off, 0), axis=0, keepdims=True)
        pos_ref[:, sl] = rank_ref[:, sl].astype(jnp.int32) + offv
    lane = lax.broadcasted_iota(jnp.int32, (N_EXPERTS, LANES), 1)
    lane1 = lax.broadcasted_iota(jnp.int32, (1, LANES), 1)
    nv = run >> shift
    te = jnp.sum((ends <= lane * tm).astype(jnp.int32), axis=0, keepdims=True)
    te_last = jnp.sum((ends <= (nv - 1) * tm).astype(jnp.int32), axis=0, keepdims=True)
    te_ref[...] = jnp.minimum(jnp.where(lane1 >= nv, te_last, te), N_EXPERTS - 1)
    nv_ref[...] = jnp.broadcast_to(nv, (1, LANES))


def _plan(e_flat, tm):
    t2 = e_flat.shape[1]
    blk = min(t2, 512)
    return pl.pallas_call(
        functools.partial(_plan_kernel, t2=t2, blk=blk, tm=tm),
        out_shape=[jax.ShapeDtypeStruct((1, t2), jnp.int32), jax.ShapeDtypeStruct((1, LANES), jnp.int32),
                   jax.ShapeDtypeStruct((1, LANES), jnp.int32)],
        scratch_shapes=[pltpu.VMEM((1, t2), F32)],
        compiler_params=pltpu.CompilerParams(vmem_limit_bytes=VMEM_LIMIT_BYTES),
    )(e_flat)


def _invert_kernel(pos_ref, te_ref, nv_ref, src_ref, *, t, tm, n_tiles):
    nv = nv_ref[0]

    def zero_tile(j, carry):
        last_of_expert = (j >= nv - 1) | (te_ref[jnp.minimum(j + 1, n_tiles - 1)] != te_ref[j])

        @pl.when(last_of_expert)
        def _():
            def zero(r, c):
                src_ref[j * tm + r] = 0
                return c
            lax.fori_loop(0, tm, zero, 0, unroll=min(tm, 32))

        return carry

    lax.fori_loop(0, n_tiles, zero_tile, 0)
    for k in range(2):
        def put(tok, carry, k=k):
            src_ref[pos_ref[k * t + tok]] = N_CHUNKS * tok
            return carry

        lax.fori_loop(0, t, put, 0, unroll=32)


def _invert(pos, te, nv, t, tm):
    n_tiles = te.shape[0]
    return pl.pallas_call(
        functools.partial(_invert_kernel, t=t, tm=tm, n_tiles=n_tiles),
        grid_spec=pltpu.PrefetchScalarGridSpec(
            num_scalar_prefetch=3,
            grid=(1,),
            in_specs=[],
            out_specs=pl.BlockSpec(memory_space=pltpu.SMEM),
        ),
        out_shape=jax.ShapeDtypeStruct((n_tiles * tm,), jnp.int32),
        compiler_params=_params(1),
    )(pos, te, nv)


def _moe_ffn_kernel(te_ref, nv_ref, src_ref, x_ref, wg_hbm, wu_hbm, wd_hbm, y_ref,
                    xbuf_ref, wg_ref, wu_ref, wd_ref, wslot_ref, sems, wsems, *, tm, layer, prec):
    i = pl.program_id(0)
    nv = nv_ref[0]
    slot = i % 2
    expert = te_ref[i]

    def start_tile(tile, dst_slot):
        base = tile * tm
        for r in range(tm):
            src = pl.multiple_of(src_ref[base + r], N_CHUNKS)
            pltpu.make_async_copy(x_ref.at[pl.ds(src, N_CHUNKS)],
                                  xbuf_ref.at[dst_slot, pl.ds(r * ROW_PITCH, N_CHUNKS)],
                                  sems.at[dst_slot]).start(priority=r % 2)

    def wait_tile():
        done = xbuf_ref.at[slot, pl.ds(0, tm * N_CHUNKS)]
        pltpu.make_async_copy(done, done, sems.at[slot]).wait()

    def weight_copies(e, ws):
        return (pltpu.make_async_copy(wg_hbm.at[layer, e], wg_ref.at[ws], wsems.at[ws]),
                pltpu.make_async_copy(wu_hbm.at[layer, e], wu_ref.at[ws], wsems.at[ws]),
                pltpu.make_async_copy(wd_hbm.at[layer, e], wd_ref.at[ws], wsems.at[ws]))

    @pl.when(i == 0)
    def _():
        start_tile(0, 0)
        wslot_ref[0] = 1
        for cp in weight_copies(expert, 0):
            cp.start()

    is_first = (i < nv) & ((i == 0) | (te_ref[jnp.maximum(i - 1, 0)] != expert))

    @pl.when(is_first)
    def _():
        ws = 1 - wslot_ref[0]
        wslot_ref[0] = ws
        for cp in weight_copies(expert, ws):
            cp.wait()
        nxt = lax.while_loop(lambda j: (j < nv) & (te_ref[jnp.minimum(j, nv - 1)] == expert), lambda j: j + 1, i + 1)

        @pl.when(nxt < nv)
        def _():
            for cp in weight_copies(te_ref[nxt], 1 - ws):
                cp.start()

    def ffn():
        ws = wslot_ref[0]
        x = _load_gathered_rows(xbuf_ref.at[slot], tm)
        hdn = jax.nn.silu(_dot(x, wg_ref[ws], prec)) * _dot(x, wu_ref[ws], prec)
        _store_chunk_rows(y_ref, _dot(hdn, wd_ref[ws], prec))

    @pl.when(i + 1 < nv)
    def _():
        wait_tile()
        start_tile(i + 1, 1 - slot)
        ffn()

    @pl.when(i + 1 == nv)
    def _():
        wait_tile()
        ffn()

    @pl.when(i >= nv)
    def _():
        y_ref[...] = jnp.zeros_like(y_ref)


def _moe_ffn(te, nv, src, x2, w_gate, w_up, w_down, layer, tm, prec):
    n_rows = src.shape[0]
    n_tiles = n_rows // tm
    any_spec = pl.BlockSpec(memory_space=pl.ANY)
    return pl.pallas_call(
        functools.partial(_moe_ffn_kernel, tm=tm, layer=layer, prec=prec),
        grid_spec=pltpu.PrefetchScalarGridSpec(
            num_scalar_prefetch=3,
            grid=(n_tiles,),
            in_specs=[any_spec, any_spec, any_spec, any_spec],
            out_specs=pl.BlockSpec((tm * N_CHUNKS, LANES), lambda i, te, nv, src: (i, 0)),
            scratch_shapes=[pltpu.VMEM((2, tm * ROW_PITCH, LANES), F32),
                            pltpu.VMEM((2, D_MODEL, D_EXPERT), F32), pltpu.VMEM((2, D_MODEL, D_EXPERT), F32),
                            pltpu.VMEM((2, D_EXPERT, D_MODEL), F32), pltpu.SMEM((1,), jnp.int32),
                            pltpu.SemaphoreType.DMA((2,)), pltpu.SemaphoreType.DMA((2,))],
        ),
        out_shape=jax.ShapeDtypeStruct((n_rows * N_CHUNKS, LANES), F32),
        compiler_params=_params(1),
    )(te, nv, src, x2, w_gate, w_up, w_down)


def _combine_kernel(pos_ref, h_ref, g_ref, y_ref, o_ref, y0_ref, y1_ref, sems, *, t, tm):
    base = pl.program_id(0) * tm
    for k, buf in enumerate((y0_ref, y1_ref)):
        for r in range(tm):
            src = pl.multiple_of(pos_ref[k * t + base + r] * N_CHUNKS, N_CHUNKS)
            pltpu.make_async_copy(y_ref.at[pl.ds(src, N_CHUNKS)], buf.at[pl.ds(r * ROW_PITCH, N_CHUNKS)],
                                  sems.at[k]).start(priority=k)
    for k, buf in enumerate((y0_ref, y1_ref)):
        done = buf.at[pl.ds(0, tm * N_CHUNKS)]
        pltpu.make_async_copy(done, done, sems.at[k]).wait()
    g = g_ref[...]
    o_ref[...] = (h_ref[...] + g[:, 0:1] * _load_gathered_rows(y0_ref, tm)
                  + g[:, 1:2] * _load_gathered_rows(y1_ref, tm))


def _combine(pos, h, gates, y):
    t = h.shape[0]
    tm = min(t, 256)
    return pl.pallas_call(
        functools.partial(_combine_kernel, t=t, tm=tm),
        grid_spec=pltpu.PrefetchScalarGridSpec(
            num_scalar_prefetch=1,
            grid=(t // tm,),
            in_specs=[pl.BlockSpec((tm, D_MODEL), lambda i, pos: (i, 0)),
                      pl.BlockSpec((tm, LANES), lambda i, pos: (i, 0)),
                      pl.BlockSpec(memory_space=pl.ANY)],
            out_specs=pl.BlockSpec((tm, D_MODEL), lambda i, pos: (i, 0)),
            scratch_shapes=[pltpu.VMEM((tm * ROW_PITCH, LANES), F32), pltpu.VMEM((tm * ROW_PITCH, LANES), F32),
                            pltpu.SemaphoreType.DMA((2,))],
        ),
        out_shape=jax.ShapeDtypeStruct((t, D_MODEL), F32),
        compiler_params=_params(1),
    )(pos, h, gates, y)


def _hier_moe(h, gain, wr, br, w_gate, w_up, w_down, layer, prec):
    t = h.shape[0]
    tm = 256 if t >= 1024 else 16
    n_tiles = (2 * t) // tm + N_EXPERTS
    assert n_tiles <= LANES
    x2, idx, gates = _router(h, gain, wr, br, prec)
    e_flat = idx[:, :2].T.reshape(1, 2 * t)
    pos, te, nv = _plan(e_flat, tm)
    pos = pos.reshape(2 * t)
    te = te[0, :n_tiles]
    nv = nv[0, :1]
    src = _invert(pos, te, nv, t, tm)
    y = _moe_ffn(te, nv, src, x2, w_gate, w_up, w_down, layer, tm, prec)
    return _combine(pos, h, gates, y)


def _fox_cumsum_kernel(lft_ref, lf_ref, ct_ref, c_ref, *, s, blk):
    r = lax.broadcasted_iota(jnp.int32, (blk, blk), 0)
    c = lax.broadcasted_iota(jnp.int32, (blk, blk), 1)
    upper = (r <= c).astype(F32)
    lower = (r >= c).astype(F32)
    carry_t = jnp.zeros((2 * SUBLANES, 1), F32)
    carry = jnp.zeros((1, LANES), F32)
    for b in range(s // blk):
        sl = slice(b * blk, (b + 1) * blk)
        ct = _dot3_left(lft_ref[:, sl], upper) + carry_t
        ct_ref[:, sl] = ct
        carry_t = ct[:, blk - 1:blk]
        cc = _dot3_right(lower, lf_ref[sl, :]) + carry
        c_ref[sl, :] = cc
        carry = cc[blk - 1:blk, :]


def _fox_cumsum(lft, lf, nb, s):
    t = lf.shape[0]
    blk = min(s, 256)
    return pl.pallas_call(
        functools.partial(_fox_cumsum_kernel, s=s, blk=blk),
        grid=(nb,),
        in_specs=[pl.BlockSpec((2 * SUBLANES, s), lambda b: (0, b)), pl.BlockSpec((s, LANES), lambda b: (b, 0))],
        out_specs=[pl.BlockSpec((2 * SUBLANES, s), lambda b: (0, b)), pl.BlockSpec((s, LANES), lambda b: (b, 0))],
        out_shape=[jax.ShapeDtypeStruct((2 * SUBLANES, t), F32), jax.ShapeDtypeStruct((t, LANES), F32)],
        compiler_params=_params(1),
    )(lft, lf)


def _fox_prompt_kernel(q_ref, k_ref, v_ref, ct_ref, c_ref, o_ref, *, s, tq):
    h = pl.program_id(1)
    lane = lax.broadcasted_iota(jnp.int32, (s, LANES), 1)
    c_col = jnp.sum(jnp.where(lane == h, c_ref[...], 0.0), axis=1, keepdims=True) * LOG2E
    c_row = ct_ref[pl.ds(h, 1), :] * LOG2E
    ri = lax.broadcasted_iota(jnp.int32, (tq, tq), 0)
    ci = lax.broadcasted_iota(jnp.int32, (tq, tq), 1)
    causal = ci <= ri
    for qi in range(s // tq):
        qs = slice(qi * tq, (qi + 1) * tq)
        q = q_ref[qs, :].astype(F32)
        cq = c_col[qs, :]
        m = jnp.full((tq, 1), -jnp.inf, F32)
        l = jnp.zeros((tq, 1), F32)
        acc = jnp.zeros((tq, HEAD_DIM), F32)
        for kj in range(qi + 1):
            ks = slice(kj * tq, (kj + 1) * tq)
            sc = lax.dot_general(q, k_ref[ks, :], _NT, preferred_element_type=F32) - c_row[:, ks]
            if kj == qi:
                sc = jnp.where(causal, sc, -jnp.inf)
            m_new = jnp.maximum(m, jnp.max(sc, axis=-1, keepdims=True) + cq)
            alpha = jnp.exp2(m - m_new)
            p = jnp.exp2(sc - (m_new - cq))
            l = alpha * l + jnp.sum(p, axis=-1, keepdims=True)
            acc = alpha * acc + _dot(p, v_ref[ks, :])
            m = m_new
        o_ref[qs, :] = (acc / l).astype(o_ref.dtype)


def _fox_prompt(q, k, v, ct, c, nb, s):
    t = q.shape[0]
    tq = min(s, 512)
    head_spec = pl.BlockSpec((s, HEAD_DIM), lambda b, h: (b, h))
    return pl.pallas_call(
        functools.partial(_fox_prompt_kernel, s=s, tq=tq),
        grid=(nb, MAIN_HEADS),
        in_specs=[head_spec, head_spec, head_spec,
                  pl.BlockSpec((2 * SUBLANES, s), lambda b, h: (0, b)),
                  pl.BlockSpec((s, LANES), lambda b, h: (b, 0))],
        out_specs=head_spec,
        out_shape=jax.ShapeDtypeStruct((t, MAIN_WIDTH), _matmul_only_dtype(None)),
        compiler_params=_params(2),
    )(q, k, v, ct, c)


def _fox_past_bias_kernel(pt_ref, lfc_ref, o_ref, buf_ref, sem, *, n_pages):
    b = pl.program_id(0)

    def page_copy(p):
        page = pt_ref[b * n_pages + p]
        return pltpu.make_async_copy(lfc_ref.at[:, pl.ds(page, 1), :], buf_ref.at[:, pl.ds(p, 1), :], sem)

    def start(p, carry):
        page_copy(p).start()
        return carry

    def wait(p, carry):
        page_copy(p).wait()
        return carry

    lax.fori_loop(0, n_pages, start, 0)
    r = lax.broadcasted_iota(jnp.int32, (PAGE_SIZE, PAGE_SIZE), 0)
    c = lax.broadcasted_iota(jnp.int32, (PAGE_SIZE, PAGE_SIZE), 1)
    after_in_page = (r > c).astype(F32)
    pr = lax.broadcasted_iota(jnp.int32, (n_pages, n_pages), 0)
    pc = lax.broadcasted_iota(jnp.int32, (n_pages, n_pages), 1)
    later_pages = (pc > pr).astype(F32)
    lax.fori_loop(0, n_pages, wait, 0)
    for h in range(MAIN_HEADS):
        lp = buf_ref[h]
        tot = jnp.broadcast_to(jnp.sum(lp, axis=1, keepdims=True), lp.shape)
        o_ref[h] = _dot3_left(lp, after_in_page) + _dot3_right(later_pages, tot)


def _fox_past_bias(page_table_flat, cache_logf_t, nb, n_pages):
    return pl.pallas_call(
        functools.partial(_fox_past_bias_kernel, n_pages=n_pages),
        grid_spec=pltpu.PrefetchScalarGridSpec(
            num_scalar_prefetch=1,
            grid=(nb,),
            in_specs=[pl.BlockSpec(memory_space=pl.ANY)],
            out_specs=pl.BlockSpec((None, MAIN_HEADS, n_pages, PAGE_SIZE), lambda b, pt: (b, 0, 0, 0)),
            scratch_shapes=[pltpu.VMEM((MAIN_HEADS, n_pages, PAGE_SIZE), F32), pltpu.SemaphoreType.DMA(())],
        ),
        out_shape=jax.ShapeDtypeStruct((nb, MAIN_HEADS, n_pages, PAGE_SIZE), F32),
        compiler_params=_params(1),
    )(page_table_flat, cache_logf_t)


def _fox_sample_kernel(*refs, ppb, n_steps, valid):
    pt_ref, q_ref, kn_ref, vn_ref, lf_ref, bias_ref = refs[:6]
    k_refs = refs[6:6 + ppb]
    v_refs = refs[6 + ppb:6 + 2 * ppb]
    o_ref, m_ref, l_ref, acc_ref = refs[6 + 2 * ppb:]
    g = pl.program_id(1)
    rows = SAMPLE_ROWS

    @pl.when(g == 0)
    def _():
        m_ref[...] = jnp.full(m_ref.shape, -jnp.inf, F32)
        l_ref[...] = jnp.zeros(l_ref.shape, F32)
        acc_ref[...] = jnp.zeros(acc_ref.shape, F32)

    r8 = lax.broadcasted_iota(jnp.int32, (rows, rows), 0)
    c8 = lax.broadcasted_iota(jnp.int32, (rows, rows), 1)
    new_mask = (c8 <= r8) & (c8 < valid)
    c_new = _dot3_right(new_mask.astype(F32), lf_ref[...])

    def qk3(q_hi, q_lo, k):
        a = _dot_nt(jnp.concatenate([q_hi, q_lo], axis=0), k)
        return a[:rows] + a[rows:]

    def pv3(p, v):
        p_hi, p_lo = _split_hi_lo(p)
        a = _dot(jnp.concatenate([p_hi, p_lo], axis=0), v)
        return a[:rows] + a[rows:]

    all_scores = []
    for h in range(MAIN_HEADS):
        q_hi, q_lo = _split_hi_lo(q_ref[:, h * HEAD_DIM:(h + 1) * HEAD_DIM])
        cq = c_new[:, h:h + 1]
        all_scores.append([qk3(q_hi, q_lo, k_refs[u][h]) + cq + bias_ref[h, u:u + 1, :]
                           for u in range(ppb)])
    all_probs, all_alpha = [], []
    for h in range(MAIN_HEADS):
        m_old = m_ref[h, :, 0:1]
        m_new = m_old
        for sc in all_scores[h]:
            m_new = jnp.maximum(m_new, jnp.max(sc, axis=-1, keepdims=True))
        alpha = jnp.exp(m_old - m_new)
        probs = [jnp.exp(sc - m_new) for sc in all_scores[h]]
        l_new = alpha * l_ref[h, :, 0:1]
        for p in probs:
            l_new = l_new + jnp.sum(p, axis=-1, keepdims=True)
        m_ref[h] = jnp.broadcast_to(m_new, (rows, LANES))
        l_ref[h] = jnp.broadcast_to(l_new, (rows, LANES))
        all_probs.append(probs)
        all_alpha.append(alpha)
    for h in range(MAIN_HEADS):
        sl = slice(h * HEAD_DIM, (h + 1) * HEAD_DIM)
        acc = all_alpha[h] * acc_ref[:, sl]
        for u in range(ppb):
            acc = acc + pv3(all_probs[h][u], v_refs[u][h])
        acc_ref[:, sl] = acc

    @pl.when(g == n_steps - 1)
    def _():
        for h in range(MAIN_HEADS):
            sl = slice(h * HEAD_DIM, (h + 1) * HEAD_DIM)
            qh = q_ref[:, sl]
            cq = c_new[:, h:h + 1]
            cq_row = jnp.sum(jnp.where(r8 == c8, jnp.broadcast_to(cq, (rows, rows)), 0.0), axis=0, keepdims=True)
            sn = _dot_nt(qh, kn_ref[:, sl], HIGHEST) + cq - cq_row
            sn = jnp.where(new_mask, sn, -jnp.inf)
            m_old = m_ref[h, :, 0:1]
            m_new = jnp.maximum(m_old, jnp.max(sn, axis=-1, keepdims=True))
            alpha = jnp.exp(m_old - m_new)
            p = jnp.exp(sn - m_new)
            l_new = alpha * l_ref[h, :, 0:1] + jnp.sum(p, axis=-1, keepdims=True)
            acc = alpha * acc_ref[:, sl] + _dot(p, vn_ref[:, sl], HIGHEST)
            o_ref[:, sl] = acc / l_new


def _fox_sample(page_table_flat, q, k_new, v_new, lf, bias, cache_k_t, cache_v_t, nb, n_pages, valid):
    ppb = 8 if n_pages % 8 == 0 else n_pages
    n_steps = n_pages // ppb
    row_spec = pl.BlockSpec((SAMPLE_ROWS, MAIN_WIDTH), lambda b, g, pt: (b, 0))

    def page_spec(u):
        return pl.BlockSpec((None, MAIN_HEADS, PAGE_SIZE, HEAD_DIM),
                            lambda b, g, pt, u=u: (pt[b * n_pages + g * ppb + u], 0, 0, 0))

    in_specs = [row_spec, row_spec, row_spec,
                pl.BlockSpec((SAMPLE_ROWS, LANES), lambda b, g, pt: (b, 0)),
                pl.BlockSpec((None, MAIN_HEADS, ppb, PAGE_SIZE), lambda b, g, pt: (b, 0, g, 0))]
    in_specs += [page_spec(u) for u in range(ppb)] + [page_spec(u) for u in range(ppb)]
    return pl.pallas_call(
        functools.partial(_fox_sample_kernel, ppb=ppb, n_steps=n_steps, valid=valid),
        grid_spec=pltpu.PrefetchScalarGridSpec(
            num_scalar_prefetch=1,
            grid=(nb, n_steps),
            in_specs=in_specs,
            out_specs=row_spec,
            scratch_shapes=[pltpu.VMEM((MAIN_HEADS, SAMPLE_ROWS, LANES), F32),
                            pltpu.VMEM((MAIN_HEADS, SAMPLE_ROWS, LANES), F32),
                            pltpu.VMEM((SAMPLE_ROWS, MAIN_WIDTH), F32)],
        ),
        out_shape=jax.ShapeDtypeStruct(q.shape, F32),
        compiler_params=_params(2),
    )(page_table_flat, q, k_new, v_new, lf, bias, *([cache_k_t] * ppb), *([cache_v_t] * ppb))


def _trunk(x, nb, rows, valid, mem_k, mem_v, fox, w, prec):
    u, v, qm = _norm_matmul(x, w["norm1_gain"][0:1], w["w_in_a"], 0,
                            [(MAIN_WIDTH, "gelu_mm", 0), (MAIN_WIDTH, "gelu", 0), (MEM_WIDTH, "headnorm_mm", 0)],
                            w["mem_q_gain"][0:1], prec=prec)
    y_main, v_rows = _gmlp(u, v, w["v_gain_a"][0:1], w["w_mix"], w["b_mix"], w["gmlp_rows"], w["gmlp_grp"], valid,
                           w["gmlp_blocks_per_seq"], prec)
    y_mem = _mem_attn(qm, mem_k, mem_v, 0, nb, rows, prec)
    h = _out_proj(y_main, y_mem, w["w_out"], 0, x, prec)
    h = _hier_moe(h, w["norm2_gain"][0:1], w["w_router"][0], w["b_router"][0], w["w_gate"], w["w_up"], w["w_down"],
                  0, prec)
    k, vv, lf, lft = _norm_matmul(h, w["kv_norm_gain"], w["w_kv_shared"], 0,
                                  [(MAIN_WIDTH, "headnorm", 0), (MAIN_WIDTH, "plain", 0)],
                                  w["k_gain_shared"], forget=w["forget"], prec=prec)
    q, qm = _norm_matmul(h, w["norm1_gain"][1:2], w["w_in_b"], 0,
                         [(MAIN_WIDTH, "headnorm_mm", 0), (MEM_WIDTH, "headnorm_mm", 1)],
                         jnp.concatenate([w["q_gain_b"][0:1] * w["q_logit_scale"], w["mem_q_gain"][1:2]], axis=0),
                         prec=prec)
    y_main = fox(q, k, vv, lf, lft)
    y_mem = _mem_attn(qm, mem_k, mem_v, 1, nb, rows, prec)
    h = _out_proj(y_main, y_mem, w["w_out"], 1, h, prec)
    h = _hier_moe(h, w["norm2_gain"][1:2], w["w_router"][1], w["b_router"][1], w["w_gate"], w["w_up"], w["w_down"],
                  1, prec)
    return h, k, vv, lf, v_rows


def kernel(x_prompt, x_sample, cache_k, cache_v, cache_logf, cache_mem_k, cache_mem_v, page_table, mem_prompt, norm1_gain, norm2_gain, w_in_a, v_gain_a, w_s_a, b_s_a, w_in_b, q_gain_b, kv_norm_gain, w_kv_shared, b_forget, k_gain_shared, mem_norm_gain, w_mem_kv, mem_q_gain, mem_k_gain, w_out, w_router_group, b_router_group, w_router_expert, b_router_expert, w_gate, w_up, w_down):
    batch, seq, _ = x_prompt.shape
    dec_batch, dec_seq, _ = x_sample.shape
    n_phys = cache_k.shape[0]
    n_pages = page_table.shape[1]
    depth = norm1_gain.shape[0]
    assert depth == 2 and dec_seq <= SAMPLE_ROWS and seq % CHUNK == 0

    n_route = N_GROUPS + N_EXPERTS
    w_router = jnp.pad(jnp.concatenate([w_router_group, w_router_expert], axis=-1),
                       ((0, 0), (0, 0), (0, LANES - n_route)))
    b_router = jnp.pad(jnp.concatenate([b_router_group, b_router_expert], axis=-1),
                       ((0, 0), (0, LANES - n_route)))[:, None, :]
    w_f = w_kv_shared[:, 2 * MAIN_WIDTH:]
    forget = (jnp.pad(w_f, ((0, 0), (0, LANES - MAIN_HEADS))),
              jnp.pad(b_forget, (0, LANES - MAIN_HEADS))[None, :],
              jnp.pad(b_forget, (0, 2 * SUBLANES - MAIN_HEADS))[:, None])
    common = dict(norm1_gain=norm1_gain, norm2_gain=norm2_gain, w_in_a=w_in_a, v_gain_a=v_gain_a, w_in_b=w_in_b,
                  q_gain_b=q_gain_b, kv_norm_gain=kv_norm_gain[None, :], w_kv_shared=w_kv_shared,
                  k_gain_shared=k_gain_shared[None, :], mem_q_gain=mem_q_gain, w_out=w_out,
                  w_router=w_router, b_router=b_router, w_gate=w_gate, w_up=w_up, w_down=w_down, forget=forget)

    t_p = batch * seq
    mem_tok = mem_prompt.reshape(batch * N_MEM, D_MODEL)
    mem_k_layers, mem_v_layers = [], []
    for l in range(depth):
        mk, mv = _norm_matmul(mem_tok, mem_norm_gain[l:l + 1], w_mem_kv, l,
                              [(MEM_WIDTH, "headnorm", 0), (MEM_WIDTH, "plain", 0)], mem_k_gain[l:l + 1])
        mem_k_layers.append(mk.reshape(batch, N_MEM, MEM_WIDTH))
        mem_v_layers.append(mv.reshape(batch, N_MEM, MEM_WIDTH))
    mem_k_p = jnp.stack(mem_k_layers)
    mem_v_p = jnp.stack(mem_v_layers)

    def fox_prompt(q, k, v, lf, lft):
        ct, c = _fox_cumsum(lft, lf, batch, seq)
        return _fox_prompt(q, k, v, ct, c, batch, seq)

    w_p = dict(common, w_mix=w_s_a[0], b_mix=b_s_a[0].T, gmlp_rows=CHUNK, gmlp_grp=CHUNK,
               gmlp_blocks_per_seq=seq // CHUNK, q_logit_scale=ATTN_SCALE * LOG2E)
    y_p, k_p, v_p, lf_p, vrows_p = _trunk(x_prompt.reshape(t_p, D_MODEL), batch, seq, seq,
                                          mem_k_p, mem_v_p, fox_prompt, w_p, None)

    rows = SAMPLE_ROWS
    t_s = dec_batch * rows
    x_s = jnp.pad(x_sample, ((0, 0), (0, rows - dec_seq), (0, 0))).reshape(t_s, D_MODEL)
    pt_flat = page_table.reshape(dec_batch * n_pages)
    bias = _fox_past_bias(pt_flat, jnp.transpose(cache_logf, (2, 0, 1)), dec_batch, n_pages)
    cache_k_t = jnp.transpose(cache_k, (0, 2, 1, 3))
    cache_v_t = jnp.transpose(cache_v, (0, 2, 1, 3))

    def fox_sample(q, k, v, lf, lft):
        return _fox_sample(pt_flat, q, k, v, lf, bias, cache_k_t, cache_v_t, dec_batch, n_pages, dec_seq)

    w_s = dict(common, w_mix=jnp.tile(w_s_a[0][:, :rows, :rows], (1, dec_batch, dec_batch)),
               b_mix=jnp.tile(b_s_a[0][:, :rows].T, (dec_batch, 1)), gmlp_rows=t_s, gmlp_grp=rows,
               gmlp_blocks_per_seq=1, q_logit_scale=ATTN_SCALE)
    mem_k_s = cache_mem_k.reshape(depth, dec_batch, N_MEM, MEM_WIDTH)
    mem_v_s = cache_mem_v.reshape(depth, dec_batch, N_MEM, MEM_WIDTH)
    y_s, k_s, v_s, lf_s, vrows_s = _trunk(x_s, dec_batch, rows, dec_seq, mem_k_s, mem_v_s, fox_sample, w_s, SPLIT3)

    def unpad(a, *tail):
        return a.reshape((dec_batch, rows) + tail)[:, :dec_seq]

    return (y_p.reshape(batch, seq, D_MODEL),
            unpad(y_s, D_MODEL),
            k_p.reshape(batch, seq, MAIN_HEADS, HEAD_DIM),
            v_p.reshape(batch, seq, MAIN_HEADS, HEAD_DIM),
            lf_p[:, :MAIN_HEADS].reshape(batch, seq, MAIN_HEADS),
            unpad(k_s, MAIN_HEADS, HEAD_DIM),
            unpad(v_s, MAIN_HEADS, HEAD_DIM),
            unpad(lf_s[:, :MAIN_HEADS], MAIN_HEADS),
            mem_k_p.reshape(depth, batch, N_MEM, MEM_HEADS, HEAD_DIM),
            mem_v_p.reshape(depth, batch, N_MEM, MEM_HEADS, HEAD_DIM),
            vrows_p.reshape(batch, CHUNK, MAIN_WIDTH)[None],
            unpad(vrows_s, MAIN_WIDTH)[None])
```

```python
import functools

import jax
import jax.numpy as jnp
from jax import lax
from jax.experimental import pallas as pl
from jax.experimental.pallas import tpu as pltpu

D_MODEL = 2048
HEAD_DIM = 128
MAIN_HEADS = 12
MEM_HEADS = 4
MAIN_WIDTH = MAIN_HEADS * HEAD_DIM
MEM_WIDTH = MEM_HEADS * HEAD_DIM
N_MEM = 256
N_GROUPS = 4
EXPERTS_PER_GROUP = 4
N_EXPERTS = N_GROUPS * EXPERTS_PER_GROUP
D_EXPERT = 512
CHUNK = 128
PAGE_SIZE = 128
EPS = 1e-6
ATTN_SCALE = HEAD_DIM ** -0.5
LOG2E = 1.4426950408889634

LANES = 128
SUBLANES = 8
VMEM_LIMIT_BYTES = 56 * 1024 * 1024
SAMPLE_ROWS = SUBLANES
N_CHUNKS = D_MODEL // LANES
ROW_PITCH = 24

PROJ_ROW_TILE = 1024
PROJ_ROW_TILE_BF16 = 2048
PROJ_COL_TILE = 512
ROUTER_ROW_TILE = 512
MOE_ROW_TILE = 256
MOE_ROW_TILE_SMALL = 16
MOE_SMALL_BELOW = 1024
COMBINE_ROW_TILE = 256
ATTN_TILE = 512
MEM_ATTN_ROW_TILE = 1024
CUMSUM_BLOCK = 256
PLAN_BLOCK = 512
PAGES_PER_STEP = SUBLANES
SCALAR_UNROLL = 32

F32 = jnp.float32
_NT = (((1,), (1,)), ((), ()))


def _params(n_axes):
    return pltpu.CompilerParams(dimension_semantics=("arbitrary",) * n_axes,
                                vmem_limit_bytes=VMEM_LIMIT_BYTES)


HIGHEST = lax.Precision.HIGHEST


SPLIT3 = "split3"


def _dot(a, b, prec=None):
    if prec == SPLIT3:
        return _dot_split3(a, b, _dot)
    return jnp.dot(a, b, preferred_element_type=F32, precision=prec)


def _dot_nt(a, b, prec=None):
    if prec == SPLIT3:
        return _dot_split3(a, b, _dot_nt)
    return lax.dot_general(a, b, _NT, preferred_element_type=F32, precision=prec)


def _dot_tn(a, b, prec=None):
    dims = (((0,), (1,)), ((), ()))
    if prec == SPLIT3:
        a_hi, a_lo = _split_hi_lo(a)
        b_hi, b_lo = _split_hi_lo(b)
        return _dot_tn(a_hi, b_hi) + _dot_tn(a_lo, b_hi) + _dot_tn(a_hi, b_lo)
    return lax.dot_general(a, b, dims, preferred_element_type=F32, precision=prec)


def _dot_split3(a, b, dot):
    a_hi, a_lo = _split_hi_lo(a)
    b_hi, b_lo = _split_hi_lo(b)
    m = a.shape[0]
    s = dot(jnp.concatenate([a_hi, a_lo], axis=0), b_hi)
    return s[:m] + s[m:] + dot(a_hi, b_lo)


def _matmul_only_dtype(prec):
    return jnp.bfloat16 if prec is None else F32


def _split_hi_lo(x):
    hi = x.astype(jnp.bfloat16).astype(F32)
    return hi, x - hi


def _split3(x):
    hi = x.astype(jnp.bfloat16).astype(F32)
    r = x - hi
    mid = r.astype(jnp.bfloat16).astype(F32)
    lo = (r - mid).astype(jnp.bfloat16).astype(F32)
    return hi, mid, lo


def _dot3_right(a_exact, x):
    hi, mid, lo = _split3(x)
    return _dot(a_exact, hi) + _dot(a_exact, mid) + _dot(a_exact, lo)


def _dot3_left(x, b_exact):
    hi, mid, lo = _split3(x)
    return _dot(hi, b_exact) + _dot(mid, b_exact) + _dot(lo, b_exact)


def _store_chunk_rows(ref, x):
    rows = x.shape[0]
    for c in range(N_CHUNKS):
        ref[pl.ds(c, rows, stride=N_CHUNKS), :] = x[:, c * LANES:(c + 1) * LANES]


def _load_gathered_rows(ref, rows):
    return jnp.concatenate([ref[pl.ds(c, rows, stride=ROW_PITCH), :] for c in range(N_CHUNKS)], axis=1)


def _head_norm(z, gain_row):
    return z * lax.rsqrt(jnp.mean(z * z, axis=-1, keepdims=True) + EPS) * gain_row


def _norm_matmul_kernel(*refs, segs, tn, with_forget, prec):
    it = iter(refs)
    x_ref, g_ref, w_ref, hg_ref = next(it), next(it), next(it), next(it)
    if with_forget:
        wf_ref, bfr_ref, bfc_ref = next(it), next(it), next(it)
    out_refs = [next(it) for _ in segs]
    if with_forget:
        lf_ref, lft_ref = next(it), next(it)
    xn_ref = next(it)
    j = pl.program_id(1)

    def project(xn, o_ref, kind, grow):
        z = _dot(xn, w_ref[...], prec)
        if kind == "plain":
            o_ref[...] = z
        elif kind in ("gelu", "gelu_mm"):
            o_ref[...] = jax.nn.gelu(z).astype(o_ref.dtype)
        else:
            gain = hg_ref[grow:grow + 1, :]
            for c in range(tn // HEAD_DIM):
                sl = slice(c * HEAD_DIM, (c + 1) * HEAD_DIM)
                o_ref[:, sl] = _head_norm(z[:, sl], gain).astype(o_ref.dtype)

    @pl.when(j == 0)
    def _first():
        x = x_ref[...]
        xn = x * lax.rsqrt(jnp.mean(x * x, axis=-1, keepdims=True) + EPS) * g_ref[...]
        xn_ref[...] = xn
        if with_forget:
            lf_ref[...] = jax.nn.log_sigmoid(_dot(xn, wf_ref[...], prec) + bfr_ref[...])
            zt = _dot_tn(wf_ref[...], xn, prec)[:2 * SUBLANES, :]
            lft_ref[...] = jax.nn.log_sigmoid(zt + bfc_ref[...])
        project(xn, out_refs[0], segs[0][1], segs[0][2])

    lo = 0
    for (ncols, kind, grow), o_ref in zip(segs, out_refs):
        nt = ncols // tn

        @pl.when((j >= max(lo, 1)) & (j < lo + nt))
        def _rest(o_ref=o_ref, kind=kind, grow=grow):
            project(xn_ref[...], o_ref, kind, grow)

        lo += nt


def _norm_matmul(x, gain, w, layer, segs, head_gains, forget=None, prec=None):
    t = x.shape[0]
    tm = min(t, PROJ_ROW_TILE)
    tn = PROJ_COL_TILE
    n_tiles = sum(s[0] for s in segs) // tn
    with_forget = forget is not None
    in_specs = [
        pl.BlockSpec((tm, D_MODEL), lambda i, j: (i, 0)),
        pl.BlockSpec((1, D_MODEL), lambda i, j: (0, 0)),
        pl.BlockSpec((None, D_MODEL, tn), lambda i, j: (layer, 0, j)) if w.ndim == 3
        else pl.BlockSpec((D_MODEL, tn), lambda i, j: (0, j)),
        pl.BlockSpec(head_gains.shape, lambda i, j: (0, 0)),
    ]
    args = [x, gain, w, head_gains]
    if with_forget:
        wf, bfr, bfc = forget
        in_specs += [pl.BlockSpec(wf.shape, lambda i, j: (0, 0)),
                     pl.BlockSpec(bfr.shape, lambda i, j: (0, 0)), pl.BlockSpec(bfc.shape, lambda i, j: (0, 0))]
        args += [wf, bfr, bfc]
    out_shape, out_specs = [], []
    lo = 0
    for ncols, kind, _ in segs:
        nt = ncols // tn
        out_shape.append(jax.ShapeDtypeStruct((t, ncols), _matmul_only_dtype(prec) if kind.endswith("_mm") else F32))
        out_specs.append(pl.BlockSpec((tm, tn), lambda i, j, lo=lo, nt=nt: (i, jnp.clip(j - lo, 0, nt - 1))))
        lo += nt
    if with_forget:
        out_shape += [jax.ShapeDtypeStruct((t, LANES), F32), jax.ShapeDtypeStruct((2 * SUBLANES, t), F32)]
        out_specs += [pl.BlockSpec((tm, LANES), lambda i, j: (i, 0)),
                      pl.BlockSpec((2 * SUBLANES, tm), lambda i, j: (0, i))]
    return pl.pallas_call(
        functools.partial(_norm_matmul_kernel, segs=tuple(segs), tn=tn, with_forget=with_forget, prec=prec),
        grid=(t // tm, n_tiles),
        in_specs=in_specs,
        out_specs=out_specs,
        out_shape=out_shape,
        scratch_shapes=[pltpu.VMEM((tm, D_MODEL), F32)],
        compiler_params=_params(2),
    )(*args)


def _gmlp_kernel(u_ref, v_ref, vg_ref, w_ref, b_ref, y_ref, vn_ref, *, rows, grp, valid, prec):
    v = v_ref[...]
    vn = v * lax.rsqrt(jnp.mean(v * v, axis=-1, keepdims=True) + EPS) * vg_ref[...]
    vn_ref[...] = vn
    r = lax.broadcasted_iota(jnp.int32, (rows, rows), 0)
    c = lax.broadcasted_iota(jnp.int32, (rows, rows), 1)
    shift = grp.bit_length() - 1
    allowed = ((c & (grp - 1)) <= (r & (grp - 1))) & ((c & (grp - 1)) < valid)
    if grp < rows:
        allowed = allowed & ((r >> shift) == (c >> shift))
    for g in range(MAIN_HEADS):
        sl = slice(g * HEAD_DIM, (g + 1) * HEAD_DIM)
        w = jnp.where(allowed, w_ref[g], 0.0)
        mixed = _dot(w, vn[:, sl], prec) + b_ref[:, g:g + 1]
        y_ref[:, sl] = (u_ref[:, sl] * mixed).astype(y_ref.dtype)


def _gmlp(u, v, v_gain, w_mix, b_mix, rows, grp, valid, blocks_per_seq, prec):
    t = u.shape[0]
    return pl.pallas_call(
        functools.partial(_gmlp_kernel, rows=rows, grp=grp, valid=valid, prec=prec),
        grid=(t // rows,),
        in_specs=[
            pl.BlockSpec((rows, MAIN_WIDTH), lambda i: (i, 0)),
            pl.BlockSpec((rows, MAIN_WIDTH), lambda i: (i, 0)),
            pl.BlockSpec((1, MAIN_WIDTH), lambda i: (0, 0)),
            pl.BlockSpec((MAIN_HEADS, rows, rows), lambda i: (0, 0, 0)),
            pl.BlockSpec((rows, MAIN_HEADS), lambda i: (0, 0)),
        ],
        out_specs=[pl.BlockSpec((rows, MAIN_WIDTH), lambda i: (i, 0)),
                   pl.BlockSpec((rows, MAIN_WIDTH), lambda i: (i // blocks_per_seq, 0))],
        out_shape=[jax.ShapeDtypeStruct((t, MAIN_WIDTH), _matmul_only_dtype(prec)),
                   jax.ShapeDtypeStruct((t // blocks_per_seq, MAIN_WIDTH), F32)],
        compiler_params=_params(1),
    )(u, v, v_gain, w_mix, b_mix)


def _mem_attn_kernel(q_ref, k_ref, v_ref, o_ref, *, prec):
    for h in range(MEM_HEADS):
        sl = slice(h * HEAD_DIM, (h + 1) * HEAD_DIM)
        s = _dot_nt(q_ref[:, sl].astype(F32), k_ref[:, sl], prec) * ATTN_SCALE
        e = jnp.exp(s - jnp.max(s, axis=-1, keepdims=True))
        p = e / jnp.sum(e, axis=-1, keepdims=True)
        o_ref[:, sl] = _dot(p, v_ref[:, sl], prec).astype(o_ref.dtype)


def _mem_attn(q, k, v, layer, nb, rows_per_batch, prec):
    tq = min(rows_per_batch, MEM_ATTN_ROW_TILE)
    nq = rows_per_batch // tq
    kv_spec = pl.BlockSpec((None, None, N_MEM, MEM_WIDTH), lambda b, i: (layer, b, 0, 0))
    return pl.pallas_call(
        functools.partial(_mem_attn_kernel, prec=prec),
        grid=(nb, nq),
        in_specs=[pl.BlockSpec((tq, MEM_WIDTH), lambda b, i: (b * nq + i, 0)), kv_spec, kv_spec],
        out_specs=pl.BlockSpec((tq, MEM_WIDTH), lambda b, i: (b * nq + i, 0)),
        out_shape=jax.ShapeDtypeStruct(q.shape, _matmul_only_dtype(prec)),
        compiler_params=_params(2),
    )(q, k, v)


def _out_proj_kernel(ym_ref, ymem_ref, w1_ref, w2_ref, h_ref, o_ref, *, prec):
    w1 = w1_ref[...].astype(ym_ref.dtype)
    w2 = w2_ref[...].astype(ymem_ref.dtype)
    o_ref[...] = h_ref[...] + _dot(ym_ref[...], w1, prec) + _dot(ymem_ref[...], w2, prec)


def _out_proj(y_main, y_mem, w_out, layer, h, prec):
    t = h.shape[0]
    tm = min(t, PROJ_ROW_TILE_BF16 if y_main.dtype == jnp.bfloat16 else PROJ_ROW_TILE)
    tn = PROJ_COL_TILE
    return pl.pallas_call(
        functools.partial(_out_proj_kernel, prec=prec),
        grid=(t // tm, D_MODEL // tn),
        in_specs=[
            pl.BlockSpec((tm, MAIN_WIDTH), lambda i, j: (i, 0)),
            pl.BlockSpec((tm, MEM_WIDTH), lambda i, j: (i, 0)),
            pl.BlockSpec((None, MAIN_WIDTH, tn), lambda i, j: (layer, 0, j)),
            pl.BlockSpec((None, MEM_WIDTH, tn), lambda i, j: (layer, MAIN_WIDTH // MEM_WIDTH, j)),
            pl.BlockSpec((tm, tn), lambda i, j: (i, j)),
        ],
        out_specs=pl.BlockSpec((tm, tn), lambda i, j: (i, j)),
        out_shape=jax.ShapeDtypeStruct((t, D_MODEL), F32),
        compiler_params=_params(2),
    )(y_main, y_mem, w_out, w_out, h)


def _router_kernel(h_ref, g_ref, wr_ref, br_ref, x2_ref, idx_ref, gate_ref, *, prec):
    x = h_ref[...]
    xn = x * lax.rsqrt(jnp.mean(x * x, axis=-1, keepdims=True) + EPS) * g_ref[...]
    _store_chunk_rows(x2_ref, xn)
    lg = _dot(xn, wr_ref[...], prec) + br_ref[...]
    lane = lax.broadcasted_iota(jnp.int32, lg.shape, 1)
    neg = -jnp.inf
    is_grp = lane < N_GROUPS
    gl = jnp.where(is_grp, lg, neg)
    ge = jnp.where(is_grp, jnp.exp(gl - jnp.max(gl, axis=-1, keepdims=True)), 0.0)
    pg = ge / jnp.sum(ge, axis=-1, keepdims=True)
    p_top = jnp.max(pg, axis=-1, keepdims=True)
    g_idx = jnp.min(jnp.where(is_grp & (pg == p_top), lane, LANES), axis=-1, keepdims=True)
    first = N_GROUPS + EXPERTS_PER_GROUP * g_idx
    in_grp = (lane >= first) & (lane < first + EXPERTS_PER_GROUP)
    e1 = jnp.max(jnp.where(in_grp, lg, neg), axis=-1, keepdims=True)
    i1 = jnp.min(jnp.where(in_grp & (lg == e1), lane, LANES), axis=-1, keepdims=True)
    rest = in_grp & (lane != i1)
    e2 = jnp.max(jnp.where(rest, lg, neg), axis=-1, keepdims=True)
    i2 = jnp.min(jnp.where(rest & (lg == e2), lane, LANES), axis=-1, keepdims=True)
    t2 = jnp.exp(e2 - e1)
    den = 1.0 + t2
    idx_ref[...] = jnp.where(lane == 0, i1 - N_GROUPS, jnp.where(lane == 1, i2 - N_GROUPS, 0))
    gate_ref[...] = jnp.where(lane == 0, p_top * (1.0 / den), jnp.where(lane == 1, p_top * (t2 / den), 0.0))


def _router(h, gain, wr, br, prec):
    t = h.shape[0]
    tm = min(t, ROUTER_ROW_TILE)
    return pl.pallas_call(
        functools.partial(_router_kernel, prec=prec),
        grid=(t // tm,),
        in_specs=[pl.BlockSpec((tm, D_MODEL), lambda i: (i, 0)), pl.BlockSpec((1, D_MODEL), lambda i: (0, 0)),
                  pl.BlockSpec((D_MODEL, LANES), lambda i: (0, 0)), pl.BlockSpec((1, LANES), lambda i: (0, 0))],
        out_specs=[pl.BlockSpec((tm * N_CHUNKS, LANES), lambda i: (i, 0)), pl.BlockSpec((tm, LANES), lambda i: (i, 0)),
                   pl.BlockSpec((tm, LANES), lambda i: (i, 0))],
        out_shape=[jax.ShapeDtypeStruct((t * N_CHUNKS, LANES), F32), jax.ShapeDtypeStruct((t, LANES), jnp.int32),
                   jax.ShapeDtypeStruct((t, LANES), F32)],
        compiler_params=_params(1),
    )(h, gain, wr, br)


def _plan_kernel(e_ref, pos_ref, te_ref, nv_ref, rank_ref, *, t2, blk, tm):
    shift = tm.bit_length() - 1
    nblk = t2 // blk
    r = lax.broadcasted_iota(jnp.int32, (blk, blk), 0)
    c = lax.broadcasted_iota(jnp.int32, (blk, blk), 1)
    upper = (r <= c).astype(F32)
    sub = lax.broadcasted_iota(jnp.int32, (N_EXPERTS, blk), 0)
    carry = jnp.zeros((N_EXPERTS, 1), F32)
    for b in range(nblk):
        sl = slice(b * blk, (b + 1) * blk)
        oh = (sub == e_ref[:, sl]).astype(F32)
        cs = _dot(oh, upper) + carry
        rank_ref[:, sl] = jnp.sum(oh * (cs - 1.0), axis=0, keepdims=True)
        carry = cs[:, blk - 1:blk]
    counts = carry.astype(jnp.int32)
    padc = ((counts + (tm - 1)) >> shift) << shift
    sub1 = lax.broadcasted_iota(jnp.int32, (N_EXPERTS, 1), 0)
    off = jnp.zeros((N_EXPERTS, 1), jnp.int32)
    run = jnp.zeros((1, 1), jnp.int32)
    for e in range(N_EXPERTS):
        off = jnp.where(sub1 == e, run, off)
        run = run + padc[e:e + 1, :]
    ends = off + padc
    for b in range(nblk):
        sl = slice(b * blk, (b + 1) * blk)
        offv = jnp.sum(jnp.where(sub == e_ref[:, sl], off, 0), axis=0, keepdims=True)
        pos_ref[:, sl] = rank_ref[:, sl].astype(jnp.int32) + offv
    lane = lax.broadcasted_iota(jnp.int32, (N_EXPERTS, LANES), 1)
    lane1 = lax.broadcasted_iota(jnp.int32, (1, LANES), 1)
    nv = run >> shift
    te = jnp.sum((ends <= lane * tm).astype(jnp.int32), axis=0, keepdims=True)
    te_last = jnp.sum((ends <= (nv - 1) * tm).astype(jnp.int32), axis=0, keepdims=True)
    te_ref[...] = jnp.minimum(jnp.where(lane1 >= nv, te_last, te), N_EXPERTS - 1)
    nv_ref[...] = jnp.broadcast_to(nv, (1, LANES))


def _plan(e_flat, tm):
    t2 = e_flat.shape[1]
    blk = min(t2, PLAN_BLOCK)
    return pl.pallas_call(
        functools.partial(_plan_kernel, t2=t2, blk=blk, tm=tm),
        out_shape=[jax.ShapeDtypeStruct((1, t2), jnp.int32), jax.ShapeDtypeStruct((1, LANES), jnp.int32),
                   jax.ShapeDtypeStruct((1, LANES), jnp.int32)],
        scratch_shapes=[pltpu.VMEM((1, t2), F32)],
        compiler_params=pltpu.CompilerParams(vmem_limit_bytes=VMEM_LIMIT_BYTES),
    )(e_flat)


def _invert_kernel(pos_ref, te_ref, nv_ref, src_ref, *, t, tm, n_tiles):
    nv = nv_ref[0]

    def zero_tile(j, carry):
        last_of_expert = (j >= nv - 1) | (te_ref[jnp.minimum(j + 1, n_tiles - 1)] != te_ref[j])

        @pl.when(last_of_expert)
        def _():
            def zero(r, c):
                src_ref[j * tm + r] = 0
                return c
            lax.fori_loop(0, tm, zero, 0, unroll=min(tm, SCALAR_UNROLL))

        return carry

    lax.fori_loop(0, n_tiles, zero_tile, 0)
    for k in range(2):
        def put(tok, carry, k=k):
            src_ref[pos_ref[k * t + tok]] = N_CHUNKS * tok
            return carry

        lax.fori_loop(0, t, put, 0, unroll=SCALAR_UNROLL)


def _invert(pos, te, nv, t, tm):
    n_tiles = te.shape[0]
    return pl.pallas_call(
        functools.partial(_invert_kernel, t=t, tm=tm, n_tiles=n_tiles),
        grid_spec=pltpu.PrefetchScalarGridSpec(
            num_scalar_prefetch=3,
            grid=(1,),
            in_specs=[],
            out_specs=pl.BlockSpec(memory_space=pltpu.SMEM),
        ),
        out_shape=jax.ShapeDtypeStruct((n_tiles * tm,), jnp.int32),
        compiler_params=_params(1),
    )(pos, te, nv)


def _moe_ffn_kernel(te_ref, nv_ref, src_ref, x_ref, wg_hbm, wu_hbm, wd_hbm, y_ref,
                    xbuf_ref, wg_ref, wu_ref, wd_ref, wslot_ref, sems, wsems, *, tm, layer, prec):
    i = pl.program_id(0)
    nv = nv_ref[0]
    slot = i % 2
    expert = te_ref[i]

    def start_tile(tile, dst_slot):
        base = tile * tm
        for r in range(tm):
            src = pl.multiple_of(src_ref[base + r], N_CHUNKS)
            pltpu.make_async_copy(x_ref.at[pl.ds(src, N_CHUNKS)],
                                  xbuf_ref.at[dst_slot, pl.ds(r * ROW_PITCH, N_CHUNKS)],
                                  sems.at[dst_slot]).start(priority=r % 2)

    def wait_tile():
        done = xbuf_ref.at[slot, pl.ds(0, tm * N_CHUNKS)]
        pltpu.make_async_copy(done, done, sems.at[slot]).wait()

    def weight_copies(e, ws):
        return (pltpu.make_async_copy(wg_hbm.at[layer, e], wg_ref.at[ws], wsems.at[ws]),
                pltpu.make_async_copy(wu_hbm.at[layer, e], wu_ref.at[ws], wsems.at[ws]),
                pltpu.make_async_copy(wd_hbm.at[layer, e], wd_ref.at[ws], wsems.at[ws]))

    @pl.when(i == 0)
    def _():
        start_tile(0, 0)
        wslot_ref[0] = 1
        for cp in weight_copies(expert, 0):
            cp.start()

    is_first = (i < nv) & ((i == 0) | (te_ref[jnp.maximum(i - 1, 0)] != expert))

    @pl.when(is_first)
    def _():
        ws = 1 - wslot_ref[0]
        wslot_ref[0] = ws
        for cp in weight_copies(expert, ws):
            cp.wait()
        nxt = lax.while_loop(lambda j: (j < nv) & (te_ref[jnp.minimum(j, nv - 1)] == expert), lambda j: j + 1, i + 1)

        @pl.when(nxt < nv)
        def _():
            for cp in weight_copies(te_ref[nxt], 1 - ws):
                cp.start()

    def ffn():
        ws = wslot_ref[0]
        x = _load_gathered_rows(xbuf_ref.at[slot], tm)
        hdn = jax.nn.silu(_dot(x, wg_ref[ws], prec)) * _dot(x, wu_ref[ws], prec)
        _store_chunk_rows(y_ref, _dot(hdn, wd_ref[ws], prec))

    @pl.when(i + 1 < nv)
    def _():
        wait_tile()
        start_tile(i + 1, 1 - slot)
        ffn()

    @pl.when(i + 1 == nv)
    def _():
        wait_tile()
        ffn()

    @pl.when(i >= nv)
    def _():
        y_ref[...] = jnp.zeros_like(y_ref)


def _moe_ffn(te, nv, src, x2, w_gate, w_up, w_down, layer, tm, prec):
    n_rows = src.shape[0]
    n_tiles = n_rows // tm
    any_spec = pl.BlockSpec(memory_space=pl.ANY)
    return pl.pallas_call(
        functools.partial(_moe_ffn_kernel, tm=tm, layer=layer, prec=prec),
        grid_spec=pltpu.PrefetchScalarGridSpec(
            num_scalar_prefetch=3,
            grid=(n_tiles,),
            in_specs=[any_spec, any_spec, any_spec, any_spec],
            out_specs=pl.BlockSpec((tm * N_CHUNKS, LANES), lambda i, te, nv, src: (i, 0)),
            scratch_shapes=[pltpu.VMEM((2, tm * ROW_PITCH, LANES), F32),
                            pltpu.VMEM((2, D_MODEL, D_EXPERT), F32), pltpu.VMEM((2, D_MODEL, D_EXPERT), F32),
                            pltpu.VMEM((2, D_EXPERT, D_MODEL), F32), pltpu.SMEM((1,), jnp.int32),
                            pltpu.SemaphoreType.DMA((2,)), pltpu.SemaphoreType.DMA((2,))],
        ),
        out_shape=jax.ShapeDtypeStruct((n_rows * N_CHUNKS, LANES), F32),
        compiler_params=_params(1),
    )(te, nv, src, x2, w_gate, w_up, w_down)


def _combine_kernel(pos_ref, h_ref, g_ref, y_ref, o_ref, y0_ref, y1_ref, sems, *, t, tm):
    base = pl.program_id(0) * tm
    for k, buf in enumerate((y0_ref, y1_ref)):
        for r in range(tm):
            src = pl.multiple_of(pos_ref[k * t + base + r] * N_CHUNKS, N_CHUNKS)
            pltpu.make_async_copy(y_ref.at[pl.ds(src, N_CHUNKS)], buf.at[pl.ds(r * ROW_PITCH, N_CHUNKS)],
                                  sems.at[k]).start(priority=k)
    for k, buf in enumerate((y0_ref, y1_ref)):
        done = buf.at[pl.ds(0, tm * N_CHUNKS)]
        pltpu.make_async_copy(done, done, sems.at[k]).wait()
    g = g_ref[...]
    o_ref[...] = (h_ref[...] + g[:, 0:1] * _load_gathered_rows(y0_ref, tm)
                  + g[:, 1:2] * _load_gathered_rows(y1_ref, tm))


def _combine(pos, h, gates, y):
    t = h.shape[0]
    tm = min(t, COMBINE_ROW_TILE)
    return pl.pallas_call(
        functools.partial(_combine_kernel, t=t, tm=tm),
        grid_spec=pltpu.PrefetchScalarGridSpec(
            num_scalar_prefetch=1,
            grid=(t // tm,),
            in_specs=[pl.BlockSpec((tm, D_MODEL), lambda i, pos: (i, 0)),
                      pl.BlockSpec((tm, LANES), lambda i, pos: (i, 0)),
                      pl.BlockSpec(memory_space=pl.ANY)],
            out_specs=pl.BlockSpec((tm, D_MODEL), lambda i, pos: (i, 0)),
            scratch_shapes=[pltpu.VMEM((tm * ROW_PITCH, LANES), F32), pltpu.VMEM((tm * ROW_PITCH, LANES), F32),
                            pltpu.SemaphoreType.DMA((2,))],
        ),
        out_shape=jax.ShapeDtypeStruct((t, D_MODEL), F32),
        compiler_params=_params(1),
    )(pos, h, gates, y)


def _hier_moe(h, gain, wr, br, w_gate, w_up, w_down, layer, prec):
    t = h.shape[0]
    tm = MOE_ROW_TILE if t >= MOE_SMALL_BELOW else MOE_ROW_TILE_SMALL
    n_tiles = (2 * t) // tm + N_EXPERTS
    assert n_tiles <= LANES
    x2, idx, gates = _router(h, gain, wr, br, prec)
    e_flat = idx[:, :2].T.reshape(1, 2 * t)
    pos, te, nv = _plan(e_flat, tm)
    pos = pos.reshape(2 * t)
    te = te[0, :n_tiles]
    nv = nv[0, :1]
    src = _invert(pos, te, nv, t, tm)
    y = _moe_ffn(te, nv, src, x2, w_gate, w_up, w_down, layer, tm, prec)
    return _combine(pos, h, gates, y)


def _fox_cumsum_kernel(lft_ref, lf_ref, ct_ref, c_ref, *, s, blk):
    r = lax.broadcasted_iota(jnp.int32, (blk, blk), 0)
    c = lax.broadcasted_iota(jnp.int32, (blk, blk), 1)
    upper = (r <= c).astype(F32)
    lower = (r >= c).astype(F32)
    carry_t = jnp.zeros((2 * SUBLANES, 1), F32)
    carry = jnp.zeros((1, LANES), F32)
    for b in range(s // blk):
        sl = slice(b * blk, (b + 1) * blk)
        ct = _dot3_left(lft_ref[:, sl], upper) + carry_t
        ct_ref[:, sl] = ct
        carry_t = ct[:, blk - 1:blk]
        cc = _dot3_right(lower, lf_ref[sl, :]) + carry
        c_ref[sl, :] = cc
        carry = cc[blk - 1:blk, :]


def _fox_cumsum(lft, lf, nb, s):
    t = lf.shape[0]
    blk = min(s, CUMSUM_BLOCK)
    return pl.pallas_call(
        functools.partial(_fox_cumsum_kernel, s=s, blk=blk),
        grid=(nb,),
        in_specs=[pl.BlockSpec((2 * SUBLANES, s), lambda b: (0, b)), pl.BlockSpec((s, LANES), lambda b: (b, 0))],
        out_specs=[pl.BlockSpec((2 * SUBLANES, s), lambda b: (0, b)), pl.BlockSpec((s, LANES), lambda b: (b, 0))],
        out_shape=[jax.ShapeDtypeStruct((2 * SUBLANES, t), F32), jax.ShapeDtypeStruct((t, LANES), F32)],
        compiler_params=_params(1),
    )(lft, lf)


def _fox_prompt_kernel(q_ref, k_ref, v_ref, ct_ref, c_ref, o_ref, *, s, tq):
    h = pl.program_id(1)
    lane = lax.broadcasted_iota(jnp.int32, (s, LANES), 1)
    c_col = jnp.sum(jnp.where(lane == h, c_ref[...], 0.0), axis=1, keepdims=True) * LOG2E
    c_row = ct_ref[pl.ds(h, 1), :] * LOG2E
    ri = lax.broadcasted_iota(jnp.int32, (tq, tq), 0)
    ci = lax.broadcasted_iota(jnp.int32, (tq, tq), 1)
    causal = ci <= ri
    for qi in range(s // tq):
        qs = slice(qi * tq, (qi + 1) * tq)
        q = q_ref[qs, :].astype(F32)
        cq = c_col[qs, :]
        m = jnp.full((tq, 1), -jnp.inf, F32)
        l = jnp.zeros((tq, 1), F32)
        acc = jnp.zeros((tq, HEAD_DIM), F32)
        for kj in range(qi + 1):
            ks = slice(kj * tq, (kj + 1) * tq)
            sc = lax.dot_general(q, k_ref[ks, :], _NT, preferred_element_type=F32) - c_row[:, ks]
            if kj == qi:
                sc = jnp.where(causal, sc, -jnp.inf)
            m_new = jnp.maximum(m, jnp.max(sc, axis=-1, keepdims=True) + cq)
            alpha = jnp.exp2(m - m_new)
            p = jnp.exp2(sc - (m_new - cq))
            l = alpha * l + jnp.sum(p, axis=-1, keepdims=True)
            acc = alpha * acc + _dot(p, v_ref[ks, :])
            m = m_new
        o_ref[qs, :] = (acc / l).astype(o_ref.dtype)


def _fox_prompt(q, k, v, ct, c, nb, s):
    t = q.shape[0]
    tq = min(s, ATTN_TILE)
    head_spec = pl.BlockSpec((s, HEAD_DIM), lambda b, h: (b, h))
    return pl.pallas_call(
        functools.partial(_fox_prompt_kernel, s=s, tq=tq),
        grid=(nb, MAIN_HEADS),
        in_specs=[head_spec, head_spec, head_spec,
                  pl.BlockSpec((2 * SUBLANES, s), lambda b, h: (0, b)),
                  pl.BlockSpec((s, LANES), lambda b, h: (b, 0))],
        out_specs=head_spec,
        out_shape=jax.ShapeDtypeStruct((t, MAIN_WIDTH), _matmul_only_dtype(None)),
        compiler_params=_params(2),
    )(q, k, v, ct, c)


def _fox_past_bias_kernel(pt_ref, lfc_ref, o_ref, buf_ref, sem, *, n_pages):
    b = pl.program_id(0)

    def page_copy(p):
        page = pt_ref[b * n_pages + p]
        return pltpu.make_async_copy(lfc_ref.at[:, pl.ds(page, 1), :], buf_ref.at[:, pl.ds(p, 1), :], sem)

    def start(p, carry):
        page_copy(p).start()
        return carry

    def wait(p, carry):
        page_copy(p).wait()
        return carry

    lax.fori_loop(0, n_pages, start, 0)
    r = lax.broadcasted_iota(jnp.int32, (PAGE_SIZE, PAGE_SIZE), 0)
    c = lax.broadcasted_iota(jnp.int32, (PAGE_SIZE, PAGE_SIZE), 1)
    after_in_page = (r > c).astype(F32)
    pr = lax.broadcasted_iota(jnp.int32, (n_pages, n_pages), 0)
    pc = lax.broadcasted_iota(jnp.int32, (n_pages, n_pages), 1)
    later_pages = (pc > pr).astype(F32)
    lax.fori_loop(0, n_pages, wait, 0)
    for h in range(MAIN_HEADS):
        lp = buf_ref[h]
        tot = jnp.broadcast_to(jnp.sum(lp, axis=1, keepdims=True), lp.shape)
        o_ref[h] = _dot3_left(lp, after_in_page) + _dot3_right(later_pages, tot)


def _fox_past_bias(page_table_flat, cache_logf_t, nb, n_pages):
    return pl.pallas_call(
        functools.partial(_fox_past_bias_kernel, n_pages=n_pages),
        grid_spec=pltpu.PrefetchScalarGridSpec(
            num_scalar_prefetch=1,
            grid=(nb,),
            in_specs=[pl.BlockSpec(memory_space=pl.ANY)],
            out_specs=pl.BlockSpec((None, MAIN_HEADS, n_pages, PAGE_SIZE), lambda b, pt: (b, 0, 0, 0)),
            scratch_shapes=[pltpu.VMEM((MAIN_HEADS, n_pages, PAGE_SIZE), F32), pltpu.SemaphoreType.DMA(())],
        ),
        out_shape=jax.ShapeDtypeStruct((nb, MAIN_HEADS, n_pages, PAGE_SIZE), F32),
        compiler_params=_params(1),
    )(page_table_flat, cache_logf_t)


def _fox_sample_kernel(*refs, ppb, n_steps, valid):
    pt_ref, q_ref, kn_ref, vn_ref, lf_ref, bias_ref = refs[:6]
    k_refs = refs[6:6 + ppb]
    v_refs = refs[6 + ppb:6 + 2 * ppb]
    o_ref, m_ref, l_ref, acc_ref = refs[6 + 2 * ppb:]
    g = pl.program_id(1)
    rows = SAMPLE_ROWS

    @pl.when(g == 0)
    def _():
        m_ref[...] = jnp.full(m_ref.shape, -jnp.inf, F32)
        l_ref[...] = jnp.zeros(l_ref.shape, F32)
        acc_ref[...] = jnp.zeros(acc_ref.shape, F32)

    r8 = lax.broadcasted_iota(jnp.int32, (rows, rows), 0)
    c8 = lax.broadcasted_iota(jnp.int32, (rows, rows), 1)
    new_mask = (c8 <= r8) & (c8 < valid)
    c_new = _dot3_right(new_mask.astype(F32), lf_ref[...])

    def qk3(q_hi, q_lo, k):
        a = _dot_nt(jnp.concatenate([q_hi, q_lo], axis=0), k)
        return a[:rows] + a[rows:]

    def pv3(p, v):
        p_hi, p_lo = _split_hi_lo(p)
        a = _dot(jnp.concatenate([p_hi, p_lo], axis=0), v)
        return a[:rows] + a[rows:]

    all_scores = []
    for h in range(MAIN_HEADS):
        q_hi, q_lo = _split_hi_lo(q_ref[:, h * HEAD_DIM:(h + 1) * HEAD_DIM])
        cq = c_new[:, h:h + 1]
        all_scores.append([qk3(q_hi, q_lo, k_refs[u][h]) + cq + bias_ref[h, u:u + 1, :]
                           for u in range(ppb)])
    all_probs, all_alpha = [], []
    for h in range(MAIN_HEADS):
        m_old = m_ref[h, :, 0:1]
        m_new = m_old
        for sc in all_scores[h]:
            m_new = jnp.maximum(m_new, jnp.max(sc, axis=-1, keepdims=True))
        alpha = jnp.exp(m_old - m_new)
        probs = [jnp.exp(sc - m_new) for sc in all_scores[h]]
        l_new = alpha * l_ref[h, :, 0:1]
        for p in probs:
            l_new = l_new + jnp.sum(p, axis=-1, keepdims=True)
        m_ref[h] = jnp.broadcast_to(m_new, (rows, LANES))
        l_ref[h] = jnp.broadcast_to(l_new, (rows, LANES))
        all_probs.append(probs)
        all_alpha.append(alpha)
    for h in range(MAIN_HEADS):
        sl = slice(h * HEAD_DIM, (h + 1) * HEAD_DIM)
        acc = all_alpha[h] * acc_ref[:, sl]
        for u in range(ppb):
            acc = acc + pv3(all_probs[h][u], v_refs[u][h])
        acc_ref[:, sl] = acc

    @pl.when(g == n_steps - 1)
    def _():
        for h in range(MAIN_HEADS):
            sl = slice(h * HEAD_DIM, (h + 1) * HEAD_DIM)
            qh = q_ref[:, sl]
            cq = c_new[:, h:h + 1]
            cq_row = jnp.sum(jnp.where(r8 == c8, jnp.broadcast_to(cq, (rows, rows)), 0.0), axis=0, keepdims=True)
            sn = _dot_nt(qh, kn_ref[:, sl], HIGHEST) + cq - cq_row
            sn = jnp.where(new_mask, sn, -jnp.inf)
            m_old = m_ref[h, :, 0:1]
            m_new = jnp.maximum(m_old, jnp.max(sn, axis=-1, keepdims=True))
            alpha = jnp.exp(m_old - m_new)
            p = jnp.exp(sn - m_new)
            l_new = alpha * l_ref[h, :, 0:1] + jnp.sum(p, axis=-1, keepdims=True)
            acc = alpha * acc_ref[:, sl] + _dot(p, vn_ref[:, sl], HIGHEST)
            o_ref[:, sl] = acc / l_new


def _fox_sample(page_table_flat, q, k_new, v_new, lf, bias, cache_k_t, cache_v_t, nb, n_pages, valid):
    ppb = PAGES_PER_STEP if n_pages % PAGES_PER_STEP == 0 else n_pages
    n_steps = n_pages // ppb
    row_spec = pl.BlockSpec((SAMPLE_ROWS, MAIN_WIDTH), lambda b, g, pt: (b, 0))

    def page_spec(u):
        return pl.BlockSpec((None, MAIN_HEADS, PAGE_SIZE, HEAD_DIM),
                            lambda b, g, pt, u=u: (pt[b * n_pages + g * ppb + u], 0, 0, 0))

    in_specs = [row_spec, row_spec, row_spec,
                pl.BlockSpec((SAMPLE_ROWS, LANES), lambda b, g, pt: (b, 0)),
                pl.BlockSpec((None, MAIN_HEADS, ppb, PAGE_SIZE), lambda b, g, pt: (b, 0, g, 0))]
    in_specs += [page_spec(u) for u in range(ppb)] + [page_spec(u) for u in range(ppb)]
    return pl.pallas_call(
        functools.partial(_fox_sample_kernel, ppb=ppb, n_steps=n_steps, valid=valid),
        grid_spec=pltpu.PrefetchScalarGridSpec(
            num_scalar_prefetch=1,
            grid=(nb, n_steps),
            in_specs=in_specs,
            out_specs=row_spec,
            scratch_shapes=[pltpu.VMEM((MAIN_HEADS, SAMPLE_ROWS, LANES), F32),
                            pltpu.VMEM((MAIN_HEADS, SAMPLE_ROWS, LANES), F32),
                            pltpu.VMEM((SAMPLE_ROWS, MAIN_WIDTH), F32)],
        ),
        out_shape=jax.ShapeDtypeStruct(q.shape, F32),
        compiler_params=_params(2),
    )(page_table_flat, q, k_new, v_new, lf, bias, *([cache_k_t] * ppb), *([cache_v_t] * ppb))


def _trunk(x, nb, rows, valid, mem_k, mem_v, fox, w, prec):
    u, v, qm = _norm_matmul(x, w["norm1_gain"][0:1], w["w_in_a"], 0,
                            [(MAIN_WIDTH, "gelu_mm", 0), (MAIN_WIDTH, "gelu", 0), (MEM_WIDTH, "headnorm_mm", 0)],
                            w["mem_q_gain"][0:1], prec=prec)
    y_main, v_rows = _gmlp(u, v, w["v_gain_a"][0:1], w["w_mix"], w["b_mix"], w["gmlp_rows"], w["gmlp_grp"], valid,
                           w["gmlp_blocks_per_seq"], prec)
    y_mem = _mem_attn(qm, mem_k, mem_v, 0, nb, rows, prec)
    h = _out_proj(y_main, y_mem, w["w_out"], 0, x, prec)
    h = _hier_moe(h, w["norm2_gain"][0:1], w["w_router"][0], w["b_router"][0], w["w_gate"], w["w_up"], w["w_down"],
                  0, prec)
    k, vv, lf, lft = _norm_matmul(h, w["kv_norm_gain"], w["w_kv_shared"], 0,
                                  [(MAIN_WIDTH, "headnorm", 0), (MAIN_WIDTH, "plain", 0)],
                                  w["k_gain_shared"], forget=w["forget"], prec=prec)
    q, qm = _norm_matmul(h, w["norm1_gain"][1:2], w["w_in_b"], 0,
                         [(MAIN_WIDTH, "headnorm_mm", 0), (MEM_WIDTH, "headnorm_mm", 1)],
                         jnp.concatenate([w["q_gain_b"][0:1] * w["q_logit_scale"], w["mem_q_gain"][1:2]], axis=0),
                         prec=prec)
    y_main = fox(q, k, vv, lf, lft)
    y_mem = _mem_attn(qm, mem_k, mem_v, 1, nb, rows, prec)
    h = _out_proj(y_main, y_mem, w["w_out"], 1, h, prec)
    h = _hier_moe(h, w["norm2_gain"][1:2], w["w_router"][1], w["b_router"][1], w["w_gate"], w["w_up"], w["w_down"],
                  1, prec)
    return h, k, vv, lf, v_rows


def kernel(x_prompt, x_sample, cache_k, cache_v, cache_logf, cache_mem_k, cache_mem_v, page_table, mem_prompt, norm1_gain, norm2_gain, w_in_a, v_gain_a, w_s_a, b_s_a, w_in_b, q_gain_b, kv_norm_gain, w_kv_shared, b_forget, k_gain_shared, mem_norm_gain, w_mem_kv, mem_q_gain, mem_k_gain, w_out, w_router_group, b_router_group, w_router_expert, b_router_expert, w_gate, w_up, w_down):
    batch, seq, _ = x_prompt.shape
    dec_batch, dec_seq, _ = x_sample.shape
    n_phys = cache_k.shape[0]
    n_pages = page_table.shape[1]
    depth = norm1_gain.shape[0]
    assert depth == 2 and dec_seq <= SAMPLE_ROWS and seq % CHUNK == 0

    n_route = N_GROUPS + N_EXPERTS
    w_router = jnp.pad(jnp.concatenate([w_router_group, w_router_expert], axis=-1),
                       ((0, 0), (0, 0), (0, LANES - n_route)))
    b_router = jnp.pad(jnp.concatenate([b_router_group, b_router_expert], axis=-1),
                       ((0, 0), (0, LANES - n_route)))[:, None, :]
    w_f = w_kv_shared[:, 2 * MAIN_WIDTH:]
    forget = (jnp.pad(w_f, ((0, 0), (0, LANES - MAIN_HEADS))),
              jnp.pad(b_forget, (0, LANES - MAIN_HEADS))[None, :],
              jnp.pad(b_forget, (0, 2 * SUBLANES - MAIN_HEADS))[:, None])
    common = dict(norm1_gain=norm1_gain, norm2_gain=norm2_gain, w_in_a=w_in_a, v_gain_a=v_gain_a, w_in_b=w_in_b,
                  q_gain_b=q_gain_b, kv_norm_gain=kv_norm_gain[None, :], w_kv_shared=w_kv_shared,
                  k_gain_shared=k_gain_shared[None, :], mem_q_gain=mem_q_gain, w_out=w_out,
                  w_router=w_router, b_router=b_router, w_gate=w_gate, w_up=w_up, w_down=w_down, forget=forget)

    t_p = batch * seq
    mem_tok = mem_prompt.reshape(batch * N_MEM, D_MODEL)
    mem_k_layers, mem_v_layers = [], []
    for l in range(depth):
        mk, mv = _norm_matmul(mem_tok, mem_norm_gain[l:l + 1], w_mem_kv, l,
                              [(MEM_WIDTH, "headnorm", 0), (MEM_WIDTH, "plain", 0)], mem_k_gain[l:l + 1])
        mem_k_layers.append(mk.reshape(batch, N_MEM, MEM_WIDTH))
        mem_v_layers.append(mv.reshape(batch, N_MEM, MEM_WIDTH))
    mem_k_p = jnp.stack(mem_k_layers)
    mem_v_p = jnp.stack(mem_v_layers)

    def fox_prompt(q, k, v, lf, lft):
        ct, c = _fox_cumsum(lft, lf, batch, seq)
        return _fox_prompt(q, k, v, ct, c, batch, seq)

    w_p = dict(common, w_mix=w_s_a[0], b_mix=b_s_a[0].T, gmlp_rows=CHUNK, gmlp_grp=CHUNK,
               gmlp_blocks_per_seq=seq // CHUNK, q_logit_scale=ATTN_SCALE * LOG2E)
    y_p, k_p, v_p, lf_p, vrows_p = _trunk(x_prompt.reshape(t_p, D_MODEL), batch, seq, seq,
                                          mem_k_p, mem_v_p, fox_prompt, w_p, None)

    rows = SAMPLE_ROWS
    t_s = dec_batch * rows
    x_s = jnp.pad(x_sample, ((0, 0), (0, rows - dec_seq), (0, 0))).reshape(t_s, D_MODEL)
    pt_flat = page_table.reshape(dec_batch * n_pages)
    bias = _fox_past_bias(pt_flat, jnp.transpose(cache_logf, (2, 0, 1)), dec_batch, n_pages)
    cache_k_t = jnp.transpose(cache_k, (0, 2, 1, 3))
    cache_v_t = jnp.transpose(cache_v, (0, 2, 1, 3))

    def fox_sample(q, k, v, lf, lft):
        return _fox_sample(pt_flat, q, k, v, lf, bias, cache_k_t, cache_v_t, dec_batch, n_pages, dec_seq)

    w_s = dict(common, w_mix=jnp.tile(w_s_a[0][:, :rows, :rows], (1, dec_batch, dec_batch)),
               b_mix=jnp.tile(b_s_a[0][:, :rows].T, (dec_batch, 1)), gmlp_rows=t_s, gmlp_grp=rows,
               gmlp_blocks_per_seq=1, q_logit_scale=ATTN_SCALE)
    mem_k_s = cache_mem_k.reshape(depth, dec_batch, N_MEM, MEM_WIDTH)
    mem_v_s = cache_mem_v.reshape(depth, dec_batch, N_MEM, MEM_WIDTH)
    y_s, k_s, v_s, lf_s, vrows_s = _trunk(x_s, dec_batch, rows, dec_seq, mem_k_s, mem_v_s, fox_sample, w_s, SPLIT3)

    def unpad(a, *tail):
        return a.reshape((dec_batch, rows) + tail)[:, :dec_seq]

    return (y_p.reshape(batch, seq, D_MODEL),
            unpad(y_s, D_MODEL),
            k_p.reshape(batch, seq, MAIN_HEADS, HEAD_DIM),
            v_p.reshape(batch, seq, MAIN_HEADS, HEAD_DIM),
            lf_p[:, :MAIN_HEADS].reshape(batch, seq, MAIN_HEADS),
            unpad(k_s, MAIN_HEADS, HEAD_DIM),
            unpad(v_s, MAIN_HEADS, HEAD_DIM),
            unpad(lf_s[:, :MAIN_HEADS], MAIN_HEADS),
            mem_k_p.reshape(depth, batch, N_MEM, MEM_HEADS, HEAD_DIM),
            mem_v_p.reshape(depth, batch, N_MEM, MEM_HEADS, HEAD_DIM),
            vrows_p.reshape(batch, CHUNK, MAIN_WIDTH)[None],
            unpad(vrows_s, MAIN_WIDTH)[None])
```

```python
import functools

import jax
import jax.numpy as jnp
from jax import lax
from jax.experimental import pallas as pl
from jax.experimental.pallas import tpu as pltpu

D_MODEL = 2048
HEAD_DIM = 128
MAIN_HEADS = 12
MEM_HEADS = 4
MAIN_WIDTH = MAIN_HEADS * HEAD_DIM
MEM_WIDTH = MEM_HEADS * HEAD_DIM
N_MEM = 256
N_GROUPS = 4
EXPERTS_PER_GROUP = 4
N_EXPERTS = N_GROUPS * EXPERTS_PER_GROUP
D_EXPERT = 512
CHUNK = 128
PAGE_SIZE = 128
EPS = 1e-6
ATTN_SCALE = HEAD_DIM ** -0.5
LOG2E = 1.4426950408889634

LANES = 128
SUBLANES = 8
VMEM_LIMIT_BYTES = 56 * 1024 * 1024
SAMPLE_ROWS = SUBLANES
N_CHUNKS = D_MODEL // LANES
ROW_PITCH = 24

PROJ_ROW_TILE = 1024
PROJ_ROW_TILE_BF16 = 2048
PROJ_COL_TILE = 512
ROUTER_ROW_TILE = 512
MOE_ROW_TILE = 256
MOE_ROW_TILE_SMALL = 16
MOE_SMALL_BELOW = 1024
COMBINE_ROW_TILE = 256
ATTN_TILE = 512
MEM_ATTN_ROW_TILE = 1024
CUMSUM_BLOCK = 256
PLAN_BLOCK = 512
PAGES_PER_STEP = SUBLANES
SCALAR_UNROLL = 32

F32 = jnp.float32
_NT = (((1,), (1,)), ((), ()))


def _params(n_axes):
    return pltpu.CompilerParams(dimension_semantics=("arbitrary",) * n_axes,
                                vmem_limit_bytes=VMEM_LIMIT_BYTES)


HIGHEST = lax.Precision.HIGHEST


SPLIT3 = "split3"


def _dot(a, b, prec=None):
    if prec == SPLIT3:
        return _dot_split3(a, b, _dot)
    return jnp.dot(a, b, preferred_element_type=F32, precision=prec)


def _dot_nt(a, b, prec=None):
    if prec == SPLIT3:
        return _dot_split3(a, b, _dot_nt)
    return lax.dot_general(a, b, _NT, preferred_element_type=F32, precision=prec)


def _dot_tn(a, b, prec=None):
    dims = (((0,), (1,)), ((), ()))
    if prec == SPLIT3:
        a_hi, a_lo = _split_hi_lo(a)
        b_hi, b_lo = _split_hi_lo(b)
        return _dot_tn(a_hi, b_hi) + _dot_tn(a_lo, b_hi) + _dot_tn(a_hi, b_lo)
    return lax.dot_general(a, b, dims, preferred_element_type=F32, precision=prec)


def _dot_split3(a, b, dot):
    a_hi, a_lo = _split_hi_lo(a)
    b_hi, b_lo = _split_hi_lo(b)
    m = a.shape[0]
    s = dot(jnp.concatenate([a_hi, a_lo], axis=0), b_hi)
    return s[:m] + s[m:] + dot(a_hi, b_lo)


def _matmul_only_dtype(prec):
    return jnp.bfloat16 if prec is None else F32


def _split_hi_lo(x):
    hi = x.astype(jnp.bfloat16).astype(F32)
    return hi, x - hi


def _split3(x):
    hi = x.astype(jnp.bfloat16).astype(F32)
    r = x - hi
    mid = r.astype(jnp.bfloat16).astype(F32)
    lo = (r - mid).astype(jnp.bfloat16).astype(F32)
    return hi, mid, lo


def _dot3_right(a_exact, x):
    hi, mid, lo = _split3(x)
    return _dot(a_exact, hi) + _dot(a_exact, mid) + _dot(a_exact, lo)


def _dot3_left(x, b_exact):
    hi, mid, lo = _split3(x)
    return _dot(hi, b_exact) + _dot(mid, b_exact) + _dot(lo, b_exact)


def _store_chunk_rows(ref, x):
    rows = x.shape[0]
    for c in range(N_CHUNKS):
        ref[pl.ds(c, rows, stride=N_CHUNKS), :] = x[:, c * LANES:(c + 1) * LANES]


def _load_gathered_rows(ref, rows):
    return jnp.concatenate([ref[pl.ds(c, rows, stride=ROW_PITCH), :] for c in range(N_CHUNKS)], axis=1)


def _head_norm(z, gain_row):
    return z * lax.rsqrt(jnp.mean(z * z, axis=-1, keepdims=True) + EPS) * gain_row


def _norm_matmul_kernel(*refs, segs, tn, with_forget, prec):
    it = iter(refs)
    x_ref, g_ref, w_ref, hg_ref = next(it), next(it), next(it), next(it)
    if with_forget:
        wf_ref, bfr_ref, bfc_ref = next(it), next(it), next(it)
    out_refs = [next(it) for _ in segs]
    if with_forget:
        lf_ref, lft_ref = next(it), next(it)
    xn_ref = next(it)
    j = pl.program_id(1)

    def project(xn, o_ref, kind, grow):
        z = _dot(xn, w_ref[...], prec)
        if kind == "plain":
            o_ref[...] = z
        elif kind in ("gelu", "gelu_mm"):
            o_ref[...] = jax.nn.gelu(z).astype(o_ref.dtype)
        else:
            gain = hg_ref[grow:grow + 1, :]
            for c in range(tn // HEAD_DIM):
                sl = slice(c * HEAD_DIM, (c + 1) * HEAD_DIM)
                o_ref[:, sl] = _head_norm(z[:, sl], gain).astype(o_ref.dtype)

    @pl.when(j == 0)
    def _first():
        x = x_ref[...]
        xn = x * lax.rsqrt(jnp.mean(x * x, axis=-1, keepdims=True) + EPS) * g_ref[...]
        xn_ref[...] = xn
        if with_forget:
            lf_ref[...] = jax.nn.log_sigmoid(_dot(xn, wf_ref[...], prec) + bfr_ref[...])
            zt = _dot_tn(wf_ref[...], xn, prec)[:2 * SUBLANES, :]
            lft_ref[...] = jax.nn.log_sigmoid(zt + bfc_ref[...])
        project(xn, out_refs[0], segs[0][1], segs[0][2])

    lo = 0
    for (ncols, kind, grow), o_ref in zip(segs, out_refs):
        nt = ncols // tn

        @pl.when((j >= max(lo, 1)) & (j < lo + nt))
        def _rest(o_ref=o_ref, kind=kind, grow=grow):
            project(xn_ref[...], o_ref, kind, grow)

        lo += nt


def _norm_matmul(x, gain, w, layer, segs, head_gains, forget=None, prec=None):
    t = x.shape[0]
    tm = min(t, PROJ_ROW_TILE)
    tn = PROJ_COL_TILE
    n_tiles = sum(s[0] for s in segs) // tn
    with_forget = forget is not None
    in_specs = [
        pl.BlockSpec((tm, D_MODEL), lambda i, j: (i, 0)),
        pl.BlockSpec((1, D_MODEL), lambda i, j: (0, 0)),
        pl.BlockSpec((None, D_MODEL, tn), lambda i, j: (layer, 0, j)) if w.ndim == 3
        else pl.BlockSpec((D_MODEL, tn), lambda i, j: (0, j)),
        pl.BlockSpec(head_gains.shape, lambda i, j: (0, 0)),
    ]
    args = [x, gain, w, head_gains]
    if with_forget:
        wf, bfr, bfc = forget
        in_specs += [pl.BlockSpec(wf.shape, lambda i, j: (0, 0)),
                     pl.BlockSpec(bfr.shape, lambda i, j: (0, 0)), pl.BlockSpec(bfc.shape, lambda i, j: (0, 0))]
        args += [wf, bfr, bfc]
    out_shape, out_specs = [], []
    lo = 0
    for ncols, kind, _ in segs:
        nt = ncols // tn
        out_shape.append(jax.ShapeDtypeStruct((t, ncols), _matmul_only_dtype(prec) if kind.endswith("_mm") else F32))
        out_specs.append(pl.BlockSpec((tm, tn), lambda i, j, lo=lo, nt=nt: (i, jnp.clip(j - lo, 0, nt - 1))))
        lo += nt
    if with_forget:
        out_shape += [jax.ShapeDtypeStruct((t, LANES), F32), jax.ShapeDtypeStruct((2 * SUBLANES, t), F32)]
        out_specs += [pl.BlockSpec((tm, LANES), lambda i, j: (i, 0)),
                      pl.BlockSpec((2 * SUBLANES, tm), lambda i, j: (0, i))]
    return pl.pallas_call(
        functools.partial(_norm_matmul_kernel, segs=tuple(segs), tn=tn, with_forget=with_forget, prec=prec),
        grid=(t // tm, n_tiles),
        in_specs=in_specs,
        out_specs=out_specs,
        out_shape=out_shape,
        scratch_shapes=[pltpu.VMEM((tm, D_MODEL), F32)],
        compiler_params=_params(2),
    )(*args)


def _gmlp_kernel(u_ref, v_ref, vg_ref, w_ref, b_ref, y_ref, vn_ref, *, rows, grp, valid, prec):
    v = v_ref[...]
    vn = v * lax.rsqrt(jnp.mean(v * v, axis=-1, keepdims=True) + EPS) * vg_ref[...]
    vn_ref[...] = vn
    r = lax.broadcasted_iota(jnp.int32, (rows, rows), 0)
    c = lax.broadcasted_iota(jnp.int32, (rows, rows), 1)
    shift = grp.bit_length() - 1
    allowed = ((c & (grp - 1)) <= (r & (grp - 1))) & ((c & (grp - 1)) < valid)
    if grp < rows:
        allowed = allowed & ((r >> shift) == (c >> shift))
    for g in range(MAIN_HEADS):
        sl = slice(g * HEAD_DIM, (g + 1) * HEAD_DIM)
        w = jnp.where(allowed, w_ref[g], 0.0)
        mixed = _dot(w, vn[:, sl], prec) + b_ref[:, g:g + 1]
        y_ref[:, sl] = (u_ref[:, sl] * mixed).astype(y_ref.dtype)


def _gmlp(u, v, v_gain, w_mix, b_mix, rows, grp, valid, blocks_per_seq, prec):
    t = u.shape[0]
    return pl.pallas_call(
        functools.partial(_gmlp_kernel, rows=rows, grp=grp, valid=valid, prec=prec),
        grid=(t // rows,),
        in_specs=[
            pl.BlockSpec((rows, MAIN_WIDTH), lambda i: (i, 0)),
            pl.BlockSpec((rows, MAIN_WIDTH), lambda i: (i, 0)),
            pl.BlockSpec((1, MAIN_WIDTH), lambda i: (0, 0)),
            pl.BlockSpec((MAIN_HEADS, rows, rows), lambda i: (0, 0, 0)),
            pl.BlockSpec((rows, MAIN_HEADS), lambda i: (0, 0)),
        ],
        out_specs=[pl.BlockSpec((rows, MAIN_WIDTH), lambda i: (i, 0)),
                   pl.BlockSpec((rows, MAIN_WIDTH), lambda i: (i // blocks_per_seq, 0))],
        out_shape=[jax.ShapeDtypeStruct((t, MAIN_WIDTH), _matmul_only_dtype(prec)),
                   jax.ShapeDtypeStruct((t // blocks_per_seq, MAIN_WIDTH), F32)],
        compiler_params=_params(1),
    )(u, v, v_gain, w_mix, b_mix)


def _mem_attn_kernel(q_ref, k_ref, v_ref, o_ref, *, prec):
    for h in range(MEM_HEADS):
        sl = slice(h * HEAD_DIM, (h + 1) * HEAD_DIM)
        s = _dot_nt(q_ref[:, sl].astype(F32), k_ref[:, sl], prec) * ATTN_SCALE
        e = jnp.exp(s - jnp.max(s, axis=-1, keepdims=True))
        p = e / jnp.sum(e, axis=-1, keepdims=True)
        o_ref[:, sl] = _dot(p, v_ref[:, sl], prec).astype(o_ref.dtype)


def _mem_attn(q, k, v, layer, nb, rows_per_batch, prec):
    tq = min(rows_per_batch, MEM_ATTN_ROW_TILE)
    nq = rows_per_batch // tq
    kv_spec = pl.BlockSpec((None, None, N_MEM, MEM_WIDTH), lambda b, i: (layer, b, 0, 0))
    return pl.pallas_call(
        functools.partial(_mem_attn_kernel, prec=prec),
        grid=(nb, nq),
        in_specs=[pl.BlockSpec((tq, MEM_WIDTH), lambda b, i: (b * nq + i, 0)), kv_spec, kv_spec],
        out_specs=pl.BlockSpec((tq, MEM_WIDTH), lambda b, i: (b * nq + i, 0)),
        out_shape=jax.ShapeDtypeStruct(q.shape, _matmul_only_dtype(prec)),
        compiler_params=_params(2),
    )(q, k, v)


def _out_proj_kernel(ym_ref, ymem_ref, w1_ref, w2_ref, h_ref, o_ref, *, prec):
    w1 = w1_ref[...].astype(ym_ref.dtype)
    w2 = w2_ref[...].astype(ymem_ref.dtype)
    o_ref[...] = h_ref[...] + _dot(ym_ref[...], w1, prec) + _dot(ymem_ref[...], w2, prec)


def _out_proj(y_main, y_mem, w_out, layer, h, prec):
    t = h.shape[0]
    tm = min(t, PROJ_ROW_TILE_BF16 if y_main.dtype == jnp.bfloat16 else PROJ_ROW_TILE)
    tn = PROJ_COL_TILE
    return pl.pallas_call(
        functools.partial(_out_proj_kernel, prec=prec),
        grid=(t // tm, D_MODEL // tn),
        in_specs=[
            pl.BlockSpec((tm, MAIN_WIDTH), lambda i, j: (i, 0)),
            pl.BlockSpec((tm, MEM_WIDTH), lambda i, j: (i, 0)),
            pl.BlockSpec((None, MAIN_WIDTH, tn), lambda i, j: (layer, 0, j)),
            pl.BlockSpec((None, MEM_WIDTH, tn), lambda i, j: (layer, MAIN_WIDTH // MEM_WIDTH, j)),
            pl.BlockSpec((tm, tn), lambda i, j: (i, j)),
        ],
        out_specs=pl.BlockSpec((tm, tn), lambda i, j: (i, j)),
        out_shape=jax.ShapeDtypeStruct((t, D_MODEL), F32),
        compiler_params=_params(2),
    )(y_main, y_mem, w_out, w_out, h)


def _router_kernel(h_ref, g_ref, wr_ref, br_ref, x2_ref, idx_ref, gate_ref, *, prec):
    x = h_ref[...]
    xn = x * lax.rsqrt(jnp.mean(x * x, axis=-1, keepdims=True) + EPS) * g_ref[...]
    x2_ref[...] = xn
    lg = _dot(xn, wr_ref[...], prec) + br_ref[...]
    lane = lax.broadcasted_iota(jnp.int32, lg.shape, 1)
    neg = -jnp.inf
    is_grp = lane < N_GROUPS
    gl = jnp.where(is_grp, lg, neg)
    ge = jnp.where(is_grp, jnp.exp(gl - jnp.max(gl, axis=-1, keepdims=True)), 0.0)
    pg = ge / jnp.sum(ge, axis=-1, keepdims=True)
    p_top = jnp.max(pg, axis=-1, keepdims=True)
    g_idx = jnp.min(jnp.where(is_grp & (pg == p_top), lane, LANES), axis=-1, keepdims=True)
    first = N_GROUPS + EXPERTS_PER_GROUP * g_idx
    in_grp = (lane >= first) & (lane < first + EXPERTS_PER_GROUP)
    e1 = jnp.max(jnp.where(in_grp, lg, neg), axis=-1, keepdims=True)
    i1 = jnp.min(jnp.where(in_grp & (lg == e1), lane, LANES), axis=-1, keepdims=True)
    rest = in_grp & (lane != i1)
    e2 = jnp.max(jnp.where(rest, lg, neg), axis=-1, keepdims=True)
    i2 = jnp.min(jnp.where(rest & (lg == e2), lane, LANES), axis=-1, keepdims=True)
    t2 = jnp.exp(e2 - e1)
    den = 1.0 + t2
    idx_ref[...] = jnp.where(lane == 0, i1 - N_GROUPS, jnp.where(lane == 1, i2 - N_GROUPS, 0))
    gate_ref[...] = jnp.where(lane == 0, p_top * (1.0 / den), jnp.where(lane == 1, p_top * (t2 / den), 0.0))


def _router(h, gain, wr, br, prec):
    t = h.shape[0]
    tm = min(t, ROUTER_ROW_TILE)
    return pl.pallas_call(
        functools.partial(_router_kernel, prec=prec),
        grid=(t // tm,),
        in_specs=[pl.BlockSpec((tm, D_MODEL), lambda i: (i, 0)), pl.BlockSpec((1, D_MODEL), lambda i: (0, 0)),
                  pl.BlockSpec((D_MODEL, LANES), lambda i: (0, 0)), pl.BlockSpec((1, LANES), lambda i: (0, 0))],
        out_specs=[pl.BlockSpec((tm, D_MODEL), lambda i: (i, 0)), pl.BlockSpec((tm, LANES), lambda i: (i, 0)),
                   pl.BlockSpec((tm, LANES), lambda i: (i, 0))],
        out_shape=[jax.ShapeDtypeStruct((t, D_MODEL), F32), jax.ShapeDtypeStruct((t, LANES), jnp.int32),
                   jax.ShapeDtypeStruct((t, LANES), F32)],
        compiler_params=_params(1),
    )(h, gain, wr, br)


def _plan_kernel(e_ref, pos_ref, te_ref, nv_ref, rank_ref, *, t2, blk, tm):
    shift = tm.bit_length() - 1
    nblk = t2 // blk
    r = lax.broadcasted_iota(jnp.int32, (blk, blk), 0)
    c = lax.broadcasted_iota(jnp.int32, (blk, blk), 1)
    upper = (r <= c).astype(F32)
    sub = lax.broadcasted_iota(jnp.int32, (N_EXPERTS, blk), 0)
    carry = jnp.zeros((N_EXPERTS, 1), F32)
    for b in range(nblk):
        sl = slice(b * blk, (b + 1) * blk)
        oh = (sub == e_ref[:, sl]).astype(F32)
        cs = _dot(oh, upper) + carry
        rank_ref[:, sl] = jnp.sum(oh * (cs - 1.0), axis=0, keepdims=True)
        carry = cs[:, blk - 1:blk]
    counts = carry.astype(jnp.int32)
    padc = ((counts + (tm - 1)) >> shift) << shift
    sub1 = lax.broadcasted_iota(jnp.int32, (N_EXPERTS, 1), 0)
    off = jnp.zeros((N_EXPERTS, 1), jnp.int32)
    run = jnp.zeros((1, 1), jnp.int32)
    for e in range(N_EXPERTS):
        off = jnp.where(sub1 == e, run, off)
        run = run + padc[e:e + 1, :]
    ends = off + padc
    for b in range(nblk):
        sl = slice(b * blk, (b + 1) * blk)
        offv = jnp.sum(jnp.where(sub == e_ref[:, sl], off, 0), axis=0, keepdims=True)
        pos_ref[:, sl] = rank_ref[:, sl].astype(jnp.int32) + offv
    lane = lax.broadcasted_iota(jnp.int32, (N_EXPERTS, LANES), 1)
    lane1 = lax.broadcasted_iota(jnp.int32, (1, LANES), 1)
    nv = run >> shift
    te = jnp.sum((ends <= lane * tm).astype(jnp.int32), axis=0, keepdims=True)
    te_last = jnp.sum((ends <= (nv - 1) * tm).astype(jnp.int32), axis=0, keepdims=True)
    te_ref[...] = jnp.minimum(jnp.where(lane1 >= nv, te_last, te), N_EXPERTS - 1)
    nv_ref[...] = jnp.broadcast_to(nv, (1, LANES))


def _plan(e_flat, tm):
    t2 = e_flat.shape[1]
    blk = min(t2, PLAN_BLOCK)
    return pl.pallas_call(
        functools.partial(_plan_kernel, t2=t2, blk=blk, tm=tm),
        out_shape=[jax.ShapeDtypeStruct((1, t2), jnp.int32), jax.ShapeDtypeStruct((1, LANES), jnp.int32),
                   jax.ShapeDtypeStruct((1, LANES), jnp.int32)],
        scratch_shapes=[pltpu.VMEM((1, t2), F32)],
        compiler_params=pltpu.CompilerParams(vmem_limit_bytes=VMEM_LIMIT_BYTES),
    )(e_flat)


def _dispatch_kernel(pos_ref, te_ref, nv_ref, x_ref, xs_ref, zero_ref, fill_sem, sems, *, t, tm, td, n_tiles):
    step = pl.program_id(0)
    nv = nv_ref[0]

    @pl.when(step == 0)
    def _():
        zero_ref[...] = jnp.zeros_like(zero_ref)

        def fill_copy(j):
            return pltpu.make_async_copy(zero_ref.at[pl.ds(0, tm)], xs_ref.at[pl.ds(j * tm, tm)], fill_sem)

        def holds_padding(j):
            return (j >= nv - 1) | (te_ref[jnp.minimum(j + 1, n_tiles - 1)] != te_ref[j])

        def start(j, carry):
            @pl.when(holds_padding(j))
            def _():
                fill_copy(j).start()
            return carry

        def wait(j, carry):
            @pl.when(holds_padding(j))
            def _():
                fill_copy(j).wait()
            return carry

        lax.fori_loop(0, n_tiles, start, 0)
        lax.fori_loop(0, n_tiles, wait, 0)

    base = step * td
    for k in range(2):
        for r in range(td):
            pltpu.make_async_copy(x_ref.at[pl.ds(r, 1)], xs_ref.at[pl.ds(pos_ref[k * t + base + r], 1)],
                                  sems.at[k]).start(priority=k)
    for k in range(2):
        done = zero_ref.at[pl.ds(0, td)]
        pltpu.make_async_copy(done, done, sems.at[k]).wait()


def _dispatch(pos, te, nv, x2, tm):
    t = x2.shape[0]
    n_tiles = te.shape[0]
    td = min(t, COMBINE_ROW_TILE)
    return pl.pallas_call(
        functools.partial(_dispatch_kernel, t=t, tm=tm, td=td, n_tiles=n_tiles),
        grid_spec=pltpu.PrefetchScalarGridSpec(
            num_scalar_prefetch=3,
            grid=(t // td,),
            in_specs=[pl.BlockSpec((td, D_MODEL), lambda i, pos, te, nv: (i, 0))],
            out_specs=pl.BlockSpec(memory_space=pl.ANY),
            scratch_shapes=[pltpu.VMEM((max(tm, td), D_MODEL), F32), pltpu.SemaphoreType.DMA(()),
                            pltpu.SemaphoreType.DMA((2,))],
        ),
        out_shape=jax.ShapeDtypeStruct((n_tiles * tm, D_MODEL), F32),
        compiler_params=_params(1),
    )(pos, te, nv, x2)


def _moe_ffn_kernel(te_ref, nv_ref, x_ref, wg_hbm, wu_hbm, wd_hbm, y_ref,
                    wg_ref, wu_ref, wd_ref, wslot_ref, wsems, *, layer, prec):
    i = pl.program_id(0)
    nv = nv_ref[0]
    expert = te_ref[i]

    def weight_copies(e, ws):
        return (pltpu.make_async_copy(wg_hbm.at[layer, e], wg_ref.at[ws], wsems.at[ws]),
                pltpu.make_async_copy(wu_hbm.at[layer, e], wu_ref.at[ws], wsems.at[ws]),
                pltpu.make_async_copy(wd_hbm.at[layer, e], wd_ref.at[ws], wsems.at[ws]))

    @pl.when(i == 0)
    def _():
        wslot_ref[0] = 1
        for cp in weight_copies(expert, 0):
            cp.start()

    is_first = (i < nv) & ((i == 0) | (te_ref[jnp.maximum(i - 1, 0)] != expert))

    @pl.when(is_first)
    def _():
        ws = 1 - wslot_ref[0]
        wslot_ref[0] = ws
        for cp in weight_copies(expert, ws):
            cp.wait()
        nxt = lax.while_loop(lambda j: (j < nv) & (te_ref[jnp.minimum(j, nv - 1)] == expert), lambda j: j + 1, i + 1)

        @pl.when(nxt < nv)
        def _():
            for cp in weight_copies(te_ref[nxt], 1 - ws):
                cp.start()

    @pl.when(i < nv)
    def _():
        ws = wslot_ref[0]
        x = x_ref[...]
        hdn = jax.nn.silu(_dot(x, wg_ref[ws], prec)) * _dot(x, wu_ref[ws], prec)
        _store_chunk_rows(y_ref, _dot(hdn, wd_ref[ws], prec))

    @pl.when(i >= nv)
    def _():
        y_ref[...] = jnp.zeros_like(y_ref)


def _moe_ffn(te, nv, xs, w_gate, w_up, w_down, layer, tm, prec):
    n_rows = xs.shape[0]
    n_tiles = n_rows // tm
    any_spec = pl.BlockSpec(memory_space=pl.ANY)
    return pl.pallas_call(
        functools.partial(_moe_ffn_kernel, layer=layer, prec=prec),
        grid_spec=pltpu.PrefetchScalarGridSpec(
            num_scalar_prefetch=2,
            grid=(n_tiles,),
            in_specs=[pl.BlockSpec((tm, D_MODEL), lambda i, te, nv: (jnp.minimum(i, nv[0] - 1), 0)),
                      any_spec, any_spec, any_spec],
            out_specs=pl.BlockSpec((tm * N_CHUNKS, LANES), lambda i, te, nv: (i, 0)),
            scratch_shapes=[pltpu.VMEM((2, D_MODEL, D_EXPERT), F32), pltpu.VMEM((2, D_MODEL, D_EXPERT), F32),
                            pltpu.VMEM((2, D_EXPERT, D_MODEL), F32), pltpu.SMEM((1,), jnp.int32),
                            pltpu.SemaphoreType.DMA((2,))],
        ),
        out_shape=jax.ShapeDtypeStruct((n_rows * N_CHUNKS, LANES), F32),
        compiler_params=_params(1),
    )(te, nv, xs, w_gate, w_up, w_down)


def _combine_kernel(pos_ref, h_ref, g_ref, y_ref, o_ref, y0_ref, y1_ref, sems, *, t, tm):
    base = pl.program_id(0) * tm
    for k, buf in enumerate((y0_ref, y1_ref)):
        for r in range(tm):
            src = pl.multiple_of(pos_ref[k * t + base + r] * N_CHUNKS, N_CHUNKS)
            pltpu.make_async_copy(y_ref.at[pl.ds(src, N_CHUNKS)], buf.at[pl.ds(r * ROW_PITCH, N_CHUNKS)],
                                  sems.at[k]).start(priority=k)
    for k, buf in enumerate((y0_ref, y1_ref)):
        done = buf.at[pl.ds(0, tm * N_CHUNKS)]
        pltpu.make_async_copy(done, done, sems.at[k]).wait()
    g = g_ref[...]
    o_ref[...] = (h_ref[...] + g[:, 0:1] * _load_gathered_rows(y0_ref, tm)
                  + g[:, 1:2] * _load_gathered_rows(y1_ref, tm))


def _combine(pos, h, gates, y):
    t = h.shape[0]
    tm = min(t, COMBINE_ROW_TILE)
    return pl.pallas_call(
        functools.partial(_combine_kernel, t=t, tm=tm),
        grid_spec=pltpu.PrefetchScalarGridSpec(
            num_scalar_prefetch=1,
            grid=(t // tm,),
            in_specs=[pl.BlockSpec((tm, D_MODEL), lambda i, pos: (i, 0)),
                      pl.BlockSpec((tm, LANES), lambda i, pos: (i, 0)),
                      pl.BlockSpec(memory_space=pl.ANY)],
            out_specs=pl.BlockSpec((tm, D_MODEL), lambda i, pos: (i, 0)),
            scratch_shapes=[pltpu.VMEM((tm * ROW_PITCH, LANES), F32), pltpu.VMEM((tm * ROW_PITCH, LANES), F32),
                            pltpu.SemaphoreType.DMA((2,))],
        ),
        out_shape=jax.ShapeDtypeStruct((t, D_MODEL), F32),
        compiler_params=_params(1),
    )(pos, h, gates, y)


def _hier_moe(h, gain, wr, br, w_gate, w_up, w_down, layer, prec):
    t = h.shape[0]
    tm = MOE_ROW_TILE if t >= MOE_SMALL_BELOW else MOE_ROW_TILE_SMALL
    n_tiles = (2 * t) // tm + N_EXPERTS
    assert n_tiles <= LANES
    x2, idx, gates = _router(h, gain, wr, br, prec)
    e_flat = idx[:, :2].T.reshape(1, 2 * t)
    pos, te, nv = _plan(e_flat, tm)
    pos = pos.reshape(2 * t)
    te = te[0, :n_tiles]
    nv = nv[0, :1]
    xs = _dispatch(pos, te, nv, x2, tm)
    y = _moe_ffn(te, nv, xs, w_gate, w_up, w_down, layer, tm, prec)
    return _combine(pos, h, gates, y)


def _fox_cumsum_kernel(lft_ref, lf_ref, ct_ref, c_ref, *, s, blk):
    r = lax.broadcasted_iota(jnp.int32, (blk, blk), 0)
    c = lax.broadcasted_iota(jnp.int32, (blk, blk), 1)
    upper = (r <= c).astype(F32)
    lower = (r >= c).astype(F32)
    carry_t = jnp.zeros((2 * SUBLANES, 1), F32)
    carry = jnp.zeros((1, LANES), F32)
    for b in range(s // blk):
        sl = slice(b * blk, (b + 1) * blk)
        ct = _dot3_left(lft_ref[:, sl], upper) + carry_t
        ct_ref[:, sl] = ct
        carry_t = ct[:, blk - 1:blk]
        cc = _dot3_right(lower, lf_ref[sl, :]) + carry
        c_ref[sl, :] = cc
        carry = cc[blk - 1:blk, :]


def _fox_cumsum(lft, lf, nb, s):
    t = lf.shape[0]
    blk = min(s, CUMSUM_BLOCK)
    return pl.pallas_call(
        functools.partial(_fox_cumsum_kernel, s=s, blk=blk),
        grid=(nb,),
        in_specs=[pl.BlockSpec((2 * SUBLANES, s), lambda b: (0, b)), pl.BlockSpec((s, LANES), lambda b: (b, 0))],
        out_specs=[pl.BlockSpec((2 * SUBLANES, s), lambda b: (0, b)), pl.BlockSpec((s, LANES), lambda b: (b, 0))],
        out_shape=[jax.ShapeDtypeStruct((2 * SUBLANES, t), F32), jax.ShapeDtypeStruct((t, LANES), F32)],
        compiler_params=_params(1),
    )(lft, lf)


def _fox_prompt_kernel(q_ref, k_ref, v_ref, ct_ref, c_ref, o_ref, *, s, tq):
    h = pl.program_id(1)
    lane = lax.broadcasted_iota(jnp.int32, (s, LANES), 1)
    c_col = jnp.sum(jnp.where(lane == h, c_ref[...], 0.0), axis=1, keepdims=True) * LOG2E
    c_row = ct_ref[pl.ds(h, 1), :] * LOG2E
    ri = lax.broadcasted_iota(jnp.int32, (tq, tq), 0)
    ci = lax.broadcasted_iota(jnp.int32, (tq, tq), 1)
    causal = ci <= ri
    for qi in range(s // tq):
        qs = slice(qi * tq, (qi + 1) * tq)
        q = q_ref[qs, :].astype(F32)
        cq = c_col[qs, :]
        m = jnp.full((tq, 1), -jnp.inf, F32)
        l = jnp.zeros((tq, 1), F32)
        acc = jnp.zeros((tq, HEAD_DIM), F32)
        for kj in range(qi + 1):
            ks = slice(kj * tq, (kj + 1) * tq)
            sc = lax.dot_general(q, k_ref[ks, :], _NT, preferred_element_type=F32) - c_row[:, ks]
            if kj == qi:
                sc = jnp.where(causal, sc, -jnp.inf)
            m_new = jnp.maximum(m, jnp.max(sc, axis=-1, keepdims=True) + cq)
            alpha = jnp.exp2(m - m_new)
            p = jnp.exp2(sc - (m_new - cq))
            l = alpha * l + jnp.sum(p, axis=-1, keepdims=True)
            acc = alpha * acc + _dot(p, v_ref[ks, :])
            m = m_new
        o_ref[qs, :] = (acc / l).astype(o_ref.dtype)


def _fox_prompt(q, k, v, ct, c, nb, s):
    t = q.shape[0]
    tq = min(s, ATTN_TILE)
    head_spec = pl.BlockSpec((s, HEAD_DIM), lambda b, h: (b, h))
    return pl.pallas_call(
        functools.partial(_fox_prompt_kernel, s=s, tq=tq),
        grid=(nb, MAIN_HEADS),
        in_specs=[head_spec, head_spec, head_spec,
                  pl.BlockSpec((2 * SUBLANES, s), lambda b, h: (0, b)),
                  pl.BlockSpec((s, LANES), lambda b, h: (b, 0))],
        out_specs=head_spec,
        out_shape=jax.ShapeDtypeStruct((t, MAIN_WIDTH), _matmul_only_dtype(None)),
        compiler_params=_params(2),
    )(q, k, v, ct, c)


def _fox_past_bias_kernel(pt_ref, lfc_ref, o_ref, buf_ref, sem, *, n_pages):
    b = pl.program_id(0)

    def page_copy(p):
        page = pt_ref[b * n_pages + p]
        return pltpu.make_async_copy(lfc_ref.at[:, pl.ds(page, 1), :], buf_ref.at[:, pl.ds(p, 1), :], sem)

    def start(p, carry):
        page_copy(p).start()
        return carry

    def wait(p, carry):
        page_copy(p).wait()
        return carry

    lax.fori_loop(0, n_pages, start, 0)
    r = lax.broadcasted_iota(jnp.int32, (PAGE_SIZE, PAGE_SIZE), 0)
    c = lax.broadcasted_iota(jnp.int32, (PAGE_SIZE, PAGE_SIZE), 1)
    after_in_page = (r > c).astype(F32)
    pr = lax.broadcasted_iota(jnp.int32, (n_pages, n_pages), 0)
    pc = lax.broadcasted_iota(jnp.int32, (n_pages, n_pages), 1)
    later_pages = (pc > pr).astype(F32)
    lax.fori_loop(0, n_pages, wait, 0)
    for h in range(MAIN_HEADS):
        lp = buf_ref[h]
        tot = jnp.broadcast_to(jnp.sum(lp, axis=1, keepdims=True), lp.shape)
        o_ref[h] = _dot3_left(lp, after_in_page) + _dot3_right(later_pages, tot)


def _fox_past_bias(page_table_flat, cache_logf_t, nb, n_pages):
    return pl.pallas_call(
        functools.partial(_fox_past_bias_kernel, n_pages=n_pages),
        grid_spec=pltpu.PrefetchScalarGridSpec(
            num_scalar_prefetch=1,
            grid=(nb,),
            in_specs=[pl.BlockSpec(memory_space=pl.ANY)],
            out_specs=pl.BlockSpec((None, MAIN_HEADS, n_pages, PAGE_SIZE), lambda b, pt: (b, 0, 0, 0)),
            scratch_shapes=[pltpu.VMEM((MAIN_HEADS, n_pages, PAGE_SIZE), F32), pltpu.SemaphoreType.DMA(())],
        ),
        out_shape=jax.ShapeDtypeStruct((nb, MAIN_HEADS, n_pages, PAGE_SIZE), F32),
        compiler_params=_params(1),
    )(page_table_flat, cache_logf_t)


def _fox_sample_kernel(*refs, ppb, n_steps, valid):
    pt_ref, q_ref, kn_ref, vn_ref, lf_ref, bias_ref = refs[:6]
    k_refs = refs[6:6 + ppb]
    v_refs = refs[6 + ppb:6 + 2 * ppb]
    o_ref, m_ref, l_ref, acc_ref = refs[6 + 2 * ppb:]
    g = pl.program_id(1)
    rows = SAMPLE_ROWS

    @pl.when(g == 0)
    def _():
        m_ref[...] = jnp.full(m_ref.shape, -jnp.inf, F32)
        l_ref[...] = jnp.zeros(l_ref.shape, F32)
        acc_ref[...] = jnp.zeros(acc_ref.shape, F32)

    r8 = lax.broadcasted_iota(jnp.int32, (rows, rows), 0)
    c8 = lax.broadcasted_iota(jnp.int32, (rows, rows), 1)
    new_mask = (c8 <= r8) & (c8 < valid)
    c_new = _dot3_right(new_mask.astype(F32), lf_ref[...])

    def qk3(q_hi, q_lo, k):
        a = _dot_nt(jnp.concatenate([q_hi, q_lo], axis=0), k)
        return a[:rows] + a[rows:]

    def pv3(p, v):
        p_hi, p_lo = _split_hi_lo(p)
        a = _dot(jnp.concatenate([p_hi, p_lo], axis=0), v)
        return a[:rows] + a[rows:]

    all_scores = []
    for h in range(MAIN_HEADS):
        q_hi, q_lo = _split_hi_lo(q_ref[:, h * HEAD_DIM:(h + 1) * HEAD_DIM])
        cq = c_new[:, h:h + 1]
        all_scores.append([qk3(q_hi, q_lo, k_refs[u][h]) + cq + bias_ref[h, u:u + 1, :]
                           for u in range(ppb)])
    all_probs, all_alpha = [], []
    for h in range(MAIN_HEADS):
        m_old = m_ref[h, :, 0:1]
        m_new = m_old
        for sc in all_scores[h]:
            m_new = jnp.maximum(m_new, jnp.max(sc, axis=-1, keepdims=True))
        alpha = jnp.exp(m_old - m_new)
        probs = [jnp.exp(sc - m_new) for sc in all_scores[h]]
        l_new = alpha * l_ref[h, :, 0:1]
        for p in probs:
            l_new = l_new + jnp.sum(p, axis=-1, keepdims=True)
        m_ref[h] = jnp.broadcast_to(m_new, (rows, LANES))
        l_ref[h] = jnp.broadcast_to(l_new, (rows, LANES))
        all_probs.append(probs)
        all_alpha.append(alpha)
    for h in range(MAIN_HEADS):
        sl = slice(h * HEAD_DIM, (h + 1) * HEAD_DIM)
        acc = all_alpha[h] * acc_ref[:, sl]
        for u in range(ppb):
            acc = acc + pv3(all_probs[h][u], v_refs[u][h])
        acc_ref[:, sl] = acc

    @pl.when(g == n_steps - 1)
    def _():
        for h in range(MAIN_HEADS):
            sl = slice(h * HEAD_DIM, (h + 1) * HEAD_DIM)
            qh = q_ref[:, sl]
            cq = c_new[:, h:h + 1]
            cq_row = jnp.sum(jnp.where(r8 == c8, jnp.broadcast_to(cq, (rows, rows)), 0.0), axis=0, keepdims=True)
            sn = _dot_nt(qh, kn_ref[:, sl], HIGHEST) + cq - cq_row
            sn = jnp.where(new_mask, sn, -jnp.inf)
            m_old = m_ref[h, :, 0:1]
            m_new = jnp.maximum(m_old, jnp.max(sn, axis=-1, keepdims=True))
            alpha = jnp.exp(m_old - m_new)
            p = jnp.exp(sn - m_new)
            l_new = alpha * l_ref[h, :, 0:1] + jnp.sum(p, axis=-1, keepdims=True)
            acc = alpha * acc_ref[:, sl] + _dot(p, vn_ref[:, sl], HIGHEST)
            o_ref[:, sl] = acc / l_new


def _fox_sample(page_table_flat, q, k_new, v_new, lf, bias, cache_k_t, cache_v_t, nb, n_pages, valid):
    ppb = PAGES_PER_STEP if n_pages % PAGES_PER_STEP == 0 else n_pages
    n_steps = n_pages // ppb
    row_spec = pl.BlockSpec((SAMPLE_ROWS, MAIN_WIDTH), lambda b, g, pt: (b, 0))

    def page_spec(u):
        return pl.BlockSpec((None, MAIN_HEADS, PAGE_SIZE, HEAD_DIM),
                            lambda b, g, pt, u=u: (pt[b * n_pages + g * ppb + u], 0, 0, 0))

    in_specs = [row_spec, row_spec, row_spec,
                pl.BlockSpec((SAMPLE_ROWS, LANES), lambda b, g, pt: (b, 0)),
                pl.BlockSpec((None, MAIN_HEADS, ppb, PAGE_SIZE), lambda b, g, pt: (b, 0, g, 0))]
    in_specs += [page_spec(u) for u in range(ppb)] + [page_spec(u) for u in range(ppb)]
    return pl.pallas_call(
        functools.partial(_fox_sample_kernel, ppb=ppb, n_steps=n_steps, valid=valid),
        grid_spec=pltpu.PrefetchScalarGridSpec(
            num_scalar_prefetch=1,
            grid=(nb, n_steps),
            in_specs=in_specs,
            out_specs=row_spec,
            scratch_shapes=[pltpu.VMEM((MAIN_HEADS, SAMPLE_ROWS, LANES), F32),
                            pltpu.VMEM((MAIN_HEADS, SAMPLE_ROWS, LANES), F32),
                            pltpu.VMEM((SAMPLE_ROWS, MAIN_WIDTH), F32)],
        ),
        out_shape=jax.ShapeDtypeStruct(q.shape, F32),
        compiler_params=_params(2),
    )(page_table_flat, q, k_new, v_new, lf, bias, *([cache_k_t] * ppb), *([cache_v_t] * ppb))


def _trunk(x, nb, rows, valid, mem_k, mem_v, fox, w, prec):
    u, v, qm = _norm_matmul(x, w["norm1_gain"][0:1], w["w_in_a"], 0,
                            [(MAIN_WIDTH, "gelu_mm", 0), (MAIN_WIDTH, "gelu", 0), (MEM_WIDTH, "headnorm_mm", 0)],
                            w["mem_q_gain"][0:1], prec=prec)
    y_main, v_rows = _gmlp(u, v, w["v_gain_a"][0:1], w["w_mix"], w["b_mix"], w["gmlp_rows"], w["gmlp_grp"], valid,
                           w["gmlp_blocks_per_seq"], prec)
    y_mem = _mem_attn(qm, mem_k, mem_v, 0, nb, rows, prec)
    h = _out_proj(y_main, y_mem, w["w_out"], 0, x, prec)
    h = _hier_moe(h, w["norm2_gain"][0:1], w["w_router"][0], w["b_router"][0], w["w_gate"], w["w_up"], w["w_down"],
                  0, prec)
    k, vv, lf, lft = _norm_matmul(h, w["kv_norm_gain"], w["w_kv_shared"], 0,
                                  [(MAIN_WIDTH, "headnorm", 0), (MAIN_WIDTH, "plain", 0)],
                                  w["k_gain_shared"], forget=w["forget"], prec=prec)
    q, qm = _norm_matmul(h, w["norm1_gain"][1:2], w["w_in_b"], 0,
                         [(MAIN_WIDTH, "headnorm_mm", 0), (MEM_WIDTH, "headnorm_mm", 1)],
                         jnp.concatenate([w["q_gain_b"][0:1] * w["q_logit_scale"], w["mem_q_gain"][1:2]], axis=0),
                         prec=prec)
    y_main = fox(q, k, vv, lf, lft)
    y_mem = _mem_attn(qm, mem_k, mem_v, 1, nb, rows, prec)
    h = _out_proj(y_main, y_mem, w["w_out"], 1, h, prec)
    h = _hier_moe(h, w["norm2_gain"][1:2], w["w_router"][1], w["b_router"][1], w["w_gate"], w["w_up"], w["w_down"],
                  1, prec)
    return h, k, vv, lf, v_rows


def kernel(x_prompt, x_sample, cache_k, cache_v, cache_logf, cache_mem_k, cache_mem_v, page_table, mem_prompt, norm1_gain, norm2_gain, w_in_a, v_gain_a, w_s_a, b_s_a, w_in_b, q_gain_b, kv_norm_gain, w_kv_shared, b_forget, k_gain_shared, mem_norm_gain, w_mem_kv, mem_q_gain, mem_k_gain, w_out, w_router_group, b_router_group, w_router_expert, b_router_expert, w_gate, w_up, w_down):
    batch, seq, _ = x_prompt.shape
    dec_batch, dec_seq, _ = x_sample.shape
    n_phys = cache_k.shape[0]
    n_pages = page_table.shape[1]
    depth = norm1_gain.shape[0]
    assert depth == 2 and dec_seq <= SAMPLE_ROWS and seq % CHUNK == 0

    n_route = N_GROUPS + N_EXPERTS
    w_router = jnp.pad(jnp.concatenate([w_router_group, w_router_expert], axis=-1),
                       ((0, 0), (0, 0), (0, LANES - n_route)))
    b_router = jnp.pad(jnp.concatenate([b_router_group, b_router_expert], axis=-1),
                       ((0, 0), (0, LANES - n_route)))[:, None, :]
    w_f = w_kv_shared[:, 2 * MAIN_WIDTH:]
    forget = (jnp.pad(w_f, ((0, 0), (0, LANES - MAIN_HEADS))),
              jnp.pad(b_forget, (0, LANES - MAIN_HEADS))[None, :],
              jnp.pad(b_forget, (0, 2 * SUBLANES - MAIN_HEADS))[:, None])
    common = dict(norm1_gain=norm1_gain, norm2_gain=norm2_gain, w_in_a=w_in_a, v_gain_a=v_gain_a, w_in_b=w_in_b,
                  q_gain_b=q_gain_b, kv_norm_gain=kv_norm_gain[None, :], w_kv_shared=w_kv_shared,
                  k_gain_shared=k_gain_shared[None, :], mem_q_gain=mem_q_gain, w_out=w_out,
                  w_router=w_router, b_router=b_router, w_gate=w_gate, w_up=w_up, w_down=w_down, forget=forget)

    t_p = batch * seq
    mem_tok = mem_prompt.reshape(batch * N_MEM, D_MODEL)
    mem_k_layers, mem_v_layers = [], []
    for l in range(depth):
        mk, mv = _norm_matmul(mem_tok, mem_norm_gain[l:l + 1], w_mem_kv, l,
                              [(MEM_WIDTH, "headnorm", 0), (MEM_WIDTH, "plain", 0)], mem_k_gain[l:l + 1])
        mem_k_layers.append(mk.reshape(batch, N_MEM, MEM_WIDTH))
        mem_v_layers.append(mv.reshape(batch, N_MEM, MEM_WIDTH))
    mem_k_p = jnp.stack(mem_k_layers)
    mem_v_p = jnp.stack(mem_v_layers)

    def fox_prompt(q, k, v, lf, lft):
        ct, c = _fox_cumsum(lft, lf, batch, seq)
        return _fox_prompt(q, k, v, ct, c, batch, seq)

    w_p = dict(common, w_mix=w_s_a[0], b_mix=b_s_a[0].T, gmlp_rows=CHUNK, gmlp_grp=CHUNK,
               gmlp_blocks_per_seq=seq // CHUNK, q_logit_scale=ATTN_SCALE * LOG2E)
    y_p, k_p, v_p, lf_p, vrows_p = _trunk(x_prompt.reshape(t_p, D_MODEL), batch, seq, seq,
                                          mem_k_p, mem_v_p, fox_prompt, w_p, None)

    rows = SAMPLE_ROWS
    t_s = dec_batch * rows
    x_s = jnp.pad(x_sample, ((0, 0), (0, rows - dec_seq), (0, 0))).reshape(t_s, D_MODEL)
    pt_flat = page_table.reshape(dec_batch * n_pages)
    bias = _fox_past_bias(pt_flat, jnp.transpose(cache_logf, (2, 0, 1)), dec_batch, n_pages)
    cache_k_t = jnp.transpose(cache_k, (0, 2, 1, 3))
    cache_v_t = jnp.transpose(cache_v, (0, 2, 1, 3))

    def fox_sample(q, k, v, lf, lft):
        return _fox_sample(pt_flat, q, k, v, lf, bias, cache_k_t, cache_v_t, dec_batch, n_pages, dec_seq)

    w_s = dict(common, w_mix=jnp.tile(w_s_a[0][:, :rows, :rows], (1, dec_batch, dec_batch)),
               b_mix=jnp.tile(b_s_a[0][:, :rows].T, (dec_batch, 1)), gmlp_rows=t_s, gmlp_grp=rows,
               gmlp_blocks_per_seq=1, q_logit_scale=ATTN_SCALE)
    mem_k_s = cache_mem_k.reshape(depth, dec_batch, N_MEM, MEM_WIDTH)
    mem_v_s = cache_mem_v.reshape(depth, dec_batch, N_MEM, MEM_WIDTH)
    y_s, k_s, v_s, lf_s, vrows_s = _trunk(x_s, dec_batch, rows, dec_seq, mem_k_s, mem_v_s, fox_sample, w_s, SPLIT3)

    def unpad(a, *tail):
        return a.reshape((dec_batch, rows) + tail)[:, :dec_seq]

    return (y_p.reshape(batch, seq, D_MODEL),
            unpad(y_s, D_MODEL),
            k_p.reshape(batch, seq, MAIN_HEADS, HEAD_DIM),
            v_p.reshape(batch, seq, MAIN_HEADS, HEAD_DIM),
            lf_p[:, :MAIN_HEADS].reshape(batch, seq, MAIN_HEADS),
            unpad(k_s, MAIN_HEADS, HEAD_DIM),
            unpad(v_s, MAIN_HEADS, HEAD_DIM),
            unpad(lf_s[:, :MAIN_HEADS], MAIN_HEADS),
            mem_k_p.reshape(depth, batch, N_MEM, MEM_HEADS, HEAD_DIM),
            mem_v_p.reshape(depth, batch, N_MEM, MEM_HEADS, HEAD_DIM),
            vrows_p.reshape(batch, CHUNK, MAIN_WIDTH)[None],
            unpad(vrows_s, MAIN_WIDTH)[None])
```

```python
import functools

import jax
import jax.numpy as jnp
from jax import lax
from jax.experimental import pallas as pl
from jax.experimental.pallas import tpu as pltpu

D_MODEL = 2048
HEAD_DIM = 128
MAIN_HEADS = 12
MEM_HEADS = 4
MAIN_WIDTH = MAIN_HEADS * HEAD_DIM
MEM_WIDTH = MEM_HEADS * HEAD_DIM
N_MEM = 256
N_GROUPS = 4
EXPERTS_PER_GROUP = 4
N_EXPERTS = N_GROUPS * EXPERTS_PER_GROUP
D_EXPERT = 512
CHUNK = 128
PAGE_SIZE = 128
EPS = 1e-6
ATTN_SCALE = HEAD_DIM ** -0.5
LOG2E = 1.4426950408889634

LANES = 128
SUBLANES = 8
VMEM_LIMIT_BYTES = 56 * 1024 * 1024
SAMPLE_ROWS = SUBLANES
N_CHUNKS = D_MODEL // LANES
ROW_PITCH = 24

PROJ_ROW_TILE = 1024
PROJ_ROW_TILE_BF16 = 2048
PROJ_COL_TILE = 512
ROUTER_ROW_TILE = 512
MOE_ROW_TILE = 256
MOE_ROW_TILE_SMALL = 16
MOE_SMALL_BELOW = 1024
COMBINE_ROW_TILE = 256
ATTN_TILE = 512
MEM_ATTN_ROW_TILE = 1024
CUMSUM_BLOCK = 256
PLAN_BLOCK = 512
PAGES_PER_STEP = SUBLANES
SCALAR_UNROLL = 32

F32 = jnp.float32
_NT = (((1,), (1,)), ((), ()))


def _params(n_axes):
    return pltpu.CompilerParams(dimension_semantics=("arbitrary",) * n_axes,
                                vmem_limit_bytes=VMEM_LIMIT_BYTES)


HIGHEST = lax.Precision.HIGHEST


SPLIT3 = "split3"


def _dot(a, b, prec=None):
    if prec == SPLIT3:
        return _dot_split3(a, b, _dot)
    return jnp.dot(a, b, preferred_element_type=F32, precision=prec)


def _dot_nt(a, b, prec=None):
    if prec == SPLIT3:
        return _dot_split3(a, b, _dot_nt)
    return lax.dot_general(a, b, _NT, preferred_element_type=F32, precision=prec)


def _dot_tn(a, b, prec=None):
    dims = (((0,), (1,)), ((), ()))
    if prec == SPLIT3:
        a_hi, a_lo = _split_hi_lo(a)
        b_hi, b_lo = _split_hi_lo(b)
        return _dot_tn(a_hi, b_hi) + _dot_tn(a_lo, b_hi) + _dot_tn(a_hi, b_lo)
    return lax.dot_general(a, b, dims, preferred_element_type=F32, precision=prec)


def _dot_split3(a, b, dot):
    a_hi, a_lo = _split_hi_lo(a)
    b_hi, b_lo = _split_hi_lo(b)
    m = a.shape[0]
    s = dot(jnp.concatenate([a_hi, a_lo], axis=0), b_hi)
    return s[:m] + s[m:] + dot(a_hi, b_lo)


def _matmul_only_dtype(prec):
    return jnp.bfloat16 if prec is None else F32


def _split_hi_lo(x):
    hi = x.astype(jnp.bfloat16).astype(F32)
    return hi, x - hi


def _split3(x):
    hi = x.astype(jnp.bfloat16).astype(F32)
    r = x - hi
    mid = r.astype(jnp.bfloat16).astype(F32)
    lo = (r - mid).astype(jnp.bfloat16).astype(F32)
    return hi, mid, lo


def _dot3_right(a_exact, x):
    hi, mid, lo = _split3(x)
    return _dot(a_exact, hi) + _dot(a_exact, mid) + _dot(a_exact, lo)


def _dot3_left(x, b_exact):
    hi, mid, lo = _split3(x)
    return _dot(hi, b_exact) + _dot(mid, b_exact) + _dot(lo, b_exact)


def _store_chunk_rows(ref, x):
    rows = x.shape[0]
    for c in range(N_CHUNKS):
        ref[pl.ds(c, rows, stride=N_CHUNKS), :] = x[:, c * LANES:(c + 1) * LANES]


def _load_gathered_rows(ref, rows):
    return jnp.concatenate([ref[pl.ds(c, rows, stride=ROW_PITCH), :] for c in range(N_CHUNKS)], axis=1)


def _head_norm(z, gain_row):
    return z * lax.rsqrt(jnp.mean(z * z, axis=-1, keepdims=True) + EPS) * gain_row


def _norm_matmul_kernel(*refs, segs, tn, with_forget, prec):
    it = iter(refs)
    x_ref, g_ref, w_ref, hg_ref = next(it), next(it), next(it), next(it)
    if with_forget:
        wf_ref, bfr_ref, bfc_ref = next(it), next(it), next(it)
    out_refs = [next(it) for _ in segs]
    if with_forget:
        lf_ref, lft_ref = next(it), next(it)
    xn_ref = next(it)
    j = pl.program_id(1)

    def project(xn, o_ref, kind, grow):
        z = _dot(xn, w_ref[...], prec)
        if kind == "plain":
            o_ref[...] = z
        elif kind in ("gelu", "gelu_mm"):
            o_ref[...] = jax.nn.gelu(z).astype(o_ref.dtype)
        else:
            gain = hg_ref[grow:grow + 1, :]
            for c in range(tn // HEAD_DIM):
                sl = slice(c * HEAD_DIM, (c + 1) * HEAD_DIM)
                o_ref[:, sl] = _head_norm(z[:, sl], gain).astype(o_ref.dtype)

    @pl.when(j == 0)
    def _first():
        x = x_ref[...]
        xn = x * lax.rsqrt(jnp.mean(x * x, axis=-1, keepdims=True) + EPS) * g_ref[...]
        xn_ref[...] = xn
        if with_forget:
            lf_ref[...] = jax.nn.log_sigmoid(_dot(xn, wf_ref[...], prec) + bfr_ref[...])
            zt = _dot_tn(wf_ref[...], xn, prec)[:2 * SUBLANES, :]
            lft_ref[...] = jax.nn.log_sigmoid(zt + bfc_ref[...])
        project(xn, out_refs[0], segs[0][1], segs[0][2])

    lo = 0
    for (ncols, kind, grow), o_ref in zip(segs, out_refs):
        nt = ncols // tn

        @pl.when((j >= max(lo, 1)) & (j < lo + nt))
        def _rest(o_ref=o_ref, kind=kind, grow=grow):
            project(xn_ref[...], o_ref, kind, grow)

        lo += nt


def _norm_matmul(x, gain, w, layer, segs, head_gains, forget=None, prec=None):
    t = x.shape[0]
    tm = min(t, PROJ_ROW_TILE)
    tn = PROJ_COL_TILE
    n_tiles = sum(s[0] for s in segs) // tn
    with_forget = forget is not None
    in_specs = [
        pl.BlockSpec((tm, D_MODEL), lambda i, j: (i, 0)),
        pl.BlockSpec((1, D_MODEL), lambda i, j: (0, 0)),
        pl.BlockSpec((None, D_MODEL, tn), lambda i, j: (layer, 0, j)) if w.ndim == 3
        else pl.BlockSpec((D_MODEL, tn), lambda i, j: (0, j)),
        pl.BlockSpec(head_gains.shape, lambda i, j: (0, 0)),
    ]
    args = [x, gain, w, head_gains]
    if with_forget:
        wf, bfr, bfc = forget
        in_specs += [pl.BlockSpec(wf.shape, lambda i, j: (0, 0)),
                     pl.BlockSpec(bfr.shape, lambda i, j: (0, 0)), pl.BlockSpec(bfc.shape, lambda i, j: (0, 0))]
        args += [wf, bfr, bfc]
    out_shape, out_specs = [], []
    lo = 0
    for ncols, kind, _ in segs:
        nt = ncols // tn
        out_shape.append(jax.ShapeDtypeStruct((t, ncols), _matmul_only_dtype(prec) if kind.endswith("_mm") else F32))
        out_specs.append(pl.BlockSpec((tm, tn), lambda i, j, lo=lo, nt=nt: (i, jnp.clip(j - lo, 0, nt - 1))))
        lo += nt
    if with_forget:
        out_shape += [jax.ShapeDtypeStruct((t, LANES), F32), jax.ShapeDtypeStruct((2 * SUBLANES, t), F32)]
        out_specs += [pl.BlockSpec((tm, LANES), lambda i, j: (i, 0)),
                      pl.BlockSpec((2 * SUBLANES, tm), lambda i, j: (0, i))]
    return pl.pallas_call(
        functools.partial(_norm_matmul_kernel, segs=tuple(segs), tn=tn, with_forget=with_forget, prec=prec),
        grid=(t // tm, n_tiles),
        in_specs=in_specs,
        out_specs=out_specs,
        out_shape=out_shape,
        scratch_shapes=[pltpu.VMEM((tm, D_MODEL), F32)],
        compiler_params=_params(2),
    )(*args)


def _gmlp_kernel(u_ref, v_ref, vg_ref, w_ref, b_ref, y_ref, vn_ref, *, rows, grp, valid, prec):
    v = v_ref[...]
    vn = v * lax.rsqrt(jnp.mean(v * v, axis=-1, keepdims=True) + EPS) * vg_ref[...]
    vn_ref[...] = vn
    r = lax.broadcasted_iota(jnp.int32, (rows, rows), 0)
    c = lax.broadcasted_iota(jnp.int32, (rows, rows), 1)
    shift = grp.bit_length() - 1
    allowed = ((c & (grp - 1)) <= (r & (grp - 1))) & ((c & (grp - 1)) < valid)
    if grp < rows:
        allowed = allowed & ((r >> shift) == (c >> shift))
    for g in range(MAIN_HEADS):
        sl = slice(g * HEAD_DIM, (g + 1) * HEAD_DIM)
        w = jnp.where(allowed, w_ref[g], 0.0)
        mixed = _dot(w, vn[:, sl], prec) + b_ref[:, g:g + 1]
        y_ref[:, sl] = (u_ref[:, sl] * mixed).astype(y_ref.dtype)


def _gmlp(u, v, v_gain, w_mix, b_mix, rows, grp, valid, blocks_per_seq, prec):
    t = u.shape[0]
    return pl.pallas_call(
        functools.partial(_gmlp_kernel, rows=rows, grp=grp, valid=valid, prec=prec),
        grid=(t // rows,),
        in_specs=[
            pl.BlockSpec((rows, MAIN_WIDTH), lambda i: (i, 0)),
            pl.BlockSpec((rows, MAIN_WIDTH), lambda i: (i, 0)),
            pl.BlockSpec((1, MAIN_WIDTH), lambda i: (0, 0)),
            pl.BlockSpec((MAIN_HEADS, rows, rows), lambda i: (0, 0, 0)),
            pl.BlockSpec((rows, MAIN_HEADS), lambda i: (0, 0)),
        ],
        out_specs=[pl.BlockSpec((rows, MAIN_WIDTH), lambda i: (i, 0)),
                   pl.BlockSpec((rows, MAIN_WIDTH), lambda i: (i // blocks_per_seq, 0))],
        out_shape=[jax.ShapeDtypeStruct((t, MAIN_WIDTH), _matmul_only_dtype(prec)),
                   jax.ShapeDtypeStruct((t // blocks_per_seq, MAIN_WIDTH), F32)],
        compiler_params=_params(1),
    )(u, v, v_gain, w_mix, b_mix)


def _mem_attn_kernel(q_ref, k_ref, v_ref, o_ref, *, prec):
    for h in range(MEM_HEADS):
        sl = slice(h * HEAD_DIM, (h + 1) * HEAD_DIM)
        s = _dot_nt(q_ref[:, sl].astype(F32), k_ref[:, sl], prec) * ATTN_SCALE
        e = jnp.exp(s - jnp.max(s, axis=-1, keepdims=True))
        p = e / jnp.sum(e, axis=-1, keepdims=True)
        o_ref[:, sl] = _dot(p, v_ref[:, sl], prec).astype(o_ref.dtype)


def _mem_attn(q, k, v, layer, nb, rows_per_batch, prec):
    tq = min(rows_per_batch, MEM_ATTN_ROW_TILE)
    nq = rows_per_batch // tq
    kv_spec = pl.BlockSpec((None, None, N_MEM, MEM_WIDTH), lambda b, i: (layer, b, 0, 0))
    return pl.pallas_call(
        functools.partial(_mem_attn_kernel, prec=prec),
        grid=(nb, nq),
        in_specs=[pl.BlockSpec((tq, MEM_WIDTH), lambda b, i: (b * nq + i, 0)), kv_spec, kv_spec],
        out_specs=pl.BlockSpec((tq, MEM_WIDTH), lambda b, i: (b * nq + i, 0)),
        out_shape=jax.ShapeDtypeStruct(q.shape, _matmul_only_dtype(prec)),
        compiler_params=_params(2),
    )(q, k, v)


def _out_proj_kernel(ym_ref, ymem_ref, w1_ref, w2_ref, h_ref, o_ref, *, prec):
    w1 = w1_ref[...].astype(ym_ref.dtype)
    w2 = w2_ref[...].astype(ymem_ref.dtype)
    o_ref[...] = h_ref[...] + _dot(ym_ref[...], w1, prec) + _dot(ymem_ref[...], w2, prec)


def _out_proj(y_main, y_mem, w_out, layer, h, prec):
    t = h.shape[0]
    tm = min(t, PROJ_ROW_TILE_BF16 if y_main.dtype == jnp.bfloat16 else PROJ_ROW_TILE)
    tn = PROJ_COL_TILE
    return pl.pallas_call(
        functools.partial(_out_proj_kernel, prec=prec),
        grid=(t // tm, D_MODEL // tn),
        in_specs=[
            pl.BlockSpec((tm, MAIN_WIDTH), lambda i, j: (i, 0)),
            pl.BlockSpec((tm, MEM_WIDTH), lambda i, j: (i, 0)),
            pl.BlockSpec((None, MAIN_WIDTH, tn), lambda i, j: (layer, 0, j)),
            pl.BlockSpec((None, MEM_WIDTH, tn), lambda i, j: (layer, MAIN_WIDTH // MEM_WIDTH, j)),
            pl.BlockSpec((tm, tn), lambda i, j: (i, j)),
        ],
        out_specs=pl.BlockSpec((tm, tn), lambda i, j: (i, j)),
        out_shape=jax.ShapeDtypeStruct((t, D_MODEL), F32),
        compiler_params=_params(2),
    )(y_main, y_mem, w_out, w_out, h)


def _rms_norm_rows(x, gain_row):
    return x * lax.rsqrt(jnp.mean(x * x, axis=-1, keepdims=True) + EPS) * gain_row


def _router_kernel(h_ref, g_ref, wr_ref, br_ref, idx_ref, gate_ref, *, prec):
    xn = _rms_norm_rows(h_ref[...], g_ref[...])
    lg = _dot(xn, wr_ref[...], prec) + br_ref[...]
    lane = lax.broadcasted_iota(jnp.int32, lg.shape, 1)
    neg = -jnp.inf
    is_grp = lane < N_GROUPS
    gl = jnp.where(is_grp, lg, neg)
    ge = jnp.where(is_grp, jnp.exp(gl - jnp.max(gl, axis=-1, keepdims=True)), 0.0)
    pg = ge / jnp.sum(ge, axis=-1, keepdims=True)
    p_top = jnp.max(pg, axis=-1, keepdims=True)
    g_idx = jnp.min(jnp.where(is_grp & (pg == p_top), lane, LANES), axis=-1, keepdims=True)
    first = N_GROUPS + EXPERTS_PER_GROUP * g_idx
    in_grp = (lane >= first) & (lane < first + EXPERTS_PER_GROUP)
    e1 = jnp.max(jnp.where(in_grp, lg, neg), axis=-1, keepdims=True)
    i1 = jnp.min(jnp.where(in_grp & (lg == e1), lane, LANES), axis=-1, keepdims=True)
    rest = in_grp & (lane != i1)
    e2 = jnp.max(jnp.where(rest, lg, neg), axis=-1, keepdims=True)
    i2 = jnp.min(jnp.where(rest & (lg == e2), lane, LANES), axis=-1, keepdims=True)
    t2 = jnp.exp(e2 - e1)
    den = 1.0 + t2
    idx_ref[...] = jnp.where(lane == 0, i1 - N_GROUPS, jnp.where(lane == 1, i2 - N_GROUPS, 0))
    gate_ref[...] = jnp.where(lane == 0, p_top * (1.0 / den), jnp.where(lane == 1, p_top * (t2 / den), 0.0))


def _router(h, gain, wr, br, prec):
    t = h.shape[0]
    tm = min(t, ROUTER_ROW_TILE)
    return pl.pallas_call(
        functools.partial(_router_kernel, prec=prec),
        grid=(t // tm,),
        in_specs=[pl.BlockSpec((tm, D_MODEL), lambda i: (i, 0)), pl.BlockSpec((1, D_MODEL), lambda i: (0, 0)),
                  pl.BlockSpec((D_MODEL, LANES), lambda i: (0, 0)), pl.BlockSpec((1, LANES), lambda i: (0, 0))],
        out_specs=[pl.BlockSpec((tm, LANES), lambda i: (i, 0)), pl.BlockSpec((tm, LANES), lambda i: (i, 0))],
        out_shape=[jax.ShapeDtypeStruct((t, LANES), jnp.int32), jax.ShapeDtypeStruct((t, LANES), F32)],
        compiler_params=_params(1),
    )(h, gain, wr, br)


def _plan_kernel(e_ref, pos_ref, te_ref, nv_ref, rank_ref, *, t2, blk, tm):
    shift = tm.bit_length() - 1
    nblk = t2 // blk
    r = lax.broadcasted_iota(jnp.int32, (blk, blk), 0)
    c = lax.broadcasted_iota(jnp.int32, (blk, blk), 1)
    upper = (r <= c).astype(F32)
    sub = lax.broadcasted_iota(jnp.int32, (N_EXPERTS, blk), 0)
    carry = jnp.zeros((N_EXPERTS, 1), F32)
    for b in range(nblk):
        sl = slice(b * blk, (b + 1) * blk)
        oh = (sub == e_ref[:, sl]).astype(F32)
        cs = _dot(oh, upper) + carry
        rank_ref[:, sl] = jnp.sum(oh * (cs - 1.0), axis=0, keepdims=True)
        carry = cs[:, blk - 1:blk]
    counts = carry.astype(jnp.int32)
    padc = ((counts + (tm - 1)) >> shift) << shift
    sub1 = lax.broadcasted_iota(jnp.int32, (N_EXPERTS, 1), 0)
    off = jnp.zeros((N_EXPERTS, 1), jnp.int32)
    run = jnp.zeros((1, 1), jnp.int32)
    for e in range(N_EXPERTS):
        off = jnp.where(sub1 == e, run, off)
        run = run + padc[e:e + 1, :]
    ends = off + padc
    for b in range(nblk):
        sl = slice(b * blk, (b + 1) * blk)
        offv = jnp.sum(jnp.where(sub == e_ref[:, sl], off, 0), axis=0, keepdims=True)
        pos_ref[:, sl] = rank_ref[:, sl].astype(jnp.int32) + offv
    lane = lax.broadcasted_iota(jnp.int32, (N_EXPERTS, LANES), 1)
    lane1 = lax.broadcasted_iota(jnp.int32, (1, LANES), 1)
    nv = run >> shift
    te = jnp.sum((ends <= lane * tm).astype(jnp.int32), axis=0, keepdims=True)
    te_last = jnp.sum((ends <= (nv - 1) * tm).astype(jnp.int32), axis=0, keepdims=True)
    te_ref[...] = jnp.minimum(jnp.where(lane1 >= nv, te_last, te), N_EXPERTS - 1)
    nv_ref[...] = jnp.broadcast_to(nv, (1, LANES))


def _plan(e_flat, tm):
    t2 = e_flat.shape[1]
    blk = min(t2, PLAN_BLOCK)
    return pl.pallas_call(
        functools.partial(_plan_kernel, t2=t2, blk=blk, tm=tm),
        out_shape=[jax.ShapeDtypeStruct((1, t2), jnp.int32), jax.ShapeDtypeStruct((1, LANES), jnp.int32),
                   jax.ShapeDtypeStruct((1, LANES), jnp.int32)],
        scratch_shapes=[pltpu.VMEM((1, t2), F32)],
        compiler_params=pltpu.CompilerParams(vmem_limit_bytes=VMEM_LIMIT_BYTES),
    )(e_flat)


def _dispatch_kernel(pos_ref, te_ref, nv_ref, h_ref, g_ref, xs_ref, x_ref, zero_ref, fill_sem, sems,
                     *, t, tm, td, n_tiles):
    step = pl.program_id(0)
    nv = nv_ref[0]

    @pl.when(step == 0)
    def _():
        zero_ref[...] = jnp.zeros_like(zero_ref)

        def fill_copy(j):
            return pltpu.make_async_copy(zero_ref.at[pl.ds(0, tm)], xs_ref.at[pl.ds(j * tm, tm)], fill_sem)

        def holds_padding(j):
            return (j >= nv - 1) | (te_ref[jnp.minimum(j + 1, n_tiles - 1)] != te_ref[j])

        def start(j, carry):
            @pl.when(holds_padding(j))
            def _():
                fill_copy(j).start()
            return carry

        def wait(j, carry):
            @pl.when(holds_padding(j))
            def _():
                fill_copy(j).wait()
            return carry

        lax.fori_loop(0, n_tiles, start, 0)
        lax.fori_loop(0, n_tiles, wait, 0)

    x_ref[...] = _rms_norm_rows(h_ref[...], g_ref[...])
    base = step * td
    for k in range(2):
        for r in range(td):
            pltpu.make_async_copy(x_ref.at[pl.ds(r, 1)], xs_ref.at[pl.ds(pos_ref[k * t + base + r], 1)],
                                  sems.at[k]).start(priority=k)
    for k in range(2):
        done = zero_ref.at[pl.ds(0, td)]
        pltpu.make_async_copy(done, done, sems.at[k]).wait()


def _dispatch(pos, te, nv, h, gain, tm):
    t = h.shape[0]
    n_tiles = te.shape[0]
    td = min(t, COMBINE_ROW_TILE)
    return pl.pallas_call(
        functools.partial(_dispatch_kernel, t=t, tm=tm, td=td, n_tiles=n_tiles),
        grid_spec=pltpu.PrefetchScalarGridSpec(
            num_scalar_prefetch=3,
            grid=(t // td,),
            in_specs=[pl.BlockSpec((td, D_MODEL), lambda i, pos, te, nv: (i, 0)),
                      pl.BlockSpec((1, D_MODEL), lambda i, pos, te, nv: (0, 0))],
            out_specs=pl.BlockSpec(memory_space=pl.ANY),
            scratch_shapes=[pltpu.VMEM((td, D_MODEL), F32), pltpu.VMEM((max(tm, td), D_MODEL), F32),
                            pltpu.SemaphoreType.DMA(()), pltpu.SemaphoreType.DMA((2,))],
        ),
        out_shape=jax.ShapeDtypeStruct((n_tiles * tm, D_MODEL), F32),
        compiler_params=_params(1),
    )(pos, te, nv, h, gain)


def _moe_ffn_kernel(te_ref, nv_ref, x_ref, wg_hbm, wu_hbm, wd_hbm, y_ref,
                    wg_ref, wu_ref, wd_ref, wslot_ref, wsems, *, layer, prec):
    i = pl.program_id(0)
    nv = nv_ref[0]
    expert = te_ref[i]

    def weight_copies(e, ws):
        return (pltpu.make_async_copy(wg_hbm.at[layer, e], wg_ref.at[ws], wsems.at[ws]),
                pltpu.make_async_copy(wu_hbm.at[layer, e], wu_ref.at[ws], wsems.at[ws]),
                pltpu.make_async_copy(wd_hbm.at[layer, e], wd_ref.at[ws], wsems.at[ws]))

    @pl.when(i == 0)
    def _():
        wslot_ref[0] = 1
        for cp in weight_copies(expert, 0):
            cp.start()

    is_first = (i < nv) & ((i == 0) | (te_ref[jnp.maximum(i - 1, 0)] != expert))

    @pl.when(is_first)
    def _():
        ws = 1 - wslot_ref[0]
        wslot_ref[0] = ws
        for cp in weight_copies(expert, ws):
            cp.wait()
        nxt = lax.while_loop(lambda j: (j < nv) & (te_ref[jnp.minimum(j, nv - 1)] == expert), lambda j: j + 1, i + 1)

        @pl.when(nxt < nv)
        def _():
            for cp in weight_copies(te_ref[nxt], 1 - ws):
                cp.start()

    @pl.when(i < nv)
    def _():
        ws = wslot_ref[0]
        x = x_ref[...]
        hdn = jax.nn.silu(_dot(x, wg_ref[ws], prec)) * _dot(x, wu_ref[ws], prec)
        _store_chunk_rows(y_ref, _dot(hdn, wd_ref[ws], prec))

    @pl.when(i >= nv)
    def _():
        y_ref[...] = jnp.zeros_like(y_ref)


def _moe_ffn(te, nv, xs, w_gate, w_up, w_down, layer, tm, prec):
    n_rows = xs.shape[0]
    n_tiles = n_rows // tm
    any_spec = pl.BlockSpec(memory_space=pl.ANY)
    return pl.pallas_call(
        functools.partial(_moe_ffn_kernel, layer=layer, prec=prec),
        grid_spec=pltpu.PrefetchScalarGridSpec(
            num_scalar_prefetch=2,
            grid=(n_tiles,),
            in_specs=[pl.BlockSpec((tm, D_MODEL), lambda i, te, nv: (jnp.minimum(i, nv[0] - 1), 0)),
                      any_spec, any_spec, any_spec],
            out_specs=pl.BlockSpec((tm * N_CHUNKS, LANES), lambda i, te, nv: (i, 0)),
            scratch_shapes=[pltpu.VMEM((2, D_MODEL, D_EXPERT), F32), pltpu.VMEM((2, D_MODEL, D_EXPERT), F32),
                            pltpu.VMEM((2, D_EXPERT, D_MODEL), F32), pltpu.SMEM((1,), jnp.int32),
                            pltpu.SemaphoreType.DMA((2,))],
        ),
        out_shape=jax.ShapeDtypeStruct((n_rows * N_CHUNKS, LANES), F32),
        compiler_params=_params(1),
    )(te, nv, xs, w_gate, w_up, w_down)


def _combine_kernel(pos_ref, h_ref, g_ref, y_ref, o_ref, y0_ref, y1_ref, sems, *, t, tm):
    base = pl.program_id(0) * tm
    for k, buf in enumerate((y0_ref, y1_ref)):
        for r in range(tm):
            src = pl.multiple_of(pos_ref[k * t + base + r] * N_CHUNKS, N_CHUNKS)
            pltpu.make_async_copy(y_ref.at[pl.ds(src, N_CHUNKS)], buf.at[pl.ds(r * ROW_PITCH, N_CHUNKS)],
                                  sems.at[k]).start(priority=k)
    for k, buf in enumerate((y0_ref, y1_ref)):
        done = buf.at[pl.ds(0, tm * N_CHUNKS)]
        pltpu.make_async_copy(done, done, sems.at[k]).wait()
    g = g_ref[...]
    o_ref[...] = (h_ref[...] + g[:, 0:1] * _load_gathered_rows(y0_ref, tm)
                  + g[:, 1:2] * _load_gathered_rows(y1_ref, tm))


def _combine(pos, h, gates, y):
    t = h.shape[0]
    tm = min(t, COMBINE_ROW_TILE)
    return pl.pallas_call(
        functools.partial(_combine_kernel, t=t, tm=tm),
        grid_spec=pltpu.PrefetchScalarGridSpec(
            num_scalar_prefetch=1,
            grid=(t // tm,),
            in_specs=[pl.BlockSpec((tm, D_MODEL), lambda i, pos: (i, 0)),
                      pl.BlockSpec((tm, LANES), lambda i, pos: (i, 0)),
                      pl.BlockSpec(memory_space=pl.ANY)],
            out_specs=pl.BlockSpec((tm, D_MODEL), lambda i, pos: (i, 0)),
            scratch_shapes=[pltpu.VMEM((tm * ROW_PITCH, LANES), F32), pltpu.VMEM((tm * ROW_PITCH, LANES), F32),
                            pltpu.SemaphoreType.DMA((2,))],
        ),
        out_shape=jax.ShapeDtypeStruct((t, D_MODEL), F32),
        compiler_params=_params(1),
    )(pos, h, gates, y)


def _hier_moe(h, gain, wr, br, w_gate, w_up, w_down, layer, prec):
    t = h.shape[0]
    tm = MOE_ROW_TILE if t >= MOE_SMALL_BELOW else MOE_ROW_TILE_SMALL
    n_tiles = (2 * t) // tm + N_EXPERTS
    assert n_tiles <= LANES
    idx, gates = _router(h, gain, wr, br, prec)
    e_flat = idx[:, :2].T.reshape(1, 2 * t)
    pos, te, nv = _plan(e_flat, tm)
    pos = pos.reshape(2 * t)
    te = te[0, :n_tiles]
    nv = nv[0, :1]
    xs = _dispatch(pos, te, nv, h, gain, tm)
    y = _moe_ffn(te, nv, xs, w_gate, w_up, w_down, layer, tm, prec)
    return _combine(pos, h, gates, y)


def _fox_cumsum_kernel(lft_ref, lf_ref, ct_ref, c_ref, *, s, blk):
    r = lax.broadcasted_iota(jnp.int32, (blk, blk), 0)
    c = lax.broadcasted_iota(jnp.int32, (blk, blk), 1)
    upper = (r <= c).astype(F32)
    lower = (r >= c).astype(F32)
    carry_t = jnp.zeros((2 * SUBLANES, 1), F32)
    carry = jnp.zeros((1, LANES), F32)
    for b in range(s // blk):
        sl = slice(b * blk, (b + 1) * blk)
        ct = _dot3_left(lft_ref[:, sl], upper) + carry_t
        ct_ref[:, sl] = ct
        carry_t = ct[:, blk - 1:blk]
        cc = _dot3_right(lower, lf_ref[sl, :]) + carry
        c_ref[sl, :] = cc
        carry = cc[blk - 1:blk, :]


def _fox_cumsum(lft, lf, nb, s):
    t = lf.shape[0]
    blk = min(s, CUMSUM_BLOCK)
    return pl.pallas_call(
        functools.partial(_fox_cumsum_kernel, s=s, blk=blk),
        grid=(nb,),
        in_specs=[pl.BlockSpec((2 * SUBLANES, s), lambda b: (0, b)), pl.BlockSpec((s, LANES), lambda b: (b, 0))],
        out_specs=[pl.BlockSpec((2 * SUBLANES, s), lambda b: (0, b)), pl.BlockSpec((s, LANES), lambda b: (b, 0))],
        out_shape=[jax.ShapeDtypeStruct((2 * SUBLANES, t), F32), jax.ShapeDtypeStruct((t, LANES), F32)],
        compiler_params=_params(1),
    )(lft, lf)


def _fox_prompt_kernel(q_ref, k_ref, v_ref, ct_ref, c_ref, o_ref, *, s, tq):
    h = pl.program_id(1)
    lane = lax.broadcasted_iota(jnp.int32, (s, LANES), 1)
    c_col = jnp.sum(jnp.where(lane == h, c_ref[...], 0.0), axis=1, keepdims=True) * LOG2E
    c_row = ct_ref[pl.ds(h, 1), :] * LOG2E
    ri = lax.broadcasted_iota(jnp.int32, (tq, tq), 0)
    ci = lax.broadcasted_iota(jnp.int32, (tq, tq), 1)
    causal = ci <= ri
    for qi in range(s // tq):
        qs = slice(qi * tq, (qi + 1) * tq)
        q = q_ref[qs, :].astype(F32)
        cq = c_col[qs, :]
        m = jnp.full((tq, 1), -jnp.inf, F32)
        l = jnp.zeros((tq, 1), F32)
        acc = jnp.zeros((tq, HEAD_DIM), F32)
        for kj in range(qi + 1):
            ks = slice(kj * tq, (kj + 1) * tq)
            sc = lax.dot_general(q, k_ref[ks, :], _NT, preferred_element_type=F32) - c_row[:, ks]
            if kj == qi:
                sc = jnp.where(causal, sc, -jnp.inf)
            m_new = jnp.maximum(m, jnp.max(sc, axis=-1, keepdims=True) + cq)
            alpha = jnp.exp2(m - m_new)
            p = jnp.exp2(sc - (m_new - cq))
            l = alpha * l + jnp.sum(p, axis=-1, keepdims=True)
            acc = alpha * acc + _dot(p, v_ref[ks, :])
            m = m_new
        o_ref[qs, :] = (acc / l).astype(o_ref.dtype)


def _fox_prompt(q, k, v, ct, c, nb, s):
    t = q.shape[0]
    tq = min(s, ATTN_TILE)
    head_spec = pl.BlockSpec((s, HEAD_DIM), lambda b, h: (b, h))
    return pl.pallas_call(
        functools.partial(_fox_prompt_kernel, s=s, tq=tq),
        grid=(nb, MAIN_HEADS),
        in_specs=[head_spec, head_spec, head_spec,
                  pl.BlockSpec((2 * SUBLANES, s), lambda b, h: (0, b)),
                  pl.BlockSpec((s, LANES), lambda b, h: (b, 0))],
        out_specs=head_spec,
        out_shape=jax.ShapeDtypeStruct((t, MAIN_WIDTH), _matmul_only_dtype(None)),
        compiler_params=_params(2),
    )(q, k, v, ct, c)


def _fox_past_bias_kernel(pt_ref, lfc_ref, o_ref, buf_ref, sem, *, n_pages):
    b = pl.program_id(0)

    def page_copy(p):
        page = pt_ref[b * n_pages + p]
        return pltpu.make_async_copy(lfc_ref.at[:, pl.ds(page, 1), :], buf_ref.at[:, pl.ds(p, 1), :], sem)

    def start(p, carry):
        page_copy(p).start()
        return carry

    def wait(p, carry):
        page_copy(p).wait()
        return carry

    lax.fori_loop(0, n_pages, start, 0)
    r = lax.broadcasted_iota(jnp.int32, (PAGE_SIZE, PAGE_SIZE), 0)
    c = lax.broadcasted_iota(jnp.int32, (PAGE_SIZE, PAGE_SIZE), 1)
    after_in_page = (r > c).astype(F32)
    pr = lax.broadcasted_iota(jnp.int32, (n_pages, n_pages), 0)
    pc = lax.broadcasted_iota(jnp.int32, (n_pages, n_pages), 1)
    later_pages = (pc > pr).astype(F32)
    lax.fori_loop(0, n_pages, wait, 0)
    for h in range(MAIN_HEADS):
        lp = buf_ref[h]
        tot = jnp.broadcast_to(jnp.sum(lp, axis=1, keepdims=True), lp.shape)
        o_ref[h] = _dot3_left(lp, after_in_page) + _dot3_right(later_pages, tot)


def _fox_past_bias(page_table_flat, cache_logf_t, nb, n_pages):
    return pl.pallas_call(
        functools.partial(_fox_past_bias_kernel, n_pages=n_pages),
        grid_spec=pltpu.PrefetchScalarGridSpec(
            num_scalar_prefetch=1,
            grid=(nb,),
            in_specs=[pl.BlockSpec(memory_space=pl.ANY)],
            out_specs=pl.BlockSpec((None, MAIN_HEADS, n_pages, PAGE_SIZE), lambda b, pt: (b, 0, 0, 0)),
            scratch_shapes=[pltpu.VMEM((MAIN_HEADS, n_pages, PAGE_SIZE), F32), pltpu.SemaphoreType.DMA(())],
        ),
        out_shape=jax.ShapeDtypeStruct((nb, MAIN_HEADS, n_pages, PAGE_SIZE), F32),
        compiler_params=_params(1),
    )(page_table_flat, cache_logf_t)


def _fox_sample_kernel(*refs, ppb, n_steps, valid):
    pt_ref, q_ref, kn_ref, vn_ref, lf_ref, bias_ref = refs[:6]
    k_refs = refs[6:6 + ppb]
    v_refs = refs[6 + ppb:6 + 2 * ppb]
    o_ref, m_ref, l_ref, acc_ref = refs[6 + 2 * ppb:]
    g = pl.program_id(1)
    rows = SAMPLE_ROWS

    @pl.when(g == 0)
    def _():
        m_ref[...] = jnp.full(m_ref.shape, -jnp.inf, F32)
        l_ref[...] = jnp.zeros(l_ref.shape, F32)
        acc_ref[...] = jnp.zeros(acc_ref.shape, F32)

    r8 = lax.broadcasted_iota(jnp.int32, (rows, rows), 0)
    c8 = lax.broadcasted_iota(jnp.int32, (rows, rows), 1)
    new_mask = (c8 <= r8) & (c8 < valid)
    c_new = _dot3_right(new_mask.astype(F32), lf_ref[...])

    def qk3(q_hi, q_lo, k):
        a = _dot_nt(jnp.concatenate([q_hi, q_lo], axis=0), k)
        return a[:rows] + a[rows:]

    def pv3(p, v):
        p_hi, p_lo = _split_hi_lo(p)
        a = _dot(jnp.concatenate([p_hi, p_lo], axis=0), v)
        return a[:rows] + a[rows:]

    all_scores = []
    for h in range(MAIN_HEADS):
        q_hi, q_lo = _split_hi_lo(q_ref[:, h * HEAD_DIM:(h + 1) * HEAD_DIM])
        cq = c_new[:, h:h + 1]
        all_scores.append([qk3(q_hi, q_lo, k_refs[u][h]) + cq + bias_ref[h, u:u + 1, :]
                           for u in range(ppb)])
    all_probs, all_alpha = [], []
    for h in range(MAIN_HEADS):
        m_old = m_ref[h, :, 0:1]
        m_new = m_old
        for sc in all_scores[h]:
            m_new = jnp.maximum(m_new, jnp.max(sc, axis=-1, keepdims=True))
        alpha = jnp.exp(m_old - m_new)
        probs = [jnp.exp(sc - m_new) for sc in all_scores[h]]
        l_new = alpha * l_ref[h, :, 0:1]
        for p in probs:
            l_new = l_new + jnp.sum(p, axis=-1, keepdims=True)
        m_ref[h] = jnp.broadcast_to(m_new, (rows, LANES))
        l_ref[h] = jnp.broadcast_to(l_new, (rows, LANES))
        all_probs.append(probs)
        all_alpha.append(alpha)
    for h in range(MAIN_HEADS):
        sl = slice(h * HEAD_DIM, (h + 1) * HEAD_DIM)
        acc = all_alpha[h] * acc_ref[:, sl]
        for u in range(ppb):
            acc = acc + pv3(all_probs[h][u], v_refs[u][h])
        acc_ref[:, sl] = acc

    @pl.when(g == n_steps - 1)
    def _():
        for h in range(MAIN_HEADS):
            sl = slice(h * HEAD_DIM, (h + 1) * HEAD_DIM)
            qh = q_ref[:, sl]
            cq = c_new[:, h:h + 1]
            cq_row = jnp.sum(jnp.where(r8 == c8, jnp.broadcast_to(cq, (rows, rows)), 0.0), axis=0, keepdims=True)
            sn = _dot_nt(qh, kn_ref[:, sl], HIGHEST) + cq - cq_row
            sn = jnp.where(new_mask, sn, -jnp.inf)
            m_old = m_ref[h, :, 0:1]
            m_new = jnp.maximum(m_old, jnp.max(sn, axis=-1, keepdims=True))
            alpha = jnp.exp(m_old - m_new)
            p = jnp.exp(sn - m_new)
            l_new = alpha * l_ref[h, :, 0:1] + jnp.sum(p, axis=-1, keepdims=True)
            acc = alpha * acc_ref[:, sl] + _dot(p, vn_ref[:, sl], HIGHEST)
            o_ref[:, sl] = acc / l_new


def _fox_sample(page_table_flat, q, k_new, v_new, lf, bias, cache_k_t, cache_v_t, nb, n_pages, valid):
    ppb = PAGES_PER_STEP if n_pages % PAGES_PER_STEP == 0 else n_pages
    n_steps = n_pages // ppb
    row_spec = pl.BlockSpec((SAMPLE_ROWS, MAIN_WIDTH), lambda b, g, pt: (b, 0))

    def page_spec(u):
        return pl.BlockSpec((None, MAIN_HEADS, PAGE_SIZE, HEAD_DIM),
                            lambda b, g, pt, u=u: (pt[b * n_pages + g * ppb + u], 0, 0, 0))

    in_specs = [row_spec, row_spec, row_spec,
                pl.BlockSpec((SAMPLE_ROWS, LANES), lambda b, g, pt: (b, 0)),
                pl.BlockSpec((None, MAIN_HEADS, ppb, PAGE_SIZE), lambda b, g, pt: (b, 0, g, 0))]
    in_specs += [page_spec(u) for u in range(ppb)] + [page_spec(u) for u in range(ppb)]
    return pl.pallas_call(
        functools.partial(_fox_sample_kernel, ppb=ppb, n_steps=n_steps, valid=valid),
        grid_spec=pltpu.PrefetchScalarGridSpec(
            num_scalar_prefetch=1,
            grid=(nb, n_steps),
            in_specs=in_specs,
            out_specs=row_spec,
            scratch_shapes=[pltpu.VMEM((MAIN_HEADS, SAMPLE_ROWS, LANES), F32),
                            pltpu.VMEM((MAIN_HEADS, SAMPLE_ROWS, LANES), F32),
                            pltpu.VMEM((SAMPLE_ROWS, MAIN_WIDTH), F32)],
        ),
        out_shape=jax.ShapeDtypeStruct(q.shape, F32),
        compiler_params=_params(2),
    )(page_table_flat, q, k_new, v_new, lf, bias, *([cache_k_t] * ppb), *([cache_v_t] * ppb))


def _trunk(x, nb, rows, valid, mem_k, mem_v, fox, w, prec):
    u, v, qm = _norm_matmul(x, w["norm1_gain"][0:1], w["w_in_a"], 0,
                            [(MAIN_WIDTH, "gelu_mm", 0), (MAIN_WIDTH, "gelu", 0), (MEM_WIDTH, "headnorm_mm", 0)],
                            w["mem_q_gain"][0:1], prec=prec)
    y_main, v_rows = _gmlp(u, v, w["v_gain_a"][0:1], w["w_mix"], w["b_mix"], w["gmlp_rows"], w["gmlp_grp"], valid,
                           w["gmlp_blocks_per_seq"], prec)
    y_mem = _mem_attn(qm, mem_k, mem_v, 0, nb, rows, prec)
    h = _out_proj(y_main, y_mem, w["w_out"], 0, x, prec)
    h = _hier_moe(h, w["norm2_gain"][0:1], w["w_router"][0], w["b_router"][0], w["w_gate"], w["w_up"], w["w_down"],
                  0, prec)
    k, vv, lf, lft = _norm_matmul(h, w["kv_norm_gain"], w["w_kv_shared"], 0,
                                  [(MAIN_WIDTH, "headnorm", 0), (MAIN_WIDTH, "plain", 0)],
                                  w["k_gain_shared"], forget=w["forget"], prec=prec)
    q, qm = _norm_matmul(h, w["norm1_gain"][1:2], w["w_in_b"], 0,
                         [(MAIN_WIDTH, "headnorm_mm", 0), (MEM_WIDTH, "headnorm_mm", 1)],
                         jnp.concatenate([w["q_gain_b"][0:1] * w["q_logit_scale"], w["mem_q_gain"][1:2]], axis=0),
                         prec=prec)
    y_main = fox(q, k, vv, lf, lft)
    y_mem = _mem_attn(qm, mem_k, mem_v, 1, nb, rows, prec)
    h = _out_proj(y_main, y_mem, w["w_out"], 1, h, prec)
    h = _hier_moe(h, w["norm2_gain"][1:2], w["w_router"][1], w["b_router"][1], w["w_gate"], w["w_up"], w["w_down"],
                  1, prec)
    return h, k, vv, lf, v_rows


def kernel(x_prompt, x_sample, cache_k, cache_v, cache_logf, cache_mem_k, cache_mem_v, page_table, mem_prompt, norm1_gain, norm2_gain, w_in_a, v_gain_a, w_s_a, b_s_a, w_in_b, q_gain_b, kv_norm_gain, w_kv_shared, b_forget, k_gain_shared, mem_norm_gain, w_mem_kv, mem_q_gain, mem_k_gain, w_out, w_router_group, b_router_group, w_router_expert, b_router_expert, w_gate, w_up, w_down):
    batch, seq, _ = x_prompt.shape
    dec_batch, dec_seq, _ = x_sample.shape
    n_phys = cache_k.shape[0]
    n_pages = page_table.shape[1]
    depth = norm1_gain.shape[0]
    assert depth == 2 and dec_seq <= SAMPLE_ROWS and seq % CHUNK == 0

    n_route = N_GROUPS + N_EXPERTS
    w_router = jnp.pad(jnp.concatenate([w_router_group, w_router_expert], axis=-1),
                       ((0, 0), (0, 0), (0, LANES - n_route)))
    b_router = jnp.pad(jnp.concatenate([b_router_group, b_router_expert], axis=-1),
                       ((0, 0), (0, LANES - n_route)))[:, None, :]
    w_f = w_kv_shared[:, 2 * MAIN_WIDTH:]
    forget = (jnp.pad(w_f, ((0, 0), (0, LANES - MAIN_HEADS))),
              jnp.pad(b_forget, (0, LANES - MAIN_HEADS))[None, :],
              jnp.pad(b_forget, (0, 2 * SUBLANES - MAIN_HEADS))[:, None])
    common = dict(norm1_gain=norm1_gain, norm2_gain=norm2_gain, w_in_a=w_in_a, v_gain_a=v_gain_a, w_in_b=w_in_b,
                  q_gain_b=q_gain_b, kv_norm_gain=kv_norm_gain[None, :], w_kv_shared=w_kv_shared,
                  k_gain_shared=k_gain_shared[None, :], mem_q_gain=mem_q_gain, w_out=w_out,
                  w_router=w_router, b_router=b_router, w_gate=w_gate, w_up=w_up, w_down=w_down, forget=forget)

    t_p = batch * seq
    mem_tok = mem_prompt.reshape(batch * N_MEM, D_MODEL)
    mem_k_layers, mem_v_layers = [], []
    for l in range(depth):
        mk, mv = _norm_matmul(mem_tok, mem_norm_gain[l:l + 1], w_mem_kv, l,
                              [(MEM_WIDTH, "headnorm", 0), (MEM_WIDTH, "plain", 0)], mem_k_gain[l:l + 1])
        mem_k_layers.append(mk.reshape(batch, N_MEM, MEM_WIDTH))
        mem_v_layers.append(mv.reshape(batch, N_MEM, MEM_WIDTH))
    mem_k_p = jnp.stack(mem_k_layers)
    mem_v_p = jnp.stack(mem_v_layers)

    def fox_prompt(q, k, v, lf, lft):
        ct, c = _fox_cumsum(lft, lf, batch, seq)
        return _fox_prompt(q, k, v, ct, c, batch, seq)

    w_p = dict(common, w_mix=w_s_a[0], b_mix=b_s_a[0].T, gmlp_rows=CHUNK, gmlp_grp=CHUNK,
               gmlp_blocks_per_seq=seq // CHUNK, q_logit_scale=ATTN_SCALE * LOG2E)
    y_p, k_p, v_p, lf_p, vrows_p = _trunk(x_prompt.reshape(t_p, D_MODEL), batch, seq, seq,
                                          mem_k_p, mem_v_p, fox_prompt, w_p, None)

    rows = SAMPLE_ROWS
    t_s = dec_batch * rows
    x_s = jnp.pad(x_sample, ((0, 0), (0, rows - dec_seq), (0, 0))).reshape(t_s, D_MODEL)
    pt_flat = page_table.reshape(dec_batch * n_pages)
    bias = _fox_past_bias(pt_flat, jnp.transpose(cache_logf, (2, 0, 1)), dec_batch, n_pages)
    cache_k_t = jnp.transpose(cache_k, (0, 2, 1, 3))
    cache_v_t = jnp.transpose(cache_v, (0, 2, 1, 3))

    def fox_sample(q, k, v, lf, lft):
        return _fox_sample(pt_flat, q, k, v, lf, bias, cache_k_t, cache_v_t, dec_batch, n_pages, dec_seq)

    w_s = dict(common, w_mix=jnp.tile(w_s_a[0][:, :rows, :rows], (1, dec_batch, dec_batch)),
               b_mix=jnp.tile(b_s_a[0][:, :rows].T, (dec_batch, 1)), gmlp_rows=t_s, gmlp_grp=rows,
               gmlp_blocks_per_seq=1, q_logit_scale=ATTN_SCALE)
    mem_k_s = cache_mem_k.reshape(depth, dec_batch, N_MEM, MEM_WIDTH)
    mem_v_s = cache_mem_v.reshape(depth, dec_batch, N_MEM, MEM_WIDTH)
    y_s, k_s, v_s, lf_s, vrows_s = _trunk(x_s, dec_batch, rows, dec_seq, mem_k_s, mem_v_s, fox_sample, w_s, SPLIT3)

    def unpad(a, *tail):
        return a.reshape((dec_batch, rows) + tail)[:, :dec_seq]

    return (y_p.reshape(batch, seq, D_MODEL),
            unpad(y_s, D_MODEL),
            k_p.reshape(batch, seq, MAIN_HEADS, HEAD_DIM),
            v_p.reshape(batch, seq, MAIN_HEADS, HEAD_DIM),
            lf_p[:, :MAIN_HEADS].reshape(batch, seq, MAIN_HEADS),
            unpad(k_s, MAIN_HEADS, HEAD_DIM),
            unpad(v_s, MAIN_HEADS, HEAD_DIM),
            unpad(lf_s[:, :MAIN_HEADS], MAIN_HEADS),
            mem_k_p.reshape(depth, batch, N_MEM, MEM_HEADS, HEAD_DIM),
            mem_v_p.reshape(depth, batch, N_MEM, MEM_HEADS, HEAD_DIM),
            vrows_p.reshape(batch, CHUNK, MAIN_WIDTH)[None],
            unpad(vrows_s, MAIN_WIDTH)[None])
```

```python
import functools

import jax
import jax.numpy as jnp
from jax import lax
from jax.experimental import pallas as pl
from jax.experimental.pallas import tpu as pltpu

D_MODEL = 2048
HEAD_DIM = 128
MAIN_HEADS = 12
MEM_HEADS = 4
MAIN_WIDTH = MAIN_HEADS * HEAD_DIM
MEM_WIDTH = MEM_HEADS * HEAD_DIM
N_MEM = 256
N_GROUPS = 4
EXPERTS_PER_GROUP = 4
N_EXPERTS = N_GROUPS * EXPERTS_PER_GROUP
D_EXPERT = 512
CHUNK = 128
PAGE_SIZE = 128
EPS = 1e-6
ATTN_SCALE = HEAD_DIM ** -0.5
LOG2E = 1.4426950408889634

LANES = 128
SUBLANES = 8
VMEM_LIMIT_BYTES = 56 * 1024 * 1024
SAMPLE_ROWS = SUBLANES
N_CHUNKS = D_MODEL // LANES
ROW_PITCH = 24

PROJ_ROW_TILE = 1024
PROJ_ROW_TILE_BF16 = 2048
PROJ_COL_TILE = 512
ROUTER_ROW_TILE = 512
MOE_ROW_TILE = 256
MOE_ROW_TILE_SMALL = 16
MOE_SMALL_BELOW = 1024
COMBINE_ROW_TILE = 256
ATTN_TILE = 512
MEM_ATTN_ROW_TILE = 1024
CUMSUM_BLOCK = 256
PLAN_BLOCK = 512
PAGES_PER_STEP = SUBLANES
SCALAR_UNROLL = 32

F32 = jnp.float32
_NT = (((1,), (1,)), ((), ()))


def _params(n_axes):
    return pltpu.CompilerParams(dimension_semantics=("arbitrary",) * n_axes,
                                vmem_limit_bytes=VMEM_LIMIT_BYTES)


HIGHEST = lax.Precision.HIGHEST


SPLIT3 = "split3"


def _dot(a, b, prec=None):
    if prec == SPLIT3:
        return _dot_split3(a, b, _dot)
    return jnp.dot(a, b, preferred_element_type=F32, precision=prec)


def _dot_nt(a, b, prec=None):
    if prec == SPLIT3:
        return _dot_split3(a, b, _dot_nt)
    return lax.dot_general(a, b, _NT, preferred_element_type=F32, precision=prec)


def _dot_tn(a, b, prec=None):
    dims = (((0,), (1,)), ((), ()))
    if prec == SPLIT3:
        a_hi, a_lo = _split_hi_lo(a)
        b_hi, b_lo = _split_hi_lo(b)
        return _dot_tn(a_hi, b_hi) + _dot_tn(a_lo, b_hi) + _dot_tn(a_hi, b_lo)
    return lax.dot_general(a, b, dims, preferred_element_type=F32, precision=prec)


def _dot_split3(a, b, dot):
    a_hi, a_lo = _split_hi_lo(a)
    b_hi, b_lo = _split_hi_lo(b)
    m = a.shape[0]
    s = dot(jnp.concatenate([a_hi, a_lo], axis=0), b_hi)
    return s[:m] + s[m:] + dot(a_hi, b_lo)


def _matmul_only_dtype(prec):
    return jnp.bfloat16 if prec is None else F32


def _split_hi_lo(x):
    hi = x.astype(jnp.bfloat16).astype(F32)
    return hi, x - hi


def _split3(x):
    hi = x.astype(jnp.bfloat16).astype(F32)
    r = x - hi
    mid = r.astype(jnp.bfloat16).astype(F32)
    lo = (r - mid).astype(jnp.bfloat16).astype(F32)
    return hi, mid, lo


def _dot3_right(a_exact, x):
    hi, mid, lo = _split3(x)
    return _dot(a_exact, hi) + _dot(a_exact, mid) + _dot(a_exact, lo)


def _dot3_left(x, b_exact):
    hi, mid, lo = _split3(x)
    return _dot(hi, b_exact) + _dot(mid, b_exact) + _dot(lo, b_exact)


def _store_chunk_rows(ref, x):
    rows = x.shape[0]
    for c in range(N_CHUNKS):
        ref[pl.ds(c, rows, stride=N_CHUNKS), :] = x[:, c * LANES:(c + 1) * LANES]


def _load_gathered_rows(ref, rows):
    return jnp.concatenate([ref[pl.ds(c, rows, stride=ROW_PITCH), :] for c in range(N_CHUNKS)], axis=1)


def _head_norm(z, gain_row):
    return z * lax.rsqrt(jnp.mean(z * z, axis=-1, keepdims=True) + EPS) * gain_row


WEIGHT_RING = 3


def _norm_matmul_kernel(*refs, segs, tn, with_forget, prec, layer, n_col, n_steps):
    it = iter(refs)
    x_ref, g_ref, w_hbm, hg_ref = next(it), next(it), next(it), next(it)
    if with_forget:
        wf_ref, bfr_ref, bfc_ref = next(it), next(it), next(it)
    out_refs = [next(it) for _ in segs]
    if with_forget:
        lf_ref, lft_ref = next(it), next(it)
    xn_ref, wbuf_ref, wsems = next(it), next(it), next(it)
    j = pl.program_id(1)
    step = pl.program_id(0) * n_col + j

    def w_copy(col_tile, slot):
        col = col_tile * tn if isinstance(col_tile, int) else pl.multiple_of(col_tile * tn, tn)
        src = w_hbm.at[layer, :, pl.ds(col, tn)] if len(w_hbm.shape) == 3 else w_hbm.at[:, pl.ds(col, tn)]
        return pltpu.make_async_copy(src, wbuf_ref.at[slot], wsems.at[slot])

    @pl.when(step == 0)
    def _():
        w_copy(0, 0).start()
        if n_steps > 1:
            w_copy(1 % n_col, 1).start()

    @pl.when(step + 2 < n_steps)
    def _():
        w_copy(lax.rem(j + 2, n_col), lax.rem(step + 2, WEIGHT_RING)).start()

    slot = lax.rem(step, WEIGHT_RING)
    w_copy(j, slot).wait()

    def project(xn, o_ref, kind, grow):
        z = _dot(xn, wbuf_ref[slot], prec)
        if kind == "plain":
            o_ref[...] = z
        elif kind in ("gelu", "gelu_mm"):
            o_ref[...] = jax.nn.gelu(z).astype(o_ref.dtype)
        else:
            gain = hg_ref[grow:grow + 1, :]
            for c in range(tn // HEAD_DIM):
                sl = slice(c * HEAD_DIM, (c + 1) * HEAD_DIM)
                o_ref[:, sl] = _head_norm(z[:, sl], gain).astype(o_ref.dtype)

    @pl.when(j == 0)
    def _first():
        x = x_ref[...]
        xn = x * lax.rsqrt(jnp.mean(x * x, axis=-1, keepdims=True) + EPS) * g_ref[...]
        xn_ref[...] = xn
        if with_forget:
            lf_ref[...] = jax.nn.log_sigmoid(_dot(xn, wf_ref[...], prec) + bfr_ref[...])
            zt = _dot_tn(wf_ref[...], xn, prec)[:2 * SUBLANES, :]
            lft_ref[...] = jax.nn.log_sigmoid(zt + bfc_ref[...])
        project(xn, out_refs[0], segs[0][1], segs[0][2])

    lo = 0
    for (ncols, kind, grow), o_ref in zip(segs, out_refs):
        nt = ncols // tn

        @pl.when((j >= max(lo, 1)) & (j < lo + nt))
        def _rest(o_ref=o_ref, kind=kind, grow=grow):
            project(xn_ref[...], o_ref, kind, grow)

        lo += nt


def _norm_matmul(x, gain, w, layer, segs, head_gains, forget=None, prec=None):
    t = x.shape[0]
    tm = min(t, PROJ_ROW_TILE)
    tn = PROJ_COL_TILE
    n_tiles = sum(s[0] for s in segs) // tn
    with_forget = forget is not None
    in_specs = [
        pl.BlockSpec((tm, D_MODEL), lambda i, j: (i, 0)),
        pl.BlockSpec((1, D_MODEL), lambda i, j: (0, 0)),
        pl.BlockSpec(memory_space=pl.ANY),
        pl.BlockSpec(head_gains.shape, lambda i, j: (0, 0)),
    ]
    args = [x, gain, w, head_gains]
    if with_forget:
        wf, bfr, bfc = forget
        in_specs += [pl.BlockSpec(wf.shape, lambda i, j: (0, 0)),
                     pl.BlockSpec(bfr.shape, lambda i, j: (0, 0)), pl.BlockSpec(bfc.shape, lambda i, j: (0, 0))]
        args += [wf, bfr, bfc]
    out_shape, out_specs = [], []
    lo = 0
    for ncols, kind, _ in segs:
        nt = ncols // tn
        out_shape.append(jax.ShapeDtypeStruct((t, ncols), _matmul_only_dtype(prec) if kind.endswith("_mm") else F32))
        out_specs.append(pl.BlockSpec((tm, tn), lambda i, j, lo=lo, nt=nt: (i, jnp.clip(j - lo, 0, nt - 1))))
        lo += nt
    if with_forget:
        out_shape += [jax.ShapeDtypeStruct((t, LANES), F32), jax.ShapeDtypeStruct((2 * SUBLANES, t), F32)]
        out_specs += [pl.BlockSpec((tm, LANES), lambda i, j: (i, 0)),
                      pl.BlockSpec((2 * SUBLANES, tm), lambda i, j: (0, i))]
    return pl.pallas_call(
        functools.partial(_norm_matmul_kernel, segs=tuple(segs), tn=tn, with_forget=with_forget, prec=prec,
                          layer=layer, n_col=n_tiles, n_steps=(t // tm) * n_tiles),
        grid=(t // tm, n_tiles),
        in_specs=in_specs,
        out_specs=out_specs,
        out_shape=out_shape,
        scratch_shapes=[pltpu.VMEM((tm, D_MODEL), F32), pltpu.VMEM((WEIGHT_RING, D_MODEL, tn), F32),
                        pltpu.SemaphoreType.DMA((WEIGHT_RING,))],
        compiler_params=_params(2),
    )(*args)


def _gmlp_kernel(u_ref, v_ref, vg_ref, w_ref, b_ref, y_ref, vn_ref, *, rows, grp, valid, prec):
    v = v_ref[...]
    vn = v * lax.rsqrt(jnp.mean(v * v, axis=-1, keepdims=True) + EPS) * vg_ref[...]
    vn_ref[...] = vn
    r = lax.broadcasted_iota(jnp.int32, (rows, rows), 0)
    c = lax.broadcasted_iota(jnp.int32, (rows, rows), 1)
    shift = grp.bit_length() - 1
    allowed = ((c & (grp - 1)) <= (r & (grp - 1))) & ((c & (grp - 1)) < valid)
    if grp < rows:
        allowed = allowed & ((r >> shift) == (c >> shift))
    for g in range(MAIN_HEADS):
        sl = slice(g * HEAD_DIM, (g + 1) * HEAD_DIM)
        w = jnp.where(allowed, w_ref[g], 0.0)
        mixed = _dot(w, vn[:, sl], prec) + b_ref[:, g:g + 1]
        y_ref[:, sl] = (u_ref[:, sl] * mixed).astype(y_ref.dtype)


def _gmlp(u, v, v_gain, w_mix, b_mix, rows, grp, valid, blocks_per_seq, prec):
    t = u.shape[0]
    return pl.pallas_call(
        functools.partial(_gmlp_kernel, rows=rows, grp=grp, valid=valid, prec=prec),
        grid=(t // rows,),
        in_specs=[
            pl.BlockSpec((rows, MAIN_WIDTH), lambda i: (i, 0)),
            pl.BlockSpec((rows, MAIN_WIDTH), lambda i: (i, 0)),
            pl.BlockSpec((1, MAIN_WIDTH), lambda i: (0, 0)),
            pl.BlockSpec((MAIN_HEADS, rows, rows), lambda i: (0, 0, 0)),
            pl.BlockSpec((rows, MAIN_HEADS), lambda i: (0, 0)),
        ],
        out_specs=[pl.BlockSpec((rows, MAIN_WIDTH), lambda i: (i, 0)),
                   pl.BlockSpec((rows, MAIN_WIDTH), lambda i: (i // blocks_per_seq, 0))],
        out_shape=[jax.ShapeDtypeStruct((t, MAIN_WIDTH), _matmul_only_dtype(prec)),
                   jax.ShapeDtypeStruct((t // blocks_per_seq, MAIN_WIDTH), F32)],
        compiler_params=_params(1),
    )(u, v, v_gain, w_mix, b_mix)


def _mem_attn_kernel(q_ref, k_ref, v_ref, o_ref, *, prec):
    for h in range(MEM_HEADS):
        sl = slice(h * HEAD_DIM, (h + 1) * HEAD_DIM)
        s = _dot_nt(q_ref[:, sl].astype(F32), k_ref[:, sl], prec) * ATTN_SCALE
        e = jnp.exp(s - jnp.max(s, axis=-1, keepdims=True))
        p = e / jnp.sum(e, axis=-1, keepdims=True)
        o_ref[:, sl] = _dot(p, v_ref[:, sl], prec).astype(o_ref.dtype)


def _mem_attn(q, k, v, layer, nb, rows_per_batch, prec):
    tq = min(rows_per_batch, MEM_ATTN_ROW_TILE)
    nq = rows_per_batch // tq
    kv_spec = pl.BlockSpec((None, None, N_MEM, MEM_WIDTH), lambda b, i: (layer, b, 0, 0))
    return pl.pallas_call(
        functools.partial(_mem_attn_kernel, prec=prec),
        grid=(nb, nq),
        in_specs=[pl.BlockSpec((tq, MEM_WIDTH), lambda b, i: (b * nq + i, 0)), kv_spec, kv_spec],
        out_specs=pl.BlockSpec((tq, MEM_WIDTH), lambda b, i: (b * nq + i, 0)),
        out_shape=jax.ShapeDtypeStruct(q.shape, _matmul_only_dtype(prec)),
        compiler_params=_params(2),
    )(q, k, v)


def _out_proj_kernel(ym_ref, ymem_ref, w1_ref, w2_ref, h_ref, o_ref, *, prec):
    w1 = w1_ref[...].astype(ym_ref.dtype)
    w2 = w2_ref[...].astype(ymem_ref.dtype)
    o_ref[...] = h_ref[...] + _dot(ym_ref[...], w1, prec) + _dot(ymem_ref[...], w2, prec)


def _out_proj(y_main, y_mem, w_out, layer, h, prec):
    t = h.shape[0]
    tm = min(t, PROJ_ROW_TILE_BF16 if y_main.dtype == jnp.bfloat16 else PROJ_ROW_TILE)
    tn = PROJ_COL_TILE
    return pl.pallas_call(
        functools.partial(_out_proj_kernel, prec=prec),
        grid=(t // tm, D_MODEL // tn),
        in_specs=[
            pl.BlockSpec((tm, MAIN_WIDTH), lambda i, j: (i, 0)),
            pl.BlockSpec((tm, MEM_WIDTH), lambda i, j: (i, 0)),
            pl.BlockSpec((None, MAIN_WIDTH, tn), lambda i, j: (layer, 0, j)),
            pl.BlockSpec((None, MEM_WIDTH, tn), lambda i, j: (layer, MAIN_WIDTH // MEM_WIDTH, j)),
            pl.BlockSpec((tm, tn), lambda i, j: (i, j)),
        ],
        out_specs=pl.BlockSpec((tm, tn), lambda i, j: (i, j)),
        out_shape=jax.ShapeDtypeStruct((t, D_MODEL), F32),
        compiler_params=_params(2),
    )(y_main, y_mem, w_out, w_out, h)


def _rms_norm_rows(x, gain_row):
    return x * lax.rsqrt(jnp.mean(x * x, axis=-1, keepdims=True) + EPS) * gain_row


def _router_kernel(h_ref, g_ref, wr_ref, br_ref, idx_ref, gate_ref, *, prec):
    xn = _rms_norm_rows(h_ref[...], g_ref[...])
    lg = _dot(xn, wr_ref[...], prec) + br_ref[...]
    lane = lax.broadcasted_iota(jnp.int32, lg.shape, 1)
    neg = -jnp.inf
    is_grp = lane < N_GROUPS
    gl = jnp.where(is_grp, lg, neg)
    ge = jnp.where(is_grp, jnp.exp(gl - jnp.max(gl, axis=-1, keepdims=True)), 0.0)
    pg = ge / jnp.sum(ge, axis=-1, keepdims=True)
    p_top = jnp.max(pg, axis=-1, keepdims=True)
    g_idx = jnp.min(jnp.where(is_grp & (pg == p_top), lane, LANES), axis=-1, keepdims=True)
    first = N_GROUPS + EXPERTS_PER_GROUP * g_idx
    in_grp = (lane >= first) & (lane < first + EXPERTS_PER_GROUP)
    e1 = jnp.max(jnp.where(in_grp, lg, neg), axis=-1, keepdims=True)
    i1 = jnp.min(jnp.where(in_grp & (lg == e1), lane, LANES), axis=-1, keepdims=True)
    rest = in_grp & (lane != i1)
    e2 = jnp.max(jnp.where(rest, lg, neg), axis=-1, keepdims=True)
    i2 = jnp.min(jnp.where(rest & (lg == e2), lane, LANES), axis=-1, keepdims=True)
    t2 = jnp.exp(e2 - e1)
    den = 1.0 + t2
    idx_ref[...] = jnp.where(lane == 0, i1 - N_GROUPS, jnp.where(lane == 1, i2 - N_GROUPS, 0))
    gate_ref[...] = jnp.where(lane == 0, p_top * (1.0 / den), jnp.where(lane == 1, p_top * (t2 / den), 0.0))


def _router(h, gain, wr, br, prec):
    t = h.shape[0]
    tm = min(t, ROUTER_ROW_TILE)
    return pl.pallas_call(
        functools.partial(_router_kernel, prec=prec),
        grid=(t // tm,),
        in_specs=[pl.BlockSpec((tm, D_MODEL), lambda i: (i, 0)), pl.BlockSpec((1, D_MODEL), lambda i: (0, 0)),
                  pl.BlockSpec((D_MODEL, LANES), lambda i: (0, 0)), pl.BlockSpec((1, LANES), lambda i: (0, 0))],
        out_specs=[pl.BlockSpec((tm, LANES), lambda i: (i, 0)), pl.BlockSpec((tm, LANES), lambda i: (i, 0))],
        out_shape=[jax.ShapeDtypeStruct((t, LANES), jnp.int32), jax.ShapeDtypeStruct((t, LANES), F32)],
        compiler_params=_params(1),
    )(h, gain, wr, br)


def _plan_kernel(e_ref, pos_ref, te_ref, nv_ref, rank_ref, *, t2, blk, tm):
    shift = tm.bit_length() - 1
    nblk = t2 // blk
    r = lax.broadcasted_iota(jnp.int32, (blk, blk), 0)
    c = lax.broadcasted_iota(jnp.int32, (blk, blk), 1)
    upper = (r <= c).astype(F32)
    sub = lax.broadcasted_iota(jnp.int32, (N_EXPERTS, blk), 0)
    carry = jnp.zeros((N_EXPERTS, 1), F32)
    for b in range(nblk):
        sl = slice(b * blk, (b + 1) * blk)
        oh = (sub == e_ref[:, sl]).astype(F32)
        cs = _dot(oh, upper) + carry
        rank_ref[:, sl] = jnp.sum(oh * (cs - 1.0), axis=0, keepdims=True)
        carry = cs[:, blk - 1:blk]
    counts = carry.astype(jnp.int32)
    padc = ((counts + (tm - 1)) >> shift) << shift
    sub1 = lax.broadcasted_iota(jnp.int32, (N_EXPERTS, 1), 0)
    off = jnp.zeros((N_EXPERTS, 1), jnp.int32)
    run = jnp.zeros((1, 1), jnp.int32)
    for e in range(N_EXPERTS):
        off = jnp.where(sub1 == e, run, off)
        run = run + padc[e:e + 1, :]
    ends = off + padc
    for b in range(nblk):
        sl = slice(b * blk, (b + 1) * blk)
        offv = jnp.sum(jnp.where(sub == e_ref[:, sl], off, 0), axis=0, keepdims=True)
        pos_ref[:, sl] = rank_ref[:, sl].astype(jnp.int32) + offv
    lane = lax.broadcasted_iota(jnp.int32, (N_EXPERTS, LANES), 1)
    lane1 = lax.broadcasted_iota(jnp.int32, (1, LANES), 1)
    nv = run >> shift
    te = jnp.sum((ends <= lane * tm).astype(jnp.int32), axis=0, keepdims=True)
    te_last = jnp.sum((ends <= (nv - 1) * tm).astype(jnp.int32), axis=0, keepdims=True)
    te_ref[...] = jnp.minimum(jnp.where(lane1 >= nv, te_last, te), N_EXPERTS - 1)
    nv_ref[...] = jnp.broadcast_to(nv, (1, LANES))


def _plan(e_flat, tm):
    t2 = e_flat.shape[1]
    blk = min(t2, PLAN_BLOCK)
    return pl.pallas_call(
        functools.partial(_plan_kernel, t2=t2, blk=blk, tm=tm),
        out_shape=[jax.ShapeDtypeStruct((1, t2), jnp.int32), jax.ShapeDtypeStruct((1, LANES), jnp.int32),
                   jax.ShapeDtypeStruct((1, LANES), jnp.int32)],
        scratch_shapes=[pltpu.VMEM((1, t2), F32)],
        compiler_params=pltpu.CompilerParams(vmem_limit_bytes=VMEM_LIMIT_BYTES),
    )(e_flat)


def _dispatch_kernel(pos_ref, te_ref, nv_ref, h_ref, g_ref, xs_ref, x_ref, zero_ref, fill_sem, sems,
                     *, t, tm, td, n_tiles):
    step = pl.program_id(0)
    nv = nv_ref[0]

    @pl.when(step == 0)
    def _():
        zero_ref[...] = jnp.zeros_like(zero_ref)

        def fill_copy(j):
            return pltpu.make_async_copy(zero_ref.at[pl.ds(0, tm)], xs_ref.at[pl.ds(j * tm, tm)], fill_sem)

        def holds_padding(j):
            return (j >= nv - 1) | (te_ref[jnp.minimum(j + 1, n_tiles - 1)] != te_ref[j])

        def start(j, carry):
            @pl.when(holds_padding(j))
            def _():
                fill_copy(j).start()
            return carry

        def wait(j, carry):
            @pl.when(holds_padding(j))
            def _():
                fill_copy(j).wait()
            return carry

        lax.fori_loop(0, n_tiles, start, 0)
        lax.fori_loop(0, n_tiles, wait, 0)

    x_ref[...] = _rms_norm_rows(h_ref[...], g_ref[...])
    base = step * td
    for k in range(2):
        for r in range(td):
            pltpu.make_async_copy(x_ref.at[pl.ds(r, 1)], xs_ref.at[pl.ds(pos_ref[k * t + base + r], 1)],
                                  sems.at[k]).start(priority=k)
    for k in range(2):
        done = zero_ref.at[pl.ds(0, td)]
        pltpu.make_async_copy(done, done, sems.at[k]).wait()


def _dispatch(pos, te, nv, h, gain, tm):
    t = h.shape[0]
    n_tiles = te.shape[0]
    td = min(t, COMBINE_ROW_TILE)
    return pl.pallas_call(
        functools.partial(_dispatch_kernel, t=t, tm=tm, td=td, n_tiles=n_tiles),
        grid_spec=pltpu.PrefetchScalarGridSpec(
            num_scalar_prefetch=3,
            grid=(t // td,),
            in_specs=[pl.BlockSpec((td, D_MODEL), lambda i, pos, te, nv: (i, 0)),
                      pl.BlockSpec((1, D_MODEL), lambda i, pos, te, nv: (0, 0))],
            out_specs=pl.BlockSpec(memory_space=pl.ANY),
            scratch_shapes=[pltpu.VMEM((td, D_MODEL), F32), pltpu.VMEM((max(tm, td), D_MODEL), F32),
                            pltpu.SemaphoreType.DMA(()), pltpu.SemaphoreType.DMA((2,))],
        ),
        out_shape=jax.ShapeDtypeStruct((n_tiles * tm, D_MODEL), F32),
        compiler_params=_params(1),
    )(pos, te, nv, h, gain)


def _moe_ffn_kernel(te_ref, nv_ref, x_ref, wg_hbm, wu_hbm, wd_hbm, y_ref,
                    wg_ref, wu_ref, wd_ref, wslot_ref, wsems, *, layer, prec):
    i = pl.program_id(0)
    nv = nv_ref[0]
    expert = te_ref[i]

    def weight_copies(e, ws):
        return (pltpu.make_async_copy(wg_hbm.at[layer, e], wg_ref.at[ws], wsems.at[ws]),
                pltpu.make_async_copy(wu_hbm.at[layer, e], wu_ref.at[ws], wsems.at[ws]),
                pltpu.make_async_copy(wd_hbm.at[layer, e], wd_ref.at[ws], wsems.at[ws]))

    @pl.when(i == 0)
    def _():
        wslot_ref[0] = 1
        for cp in weight_copies(expert, 0):
            cp.start()

    is_first = (i < nv) & ((i == 0) | (te_ref[jnp.maximum(i - 1, 0)] != expert))

    @pl.when(is_first)
    def _():
        ws = 1 - wslot_ref[0]
        wslot_ref[0] = ws
        for cp in weight_copies(expert, ws):
            cp.wait()
        nxt = lax.while_loop(lambda j: (j < nv) & (te_ref[jnp.minimum(j, nv - 1)] == expert), lambda j: j + 1, i + 1)

        @pl.when(nxt < nv)
        def _():
            for cp in weight_copies(te_ref[nxt], 1 - ws):
                cp.start()

    @pl.when(i < nv)
    def _():
        ws = wslot_ref[0]
        x = x_ref[...]
        hdn = jax.nn.silu(_dot(x, wg_ref[ws], prec)) * _dot(x, wu_ref[ws], prec)
        _store_chunk_rows(y_ref, _dot(hdn, wd_ref[ws], prec))

    @pl.when(i >= nv)
    def _():
        y_ref[...] = jnp.zeros_like(y_ref)


def _moe_ffn(te, nv, xs, w_gate, w_up, w_down, layer, tm, prec):
    n_rows = xs.shape[0]
    n_tiles = n_rows // tm
    any_spec = pl.BlockSpec(memory_space=pl.ANY)
    return pl.pallas_call(
        functools.partial(_moe_ffn_kernel, layer=layer, prec=prec),
        grid_spec=pltpu.PrefetchScalarGridSpec(
            num_scalar_prefetch=2,
            grid=(n_tiles,),
            in_specs=[pl.BlockSpec((tm, D_MODEL), lambda i, te, nv: (jnp.minimum(i, nv[0] - 1), 0)),
                      any_spec, any_spec, any_spec],
            out_specs=pl.BlockSpec((tm * N_CHUNKS, LANES), lambda i, te, nv: (i, 0)),
            scratch_shapes=[pltpu.VMEM((2, D_MODEL, D_EXPERT), F32), pltpu.VMEM((2, D_MODEL, D_EXPERT), F32),
                            pltpu.VMEM((2, D_EXPERT, D_MODEL), F32), pltpu.SMEM((1,), jnp.int32),
                            pltpu.SemaphoreType.DMA((2,))],
        ),
        out_shape=jax.ShapeDtypeStruct((n_rows * N_CHUNKS, LANES), F32),
        compiler_params=_params(1),
    )(te, nv, xs, w_gate, w_up, w_down)


def _combine_kernel(pos_ref, h_ref, g_ref, y_ref, o_ref, y0_ref, y1_ref, sems, *, t, tm):
    base = pl.program_id(0) * tm
    for k, buf in enumerate((y0_ref, y1_ref)):
        for r in range(tm):
            src = pl.multiple_of(pos_ref[k * t + base + r] * N_CHUNKS, N_CHUNKS)
            pltpu.make_async_copy(y_ref.at[pl.ds(src, N_CHUNKS)], buf.at[pl.ds(r * ROW_PITCH, N_CHUNKS)],
                                  sems.at[k]).start(priority=k)
    for k, buf in enumerate((y0_ref, y1_ref)):
        done = buf.at[pl.ds(0, tm * N_CHUNKS)]
        pltpu.make_async_copy(done, done, sems.at[k]).wait()
    g = g_ref[...]
    o_ref[...] = (h_ref[...] + g[:, 0:1] * _load_gathered_rows(y0_ref, tm)
                  + g[:, 1:2] * _load_gathered_rows(y1_ref, tm))


def _combine(pos, h, gates, y):
    t = h.shape[0]
    tm = min(t, COMBINE_ROW_TILE)
    return pl.pallas_call(
        functools.partial(_combine_kernel, t=t, tm=tm),
        grid_spec=pltpu.PrefetchScalarGridSpec(
            num_scalar_prefetch=1,
            grid=(t // tm,),
            in_specs=[pl.BlockSpec((tm, D_MODEL), lambda i, pos: (i, 0)),
                      pl.BlockSpec((tm, LANES), lambda i, pos: (i, 0)),
                      pl.BlockSpec(memory_space=pl.ANY)],
            out_specs=pl.BlockSpec((tm, D_MODEL), lambda i, pos: (i, 0)),
            scratch_shapes=[pltpu.VMEM((tm * ROW_PITCH, LANES), F32), pltpu.VMEM((tm * ROW_PITCH, LANES), F32),
                            pltpu.SemaphoreType.DMA((2,))],
        ),
        out_shape=jax.ShapeDtypeStruct((t, D_MODEL), F32),
        compiler_params=_params(1),
    )(pos, h, gates, y)


def _hier_moe(h, gain, wr, br, w_gate, w_up, w_down, layer, prec):
    t = h.shape[0]
    tm = MOE_ROW_TILE if t >= MOE_SMALL_BELOW else MOE_ROW_TILE_SMALL
    n_tiles = (2 * t) // tm + N_EXPERTS
    assert n_tiles <= LANES
    idx, gates = _router(h, gain, wr, br, prec)
    e_flat = idx[:, :2].T.reshape(1, 2 * t)
    pos, te, nv = _plan(e_flat, tm)
    pos = pos.reshape(2 * t)
    te = te[0, :n_tiles]
    nv = nv[0, :1]
    xs = _dispatch(pos, te, nv, h, gain, tm)
    y = _moe_ffn(te, nv, xs, w_gate, w_up, w_down, layer, tm, prec)
    return _combine(pos, h, gates, y)


def _fox_cumsum_kernel(lft_ref, lf_ref, ct_ref, c_ref, *, s, blk):
    r = lax.broadcasted_iota(jnp.int32, (blk, blk), 0)
    c = lax.broadcasted_iota(jnp.int32, (blk, blk), 1)
    upper = (r <= c).astype(F32)
    lower = (r >= c).astype(F32)
    carry_t = jnp.zeros((2 * SUBLANES, 1), F32)
    carry = jnp.zeros((1, LANES), F32)
    for b in range(s // blk):
        sl = slice(b * blk, (b + 1) * blk)
        ct = _dot3_left(lft_ref[:, sl], upper) + carry_t
        ct_ref[:, sl] = ct
        carry_t = ct[:, blk - 1:blk]
        cc = _dot3_right(lower, lf_ref[sl, :]) + carry
        c_ref[sl, :] = cc
        carry = cc[blk - 1:blk, :]


def _fox_cumsum(lft, lf, nb, s):
    t = lf.shape[0]
    blk = min(s, CUMSUM_BLOCK)
    return pl.pallas_call(
        functools.partial(_fox_cumsum_kernel, s=s, blk=blk),
        grid=(nb,),
        in_specs=[pl.BlockSpec((2 * SUBLANES, s), lambda b: (0, b)), pl.BlockSpec((s, LANES), lambda b: (b, 0))],
        out_specs=[pl.BlockSpec((2 * SUBLANES, s), lambda b: (0, b)), pl.BlockSpec((s, LANES), lambda b: (b, 0))],
        out_shape=[jax.ShapeDtypeStruct((2 * SUBLANES, t), F32), jax.ShapeDtypeStruct((t, LANES), F32)],
        compiler_params=_params(1),
    )(lft, lf)


def _fox_prompt_kernel(q_ref, k_ref, v_ref, ct_ref, c_ref, o_ref, *, s, tq):
    h = pl.program_id(1)
    lane = lax.broadcasted_iota(jnp.int32, (s, LANES), 1)
    c_col = jnp.sum(jnp.where(lane == h, c_ref[...], 0.0), axis=1, keepdims=True) * LOG2E
    c_row = ct_ref[pl.ds(h, 1), :] * LOG2E
    ri = lax.broadcasted_iota(jnp.int32, (tq, tq), 0)
    ci = lax.broadcasted_iota(jnp.int32, (tq, tq), 1)
    causal = ci <= ri
    for qi in range(s // tq):
        qs = slice(qi * tq, (qi + 1) * tq)
        q = q_ref[qs, :].astype(F32)
        cq = c_col[qs, :]
        m = jnp.full((tq, 1), -jnp.inf, F32)
        l = jnp.zeros((tq, 1), F32)
        acc = jnp.zeros((tq, HEAD_DIM), F32)
        for kj in range(qi + 1):
            ks = slice(kj * tq, (kj + 1) * tq)
            sc = lax.dot_general(q, k_ref[ks, :], _NT, preferred_element_type=F32) - c_row[:, ks]
            if kj == qi:
                sc = jnp.where(causal, sc, -jnp.inf)
            m_new = jnp.maximum(m, jnp.max(sc, axis=-1, keepdims=True) + cq)
            alpha = jnp.exp2(m - m_new)
            p = jnp.exp2(sc - (m_new - cq))
            l = alpha * l + jnp.sum(p, axis=-1, keepdims=True)
            acc = alpha * acc + _dot(p, v_ref[ks, :])
            m = m_new
        o_ref[qs, :] = (acc / l).astype(o_ref.dtype)


def _fox_prompt(q, k, v, ct, c, nb, s):
    t = q.shape[0]
    tq = min(s, ATTN_TILE)
    head_spec = pl.BlockSpec((s, HEAD_DIM), lambda b, h: (b, h))
    return pl.pallas_call(
        functools.partial(_fox_prompt_kernel, s=s, tq=tq),
        grid=(nb, MAIN_HEADS),
        in_specs=[head_spec, head_spec, head_spec,
                  pl.BlockSpec((2 * SUBLANES, s), lambda b, h: (0, b)),
                  pl.BlockSpec((s, LANES), lambda b, h: (b, 0))],
        out_specs=head_spec,
        out_shape=jax.ShapeDtypeStruct((t, MAIN_WIDTH), _matmul_only_dtype(None)),
        compiler_params=_params(2),
    )(q, k, v, ct, c)


def _fox_past_bias_kernel(pt_ref, lfc_ref, o_ref, buf_ref, sem, *, n_pages):
    b = pl.program_id(0)

    def page_copy(p):
        page = pt_ref[b * n_pages + p]
        return pltpu.make_async_copy(lfc_ref.at[:, pl.ds(page, 1), :], buf_ref.at[:, pl.ds(p, 1), :], sem)

    def start(p, carry):
        page_copy(p).start()
        return carry

    def wait(p, carry):
        page_copy(p).wait()
        return carry

    lax.fori_loop(0, n_pages, start, 0)
    r = lax.broadcasted_iota(jnp.int32, (PAGE_SIZE, PAGE_SIZE), 0)
    c = lax.broadcasted_iota(jnp.int32, (PAGE_SIZE, PAGE_SIZE), 1)
    after_in_page = (r > c).astype(F32)
    pr = lax.broadcasted_iota(jnp.int32, (n_pages, n_pages), 0)
    pc = lax.broadcasted_iota(jnp.int32, (n_pages, n_pages), 1)
    later_pages = (pc > pr).astype(F32)
    lax.fori_loop(0, n_pages, wait, 0)
    for h in range(MAIN_HEADS):
        lp = buf_ref[h]
        tot = jnp.broadcast_to(jnp.sum(lp, axis=1, keepdims=True), lp.shape)
        o_ref[h] = _dot3_left(lp, after_in_page) + _dot3_right(later_pages, tot)


def _fox_past_bias(page_table_flat, cache_logf_t, nb, n_pages):
    return pl.pallas_call(
        functools.partial(_fox_past_bias_kernel, n_pages=n_pages),
        grid_spec=pltpu.PrefetchScalarGridSpec(
            num_scalar_prefetch=1,
            grid=(nb,),
            in_specs=[pl.BlockSpec(memory_space=pl.ANY)],
            out_specs=pl.BlockSpec((None, MAIN_HEADS, n_pages, PAGE_SIZE), lambda b, pt: (b, 0, 0, 0)),
            scratch_shapes=[pltpu.VMEM((MAIN_HEADS, n_pages, PAGE_SIZE), F32), pltpu.SemaphoreType.DMA(())],
        ),
        out_shape=jax.ShapeDtypeStruct((nb, MAIN_HEADS, n_pages, PAGE_SIZE), F32),
        compiler_params=_params(1),
    )(page_table_flat, cache_logf_t)


def _fox_sample_kernel(*refs, ppb, n_steps, valid):
    pt_ref, q_ref, kn_ref, vn_ref, lf_ref, bias_ref = refs[:6]
    k_refs = refs[6:6 + ppb]
    v_refs = refs[6 + ppb:6 + 2 * ppb]
    o_ref, m_ref, l_ref, acc_ref = refs[6 + 2 * ppb:]
    g = pl.program_id(1)
    rows = SAMPLE_ROWS

    @pl.when(g == 0)
    def _():
        m_ref[...] = jnp.full(m_ref.shape, -jnp.inf, F32)
        l_ref[...] = jnp.zeros(l_ref.shape, F32)
        acc_ref[...] = jnp.zeros(acc_ref.shape, F32)

    r8 = lax.broadcasted_iota(jnp.int32, (rows, rows), 0)
    c8 = lax.broadcasted_iota(jnp.int32, (rows, rows), 1)
    new_mask = (c8 <= r8) & (c8 < valid)
    c_new = _dot3_right(new_mask.astype(F32), lf_ref[...])

    def qk3(q_hi, q_lo, k):
        a = _dot_nt(jnp.concatenate([q_hi, q_lo], axis=0), k)
        return a[:rows] + a[rows:]

    def pv3(p, v):
        p_hi, p_lo = _split_hi_lo(p)
        a = _dot(jnp.concatenate([p_hi, p_lo], axis=0), v)
        return a[:rows] + a[rows:]

    all_scores = []
    for h in range(MAIN_HEADS):
        q_hi, q_lo = _split_hi_lo(q_ref[:, h * HEAD_DIM:(h + 1) * HEAD_DIM])
        cq = c_new[:, h:h + 1]
        all_scores.append([qk3(q_hi, q_lo, k_refs[u][h]) + cq + bias_ref[h, u:u + 1, :]
                           for u in range(ppb)])
    all_probs, all_alpha = [], []
    for h in range(MAIN_HEADS):
        m_old = m_ref[h, :, 0:1]
        m_new = m_old
        for sc in all_scores[h]:
            m_new = jnp.maximum(m_new, jnp.max(sc, axis=-1, keepdims=True))
        alpha = jnp.exp(m_old - m_new)
        probs = [jnp.exp(sc - m_new) for sc in all_scores[h]]
        l_new = alpha * l_ref[h, :, 0:1]
        for p in probs:
            l_new = l_new + jnp.sum(p, axis=-1, keepdims=True)
        m_ref[h] = jnp.broadcast_to(m_new, (rows, LANES))
        l_ref[h] = jnp.broadcast_to(l_new, (rows, LANES))
        all_probs.append(probs)
        all_alpha.append(alpha)
    for h in range(MAIN_HEADS):
        sl = slice(h * HEAD_DIM, (h + 1) * HEAD_DIM)
        acc = all_alpha[h] * acc_ref[:, sl]
        for u in range(ppb):
            acc = acc + pv3(all_probs[h][u], v_refs[u][h])
        acc_ref[:, sl] = acc

    @pl.when(g == n_steps - 1)
    def _():
        for h in range(MAIN_HEADS):
            sl = slice(h * HEAD_DIM, (h + 1) * HEAD_DIM)
            qh = q_ref[:, sl]
            cq = c_new[:, h:h + 1]
            cq_row = jnp.sum(jnp.where(r8 == c8, jnp.broadcast_to(cq, (rows, rows)), 0.0), axis=0, keepdims=True)
            sn = _dot_nt(qh, kn_ref[:, sl], HIGHEST) + cq - cq_row
            sn = jnp.where(new_mask, sn, -jnp.inf)
            m_old = m_ref[h, :, 0:1]
            m_new = jnp.maximum(m_old, jnp.max(sn, axis=-1, keepdims=True))
            alpha = jnp.exp(m_old - m_new)
            p = jnp.exp(sn - m_new)
            l_new = alpha * l_ref[h, :, 0:1] + jnp.sum(p, axis=-1, keepdims=True)
            acc = alpha * acc_ref[:, sl] + _dot(p, vn_ref[:, sl], HIGHEST)
            o_ref[:, sl] = acc / l_new


def _fox_sample(page_table_flat, q, k_new, v_new, lf, bias, cache_k_t, cache_v_t, nb, n_pages, valid):
    ppb = PAGES_PER_STEP if n_pages % PAGES_PER_STEP == 0 else n_pages
    n_steps = n_pages // ppb
    row_spec = pl.BlockSpec((SAMPLE_ROWS, MAIN_WIDTH), lambda b, g, pt: (b, 0))

    def page_spec(u):
        return pl.BlockSpec((None, MAIN_HEADS, PAGE_SIZE, HEAD_DIM),
                            lambda b, g, pt, u=u: (pt[b * n_pages + g * ppb + u], 0, 0, 0))

    in_specs = [row_spec, row_spec, row_spec,
                pl.BlockSpec((SAMPLE_ROWS, LANES), lambda b, g, pt: (b, 0)),
                pl.BlockSpec((None, MAIN_HEADS, ppb, PAGE_SIZE), lambda b, g, pt: (b, 0, g, 0))]
    in_specs += [page_spec(u) for u in range(ppb)] + [page_spec(u) for u in range(ppb)]
    return pl.pallas_call(
        functools.partial(_fox_sample_kernel, ppb=ppb, n_steps=n_steps, valid=valid),
        grid_spec=pltpu.PrefetchScalarGridSpec(
            num_scalar_prefetch=1,
            grid=(nb, n_steps),
            in_specs=in_specs,
            out_specs=row_spec,
            scratch_shapes=[pltpu.VMEM((MAIN_HEADS, SAMPLE_ROWS, LANES), F32),
                            pltpu.VMEM((MAIN_HEADS, SAMPLE_ROWS, LANES), F32),
                            pltpu.VMEM((SAMPLE_ROWS, MAIN_WIDTH), F32)],
        ),
        out_shape=jax.ShapeDtypeStruct(q.shape, F32),
        compiler_params=_params(2),
    )(page_table_flat, q, k_new, v_new, lf, bias, *([cache_k_t] * ppb), *([cache_v_t] * ppb))


def _trunk(x, nb, rows, valid, mem_k, mem_v, fox, w, prec):
    u, v, qm = _norm_matmul(x, w["norm1_gain"][0:1], w["w_in_a"], 0,
                            [(MAIN_WIDTH, "gelu_mm", 0), (MAIN_WIDTH, "gelu", 0), (MEM_WIDTH, "headnorm_mm", 0)],
                            w["mem_q_gain"][0:1], prec=prec)
    y_main, v_rows = _gmlp(u, v, w["v_gain_a"][0:1], w["w_mix"], w["b_mix"], w["gmlp_rows"], w["gmlp_grp"], valid,
                           w["gmlp_blocks_per_seq"], prec)
    y_mem = _mem_attn(qm, mem_k, mem_v, 0, nb, rows, prec)
    h = _out_proj(y_main, y_mem, w["w_out"], 0, x, prec)
    h = _hier_moe(h, w["norm2_gain"][0:1], w["w_router"][0], w["b_router"][0], w["w_gate"], w["w_up"], w["w_down"],
                  0, prec)
    k, vv, lf, lft = _norm_matmul(h, w["kv_norm_gain"], w["w_kv_shared"], 0,
                                  [(MAIN_WIDTH, "headnorm", 0), (MAIN_WIDTH, "plain", 0)],
                                  w["k_gain_shared"], forget=w["forget"], prec=prec)
    q, qm = _norm_matmul(h, w["norm1_gain"][1:2], w["w_in_b"], 0,
                         [(MAIN_WIDTH, "headnorm_mm", 0), (MEM_WIDTH, "headnorm_mm", 1)],
                         jnp.concatenate([w["q_gain_b"][0:1] * w["q_logit_scale"], w["mem_q_gain"][1:2]], axis=0),
                         prec=prec)
    y_main = fox(q, k, vv, lf, lft)
    y_mem = _mem_attn(qm, mem_k, mem_v, 1, nb, rows, prec)
    h = _out_proj(y_main, y_mem, w["w_out"], 1, h, prec)
    h = _hier_moe(h, w["norm2_gain"][1:2], w["w_router"][1], w["b_router"][1], w["w_gate"], w["w_up"], w["w_down"],
                  1, prec)
    return h, k, vv, lf, v_rows


def kernel(x_prompt, x_sample, cache_k, cache_v, cache_logf, cache_mem_k, cache_mem_v, page_table, mem_prompt, norm1_gain, norm2_gain, w_in_a, v_gain_a, w_s_a, b_s_a, w_in_b, q_gain_b, kv_norm_gain, w_kv_shared, b_forget, k_gain_shared, mem_norm_gain, w_mem_kv, mem_q_gain, mem_k_gain, w_out, w_router_group, b_router_group, w_router_expert, b_router_expert, w_gate, w_up, w_down):
    batch, seq, _ = x_prompt.shape
    dec_batch, dec_seq, _ = x_sample.shape
    n_phys = cache_k.shape[0]
    n_pages = page_table.shape[1]
    depth = norm1_gain.shape[0]
    assert depth == 2 and dec_seq <= SAMPLE_ROWS and seq % CHUNK == 0

    n_route = N_GROUPS + N_EXPERTS
    w_router = jnp.pad(jnp.concatenate([w_router_group, w_router_expert], axis=-1),
                       ((0, 0), (0, 0), (0, LANES - n_route)))
    b_router = jnp.pad(jnp.concatenate([b_router_group, b_router_expert], axis=-1),
                       ((0, 0), (0, LANES - n_route)))[:, None, :]
    w_f = w_kv_shared[:, 2 * MAIN_WIDTH:]
    forget = (jnp.pad(w_f, ((0, 0), (0, LANES - MAIN_HEADS))),
              jnp.pad(b_forget, (0, LANES - MAIN_HEADS))[None, :],
              jnp.pad(b_forget, (0, 2 * SUBLANES - MAIN_HEADS))[:, None])
    common = dict(norm1_gain=norm1_gain, norm2_gain=norm2_gain, w_in_a=w_in_a, v_gain_a=v_gain_a, w_in_b=w_in_b,
                  q_gain_b=q_gain_b, kv_norm_gain=kv_norm_gain[None, :], w_kv_shared=w_kv_shared,
                  k_gain_shared=k_gain_shared[None, :], mem_q_gain=mem_q_gain, w_out=w_out,
                  w_router=w_router, b_router=b_router, w_gate=w_gate, w_up=w_up, w_down=w_down, forget=forget)

    t_p = batch * seq
    mem_tok = mem_prompt.reshape(batch * N_MEM, D_MODEL)
    mem_k_layers, mem_v_layers = [], []
    for l in range(depth):
        mk, mv = _norm_matmul(mem_tok, mem_norm_gain[l:l + 1], w_mem_kv, l,
                              [(MEM_WIDTH, "headnorm", 0), (MEM_WIDTH, "plain", 0)], mem_k_gain[l:l + 1])
        mem_k_layers.append(mk.reshape(batch, N_MEM, MEM_WIDTH))
        mem_v_layers.append(mv.reshape(batch, N_MEM, MEM_WIDTH))
    mem_k_p = jnp.stack(mem_k_layers)
    mem_v_p = jnp.stack(mem_v_layers)

    def fox_prompt(q, k, v, lf, lft):
        ct, c = _fox_cumsum(lft, lf, batch, seq)
        return _fox_prompt(q, k, v, ct, c, batch, seq)

    w_p = dict(common, w_mix=w_s_a[0], b_mix=b_s_a[0].T, gmlp_rows=CHUNK, gmlp_grp=CHUNK,
               gmlp_blocks_per_seq=seq // CHUNK, q_logit_scale=ATTN_SCALE * LOG2E)
    y_p, k_p, v_p, lf_p, vrows_p = _trunk(x_prompt.reshape(t_p, D_MODEL), batch, seq, seq,
                                          mem_k_p, mem_v_p, fox_prompt, w_p, None)

    rows = SAMPLE_ROWS
    t_s = dec_batch * rows
    x_s = jnp.pad(x_sample, ((0, 0), (0, rows - dec_seq), (0, 0))).reshape(t_s, D_MODEL)
    pt_flat = page_table.reshape(dec_batch * n_pages)
    bias = _fox_past_bias(pt_flat, jnp.transpose(cache_logf, (2, 0, 1)), dec_batch, n_pages)
    cache_k_t = jnp.transpose(cache_k, (0, 2, 1, 3))
    cache_v_t = jnp.transpose(cache_v, (0, 2, 1, 3))

    def fox_sample(q, k, v, lf, lft):
        return _fox_sample(pt_flat, q, k, v, lf, bias, cache_k_t, cache_v_t, dec_batch, n_pages, dec_seq)

    w_s = dict(common, w_mix=jnp.tile(w_s_a[0][:, :rows, :rows], (1, dec_batch, dec_batch)),
               b_mix=jnp.tile(b_s_a[0][:, :rows].T, (dec_batch, 1)), gmlp_rows=t_s, gmlp_grp=rows,
               gmlp_blocks_per_seq=1, q_logit_scale=ATTN_SCALE)
    mem_k_s = cache_mem_k.reshape(depth, dec_batch, N_MEM, MEM_WIDTH)
    mem_v_s = cache_mem_v.reshape(depth, dec_batch, N_MEM, MEM_WIDTH)
    y_s, k_s, v_s, lf_s, vrows_s = _trunk(x_s, dec_batch, rows, dec_seq, mem_k_s, mem_v_s, fox_sample, w_s, SPLIT3)

    def unpad(a, *tail):
        return a.reshape((dec_batch, rows) + tail)[:, :dec_seq]

    return (y_p.reshape(batch, seq, D_MODEL),
            unpad(y_s, D_MODEL),
            k_p.reshape(batch, seq, MAIN_HEADS, HEAD_DIM),
            v_p.reshape(batch, seq, MAIN_HEADS, HEAD_DIM),
            lf_p[:, :MAIN_HEADS].reshape(batch, seq, MAIN_HEADS),
            unpad(k_s, MAIN_HEADS, HEAD_DIM),
            unpad(v_s, MAIN_HEADS, HEAD_DIM),
            unpad(lf_s[:, :MAIN_HEADS], MAIN_HEADS),
            mem_k_p.reshape(depth, batch, N_MEM, MEM_HEADS, HEAD_DIM),
            mem_v_p.reshape(depth, batch, N_MEM, MEM_HEADS, HEAD_DIM),
            vrows_p.reshape(batch, CHUNK, MAIN_WIDTH)[None],
            unpad(vrows_s, MAIN_WIDTH)[None])
```

```python
import functools

import jax
import jax.numpy as jnp
from jax import lax
from jax.experimental import pallas as pl
from jax.experimental.pallas import tpu as pltpu

D_MODEL = 2048
HEAD_DIM = 128
MAIN_HEADS = 12
MEM_HEADS = 4
MAIN_WIDTH = MAIN_HEADS * HEAD_DIM
MEM_WIDTH = MEM_HEADS * HEAD_DIM
N_MEM = 256
N_GROUPS = 4
EXPERTS_PER_GROUP = 4
N_EXPERTS = N_GROUPS * EXPERTS_PER_GROUP
D_EXPERT = 512
CHUNK = 128
PAGE_SIZE = 128
EPS = 1e-6
ATTN_SCALE = HEAD_DIM ** -0.5
LOG2E = 1.4426950408889634

LANES = 128
SUBLANES = 8
VMEM_LIMIT_BYTES = 56 * 1024 * 1024
SAMPLE_ROWS = SUBLANES
N_CHUNKS = D_MODEL // LANES
ROW_PITCH = 24

PROJ_ROW_TILE = 1024
PROJ_ROW_TILE_BF16 = 2048
PROJ_COL_TILE = 512
ROUTER_ROW_TILE = 512
MOE_ROW_TILE = 256
MOE_ROW_TILE_SMALL = 16
MOE_SMALL_BELOW = 1024
COMBINE_ROW_TILE = 256
ATTN_TILE = 512
MEM_ATTN_ROW_TILE = 1024
CUMSUM_BLOCK = 256
PLAN_BLOCK = 512
PAGES_PER_STEP = SUBLANES
SCALAR_UNROLL = 32

F32 = jnp.float32
_NT = (((1,), (1,)), ((), ()))


def _params(n_axes):
    return pltpu.CompilerParams(dimension_semantics=("arbitrary",) * n_axes,
                                vmem_limit_bytes=VMEM_LIMIT_BYTES)


HIGHEST = lax.Precision.HIGHEST


SPLIT3 = "split3"


def _dot(a, b, prec=None):
    if prec == SPLIT3:
        return _dot_split3(a, b, _dot)
    return jnp.dot(a, b, preferred_element_type=F32, precision=prec)


def _dot_nt(a, b, prec=None):
    if prec == SPLIT3:
        return _dot_split3(a, b, _dot_nt)
    return lax.dot_general(a, b, _NT, preferred_element_type=F32, precision=prec)


def _dot_tn(a, b, prec=None):
    dims = (((0,), (1,)), ((), ()))
    if prec == SPLIT3:
        a_hi, a_lo = _split_hi_lo(a)
        b_hi, b_lo = _split_hi_lo(b)
        return _dot_tn(a_hi, b_hi) + _dot_tn(a_lo, b_hi) + _dot_tn(a_hi, b_lo)
    return lax.dot_general(a, b, dims, preferred_element_type=F32, precision=prec)


def _dot_split3(a, b, dot):
    a_hi, a_lo = _split_hi_lo(a)
    b_hi, b_lo = _split_hi_lo(b)
    m = a.shape[0]
    s = dot(jnp.concatenate([a_hi, a_lo], axis=0), b_hi)
    return s[:m] + s[m:] + dot(a_hi, b_lo)


def _matmul_only_dtype(prec):
    return jnp.bfloat16 if prec is None else F32


def _split_hi_lo(x):
    hi = x.astype(jnp.bfloat16).astype(F32)
    return hi, x - hi


def _split3(x):
    hi = x.astype(jnp.bfloat16).astype(F32)
    r = x - hi
    mid = r.astype(jnp.bfloat16).astype(F32)
    lo = (r - mid).astype(jnp.bfloat16).astype(F32)
    return hi, mid, lo


def _dot3_right(a_exact, x):
    hi, mid, lo = _split3(x)
    return _dot(a_exact, hi) + _dot(a_exact, mid) + _dot(a_exact, lo)


def _dot3_left(x, b_exact):
    hi, mid, lo = _split3(x)
    return _dot(hi, b_exact) + _dot(mid, b_exact) + _dot(lo, b_exact)


def _store_chunk_rows(ref, x):
    rows = x.shape[0]
    for c in range(N_CHUNKS):
        ref[pl.ds(c, rows, stride=N_CHUNKS), :] = x[:, c * LANES:(c + 1) * LANES]


def _load_gathered_rows(ref, rows):
    return jnp.concatenate([ref[pl.ds(c, rows, stride=ROW_PITCH), :] for c in range(N_CHUNKS)], axis=1)


def _head_norm(z, gain_row):
    return z * lax.rsqrt(jnp.mean(z * z, axis=-1, keepdims=True) + EPS) * gain_row


WEIGHT_RING = 3


def _norm_matmul_kernel(*refs, segs, tn, with_forget, prec, layer, n_col, n_steps):
    it = iter(refs)
    x_ref, g_ref, w_hbm, hg_ref = next(it), next(it), next(it), next(it)
    if with_forget:
        wf_ref, bfr_ref, bfc_ref = next(it), next(it), next(it)
    out_refs = [next(it) for _ in segs]
    if with_forget:
        lf_ref, lft_ref = next(it), next(it)
    xn_ref, wbuf_ref, wsems = next(it), next(it), next(it)
    j = pl.program_id(1)
    step = pl.program_id(0) * n_col + j

    def w_copy(col_tile, slot):
        col = col_tile * tn if isinstance(col_tile, int) else pl.multiple_of(col_tile * tn, tn)
        src = w_hbm.at[layer, :, pl.ds(col, tn)] if len(w_hbm.shape) == 3 else w_hbm.at[:, pl.ds(col, tn)]
        return pltpu.make_async_copy(src, wbuf_ref.at[slot], wsems.at[slot])

    @pl.when(step == 0)
    def _():
        w_copy(0, 0).start()
        if n_steps > 1:
            w_copy(1 % n_col, 1).start()

    @pl.when(step + 2 < n_steps)
    def _():
        w_copy(lax.rem(j + 2, n_col), lax.rem(step + 2, WEIGHT_RING)).start()

    slot = lax.rem(step, WEIGHT_RING)
    w_copy(j, slot).wait()

    def project(xn, o_ref, kind, grow):
        z = _dot(xn, wbuf_ref[slot], prec)
        if kind == "plain":
            o_ref[...] = z
        elif kind in ("gelu", "gelu_mm"):
            o_ref[...] = jax.nn.gelu(z).astype(o_ref.dtype)
        else:
            gain = hg_ref[grow:grow + 1, :]
            for c in range(tn // HEAD_DIM):
                sl = slice(c * HEAD_DIM, (c + 1) * HEAD_DIM)
                o_ref[:, sl] = _head_norm(z[:, sl], gain).astype(o_ref.dtype)

    @pl.when(j == 0)
    def _first():
        x = x_ref[...]
        xn = x * lax.rsqrt(jnp.mean(x * x, axis=-1, keepdims=True) + EPS) * g_ref[...]
        xn_ref[...] = xn
        if with_forget:
            lf_ref[...] = jax.nn.log_sigmoid(_dot(xn, wf_ref[...], prec) + bfr_ref[...])
            zt = _dot_tn(wf_ref[...], xn, prec)[:2 * SUBLANES, :]
            lft_ref[...] = jax.nn.log_sigmoid(zt + bfc_ref[...])
        project(xn, out_refs[0], segs[0][1], segs[0][2])

    lo = 0
    for (ncols, kind, grow), o_ref in zip(segs, out_refs):
        nt = ncols // tn

        @pl.when((j >= max(lo, 1)) & (j < lo + nt))
        def _rest(o_ref=o_ref, kind=kind, grow=grow):
            project(xn_ref[...], o_ref, kind, grow)

        lo += nt


def _norm_matmul(x, gain, w, layer, segs, head_gains, forget=None, prec=None):
    t = x.shape[0]
    tm = min(t, PROJ_ROW_TILE)
    tn = PROJ_COL_TILE
    n_tiles = sum(s[0] for s in segs) // tn
    with_forget = forget is not None
    in_specs = [
        pl.BlockSpec((tm, D_MODEL), lambda i, j: (i, 0)),
        pl.BlockSpec((1, D_MODEL), lambda i, j: (0, 0)),
        pl.BlockSpec(memory_space=pl.ANY),
        pl.BlockSpec(head_gains.shape, lambda i, j: (0, 0)),
    ]
    args = [x, gain, w, head_gains]
    if with_forget:
        wf, bfr, bfc = forget
        in_specs += [pl.BlockSpec(wf.shape, lambda i, j: (0, 0)),
                     pl.BlockSpec(bfr.shape, lambda i, j: (0, 0)), pl.BlockSpec(bfc.shape, lambda i, j: (0, 0))]
        args += [wf, bfr, bfc]
    out_shape, out_specs = [], []
    lo = 0
    for ncols, kind, _ in segs:
        nt = ncols // tn
        out_shape.append(jax.ShapeDtypeStruct((t, ncols), _matmul_only_dtype(prec) if kind.endswith("_mm") else F32))
        out_specs.append(pl.BlockSpec((tm, tn), lambda i, j, lo=lo, nt=nt: (i, jnp.clip(j - lo, 0, nt - 1))))
        lo += nt
    if with_forget:
        out_shape += [jax.ShapeDtypeStruct((t, LANES), F32), jax.ShapeDtypeStruct((2 * SUBLANES, t), F32)]
        out_specs += [pl.BlockSpec((tm, LANES), lambda i, j: (i, 0)),
                      pl.BlockSpec((2 * SUBLANES, tm), lambda i, j: (0, i))]
    return pl.pallas_call(
        functools.partial(_norm_matmul_kernel, segs=tuple(segs), tn=tn, with_forget=with_forget, prec=prec,
                          layer=layer, n_col=n_tiles, n_steps=(t // tm) * n_tiles),
        grid=(t // tm, n_tiles),
        in_specs=in_specs,
        out_specs=out_specs,
        out_shape=out_shape,
        scratch_shapes=[pltpu.VMEM((tm, D_MODEL), F32), pltpu.VMEM((WEIGHT_RING, D_MODEL, tn), F32),
                        pltpu.SemaphoreType.DMA((WEIGHT_RING,))],
        compiler_params=_params(2),
    )(*args)


def _gmlp_kernel(u_ref, v_ref, vg_ref, w_ref, b_ref, y_ref, vn_ref, *, rows, grp, valid, prec):
    v = v_ref[...]
    vn = v * lax.rsqrt(jnp.mean(v * v, axis=-1, keepdims=True) + EPS) * vg_ref[...]
    vn_ref[...] = vn
    r = lax.broadcasted_iota(jnp.int32, (rows, rows), 0)
    c = lax.broadcasted_iota(jnp.int32, (rows, rows), 1)
    shift = grp.bit_length() - 1
    allowed = ((c & (grp - 1)) <= (r & (grp - 1))) & ((c & (grp - 1)) < valid)
    if grp < rows:
        allowed = allowed & ((r >> shift) == (c >> shift))
    for g in range(MAIN_HEADS):
        sl = slice(g * HEAD_DIM, (g + 1) * HEAD_DIM)
        w = jnp.where(allowed, w_ref[g], 0.0)
        mixed = _dot(w, vn[:, sl], prec) + b_ref[:, g:g + 1]
        y_ref[:, sl] = (u_ref[:, sl] * mixed).astype(y_ref.dtype)


def _gmlp(u, v, v_gain, w_mix, b_mix, rows, grp, valid, blocks_per_seq, prec):
    t = u.shape[0]
    return pl.pallas_call(
        functools.partial(_gmlp_kernel, rows=rows, grp=grp, valid=valid, prec=prec),
        grid=(t // rows,),
        in_specs=[
            pl.BlockSpec((rows, MAIN_WIDTH), lambda i: (i, 0)),
            pl.BlockSpec((rows, MAIN_WIDTH), lambda i: (i, 0)),
            pl.BlockSpec((1, MAIN_WIDTH), lambda i: (0, 0)),
            pl.BlockSpec((MAIN_HEADS, rows, rows), lambda i: (0, 0, 0)),
            pl.BlockSpec((rows, MAIN_HEADS), lambda i: (0, 0)),
        ],
        out_specs=[pl.BlockSpec((rows, MAIN_WIDTH), lambda i: (i, 0)),
                   pl.BlockSpec((rows, MAIN_WIDTH), lambda i: (i // blocks_per_seq, 0))],
        out_shape=[jax.ShapeDtypeStruct((t, MAIN_WIDTH), _matmul_only_dtype(prec)),
                   jax.ShapeDtypeStruct((t // blocks_per_seq, MAIN_WIDTH), F32)],
        compiler_params=_params(1),
    )(u, v, v_gain, w_mix, b_mix)


def _mem_attn_kernel(q_ref, k_ref, v_ref, o_ref, *, prec):
    for h in range(MEM_HEADS):
        sl = slice(h * HEAD_DIM, (h + 1) * HEAD_DIM)
        s = _dot_nt(q_ref[:, sl].astype(F32), k_ref[:, sl], prec) * ATTN_SCALE
        e = jnp.exp(s - jnp.max(s, axis=-1, keepdims=True))
        p = e / jnp.sum(e, axis=-1, keepdims=True)
        o_ref[:, sl] = _dot(p, v_ref[:, sl], prec).astype(o_ref.dtype)


def _mem_attn(q, k, v, layer, nb, rows_per_batch, prec):
    tq = min(rows_per_batch, MEM_ATTN_ROW_TILE)
    nq = rows_per_batch // tq
    kv_spec = pl.BlockSpec((None, None, N_MEM, MEM_WIDTH), lambda b, i: (layer, b, 0, 0))
    return pl.pallas_call(
        functools.partial(_mem_attn_kernel, prec=prec),
        grid=(nb, nq),
        in_specs=[pl.BlockSpec((tq, MEM_WIDTH), lambda b, i: (b * nq + i, 0)), kv_spec, kv_spec],
        out_specs=pl.BlockSpec((tq, MEM_WIDTH), lambda b, i: (b * nq + i, 0)),
        out_shape=jax.ShapeDtypeStruct(q.shape, _matmul_only_dtype(prec)),
        compiler_params=_params(2),
    )(q, k, v)


def _out_proj_kernel(ym_ref, ymem_ref, w_hbm, h_ref, o_ref, wbuf_ref, wsems, *, prec, layer, tn, n_col, n_steps):
    j = pl.program_id(1)
    step = pl.program_id(0) * n_col + j

    def w_copy(col_tile, slot):
        col = col_tile * tn if isinstance(col_tile, int) else pl.multiple_of(col_tile * tn, tn)
        return pltpu.make_async_copy(w_hbm.at[layer, :, pl.ds(col, tn)], wbuf_ref.at[slot], wsems.at[slot])

    @pl.when(step == 0)
    def _():
        w_copy(0, 0).start()
        if n_steps > 1:
            w_copy(1 % n_col, 1).start()

    @pl.when(step + 2 < n_steps)
    def _():
        w_copy(lax.rem(j + 2, n_col), lax.rem(step + 2, WEIGHT_RING)).start()

    slot = lax.rem(step, WEIGHT_RING)
    w_copy(j, slot).wait()
    w1 = wbuf_ref[slot, :MAIN_WIDTH, :].astype(ym_ref.dtype)
    w2 = wbuf_ref[slot, MAIN_WIDTH:, :].astype(ymem_ref.dtype)
    o_ref[...] = h_ref[...] + _dot(ym_ref[...], w1, prec) + _dot(ymem_ref[...], w2, prec)


def _out_proj(y_main, y_mem, w_out, layer, h, prec):
    t = h.shape[0]
    tm = min(t, PROJ_ROW_TILE_BF16 if y_main.dtype == jnp.bfloat16 else PROJ_ROW_TILE)
    tn = PROJ_COL_TILE
    n_col = D_MODEL // tn
    return pl.pallas_call(
        functools.partial(_out_proj_kernel, prec=prec, layer=layer, tn=tn, n_col=n_col, n_steps=(t // tm) * n_col),
        grid=(t // tm, n_col),
        in_specs=[
            pl.BlockSpec((tm, MAIN_WIDTH), lambda i, j: (i, 0)),
            pl.BlockSpec((tm, MEM_WIDTH), lambda i, j: (i, 0)),
            pl.BlockSpec(memory_space=pl.ANY),
            pl.BlockSpec((tm, tn), lambda i, j: (i, j)),
        ],
        out_specs=pl.BlockSpec((tm, tn), lambda i, j: (i, j)),
        out_shape=jax.ShapeDtypeStruct((t, D_MODEL), F32),
        scratch_shapes=[pltpu.VMEM((WEIGHT_RING, D_MODEL, tn), F32), pltpu.SemaphoreType.DMA((WEIGHT_RING,))],
        compiler_params=_params(2),
    )(y_main, y_mem, w_out, h)


def _rms_norm_rows(x, gain_row):
    return x * lax.rsqrt(jnp.mean(x * x, axis=-1, keepdims=True) + EPS) * gain_row


def _router_kernel(h_ref, g_ref, wr_ref, br_ref, idx_ref, gate_ref, *, prec):
    xn = _rms_norm_rows(h_ref[...], g_ref[...])
    lg = _dot(xn, wr_ref[...], prec) + br_ref[...]
    lane = lax.broadcasted_iota(jnp.int32, lg.shape, 1)
    neg = -jnp.inf
    is_grp = lane < N_GROUPS
    gl = jnp.where(is_grp, lg, neg)
    ge = jnp.where(is_grp, jnp.exp(gl - jnp.max(gl, axis=-1, keepdims=True)), 0.0)
    pg = ge / jnp.sum(ge, axis=-1, keepdims=True)
    p_top = jnp.max(pg, axis=-1, keepdims=True)
    g_idx = jnp.min(jnp.where(is_grp & (pg == p_top), lane, LANES), axis=-1, keepdims=True)
    first = N_GROUPS + EXPERTS_PER_GROUP * g_idx
    in_grp = (lane >= first) & (lane < first + EXPERTS_PER_GROUP)
    e1 = jnp.max(jnp.where(in_grp, lg, neg), axis=-1, keepdims=True)
    i1 = jnp.min(jnp.where(in_grp & (lg == e1), lane, LANES), axis=-1, keepdims=True)
    rest = in_grp & (lane != i1)
    e2 = jnp.max(jnp.where(rest, lg, neg), axis=-1, keepdims=True)
    i2 = jnp.min(jnp.where(rest & (lg == e2), lane, LANES), axis=-1, keepdims=True)
    t2 = jnp.exp(e2 - e1)
    den = 1.0 + t2
    idx_ref[...] = jnp.where(lane == 0, i1 - N_GROUPS, jnp.where(lane == 1, i2 - N_GROUPS, 0))
    gate_ref[...] = jnp.where(lane == 0, p_top * (1.0 / den), jnp.where(lane == 1, p_top * (t2 / den), 0.0))


def _router(h, gain, wr, br, prec):
    t = h.shape[0]
    tm = min(t, ROUTER_ROW_TILE)
    return pl.pallas_call(
        functools.partial(_router_kernel, prec=prec),
        grid=(t // tm,),
        in_specs=[pl.BlockSpec((tm, D_MODEL), lambda i: (i, 0)), pl.BlockSpec((1, D_MODEL), lambda i: (0, 0)),
                  pl.BlockSpec((D_MODEL, LANES), lambda i: (0, 0)), pl.BlockSpec((1, LANES), lambda i: (0, 0))],
        out_specs=[pl.BlockSpec((tm, LANES), lambda i: (i, 0)), pl.BlockSpec((tm, LANES), lambda i: (i, 0))],
        out_shape=[jax.ShapeDtypeStruct((t, LANES), jnp.int32), jax.ShapeDtypeStruct((t, LANES), F32)],
        compiler_params=_params(1),
    )(h, gain, wr, br)


def _plan_kernel(e_ref, pos_ref, te_ref, nv_ref, rank_ref, *, t2, blk, tm):
    shift = tm.bit_length() - 1
    nblk = t2 // blk
    r = lax.broadcasted_iota(jnp.int32, (blk, blk), 0)
    c = lax.broadcasted_iota(jnp.int32, (blk, blk), 1)
    upper = (r <= c).astype(F32)
    sub = lax.broadcasted_iota(jnp.int32, (N_EXPERTS, blk), 0)
    carry = jnp.zeros((N_EXPERTS, 1), F32)
    for b in range(nblk):
        sl = slice(b * blk, (b + 1) * blk)
        oh = (sub == e_ref[:, sl]).astype(F32)
        cs = _dot(oh, upper) + carry
        rank_ref[:, sl] = jnp.sum(oh * (cs - 1.0), axis=0, keepdims=True)
        carry = cs[:, blk - 1:blk]
    counts = carry.astype(jnp.int32)
    padc = ((counts + (tm - 1)) >> shift) << shift
    sub1 = lax.broadcasted_iota(jnp.int32, (N_EXPERTS, 1), 0)
    off = jnp.zeros((N_EXPERTS, 1), jnp.int32)
    run = jnp.zeros((1, 1), jnp.int32)
    for e in range(N_EXPERTS):
        off = jnp.where(sub1 == e, run, off)
        run = run + padc[e:e + 1, :]
    ends = off + padc
    for b in range(nblk):
        sl = slice(b * blk, (b + 1) * blk)
        offv = jnp.sum(jnp.where(sub == e_ref[:, sl], off, 0), axis=0, keepdims=True)
        pos_ref[:, sl] = rank_ref[:, sl].astype(jnp.int32) + offv
    lane = lax.broadcasted_iota(jnp.int32, (N_EXPERTS, LANES), 1)
    lane1 = lax.broadcasted_iota(jnp.int32, (1, LANES), 1)
    nv = run >> shift
    te = jnp.sum((ends <= lane * tm).astype(jnp.int32), axis=0, keepdims=True)
    te_last = jnp.sum((ends <= (nv - 1) * tm).astype(jnp.int32), axis=0, keepdims=True)
    te_ref[...] = jnp.minimum(jnp.where(lane1 >= nv, te_last, te), N_EXPERTS - 1)
    nv_ref[...] = jnp.broadcast_to(nv, (1, LANES))


def _plan(e_flat, tm):
    t2 = e_flat.shape[1]
    blk = min(t2, PLAN_BLOCK)
    return pl.pallas_call(
        functools.partial(_plan_kernel, t2=t2, blk=blk, tm=tm),
        out_shape=[jax.ShapeDtypeStruct((1, t2), jnp.int32), jax.ShapeDtypeStruct((1, LANES), jnp.int32),
                   jax.ShapeDtypeStruct((1, LANES), jnp.int32)],
        scratch_shapes=[pltpu.VMEM((1, t2), F32)],
        compiler_params=pltpu.CompilerParams(vmem_limit_bytes=VMEM_LIMIT_BYTES),
    )(e_flat)


def _dispatch_kernel(pos_ref, te_ref, nv_ref, h_ref, g_ref, xs_ref, x_ref, zero_ref, fill_sem, sems,
                     *, t, tm, td, n_tiles):
    step = pl.program_id(0)
    nv = nv_ref[0]

    @pl.when(step == 0)
    def _():
        zero_ref[...] = jnp.zeros_like(zero_ref)

        def fill_copy(j):
            return pltpu.make_async_copy(zero_ref.at[pl.ds(0, tm)], xs_ref.at[pl.ds(j * tm, tm)], fill_sem)

        def holds_padding(j):
            return (j >= nv - 1) | (te_ref[jnp.minimum(j + 1, n_tiles - 1)] != te_ref[j])

        def start(j, carry):
            @pl.when(holds_padding(j))
            def _():
                fill_copy(j).start()
            return carry

        def wait(j, carry):
            @pl.when(holds_padding(j))
            def _():
                fill_copy(j).wait()
            return carry

        lax.fori_loop(0, n_tiles, start, 0)
        lax.fori_loop(0, n_tiles, wait, 0)

    x_ref[...] = _rms_norm_rows(h_ref[...], g_ref[...])
    base = step * td
    for k in range(2):
        for r in range(td):
            pltpu.make_async_copy(x_ref.at[pl.ds(r, 1)], xs_ref.at[pl.ds(pos_ref[k * t + base + r], 1)],
                                  sems.at[k]).start(priority=k)
    for k in range(2):
        done = zero_ref.at[pl.ds(0, td)]
        pltpu.make_async_copy(done, done, sems.at[k]).wait()


def _dispatch(pos, te, nv, h, gain, tm):
    t = h.shape[0]
    n_tiles = te.shape[0]
    td = min(t, COMBINE_ROW_TILE)
    return pl.pallas_call(
        functools.partial(_dispatch_kernel, t=t, tm=tm, td=td, n_tiles=n_tiles),
        grid_spec=pltpu.PrefetchScalarGridSpec(
            num_scalar_prefetch=3,
            grid=(t // td,),
            in_specs=[pl.BlockSpec((td, D_MODEL), lambda i, pos, te, nv: (i, 0)),
                      pl.BlockSpec((1, D_MODEL), lambda i, pos, te, nv: (0, 0))],
            out_specs=pl.BlockSpec(memory_space=pl.ANY),
            scratch_shapes=[pltpu.VMEM((td, D_MODEL), F32), pltpu.VMEM((max(tm, td), D_MODEL), F32),
                            pltpu.SemaphoreType.DMA(()), pltpu.SemaphoreType.DMA((2,))],
        ),
        out_shape=jax.ShapeDtypeStruct((n_tiles * tm, D_MODEL), F32),
        compiler_params=_params(1),
    )(pos, te, nv, h, gain)


def _moe_ffn_kernel(te_ref, nv_ref, x_ref, wg_hbm, wu_hbm, wd_hbm, y_ref,
                    wg_ref, wu_ref, wd_ref, wslot_ref, wsems, *, layer, prec):
    i = pl.program_id(0)
    nv = nv_ref[0]
    expert = te_ref[i]

    def weight_copies(e, ws):
        return (pltpu.make_async_copy(wg_hbm.at[layer, e], wg_ref.at[ws], wsems.at[ws]),
                pltpu.make_async_copy(wu_hbm.at[layer, e], wu_ref.at[ws], wsems.at[ws]),
                pltpu.make_async_copy(wd_hbm.at[layer, e], wd_ref.at[ws], wsems.at[ws]))

    @pl.when(i == 0)
    def _():
        wslot_ref[0] = 1
        for cp in weight_copies(expert, 0):
            cp.start()

    is_first = (i < nv) & ((i == 0) | (te_ref[jnp.maximum(i - 1, 0)] != expert))

    @pl.when(is_first)
    def _():
        ws = 1 - wslot_ref[0]
        wslot_ref[0] = ws
        for cp in weight_copies(expert, ws):
            cp.wait()
        nxt = lax.while_loop(lambda j: (j < nv) & (te_ref[jnp.minimum(j, nv - 1)] == expert), lambda j: j + 1, i + 1)

        @pl.when(nxt < nv)
        def _():
            for cp in weight_copies(te_ref[nxt], 1 - ws):
                cp.start()

    @pl.when(i < nv)
    def _():
        ws = wslot_ref[0]
        x = x_ref[...]
        hdn = jax.nn.silu(_dot(x, wg_ref[ws], prec)) * _dot(x, wu_ref[ws], prec)
        _store_chunk_rows(y_ref, _dot(hdn, wd_ref[ws], prec))

    @pl.when(i >= nv)
    def _():
        y_ref[...] = jnp.zeros_like(y_ref)


def _moe_ffn(te, nv, xs, w_gate, w_up, w_down, layer, tm, prec):
    n_rows = xs.shape[0]
    n_tiles = n_rows // tm
    any_spec = pl.BlockSpec(memory_space=pl.ANY)
    return pl.pallas_call(
        functools.partial(_moe_ffn_kernel, layer=layer, prec=prec),
        grid_spec=pltpu.PrefetchScalarGridSpec(
            num_scalar_prefetch=2,
            grid=(n_tiles,),
            in_specs=[pl.BlockSpec((tm, D_MODEL), lambda i, te, nv: (jnp.minimum(i, nv[0] - 1), 0)),
                      any_spec, any_spec, any_spec],
            out_specs=pl.BlockSpec((tm * N_CHUNKS, LANES), lambda i, te, nv: (i, 0)),
            scratch_shapes=[pltpu.VMEM((2, D_MODEL, D_EXPERT), F32), pltpu.VMEM((2, D_MODEL, D_EXPERT), F32),
                            pltpu.VMEM((2, D_EXPERT, D_MODEL), F32), pltpu.SMEM((1,), jnp.int32),
                            pltpu.SemaphoreType.DMA((2,))],
        ),
        out_shape=jax.ShapeDtypeStruct((n_rows * N_CHUNKS, LANES), F32),
        compiler_params=_params(1),
    )(te, nv, xs, w_gate, w_up, w_down)


def _combine_kernel(pos_ref, h_ref, g_ref, y_ref, o_ref, y0_ref, y1_ref, sems, *, t, tm):
    base = pl.program_id(0) * tm
    for k, buf in enumerate((y0_ref, y1_ref)):
        for r in range(tm):
            src = pl.multiple_of(pos_ref[k * t + base + r] * N_CHUNKS, N_CHUNKS)
            pltpu.make_async_copy(y_ref.at[pl.ds(src, N_CHUNKS)], buf.at[pl.ds(r * ROW_PITCH, N_CHUNKS)],
                                  sems.at[k]).start(priority=k)
    for k, buf in enumerate((y0_ref, y1_ref)):
        done = buf.at[pl.ds(0, tm * N_CHUNKS)]
        pltpu.make_async_copy(done, done, sems.at[k]).wait()
    g = g_ref[...]
    o_ref[...] = (h_ref[...] + g[:, 0:1] * _load_gathered_rows(y0_ref, tm)
                  + g[:, 1:2] * _load_gathered_rows(y1_ref, tm))


def _combine(pos, h, gates, y):
    t = h.shape[0]
    tm = min(t, COMBINE_ROW_TILE)
    return pl.pallas_call(
        functools.partial(_combine_kernel, t=t, tm=tm),
        grid_spec=pltpu.PrefetchScalarGridSpec(
            num_scalar_prefetch=1,
            grid=(t // tm,),
            in_specs=[pl.BlockSpec((tm, D_MODEL), lambda i, pos: (i, 0)),
                      pl.BlockSpec((tm, LANES), lambda i, pos: (i, 0)),
                      pl.BlockSpec(memory_space=pl.ANY)],
            out_specs=pl.BlockSpec((tm, D_MODEL), lambda i, pos: (i, 0)),
            scratch_shapes=[pltpu.VMEM((tm * ROW_PITCH, LANES), F32), pltpu.VMEM((tm * ROW_PITCH, LANES), F32),
                            pltpu.SemaphoreType.DMA((2,))],
        ),
        out_shape=jax.ShapeDtypeStruct((t, D_MODEL), F32),
        compiler_params=_params(1),
    )(pos, h, gates, y)


def _hier_moe(h, gain, wr, br, w_gate, w_up, w_down, layer, prec):
    t = h.shape[0]
    tm = MOE_ROW_TILE if t >= MOE_SMALL_BELOW else MOE_ROW_TILE_SMALL
    n_tiles = (2 * t) // tm + N_EXPERTS
    assert n_tiles <= LANES
    idx, gates = _router(h, gain, wr, br, prec)
    e_flat = idx[:, :2].T.reshape(1, 2 * t)
    pos, te, nv = _plan(e_flat, tm)
    pos = pos.reshape(2 * t)
    te = te[0, :n_tiles]
    nv = nv[0, :1]
    xs = _dispatch(pos, te, nv, h, gain, tm)
    y = _moe_ffn(te, nv, xs, w_gate, w_up, w_down, layer, tm, prec)
    return _combine(pos, h, gates, y)


def _fox_cumsum_kernel(lft_ref, lf_ref, ct_ref, c_ref, *, s, blk):
    r = lax.broadcasted_iota(jnp.int32, (blk, blk), 0)
    c = lax.broadcasted_iota(jnp.int32, (blk, blk), 1)
    upper = (r <= c).astype(F32)
    lower = (r >= c).astype(F32)
    carry_t = jnp.zeros((2 * SUBLANES, 1), F32)
    carry = jnp.zeros((1, LANES), F32)
    for b in range(s // blk):
        sl = slice(b * blk, (b + 1) * blk)
        ct = _dot3_left(lft_ref[:, sl], upper) + carry_t
        ct_ref[:, sl] = ct
        carry_t = ct[:, blk - 1:blk]
        cc = _dot3_right(lower, lf_ref[sl, :]) + carry
        c_ref[sl, :] = cc
        carry = cc[blk - 1:blk, :]


def _fox_cumsum(lft, lf, nb, s):
    t = lf.shape[0]
    blk = min(s, CUMSUM_BLOCK)
    return pl.pallas_call(
        functools.partial(_fox_cumsum_kernel, s=s, blk=blk),
        grid=(nb,),
        in_specs=[pl.BlockSpec((2 * SUBLANES, s), lambda b: (0, b)), pl.BlockSpec((s, LANES), lambda b: (b, 0))],
        out_specs=[pl.BlockSpec((2 * SUBLANES, s), lambda b: (0, b)), pl.BlockSpec((s, LANES), lambda b: (b, 0))],
        out_shape=[jax.ShapeDtypeStruct((2 * SUBLANES, t), F32), jax.ShapeDtypeStruct((t, LANES), F32)],
        compiler_params=_params(1),
    )(lft, lf)


def _fox_prompt_kernel(q_ref, k_ref, v_ref, ct_ref, c_ref, o_ref, *, s, tq):
    h = pl.program_id(1)
    lane = lax.broadcasted_iota(jnp.int32, (s, LANES), 1)
    c_col = jnp.sum(jnp.where(lane == h, c_ref[...], 0.0), axis=1, keepdims=True) * LOG2E
    c_row = ct_ref[pl.ds(h, 1), :] * LOG2E
    ri = lax.broadcasted_iota(jnp.int32, (tq, tq), 0)
    ci = lax.broadcasted_iota(jnp.int32, (tq, tq), 1)
    causal = ci <= ri
    for qi in range(s // tq):
        qs = slice(qi * tq, (qi + 1) * tq)
        q = q_ref[qs, :].astype(F32)
        cq = c_col[qs, :]
        m = jnp.full((tq, 1), -jnp.inf, F32)
        l = jnp.zeros((tq, 1), F32)
        acc = jnp.zeros((tq, HEAD_DIM), F32)
        for kj in range(qi + 1):
            ks = slice(kj * tq, (kj + 1) * tq)
            sc = lax.dot_general(q, k_ref[ks, :], _NT, preferred_element_type=F32) - c_row[:, ks]
            if kj == qi:
                sc = jnp.where(causal, sc, -jnp.inf)
            m_new = jnp.maximum(m, jnp.max(sc, axis=-1, keepdims=True) + cq)
            alpha = jnp.exp2(m - m_new)
            p = jnp.exp2(sc - (m_new - cq))
            l = alpha * l + jnp.sum(p, axis=-1, keepdims=True)
            acc = alpha * acc + _dot(p, v_ref[ks, :])
            m = m_new
        o_ref[qs, :] = (acc / l).astype(o_ref.dtype)


def _fox_prompt(q, k, v, ct, c, nb, s):
    t = q.shape[0]
    tq = min(s, ATTN_TILE)
    head_spec = pl.BlockSpec((s, HEAD_DIM), lambda b, h: (b, h))
    return pl.pallas_call(
        functools.partial(_fox_prompt_kernel, s=s, tq=tq),
        grid=(nb, MAIN_HEADS),
        in_specs=[head_spec, head_spec, head_spec,
                  pl.BlockSpec((2 * SUBLANES, s), lambda b, h: (0, b)),
                  pl.BlockSpec((s, LANES), lambda b, h: (b, 0))],
        out_specs=head_spec,
        out_shape=jax.ShapeDtypeStruct((t, MAIN_WIDTH), _matmul_only_dtype(None)),
        compiler_params=_params(2),
    )(q, k, v, ct, c)


def _fox_past_bias_kernel(pt_ref, lfc_ref, o_ref, buf_ref, sem, *, n_pages):
    b = pl.program_id(0)

    def page_copy(p):
        page = pt_ref[b * n_pages + p]
        return pltpu.make_async_copy(lfc_ref.at[:, pl.ds(page, 1), :], buf_ref.at[:, pl.ds(p, 1), :], sem)

    def start(p, carry):
        page_copy(p).start()
        return carry

    def wait(p, carry):
        page_copy(p).wait()
        return carry

    lax.fori_loop(0, n_pages, start, 0)
    r = lax.broadcasted_iota(jnp.int32, (PAGE_SIZE, PAGE_SIZE), 0)
    c = lax.broadcasted_iota(jnp.int32, (PAGE_SIZE, PAGE_SIZE), 1)
    after_in_page = (r > c).astype(F32)
    pr = lax.broadcasted_iota(jnp.int32, (n_pages, n_pages), 0)
    pc = lax.broadcasted_iota(jnp.int32, (n_pages, n_pages), 1)
    later_pages = (pc > pr).astype(F32)
    lax.fori_loop(0, n_pages, wait, 0)
    for h in range(MAIN_HEADS):
        lp = buf_ref[h]
        tot = jnp.broadcast_to(jnp.sum(lp, axis=1, keepdims=True), lp.shape)
        o_ref[h] = _dot3_left(lp, after_in_page) + _dot3_right(later_pages, tot)


def _fox_past_bias(page_table_flat, cache_logf_t, nb, n_pages):
    return pl.pallas_call(
        functools.partial(_fox_past_bias_kernel, n_pages=n_pages),
        grid_spec=pltpu.PrefetchScalarGridSpec(
            num_scalar_prefetch=1,
            grid=(nb,),
            in_specs=[pl.BlockSpec(memory_space=pl.ANY)],
            out_specs=pl.BlockSpec((None, MAIN_HEADS, n_pages, PAGE_SIZE), lambda b, pt: (b, 0, 0, 0)),
            scratch_shapes=[pltpu.VMEM((MAIN_HEADS, n_pages, PAGE_SIZE), F32), pltpu.SemaphoreType.DMA(())],
        ),
        out_shape=jax.ShapeDtypeStruct((nb, MAIN_HEADS, n_pages, PAGE_SIZE), F32),
        compiler_params=_params(1),
    )(page_table_flat, cache_logf_t)


def _fox_sample_kernel(*refs, ppb, n_steps, valid):
    pt_ref, q_ref, kn_ref, vn_ref, lf_ref, bias_ref = refs[:6]
    k_refs = refs[6:6 + ppb]
    v_refs = refs[6 + ppb:6 + 2 * ppb]
    o_ref, m_ref, l_ref, acc_ref = refs[6 + 2 * ppb:]
    g = pl.program_id(1)
    rows = SAMPLE_ROWS

    @pl.when(g == 0)
    def _():
        m_ref[...] = jnp.full(m_ref.shape, -jnp.inf, F32)
        l_ref[...] = jnp.zeros(l_ref.shape, F32)
        acc_ref[...] = jnp.zeros(acc_ref.shape, F32)

    r8 = lax.broadcasted_iota(jnp.int32, (rows, rows), 0)
    c8 = lax.broadcasted_iota(jnp.int32, (rows, rows), 1)
    new_mask = (c8 <= r8) & (c8 < valid)
    c_new = _dot3_right(new_mask.astype(F32), lf_ref[...])

    def qk3(q_hi, q_lo, k):
        a = _dot_nt(jnp.concatenate([q_hi, q_lo], axis=0), k)
        return a[:rows] + a[rows:]

    def pv3(p, v):
        p_hi, p_lo = _split_hi_lo(p)
        a = _dot(jnp.concatenate([p_hi, p_lo], axis=0), v)
        return a[:rows] + a[rows:]

    all_scores = []
    for h in range(MAIN_HEADS):
        q_hi, q_lo = _split_hi_lo(q_ref[:, h * HEAD_DIM:(h + 1) * HEAD_DIM])
        cq = c_new[:, h:h + 1]
        all_scores.append([qk3(q_hi, q_lo, k_refs[u][h]) + cq + bias_ref[h, u:u + 1, :]
                           for u in range(ppb)])
    all_probs, all_alpha = [], []
    for h in range(MAIN_HEADS):
        m_old = m_ref[h, :, 0:1]
        m_new = m_old
        for sc in all_scores[h]:
            m_new = jnp.maximum(m_new, jnp.max(sc, axis=-1, keepdims=True))
        alpha = jnp.exp(m_old - m_new)
        probs = [jnp.exp(sc - m_new) for sc in all_scores[h]]
        l_new = alpha * l_ref[h, :, 0:1]
        for p in probs:
            l_new = l_new + jnp.sum(p, axis=-1, keepdims=True)
        m_ref[h] = jnp.broadcast_to(m_new, (rows, LANES))
        l_ref[h] = jnp.broadcast_to(l_new, (rows, LANES))
        all_probs.append(probs)
        all_alpha.append(alpha)
    for h in range(MAIN_HEADS):
        sl = slice(h * HEAD_DIM, (h + 1) * HEAD_DIM)
        acc = all_alpha[h] * acc_ref[:, sl]
        for u in range(ppb):
            acc = acc + pv3(all_probs[h][u], v_refs[u][h])
        acc_ref[:, sl] = acc

    @pl.when(g == n_steps - 1)
    def _():
        for h in range(MAIN_HEADS):
            sl = slice(h * HEAD_DIM, (h + 1) * HEAD_DIM)
            qh = q_ref[:, sl]
            cq = c_new[:, h:h + 1]
            cq_row = jnp.sum(jnp.where(r8 == c8, jnp.broadcast_to(cq, (rows, rows)), 0.0), axis=0, keepdims=True)
            sn = _dot_nt(qh, kn_ref[:, sl], HIGHEST) + cq - cq_row
            sn = jnp.where(new_mask, sn, -jnp.inf)
            m_old = m_ref[h, :, 0:1]
            m_new = jnp.maximum(m_old, jnp.max(sn, axis=-1, keepdims=True))
            alpha = jnp.exp(m_old - m_new)
            p = jnp.exp(sn - m_new)
            l_new = alpha * l_ref[h, :, 0:1] + jnp.sum(p, axis=-1, keepdims=True)
            acc = alpha * acc_ref[:, sl] + _dot(p, vn_ref[:, sl], HIGHEST)
            o_ref[:, sl] = acc / l_new


def _fox_sample(page_table_flat, q, k_new, v_new, lf, bias, cache_k_t, cache_v_t, nb, n_pages, valid):
    ppb = PAGES_PER_STEP if n_pages % PAGES_PER_STEP == 0 else n_pages
    n_steps = n_pages // ppb
    row_spec = pl.BlockSpec((SAMPLE_ROWS, MAIN_WIDTH), lambda b, g, pt: (b, 0))

    def page_spec(u):
        return pl.BlockSpec((None, MAIN_HEADS, PAGE_SIZE, HEAD_DIM),
                            lambda b, g, pt, u=u: (pt[b * n_pages + g * ppb + u], 0, 0, 0))

    in_specs = [row_spec, row_spec, row_spec,
                pl.BlockSpec((SAMPLE_ROWS, LANES), lambda b, g, pt: (b, 0)),
                pl.BlockSpec((None, MAIN_HEADS, ppb, PAGE_SIZE), lambda b, g, pt: (b, 0, g, 0))]
    in_specs += [page_spec(u) for u in range(ppb)] + [page_spec(u) for u in range(ppb)]
    return pl.pallas_call(
        functools.partial(_fox_sample_kernel, ppb=ppb, n_steps=n_steps, valid=valid),
        grid_spec=pltpu.PrefetchScalarGridSpec(
            num_scalar_prefetch=1,
            grid=(nb, n_steps),
            in_specs=in_specs,
            out_specs=row_spec,
            scratch_shapes=[pltpu.VMEM((MAIN_HEADS, SAMPLE_ROWS, LANES), F32),
                            pltpu.VMEM((MAIN_HEADS, SAMPLE_ROWS, LANES), F32),
                            pltpu.VMEM((SAMPLE_ROWS, MAIN_WIDTH), F32)],
        ),
        out_shape=jax.ShapeDtypeStruct(q.shape, F32),
        compiler_params=_params(2),
    )(page_table_flat, q, k_new, v_new, lf, bias, *([cache_k_t] * ppb), *([cache_v_t] * ppb))


def _trunk(x, nb, rows, valid, mem_k, mem_v, fox, w, prec):
    u, v, qm = _norm_matmul(x, w["norm1_gain"][0:1], w["w_in_a"], 0,
                            [(MAIN_WIDTH, "gelu_mm", 0), (MAIN_WIDTH, "gelu", 0), (MEM_WIDTH, "headnorm_mm", 0)],
                            w["mem_q_gain"][0:1], prec=prec)
    y_main, v_rows = _gmlp(u, v, w["v_gain_a"][0:1], w["w_mix"], w["b_mix"], w["gmlp_rows"], w["gmlp_grp"], valid,
                           w["gmlp_blocks_per_seq"], prec)
    y_mem = _mem_attn(qm, mem_k, mem_v, 0, nb, rows, prec)
    h = _out_proj(y_main, y_mem, w["w_out"], 0, x, prec)
    h = _hier_moe(h, w["norm2_gain"][0:1], w["w_router"][0], w["b_router"][0], w["w_gate"], w["w_up"], w["w_down"],
                  0, prec)
    k, vv, lf, lft = _norm_matmul(h, w["kv_norm_gain"], w["w_kv_shared"], 0,
                                  [(MAIN_WIDTH, "headnorm", 0), (MAIN_WIDTH, "plain", 0)],
                                  w["k_gain_shared"], forget=w["forget"], prec=prec)
    q, qm = _norm_matmul(h, w["norm1_gain"][1:2], w["w_in_b"], 0,
                         [(MAIN_WIDTH, "headnorm_mm", 0), (MEM_WIDTH, "headnorm_mm", 1)],
                         jnp.concatenate([w["q_gain_b"][0:1] * w["q_logit_scale"], w["mem_q_gain"][1:2]], axis=0),
                         prec=prec)
    y_main = fox(q, k, vv, lf, lft)
    y_mem = _mem_attn(qm, mem_k, mem_v, 1, nb, rows, prec)
    h = _out_proj(y_main, y_mem, w["w_out"], 1, h, prec)
    h = _hier_moe(h, w["norm2_gain"][1:2], w["w_router"][1], w["b_router"][1], w["w_gate"], w["w_up"], w["w_down"],
                  1, prec)
    return h, k, vv, lf, v_rows


def kernel(x_prompt, x_sample, cache_k, cache_v, cache_logf, cache_mem_k, cache_mem_v, page_table, mem_prompt, norm1_gain, norm2_gain, w_in_a, v_gain_a, w_s_a, b_s_a, w_in_b, q_gain_b, kv_norm_gain, w_kv_shared, b_forget, k_gain_shared, mem_norm_gain, w_mem_kv, mem_q_gain, mem_k_gain, w_out, w_router_group, b_router_group, w_router_expert, b_router_expert, w_gate, w_up, w_down):
    batch, seq, _ = x_prompt.shape
    dec_batch, dec_seq, _ = x_sample.shape
    n_phys = cache_k.shape[0]
    n_pages = page_table.shape[1]
    depth = norm1_gain.shape[0]
    assert depth == 2 and dec_seq <= SAMPLE_ROWS and seq % CHUNK == 0

    n_route = N_GROUPS + N_EXPERTS
    w_router = jnp.pad(jnp.concatenate([w_router_group, w_router_expert], axis=-1),
                       ((0, 0), (0, 0), (0, LANES - n_route)))
    b_router = jnp.pad(jnp.concatenate([b_router_group, b_router_expert], axis=-1),
                       ((0, 0), (0, LANES - n_route)))[:, None, :]
    w_f = w_kv_shared[:, 2 * MAIN_WIDTH:]
    forget = (jnp.pad(w_f, ((0, 0), (0, LANES - MAIN_HEADS))),
              jnp.pad(b_forget, (0, LANES - MAIN_HEADS))[None, :],
              jnp.pad(b_forget, (0, 2 * SUBLANES - MAIN_HEADS))[:, None])
    common = dict(norm1_gain=norm1_gain, norm2_gain=norm2_gain, w_in_a=w_in_a, v_gain_a=v_gain_a, w_in_b=w_in_b,
                  q_gain_b=q_gain_b, kv_norm_gain=kv_norm_gain[None, :], w_kv_shared=w_kv_shared,
                  k_gain_shared=k_gain_shared[None, :], mem_q_gain=mem_q_gain, w_out=w_out,
                  w_router=w_router, b_router=b_router, w_gate=w_gate, w_up=w_up, w_down=w_down, forget=forget)

    t_p = batch * seq
    mem_tok = mem_prompt.reshape(batch * N_MEM, D_MODEL)
    mem_k_layers, mem_v_layers = [], []
    for l in range(depth):
        mk, mv = _norm_matmul(mem_tok, mem_norm_gain[l:l + 1], w_mem_kv, l,
                              [(MEM_WIDTH, "headnorm", 0), (MEM_WIDTH, "plain", 0)], mem_k_gain[l:l + 1])
        mem_k_layers.append(mk.reshape(batch, N_MEM, MEM_WIDTH))
        mem_v_layers.append(mv.reshape(batch, N_MEM, MEM_WIDTH))
    mem_k_p = jnp.stack(mem_k_layers)
    mem_v_p = jnp.stack(mem_v_layers)

    def fox_prompt(q, k, v, lf, lft):
        ct, c = _fox_cumsum(lft, lf, batch, seq)
        return _fox_prompt(q, k, v, ct, c, batch, seq)

    w_p = dict(common, w_mix=w_s_a[0], b_mix=b_s_a[0].T, gmlp_rows=CHUNK, gmlp_grp=CHUNK,
               gmlp_blocks_per_seq=seq // CHUNK, q_logit_scale=ATTN_SCALE * LOG2E)
    y_p, k_p, v_p, lf_p, vrows_p = _trunk(x_prompt.reshape(t_p, D_MODEL), batch, seq, seq,
                                          mem_k_p, mem_v_p, fox_prompt, w_p, None)

    rows = SAMPLE_ROWS
    t_s = dec_batch * rows
    x_s = jnp.pad(x_sample, ((0, 0), (0, rows - dec_seq), (0, 0))).reshape(t_s, D_MODEL)
    pt_flat = page_table.reshape(dec_batch * n_pages)
    bias = _fox_past_bias(pt_flat, jnp.transpose(cache_logf, (2, 0, 1)), dec_batch, n_pages)
    cache_k_t = jnp.transpose(cache_k, (0, 2, 1, 3))
    cache_v_t = jnp.transpose(cache_v, (0, 2, 1, 3))

    def fox_sample(q, k, v, lf, lft):
        return _fox_sample(pt_flat, q, k, v, lf, bias, cache_k_t, cache_v_t, dec_batch, n_pages, dec_seq)

    w_s = dict(common, w_mix=jnp.tile(w_s_a[0][:, :rows, :rows], (1, dec_batch, dec_batch)),
               b_mix=jnp.tile(b_s_a[0][:, :rows].T, (dec_batch, 1)), gmlp_rows=t_s, gmlp_grp=rows,
               gmlp_blocks_per_seq=1, q_logit_scale=ATTN_SCALE)
    mem_k_s = cache_mem_k.reshape(depth, dec_batch, N_MEM, MEM_WIDTH)
    mem_v_s = cache_mem_v.reshape(depth, dec_batch, N_MEM, MEM_WIDTH)
    y_s, k_s, v_s, lf_s, vrows_s = _trunk(x_s, dec_batch, rows, dec_seq, mem_k_s, mem_v_s, fox_sample, w_s, SPLIT3)

    def unpad(a, *tail):
        return a.reshape((dec_batch, rows) + tail)[:, :dec_seq]

    return (y_p.reshape(batch, seq, D_MODEL),
            unpad(y_s, D_MODEL),
            k_p.reshape(batch, seq, MAIN_HEADS, HEAD_DIM),
            v_p.reshape(batch, seq, MAIN_HEADS, HEAD_DIM),
            lf_p[:, :MAIN_HEADS].reshape(batch, seq, MAIN_HEADS),
            unpad(k_s, MAIN_HEADS, HEAD_DIM),
            unpad(v_s, MAIN_HEADS, HEAD_DIM),
            unpad(lf_s[:, :MAIN_HEADS], MAIN_HEADS),
            mem_k_p.reshape(depth, batch, N_MEM, MEM_HEADS, HEAD_DIM),
            mem_v_p.reshape(depth, batch, N_MEM, MEM_HEADS, HEAD_DIM),
            vrows_p.reshape(batch, CHUNK, MAIN_WIDTH)[None],
            unpad(vrows_s, MAIN_WIDTH)[None])
```
